```python
import jax
import jax.numpy as jnp
from jax import lax
import numpy as np

D_MODEL = 1024
BATCH = 8
SEQ = 4096
DEPTH = 1

GRID_W = 64
CTX_LEN = 256
N_HEADS = 8
N_KV_HEADS = 2
Q_PER_KV = N_HEADS // N_KV_HEADS
HEAD_DIM = 64
ATTN_WIDTH = N_HEADS * HEAD_DIM
KV_WIDTH = N_KV_HEADS * HEAD_DIM
WINDOW = 128
ATTN_BLOCK = 128
SGU_HEADS = 8
SGU_HEAD_DIM = 64
SGU_WIDTH = SGU_HEADS * SGU_HEAD_DIM
SGU_CHUNK = 128
MIX_WIDTH = ATTN_WIDTH + SGU_WIDTH
KV_START = ATTN_WIDTH
KV_END = ATTN_WIDTH + 2 * KV_WIDTH
IN_WIDTH = KV_END + 2 * SGU_WIDTH
N_EXPERTS = 32
TOP_K = 4
D_FF_EXPERT = 1024
SWIGLU_LIMIT = 7.0
SWIGLU_ALPHA = 1.702
MOE_BLOCK = 256
ROPE_THETA = 10000.0
EPS = 1e-6
N_MOD = 6

kernel_name = "hybrid_swa_sgu_moe_dit_block"


def rms_norm(x, g):
    xf = x.astype(jnp.float32)
    y = xf * lax.rsqrt(jnp.mean(xf * xf, axis=-1, keepdims=True) + EPS)
    return (y * g.astype(jnp.float32)).astype(x.dtype)


def layer_norm(x, g, b):
    xf = x.astype(jnp.float32)
    mu = jnp.mean(xf, axis=-1, keepdims=True)
    var = jnp.mean(jnp.square(xf - mu), axis=-1, keepdims=True)
    y = (xf - mu) * lax.rsqrt(var + EPS) * g.astype(jnp.float32) + b.astype(jnp.float32)
    return y.astype(x.dtype)


def modulate(x, shift, scale):
    return x * (1 + scale) + shift


def axial_rope_tables(n_tokens, dtype):
    rows = n_tokens // GRID_W
    pos_row = jnp.repeat(jnp.arange(rows, dtype=jnp.float32), GRID_W)
    pos_col = jnp.tile(jnp.arange(GRID_W, dtype=jnp.float32), rows)
    n_freq = HEAD_DIM // 4
    inv_freq = ROPE_THETA ** (-jnp.arange(n_freq, dtype=jnp.float32) / n_freq)
    ang_r = pos_row[:, None, None] * inv_freq
    ang_c = pos_col[:, None, None] * inv_freq
    return (jnp.cos(ang_r).astype(dtype), jnp.sin(ang_r).astype(dtype),
            jnp.cos(ang_c).astype(dtype), jnp.sin(ang_c).astype(dtype))


def rope_axis(x, cos, sin):
    x1, x2 = jnp.split(x, 2, axis=-1)
    return jnp.concatenate([x1 * cos - x2 * sin, x2 * cos + x1 * sin], axis=-1)


def apply_axial_rope(x, tables):
    cos_r, sin_r, cos_c, sin_c = tables
    half = HEAD_DIM // 2
    return jnp.concatenate([rope_axis(x[..., :half], cos_r, sin_r),
                            rope_axis(x[..., half:], cos_c, sin_c)], axis=-1)


def split_projection(p):
    return jnp.split(p, [ATTN_WIDTH, ATTN_WIDTH + KV_WIDTH, KV_END, KV_END + SGU_WIDTH], axis=-1)


def window_attention(q, k, v, k_ctx, v_ctx, sink):
    B, S = q.shape[0], q.shape[1]
    L = k_ctx.shape[1]
    nb = S // ATTN_BLOCK
    scale = HEAD_DIM ** -0.5
    qb = q.reshape(B, nb, ATTN_BLOCK, N_KV_HEADS, Q_PER_KV, HEAD_DIM)

    def band(t):
        tp = jnp.pad(t, ((0, 0), (ATTN_BLOCK, ATTN_BLOCK), (0, 0), (0, 0)))
        tp = tp.reshape(B, nb + 2, ATTN_BLOCK, N_KV_HEADS, HEAD_DIM)
        return jnp.concatenate([tp[:, :-2], tp[:, 1:-1], tp[:, 2:]], axis=2)

    kb, vb = band(k), band(v)
    s_loc = jnp.einsum('bnqkgd,bnjkd->bnkgqj', qb, kb).astype(jnp.float32) * scale
    qi = jnp.arange(ATTN_BLOCK)[:, None]
    kj = jnp.arange(3 * ATTN_BLOCK)[None, :] - ATTN_BLOCK
    key_pos = jnp.arange(nb)[:, None, None] * ATTN_BLOCK + kj[None]
    valid = (jnp.abs(kj - qi) <= WINDOW)[None] & (key_pos >= 0) & (key_pos < S)
    s_loc = jnp.where(valid[None, :, None, None], s_loc, -jnp.inf)
    s_ctx = jnp.einsum('bnqkgd,bjkd->bnkgqj', qb, k_ctx).astype(jnp.float32) * scale
    s_sink = jnp.broadcast_to(sink.astype(jnp.float32).reshape(N_KV_HEADS, Q_PER_KV, 1, 1),
                              s_loc.shape[:-1] + (1,))
    p = jax.nn.softmax(jnp.concatenate([s_loc, s_ctx, s_sink], axis=-1), axis=-1)
    n_loc = 3 * ATTN_BLOCK
    p_loc = p[..., :n_loc].astype(v.dtype)
    p_ctx = p[..., n_loc:n_loc + L].astype(v.dtype)
    o = (jnp.einsum('bnkgqj,bnjkd->bnqkgd', p_loc, vb)
         + jnp.einsum('bnkgqj,bjkd->bnqkgd', p_ctx, v_ctx))
    return o.reshape(B, S, ATTN_WIDTH)


def context_attention(q, k, v, sink):
    B, L = q.shape[0], q.shape[1]
    qg = q.reshape(B, L, N_KV_HEADS, Q_PER_KV, HEAD_DIM)
    s = jnp.einsum('bqkgd,bjkd->bkgqj', qg, k).astype(jnp.float32) * (HEAD_DIM ** -0.5)
    s_sink = jnp.broadcast_to(sink.astype(jnp.float32).reshape(N_KV_HEADS, Q_PER_KV, 1, 1),
                              s.shape[:-1] + (1,))
    p = jax.nn.softmax(jnp.concatenate([s, s_sink], axis=-1), axis=-1)
    o = jnp.einsum('bkgqj,bjkd->bqkgd', p[..., :L].astype(v.dtype), v)
    return o.reshape(B, L, ATTN_WIDTH)


def spatial_gating(u, v, ln_g, ln_b, w_s, b_s):
    B, N = u.shape[0], u.shape[1]
    nc = N // SGU_CHUNK
    u = jax.nn.gelu(u)
    v = layer_norm(jax.nn.gelu(v), ln_g, ln_b)
    vc = v.reshape(B, nc, SGU_CHUNK, SGU_HEADS, SGU_HEAD_DIM)
    mixed = jnp.einsum('hij,bnjhc->bnihc', w_s, vc) + b_s.T[None, None, :, :, None]
    return u * mixed.reshape(B, N, SGU_WIDTH)


def merge_heads(attn_o, sgu_o, g_attn_out, g_sgu_out, w_out, b_out):
    o = jnp.concatenate([rms_norm(attn_o, g_attn_out), rms_norm(sgu_o, g_sgu_out)], axis=-1)
    return o @ w_out + b_out


def moe_ffn(h, w_router, b_router, w_gate_up, b_gate_up, w_down, b_down):
    shape = h.shape
    xt = h.reshape(-1, D_MODEL)
    T = xt.shape[0]
    logits = (xt @ w_router + b_router).astype(jnp.float32)
    top_logits, top_idx = lax.top_k(logits, TOP_K)
    gates = jax.nn.softmax(top_logits, axis=-1)
    M = T * TOP_K
    expert_flat = top_idx.reshape(-1)
    order = jnp.argsort(expert_flat)
    e_sorted = expert_flat[order]
    tok_sorted = order // TOP_K
    gate_sorted = gates.reshape(-1)[order]
    counts = jnp.zeros((N_EXPERTS,), jnp.int32).at[expert_flat].add(1)
    start = jnp.cumsum(counts) - counts
    padded = (counts + MOE_BLOCK - 1) // MOE_BLOCK * MOE_BLOCK
    pad_end = jnp.cumsum(padded)
    pad_start = pad_end - padded
    dest = pad_start[e_sorted] + jnp.arange(M) - start[e_sorted]
    n_blocks = -(-M // MOE_BLOCK) + N_EXPERTS
    P = n_blocks * MOE_BLOCK
    row_tok = jnp.full((P,), T, jnp.int32).at[dest].set(tok_sorted.astype(jnp.int32))
    row_gate = jnp.zeros((P,), jnp.float32).at[dest].set(gate_sorted)
    block_start = jnp.arange(n_blocks) * MOE_BLOCK
    block_expert = jnp.minimum(jnp.sum(pad_end[None, :] <= block_start[:, None], axis=1), N_EXPERTS - 1)
    x_pad = jnp.concatenate([xt, jnp.zeros((1, D_MODEL), xt.dtype)], axis=0)

    def expert_block(args):
        rows, e = args
        xb = x_pad[rows]
        gu = xb @ w_gate_up[e] + b_gate_up[e]
        gate, up = jnp.split(gu, 2, axis=-1)
        gate = jnp.minimum(gate, SWIGLU_LIMIT)
        up = jnp.clip(up, -SWIGLU_LIMIT, SWIGLU_LIMIT)
        act = (up + 1) * gate * jax.nn.sigmoid(SWIGLU_ALPHA * gate)
        return act @ w_down[e] + b_down[e]

    y_rows = lax.map(expert_block, (row_tok.reshape(n_blocks, MOE_BLOCK), block_expert))
    y_rows = y_rows.reshape(P, D_MODEL) * row_gate[:, None].astype(y_rows.dtype)
    y = jax.ops.segment_sum(y_rows, row_tok, num_segments=T + 1)[:T]
    return y.reshape(shape)


def setup_inputs(seed: int = 0) -> dict:
    key = jax.random.key(seed)
    ks = jax.random.split(key, 27)
    f32 = jnp.float32

    def nrm(k, shape, scale):
        return jax.random.normal(k, shape, f32) * scale

    def gain(k, shape):
        return 1.0 + 0.02 * jax.random.normal(k, shape, f32)

    L = DEPTH
    return {
        "x": nrm(ks[0], (BATCH, SEQ, D_MODEL), 1.0),
        "c": nrm(ks[1], (BATCH, D_MODEL), 1.0),
        "ctx": nrm(ks[2], (BATCH, CTX_LEN, D_MODEL), 1.0),
        "c_ctx": nrm(ks[3], (D_MODEL,), 1.0),
        "w_ada": nrm(ks[4], (L, D_MODEL, N_MOD * D_MODEL), 0.5 * D_MODEL ** -0.5),
        "b_ada": nrm(ks[5], (L, N_MOD * D_MODEL), 0.02),
        "g_pre_mix": gain(ks[6], (L, D_MODEL)),
        "g_post_mix": gain(ks[7], (L, D_MODEL)),
        "g_pre_ffn": gain(ks[8], (L, D_MODEL)),
        "g_post_ffn": gain(ks[9], (L, D_MODEL)),
        "w_in": nrm(ks[10], (L, D_MODEL, IN_WIDTH), D_MODEL ** -0.5),
        "b_in": nrm(ks[11], (L, IN_WIDTH), 0.02),
        "attn_sink": nrm(ks[12], (L, N_HEADS), 0.5),
        "sgu_ln_g": gain(ks[13], (L, SGU_WIDTH)),
        "sgu_ln_b": nrm(ks[14], (L, SGU_WIDTH), 0.02),
        "sgu_w": nrm(ks[15], (L, SGU_HEADS, SGU_CHUNK, SGU_CHUNK), SGU_CHUNK ** -0.5),
        "sgu_b": gain(ks[16], (L, SGU_HEADS, SGU_CHUNK)),
        "g_attn_out": gain(ks[17], (L, ATTN_WIDTH)),
        "g_sgu_out": gain(ks[18], (L, SGU_WIDTH)),
        "w_out": nrm(ks[19], (L, MIX_WIDTH, D_MODEL), MIX_WIDTH ** -0.5),
        "b_out": nrm(ks[20], (L, D_MODEL), 0.02),
        "w_router": nrm(ks[21], (L, D_MODEL, N_EXPERTS), D_MODEL ** -0.5),
        "b_router": nrm(ks[22], (L, N_EXPERTS), 0.01),
        "w_gate_up": nrm(ks[23], (L, N_EXPERTS, D_MODEL, 2 * D_FF_EXPERT), D_MODEL ** -0.5),
        "b_gate_up": nrm(ks[24], (L, N_EXPERTS, 2 * D_FF_EXPERT), 0.02),
        "w_down": nrm(ks[25], (L, N_EXPERTS, D_FF_EXPERT, D_MODEL), D_FF_EXPERT ** -0.5),
        "b_down": nrm(ks[26], (L, N_EXPERTS, D_MODEL), 0.02),
    }


def reference(x, c, ctx, c_ctx, w_ada, b_ada, g_pre_mix, g_post_mix, g_pre_ffn, g_post_ffn,
              w_in, b_in, attn_sink, sgu_ln_g, sgu_ln_b, sgu_w, sgu_b, g_attn_out, g_sgu_out,
              w_out, b_out, w_router, b_router, w_gate_up, b_gate_up, w_down, b_down):
    B, S = x.shape[0], x.shape[1]
    Lc = ctx.shape[1]
    rope = axial_rope_tables(S, x.dtype)
    for l in range(DEPTH):
        mod = jax.nn.silu(c) @ w_ada[l] + b_ada[l]
        sh1, sc1, gt1, sh2, sc2, gt2 = jnp.split(mod[:, None, :], N_MOD, axis=-1)
        mod_ctx = jax.nn.silu(c_ctx) @ w_ada[l] + b_ada[l]
        csh1, csc1, cgt1, csh2, csc2, cgt2 = jnp.split(mod_ctx, N_MOD)

        h = modulate(rms_norm(x, g_pre_mix[l]), sh1, sc1)
        q, k, v, su, sv = split_projection(h @ w_in[l] + b_in[l])
        q = apply_axial_rope(q.reshape(B, S, N_HEADS, HEAD_DIM), rope)
        k = apply_axial_rope(k.reshape(B, S, N_KV_HEADS, HEAD_DIM), rope)
        v = v.reshape(B, S, N_KV_HEADS, HEAD_DIM)

        hc = modulate(rms_norm(ctx, g_pre_mix[l]), csh1, csc1)
        if l < DEPTH - 1:
            qc, kc, vc, suc, svc = split_projection(hc @ w_in[l] + b_in[l])
        else:
            kc, vc = jnp.split(hc @ w_in[l][:, KV_START:KV_END] + b_in[l][KV_START:KV_END], 2, axis=-1)
        kc = kc.reshape(B, Lc, N_KV_HEADS, HEAD_DIM)
        vc = vc.reshape(B, Lc, N_KV_HEADS, HEAD_DIM)

        attn_o = window_attention(q, k, v, kc, vc, attn_sink[l])
        sgu_o = spatial_gating(su, sv, sgu_ln_g[l], sgu_ln_b[l], sgu_w[l], sgu_b[l])
        mix = merge_heads(attn_o, sgu_o, g_attn_out[l], g_sgu_out[l], w_out[l], b_out[l])
        x_mid = x + gt1 * rms_norm(mix, g_post_mix[l])

        h2 = modulate(rms_norm(x_mid, g_pre_ffn[l]), sh2, sc2)
        ffn = moe_ffn(h2, w_router[l], b_router[l], w_gate_up[l], b_gate_up[l], w_down[l], b_down[l])
        x_new = x_mid + gt2 * rms_norm(ffn, g_post_ffn[l])

        if l < DEPTH - 1:
            qc = qc.reshape(B, Lc, N_HEADS, HEAD_DIM)
            attn_c = context_attention(qc, kc, vc, attn_sink[l])
            sgu_c = spatial_gating(suc, svc, sgu_ln_g[l], sgu_ln_b[l], sgu_w[l], sgu_b[l])
            mix_c = merge_heads(attn_c, sgu_c, g_attn_out[l], g_sgu_out[l], w_out[l], b_out[l])
            ctx_mid = ctx + cgt1 * rms_norm(mix_c, g_post_mix[l])
            hc2 = modulate(rms_norm(ctx_mid, g_pre_ffn[l]), csh2, csc2)
            ffn_c = moe_ffn(hc2, w_router[l], b_router[l], w_gate_up[l], b_gate_up[l], w_down[l], b_down[l])
            ctx = ctx_mid + cgt2 * rms_norm(ffn_c, g_post_ffn[l])
        x = x_new
    return x
```

```python
import functools

import jax
import jax.numpy as jnp
from jax import lax
from jax.experimental import pallas as pl
from jax.experimental.pallas import tpu as pltpu

D_MODEL = 1024
BATCH = 8
SEQ = 4096
TOKENS = BATCH * SEQ
GRID_W = 64
CTX_LEN = 256
N_HEADS = 8
N_KV_HEADS = 2
HEAD_DIM = 64
ATTN_WIDTH = N_HEADS * HEAD_DIM
KV_WIDTH = N_KV_HEADS * HEAD_DIM
WINDOW = 128
ATTN_BLOCK = 128
SGU_HEADS = 8
SGU_HEAD_DIM = 64
SGU_WIDTH = SGU_HEADS * SGU_HEAD_DIM
SGU_CHUNK = 128
KV_START = ATTN_WIDTH
KV_END = ATTN_WIDTH + 2 * KV_WIDTH
IN_WIDTH = KV_END + 2 * SGU_WIDTH
N_EXPERTS = 32
TOP_K = 4
D_FF_EXPERT = 1024
SWIGLU_LIMIT = 7.0
SWIGLU_ALPHA = 1.702
ROPE_THETA = 10000.0
EPS = 1e-6
N_MOD = 6

LANES = 128
NEG_BIG = -1e30
VMEM_LIMIT = 48 * 1024 * 1024

ROW_TILE = 512
MOVE_TILE = 256
EXPERT_BLOCK = 512
N_EXPERT_BLOCKS = TOKENS * TOP_K // EXPERT_BLOCK + N_EXPERTS
SORTED_ROWS = N_EXPERT_BLOCKS * EXPERT_BLOCK
HEAD_PERM = (0, 4, 1, 5, 2, 6, 3, 7)

_F32 = jnp.float32
_BF16 = jnp.bfloat16


def _rms(x, g):
    ms = jnp.mean(x * x, axis=-1, keepdims=True)
    return x * lax.rsqrt(ms + EPS) * g


def _gelu_tanh(x):
    c = 0.7978845608028654
    return x * (0.5 * (1.0 + jnp.tanh(c * (x + 0.044715 * (x * x * x)))))


def _ada_kernel(c_ref, w_ref, b_ref, o_ref):
    c = c_ref[...]
    a = c / (1.0 + jnp.exp(-c))
    o_ref[...] = jnp.dot(a, w_ref[...], preferred_element_type=_F32,
                         precision=lax.Precision.HIGHEST) + b_ref[...]


def _ada_call(cc, w_ada, b_ada):
    n = N_MOD * D_MODEL
    tn = 512
    return pl.pallas_call(
        _ada_kernel,
        out_shape=jax.ShapeDtypeStruct((16, n), _F32),
        grid=(n // tn,),
        in_specs=[pl.BlockSpec((16, D_MODEL), lambda j: (0, 0)),
                  pl.BlockSpec((D_MODEL, tn), lambda j: (0, j)),
                  pl.BlockSpec((1, tn), lambda j: (0, j))],
        out_specs=pl.BlockSpec((16, tn), lambda j: (0, j)),
        compiler_params=pltpu.CompilerParams(dimension_semantics=("arbitrary",),
                                             vmem_limit_bytes=VMEM_LIMIT),
        name="ada",
    )(cc, w_ada, b_ada)


def _rope(x, cos, sin_signed, first_half):
    nxt = pltpu.roll(x, LANES - 16, 1)
    prv = pltpu.roll(x, 16, 1)
    return x * cos + jnp.where(first_half, nxt, prv) * sin_signed


def _inproj_kernel(x_ref, mod_ref, g_ref, w_ref, b_ref, cos_ref, sin_ref,
                   q_ref, k_ref, v_ref, su_ref, sv_ref):
    m = mod_ref[0]
    h = _rms(x_ref[...], g_ref[...]) * (1.0 + m[1:2]) + m[0:1]
    hb = h.astype(_BF16)
    cos = cos_ref[...]
    sin = sin_ref[...]
    lane = lax.broadcasted_iota(jnp.int32, cos.shape, 1)
    first_half = (lane & 31) < 16

    q = jnp.dot(hb, w_ref[:, 0:ATTN_WIDTH], preferred_element_type=_F32) + b_ref[:, 0:ATTN_WIDTH]
    for j in range(ATTN_WIDTH // LANES):
        qj = _rope(q[:, j * LANES:(j + 1) * LANES], cos, sin, first_half)
        q_ref[:, j * LANES:(j + 1) * LANES] = (qj * (HEAD_DIM ** -0.5)).astype(_BF16)
    kv = jnp.dot(hb, w_ref[:, KV_START:KV_END], preferred_element_type=_F32) + b_ref[:, KV_START:KV_END]
    k_ref[...] = _rope(kv[:, 0:KV_WIDTH], cos, sin, first_half).astype(_BF16)
    v_ref[...] = kv[:, KV_WIDTH:].astype(_BF16)
    u0 = KV_END
    u1 = KV_END + SGU_WIDTH
    su_ref[...] = jnp.dot(hb, w_ref[:, u0:u1], preferred_element_type=_F32) + b_ref[:, u0:u1]
    sv_ref[...] = jnp.dot(hb, w_ref[:, u1:IN_WIDTH], preferred_element_type=_F32) + b_ref[:, u1:IN_WIDTH]


def _inproj_call(x2, mod3, g_pre, w_in_b, b_in2, cos_t, sin_t):
    tm = ROW_TILE
    tiles_per_batch = SEQ // tm
    row = lambda i: (i, 0)
    const = lambda i: (0, 0)
    return pl.pallas_call(
        _inproj_kernel,
        out_shape=(jax.ShapeDtypeStruct((TOKENS, ATTN_WIDTH), _BF16),
                   jax.ShapeDtypeStruct((TOKENS, KV_WIDTH), _BF16),
                   jax.ShapeDtypeStruct((TOKENS, KV_WIDTH), _BF16),
                   jax.ShapeDtypeStruct((TOKENS, SGU_WIDTH), _F32),
                   jax.ShapeDtypeStruct((TOKENS, SGU_WIDTH), _F32)),
        grid=(TOKENS // tm,),
        in_specs=[pl.BlockSpec((tm, D_MODEL), row),
                  pl.BlockSpec((1, N_MOD, D_MODEL), lambda i: (i // tiles_per_batch, 0, 0)),
                  pl.BlockSpec((1, D_MODEL), const),
                  pl.BlockSpec((D_MODEL, IN_WIDTH), const),
                  pl.BlockSpec((1, IN_WIDTH), const),
                  pl.BlockSpec((tm, LANES), lambda i: (i % tiles_per_batch, 0)),
                  pl.BlockSpec((tm, LANES), lambda i: (i % tiles_per_batch, 0))],
        out_specs=(pl.BlockSpec((tm, ATTN_WIDTH), row),
                   pl.BlockSpec((tm, KV_WIDTH), row),
                   pl.BlockSpec((tm, KV_WIDTH), row),
                   pl.BlockSpec((tm, SGU_WIDTH), row),
                   pl.BlockSpec((tm, SGU_WIDTH), row)),
        compiler_params=pltpu.CompilerParams(dimension_semantics=("arbitrary",),
                                             vmem_limit_bytes=VMEM_LIMIT),
        name="inproj",
    )(x2, mod3, g_pre, w_in_b, b_in2, cos_t, sin_t)


def _ctxproj_kernel(x_ref, mod_ref, g_ref, w_ref, b_ref, k_ref, v_ref):
    m = mod_ref[0]
    h = _rms(x_ref[...], g_ref[...]) * (1.0 + m[1:2]) + m[0:1]
    kv = jnp.dot(h.astype(_BF16), w_ref[...], preferred_element_type=_F32) + b_ref[...]
    k_ref[...] = kv[:, 0:KV_WIDTH].astype(_BF16)
    v_ref[...] = kv[:, KV_WIDTH:].astype(_BF16)


def _ctxproj_call(ctx2, mod3, g_pre, w_kv_b, b_kv2):
    tm = ROW_TILE
    rows = ctx2.shape[0]
    row = lambda i: (i, 0)
    const = lambda i: (0, 0)
    return pl.pallas_call(
        _ctxproj_kernel,
        out_shape=(jax.ShapeDtypeStruct((rows, KV_WIDTH), _BF16),
                   jax.ShapeDtypeStruct((rows, KV_WIDTH), _BF16)),
        grid=(rows // tm,),
        in_specs=[pl.BlockSpec((tm, D_MODEL), row),
                  pl.BlockSpec((1, N_MOD, D_MODEL), lambda i: (BATCH, 0, 0)),
                  pl.BlockSpec((1, D_MODEL), const),
                  pl.BlockSpec((D_MODEL, 2 * KV_WIDTH), const),
                  pl.BlockSpec((1, 2 * KV_WIDTH), const)],
        out_specs=(pl.BlockSpec((tm, KV_WIDTH), row),
                   pl.BlockSpec((tm, KV_WIDTH), row)),
        compiler_params=pltpu.CompilerParams(dimension_semantics=("arbitrary",),
                                             vmem_limit_bytes=VMEM_LIMIT),
        name="ctxproj",
    )(ctx2, mod3, g_pre, w_kv_b, b_kv2)


def _attn_kernel(sink_ref, q_ref, kp_ref, ko_ref, kn_ref, vp_ref, vo_ref, vn_ref,
                 kc_ref, vc_ref, o_ref):
    n = pl.program_id(1)
    nblk = pl.num_programs(1)
    keys = jnp.concatenate([kp_ref[...], ko_ref[...], kn_ref[...], kc_ref[...]], axis=0)
    vals = jnp.concatenate([vp_ref[...], vo_ref[...], vn_ref[...], vc_ref[...]], axis=0)
    nk = 3 * ATTN_BLOCK + CTX_LEN
    rows2 = 2 * ATTN_BLOCK
    row = lax.broadcasted_iota(jnp.int32, (rows2, nk), 0)
    col = lax.broadcasted_iota(jnp.int32, (rows2, nk), 1)
    qi = row & (ATTN_BLOCK - 1)
    off = col - ATTN_BLOCK - qi
    in_band = (off <= WINDOW) & (off >= -WINDOW)
    in_seq = ((col >= ATTN_BLOCK) | (n > 0)) & ((col < 2 * ATTN_BLOCK) | (n < nblk - 1))
    valid = (col >= 3 * ATTN_BLOCK) | (in_band & in_seq)
    row1 = lax.broadcasted_iota(jnp.int32, (rows2, 1), 0)
    lane_q = lax.broadcasted_iota(jnp.int32, (ATTN_BLOCK, LANES), 1)
    for j in range(ATTN_WIDTH // LANES):
        qg = q_ref[:, j * LANES:(j + 1) * LANES]
        zero = jnp.zeros_like(qg)
        q2 = jnp.concatenate([jnp.where(lane_q < HEAD_DIM, qg, zero),
                              jnp.where(lane_q >= HEAD_DIM, qg, zero)], axis=0)
        s = lax.dot_general(q2, keys, (((1,), (1,)), ((), ())), preferred_element_type=_F32)
        s = jnp.where(valid, s, NEG_BIG)
        sk = jnp.where(row1 < ATTN_BLOCK, sink_ref[2 * j], sink_ref[2 * j + 1])
        m = jnp.maximum(jnp.max(s, axis=-1, keepdims=True), sk)
        p = jnp.exp(s - m)
        denom = jnp.sum(p, axis=-1, keepdims=True) + jnp.exp(sk - m)
        o2 = jnp.dot(p.astype(_BF16), vals, preferred_element_type=_F32) / denom
        og = jnp.where(lane_q < HEAD_DIM, o2[0:ATTN_BLOCK], o2[ATTN_BLOCK:])
        o_ref[:, j * LANES:(j + 1) * LANES] = og.astype(_BF16)


def _attn_call(sink_p, q, k, v, kc, vc):
    nblk = SEQ // ATTN_BLOCK
    own = lambda b, n: (b * nblk + n, 0)
    prev = lambda b, n: (b * nblk + jnp.maximum(n - 1, 0), 0)
    nxt = lambda b, n: (b * nblk + jnp.minimum(n + 1, nblk - 1), 0)
    ctx = lambda b, n: (b, 0)
    kvb = (ATTN_BLOCK, KV_WIDTH)
    return pl.pallas_call(
        _attn_kernel,
        out_shape=jax.ShapeDtypeStruct((TOKENS, ATTN_WIDTH), _BF16),
        grid=(BATCH, nblk),
        in_specs=[pl.BlockSpec(memory_space=pltpu.SMEM),
                  pl.BlockSpec((ATTN_BLOCK, ATTN_WIDTH), own),
                  pl.BlockSpec(kvb, prev), pl.BlockSpec(kvb, own), pl.BlockSpec(kvb, nxt),
                  pl.BlockSpec(kvb, prev), pl.BlockSpec(kvb, own), pl.BlockSpec(kvb, nxt),
                  pl.BlockSpec((CTX_LEN, KV_WIDTH), ctx),
                  pl.BlockSpec((CTX_LEN, KV_WIDTH), ctx)],
        out_specs=pl.BlockSpec((ATTN_BLOCK, ATTN_WIDTH), own),
        compiler_params=pltpu.CompilerParams(dimension_semantics=("arbitrary", "arbitrary"),
                                             vmem_limit_bytes=VMEM_LIMIT),
        name="attn",
    )(sink_p, q, k, k, k, v, v, v, kc, vc)


def _post_kernel(x_ref, ao_ref, su_ref, sv_ref, mod_ref, lng_ref, lnb_ref, ws_ref, bs_ref,
                 ga_ref, gs_ref, woa_ref, wos_ref, bo_ref, gpost_ref, gpre_ref, wr_ref, br_ref,
                 tri_ref,
                 xmid_ref, h2_ref, idx_ref, gate_ref, rank_ref, cnt_ref,
                 mixed_ref, carry_ref):
    tm = x_ref.shape[0]
    m = mod_ref[0]

    @pl.when(pl.program_id(0) == 0)
    def _():
        carry_ref[...] = jnp.zeros_like(carry_ref)

    gv = _gelu_tanh(sv_ref[...])
    mu = jnp.mean(gv, axis=-1, keepdims=True)
    gc = gv - mu
    var = jnp.mean(gc * gc, axis=-1, keepdims=True)
    vb = (gc * lax.rsqrt(var + EPS) * lng_ref[...] + lnb_ref[...]).astype(_BF16)
    lane = lax.broadcasted_iota(jnp.int32, (SGU_CHUNK, LANES), 1)
    for c in range(tm // SGU_CHUNK):
        r0 = c * SGU_CHUNK
        for p in range(SGU_WIDTH // LANES):
            l0 = p * LANES
            r = jnp.dot(ws_ref[p], vb[r0:r0 + SGU_CHUNK, l0:l0 + LANES], preferred_element_type=_F32)
            mixed = jnp.where(lane < SGU_HEAD_DIM, r[0:SGU_CHUNK], r[SGU_CHUNK:])
            mixed_ref[r0:r0 + SGU_CHUNK, l0:l0 + LANES] = mixed + bs_ref[:, l0:l0 + LANES]
    sgu_o = _gelu_tanh(su_ref[...]) * mixed_ref[...]

    oa = _rms(ao_ref[...].astype(_F32), ga_ref[...]).astype(_BF16)
    os_ = _rms(sgu_o, gs_ref[...]).astype(_BF16)
    mix = (jnp.dot(oa, woa_ref[...], preferred_element_type=_F32)
           + jnp.dot(os_, wos_ref[...], preferred_element_type=_F32) + bo_ref[...])
    x_mid = x_ref[...] + m[2:3] * _rms(mix, gpost_ref[...])
    xmid_ref[...] = x_mid
    h2 = _rms(x_mid, gpre_ref[...]) * (1.0 + m[4:5]) + m[3:4]
    h2_ref[...] = h2

    lg = jnp.dot(h2, wr_ref[...], preferred_element_type=_F32,
                 precision=lax.Precision.HIGHEST) + br_ref[...]
    lane_r = lax.broadcasted_iota(jnp.int32, lg.shape, 1)
    lane_f = lane_r.astype(_F32)
    tops, hots = [], []
    for _k in range(TOP_K):
        mx = jnp.max(lg, axis=-1, keepdims=True)
        pick = jnp.min(jnp.where(lg == mx, lane_f, float(LANES)), axis=-1, keepdims=True)
        hot = lane_f == pick
        tops.append((mx, pick))
        hots.append(hot)
        lg = jnp.where(hot, 2.0 * NEG_BIG, lg)
    es = [jnp.exp(t[0] - tops[0][0]) for t in tops]
    esum = es[0] + es[1] + es[2] + es[3]
    multi = jnp.zeros(lg.shape, _F32)
    for hot in hots:
        multi = multi + jnp.where(hot, 1.0, 0.0)
    cum = jnp.dot(tri_ref[...], multi.astype(_BF16), preferred_element_type=_F32) + carry_ref[...]
    idx_o = jnp.zeros(lg.shape, _F32)
    gate_o = jnp.zeros(lg.shape, _F32)
    rank_o = jnp.zeros(lg.shape, _F32)
    for kk in range(TOP_K):
        sel = lane_r == kk
        rk = jnp.sum(jnp.where(hots[kk], cum, 0.0), axis=-1, keepdims=True)
        idx_o = jnp.where(sel, tops[kk][1], idx_o)
        gate_o = jnp.where(sel, es[kk] / esum, gate_o)
        rank_o = jnp.where(sel, rk, rank_o)
    idx_ref[...] = idx_o.astype(jnp.int32)
    gate_ref[...] = gate_o
    rank_ref[...] = rank_o.astype(jnp.int32)
    carry_ref[...] += jnp.sum(multi, axis=0, keepdims=True)
    cnt_ref[...] = carry_ref[...].astype(jnp.int32)


def _post_call(x2, ao, su, sv, mod3, lng, lnb, ws2, bsf, ga, gs, woa, wos, bo, gpost, gpre,
               wr, br, tri):
    tm = ROW_TILE
    tiles_per_batch = SEQ // tm
    row = lambda i: (i, 0)
    const = lambda i: (0, 0)
    const3 = lambda i: (0, 0, 0)
    return pl.pallas_call(
        _post_kernel,
        out_shape=(jax.ShapeDtypeStruct((TOKENS, D_MODEL), _F32),
                   jax.ShapeDtypeStruct((TOKENS, D_MODEL), _F32),
                   jax.ShapeDtypeStruct((TOKENS, LANES), jnp.int32),
                   jax.ShapeDtypeStruct((TOKENS, LANES), _F32),
                   jax.ShapeDtypeStruct((TOKENS, LANES), jnp.int32),
                   jax.ShapeDtypeStruct((1, LANES), jnp.int32)),
        grid=(TOKENS // tm,),
        in_specs=[pl.BlockSpec((tm, D_MODEL), row),
                  pl.BlockSpec((tm, ATTN_WIDTH), row),
                  pl.BlockSpec((tm, SGU_WIDTH), row),
                  pl.BlockSpec((tm, SGU_WIDTH), row),
                  pl.BlockSpec((1, N_MOD, D_MODEL), lambda i: (i // tiles_per_batch, 0, 0)),
                  pl.BlockSpec((1, SGU_WIDTH), const),
                  pl.BlockSpec((1, SGU_WIDTH), const),
                  pl.BlockSpec((SGU_WIDTH // LANES, 2 * SGU_CHUNK, SGU_CHUNK), const3),
                  pl.BlockSpec((SGU_CHUNK, SGU_WIDTH), const),
                  pl.BlockSpec((1, ATTN_WIDTH), const),
                  pl.BlockSpec((1, SGU_WIDTH), const),
                  pl.BlockSpec((ATTN_WIDTH, D_MODEL), const),
                  pl.BlockSpec((SGU_WIDTH, D_MODEL), const),
                  pl.BlockSpec((1, D_MODEL), const),
                  pl.BlockSpec((1, D_MODEL), const),
                  pl.BlockSpec((1, D_MODEL), const),
                  pl.BlockSpec((D_MODEL, LANES), const),
                  pl.BlockSpec((1, LANES), const),
                  pl.BlockSpec((tm, tm), const)],
        out_specs=(pl.BlockSpec((tm, D_MODEL), row),
                   pl.BlockSpec((tm, D_MODEL), row),
                   pl.BlockSpec((tm, LANES), row),
                   pl.BlockSpec((tm, LANES), row),
                   pl.BlockSpec((tm, LANES), row),
                   pl.BlockSpec((1, LANES), const)),
        scratch_shapes=[pltpu.VMEM((tm, SGU_WIDTH), _F32),
                        pltpu.VMEM((1, LANES), _F32)],
        compiler_params=pltpu.CompilerParams(dimension_semantics=("arbitrary",),
                                             vmem_limit_bytes=VMEM_LIMIT),
        name="post",
    )(x2, ao, su, sv, mod3, lng, lnb, ws2, bsf, ga, gs, woa, wos, bo, gpost, gpre, wr, br, tri)


def _dispatch_kernel(dest_ref, h2_ref, init_ref, xs_ref, sem):
    del init_ref
    tm = h2_ref.shape[0]

    def issue(r, carry):
        for kk in range(TOP_K):
            d = dest_ref[r * TOP_K + kk]
            pltpu.make_async_copy(h2_ref.at[pl.ds(r, 1)], xs_ref.at[pl.ds(d, 1)], sem).start()
        return carry

    lax.fori_loop(0, tm, issue, 0)
    for kk in range(TOP_K):
        pltpu.make_async_copy(h2_ref, xs_ref.at[pl.ds(0, tm)], sem).wait()


def _dispatch_call(dest_flat, h2, xs_init):
    tm = MOVE_TILE
    return pl.pallas_call(
        _dispatch_kernel,
        out_shape=jax.ShapeDtypeStruct((SORTED_ROWS, D_MODEL), _F32),
        grid=(TOKENS // tm,),
        in_specs=[pl.BlockSpec((tm * TOP_K,), lambda i: (i,), memory_space=pltpu.SMEM),
                  pl.BlockSpec((tm, D_MODEL), lambda i: (i, 0)),
                  pl.BlockSpec(memory_space=pl.ANY)],
        out_specs=pl.BlockSpec(memory_space=pl.ANY),
        scratch_shapes=[pltpu.SemaphoreType.DMA],
        input_output_aliases={2: 0},
        compiler_params=pltpu.CompilerParams(dimension_semantics=("arbitrary",),
                                             vmem_limit_bytes=VMEM_LIMIT),
        name="dispatch",
    )(dest_flat, h2, xs_init)


def _expert_kernel(be_ref, nu_ref, xs_ref, wgu_ref, bgu_ref, wd_ref, bd_ref, y_ref):
    del be_ref

    @pl.when(pl.program_id(0) < nu_ref[0])
    def _():
        x = xs_ref[...].astype(_BF16)
        gu = jnp.dot(x, wgu_ref[0], preferred_element_type=_F32) + bgu_ref[0]
        gate = jnp.minimum(gu[:, 0:D_FF_EXPERT], SWIGLU_LIMIT)
        up = jnp.clip(gu[:, D_FF_EXPERT:], -SWIGLU_LIMIT, SWIGLU_LIMIT)
        act = (up + 1.0) * gate * (1.0 / (1.0 + jnp.exp(-SWIGLU_ALPHA * gate)))
        y_ref[...] = jnp.dot(act.astype(_BF16), wd_ref[0], preferred_element_type=_F32) + bd_ref[0]

    @pl.when(pl.program_id(0) >= nu_ref[0])
    def _():
        y_ref[...] = jnp.zeros_like(y_ref)


def _expert_call(block_expert, n_used, xs, wgu_b, bgu3, wd_b, bd3):
    tb = EXPERT_BLOCK
    live = lambda b, be, nu: (jnp.minimum(b, nu[0] - 1), 0)
    wsel = lambda b, be, nu: (be[b], 0, 0)
    grid_spec = pltpu.PrefetchScalarGridSpec(
        num_scalar_prefetch=2,
        grid=(N_EXPERT_BLOCKS,),
        in_specs=[pl.BlockSpec((tb, D_MODEL), live),
                  pl.BlockSpec((1, D_MODEL, 2 * D_FF_EXPERT), wsel),
                  pl.BlockSpec((1, 1, 2 * D_FF_EXPERT), wsel),
                  pl.BlockSpec((1, D_FF_EXPERT, D_MODEL), wsel),
                  pl.BlockSpec((1, 1, D_MODEL), wsel)],
        out_specs=pl.BlockSpec((tb, D_MODEL), lambda b, be, nu: (b, 0)))
    return pl.pallas_call(
        _expert_kernel,
        out_shape=jax.ShapeDtypeStruct((SORTED_ROWS, D_MODEL), _F32),
        grid_spec=grid_spec,
        compiler_params=pltpu.CompilerParams(dimension_semantics=("arbitrary",),
                                             vmem_limit_bytes=VMEM_LIMIT),
        name="expert",
    )(block_expert, n_used, xs, wgu_b, bgu3, wd_b, bd3)


def _combine_kernel(dest_ref, gate_ref, xmid_ref, mod_ref, g_ref, y_ref, o_ref, buf_ref, sem):
    tm = xmid_ref.shape[0]
    m = mod_ref[0]

    def issue(r, carry):
        for kk in range(TOP_K):
            d = dest_ref[r * TOP_K + kk]
            pltpu.make_async_copy(y_ref.at[pl.ds(d, 1)], buf_ref.at[kk, pl.ds(r, 1)], sem).start()
        return carry

    lax.fori_loop(0, tm, issue, 0)
    for kk in range(TOP_K):
        pltpu.make_async_copy(y_ref.at[pl.ds(0, tm)], buf_ref.at[kk], sem).wait()
    g = gate_ref[...]
    ffn = buf_ref[0] * g[:, 0:1]
    for kk in range(1, TOP_K):
        ffn = ffn + buf_ref[kk] * g[:, kk:kk + 1]
    o_ref[...] = xmid_ref[...] + m[5:6] * _rms(ffn, g_ref[...])


def _combine_call(dest_flat, gates, x_mid, mod3, gpost_ffn, y_rows):
    tm = MOVE_TILE
    tiles_per_batch = SEQ // tm
    return pl.pallas_call(
        _combine_kernel,
        out_shape=jax.ShapeDtypeStruct((TOKENS, D_MODEL), _F32),
        grid=(TOKENS // tm,),
        in_specs=[pl.BlockSpec((tm * TOP_K,), lambda i: (i,), memory_space=pltpu.SMEM),
                  pl.BlockSpec((tm, LANES), lambda i: (i, 0)),
                  pl.BlockSpec((tm, D_MODEL), lambda i: (i, 0)),
                  pl.BlockSpec((1, N_MOD, D_MODEL), lambda i: (i // tiles_per_batch, 0, 0)),
                  pl.BlockSpec((1, D_MODEL), lambda i: (0, 0)),
                  pl.BlockSpec(memory_space=pl.ANY)],
        out_specs=pl.BlockSpec((tm, D_MODEL), lambda i: (i, 0)),
        scratch_shapes=[pltpu.VMEM((TOP_K, tm, D_MODEL), _F32),
                        pltpu.SemaphoreType.DMA],
        compiler_params=pltpu.CompilerParams(dimension_semantics=("arbitrary",),
                                             vmem_limit_bytes=VMEM_LIMIT),
        name="combine",
    )(dest_flat, gates, x_mid, mod3, gpost_ffn, y_rows)


def _rope_tables():
    pos = jnp.arange(SEQ, dtype=_F32)
    pos_row = jnp.floor(pos / GRID_W)
    pos_col = pos - pos_row * GRID_W
    n_freq = HEAD_DIM // 4
    inv_freq = ROPE_THETA ** (-jnp.arange(n_freq, dtype=_F32) / n_freq)
    d = jnp.arange(LANES) % HEAD_DIM
    f = inv_freq[d % n_freq]
    ang = jnp.where((d < HEAD_DIM // 2)[None, :], pos_row[:, None] * f[None, :], pos_col[:, None] * f[None, :])
    sign = jnp.where((d % (HEAD_DIM // 2)) < n_freq, -1.0, 1.0).astype(_F32)
    return jnp.cos(ang), jnp.sin(ang) * sign[None, :]


def _perm_heads(a, axis):
    shape = a.shape
    a = a.reshape(shape[:axis] + (N_HEADS, HEAD_DIM) + shape[axis + 1:])
    a = jnp.take(a, jnp.array(HEAD_PERM), axis=axis)
    return a.reshape(shape)


def kernel(x, c, ctx, c_ctx, w_ada, b_ada, g_pre_mix, g_post_mix, g_pre_ffn, g_post_ffn, w_in, b_in, attn_sink, sgu_ln_g, sgu_ln_b, sgu_w, sgu_b, g_attn_out, g_sgu_out, w_out, b_out, w_router, b_router, w_gate_up, b_gate_up, w_down, b_down):
    l = 0
    x2 = x.reshape(TOKENS, D_MODEL)
    ctx2 = ctx.reshape(BATCH * CTX_LEN, D_MODEL)

    cc = jnp.zeros((16, D_MODEL), _F32).at[:BATCH].set(c).at[BATCH].set(c_ctx)
    mod = _ada_call(cc, w_ada[l], b_ada[l].reshape(1, -1))
    mod3 = mod.reshape(16, N_MOD, D_MODEL)

    w_in_l = w_in[l]
    b_in_l = b_in[l]
    w_in_p = jnp.concatenate([_perm_heads(w_in_l[:, :ATTN_WIDTH], 1), w_in_l[:, ATTN_WIDTH:]], axis=1)
    b_in_p = jnp.concatenate([_perm_heads(b_in_l[:ATTN_WIDTH], 0), b_in_l[ATTN_WIDTH:]], axis=0)
    cos_t, sin_t = _rope_tables()
    g_pre = g_pre_mix[l].reshape(1, -1)

    q, k, v, su, sv = _inproj_call(x2, mod3, g_pre, w_in_p.astype(_BF16), b_in_p.reshape(1, -1), cos_t, sin_t)
    kc, vc = _ctxproj_call(ctx2, mod3, g_pre, w_in_l[:, KV_START:KV_END].astype(_BF16),
                           b_in_l[KV_START:KV_END].reshape(1, -1))

    sink_p = jnp.take(attn_sink[l], jnp.array(HEAD_PERM))
    ao = _attn_call(sink_p, q, k, v, kc, vc)

    ws2 = sgu_w[l].reshape(SGU_WIDTH // LANES, 2 * SGU_CHUNK, SGU_CHUNK).astype(_BF16)
    bsf = jnp.repeat(sgu_b[l].T, SGU_HEAD_DIM, axis=1)
    w_out_l = w_out[l]
    woa = _perm_heads(w_out_l[:ATTN_WIDTH], 0).astype(_BF16)
    wos = w_out_l[ATTN_WIDTH:].astype(_BF16)
    ga = _perm_heads(g_attn_out[l], 0).reshape(1, -1)
    wr = jnp.zeros((D_MODEL, LANES), _F32).at[:, :N_EXPERTS].set(w_router[l])
    br = jnp.full((1, LANES), NEG_BIG, _F32).at[0, :N_EXPERTS].set(b_router[l])
    ii = jnp.arange(ROW_TILE)
    tri = (ii[None, :] < ii[:, None]).astype(_BF16)

    x_mid, h2, idx_o, gate_o, rank_o, cnt_o = _post_call(
        x2, ao, su, sv, mod3, sgu_ln_g[l].reshape(1, -1), sgu_ln_b[l].reshape(1, -1), ws2, bsf,
        ga, g_sgu_out[l].reshape(1, -1), woa, wos, b_out[l].reshape(1, -1),
        g_post_mix[l].reshape(1, -1), g_pre_ffn[l].reshape(1, -1), wr, br, tri)

    counts = cnt_o[0, :N_EXPERTS]
    padded = (counts + EXPERT_BLOCK - 1) // EXPERT_BLOCK * EXPERT_BLOCK
    pad_end = jnp.cumsum(padded)
    pad_start = pad_end - padded
    top_idx = idx_o[:, :TOP_K]
    dest = (pad_start[top_idx] + rank_o[:, :TOP_K]).astype(jnp.int32).reshape(-1)
    block_start = jnp.arange(N_EXPERT_BLOCKS, dtype=jnp.int32) * EXPERT_BLOCK
    block_expert = jnp.minimum(jnp.sum(pad_end[None, :] <= block_start[:, None], axis=1),
                               N_EXPERTS - 1).astype(jnp.int32)
    n_used = (pad_end[-1:] // EXPERT_BLOCK).astype(jnp.int32)

    xs = _dispatch_call(dest, h2, jnp.zeros((SORTED_ROWS, D_MODEL), _F32))
    y_rows = _expert_call(block_expert, n_used, xs,
                          w_gate_up[l].astype(_BF16), b_gate_up[l].reshape(N_EXPERTS, 1, -1),
                          w_down[l].astype(_BF16), b_down[l].reshape(N_EXPERTS, 1, -1))
    out = _combine_call(dest, gate_o, x_mid, mod3, g_post_ffn[l].reshape(1, -1), y_rows)
    return out.reshape(BATCH, SEQ, D_MODEL)
```

```python
import functools

import jax
import jax.numpy as jnp
from jax import lax
from jax.experimental import pallas as pl
from jax.experimental.pallas import tpu as pltpu

D_MODEL = 1024
BATCH = 8
SEQ = 4096
TOKENS = BATCH * SEQ
GRID_W = 64
CTX_LEN = 256
N_HEADS = 8
N_KV_HEADS = 2
HEAD_DIM = 64
ATTN_WIDTH = N_HEADS * HEAD_DIM
KV_WIDTH = N_KV_HEADS * HEAD_DIM
WINDOW = 128
ATTN_BLOCK = 128
SGU_HEADS = 8
SGU_HEAD_DIM = 64
SGU_WIDTH = SGU_HEADS * SGU_HEAD_DIM
SGU_CHUNK = 128
KV_START = ATTN_WIDTH
KV_END = ATTN_WIDTH + 2 * KV_WIDTH
IN_WIDTH = KV_END + 2 * SGU_WIDTH
N_EXPERTS = 32
TOP_K = 4
D_FF_EXPERT = 1024
SWIGLU_LIMIT = 7.0
SWIGLU_ALPHA = 1.702
ROPE_THETA = 10000.0
EPS = 1e-6
N_MOD = 6

LANES = 128
NEG_BIG = -1e30
VMEM_LIMIT = 48 * 1024 * 1024

ROW_TILE = 512
MOVE_TILE = 256
EXPERT_BLOCK = 512
N_EXPERT_BLOCKS = TOKENS * TOP_K // EXPERT_BLOCK + N_EXPERTS
SORTED_ROWS = N_EXPERT_BLOCKS * EXPERT_BLOCK
HEAD_PERM = (0, 4, 1, 5, 2, 6, 3, 7)

_F32 = jnp.float32
_BF16 = jnp.bfloat16


def _rms(x, g):
    ms = jnp.mean(x * x, axis=-1, keepdims=True)
    return x * lax.rsqrt(ms + EPS) * g


def _gelu_tanh(x):
    c = 0.7978845608028654
    return x * (0.5 * (1.0 + jnp.tanh(c * (x + 0.044715 * (x * x * x)))))


def _ada_kernel(c_ref, w_ref, b_ref, o_ref):
    c = c_ref[...]
    a = c / (1.0 + jnp.exp(-c))
    o_ref[...] = jnp.dot(a, w_ref[...], preferred_element_type=_F32,
                         precision=lax.Precision.HIGHEST) + b_ref[...]


def _ada_call(cc, w_ada, b_ada):
    n = N_MOD * D_MODEL
    tn = 512
    return pl.pallas_call(
        _ada_kernel,
        out_shape=jax.ShapeDtypeStruct((16, n), _F32),
        grid=(n // tn,),
        in_specs=[pl.BlockSpec((16, D_MODEL), lambda j: (0, 0)),
                  pl.BlockSpec((D_MODEL, tn), lambda j: (0, j)),
                  pl.BlockSpec((1, tn), lambda j: (0, j))],
        out_specs=pl.BlockSpec((16, tn), lambda j: (0, j)),
        compiler_params=pltpu.CompilerParams(dimension_semantics=("arbitrary",),
                                             vmem_limit_bytes=VMEM_LIMIT),
        name="ada",
    )(cc, w_ada, b_ada)


def _rope(x, cos, sin_signed, first_half):
    nxt = pltpu.roll(x, LANES - 16, 1)
    prv = pltpu.roll(x, 16, 1)
    return x * cos + jnp.where(first_half, nxt, prv) * sin_signed


def _inproj_kernel(x_ref, mod_ref, g_ref, w_ref, b_ref, cos_ref, sin_ref,
                   q_ref, k_ref, v_ref, su_ref, sv_ref):
    m = mod_ref[0]
    h = _rms(x_ref[...], g_ref[...]) * (1.0 + m[1:2]) + m[0:1]
    hb = h.astype(_BF16)
    cos = cos_ref[...]
    sin = sin_ref[...]
    lane = lax.broadcasted_iota(jnp.int32, cos.shape, 1)
    first_half = (lane & 31) < 16

    q = jnp.dot(hb, w_ref[:, 0:ATTN_WIDTH], preferred_element_type=_F32) + b_ref[:, 0:ATTN_WIDTH]
    for j in range(ATTN_WIDTH // LANES):
        qj = _rope(q[:, j * LANES:(j + 1) * LANES], cos, sin, first_half)
        q_ref[:, j * LANES:(j + 1) * LANES] = (qj * (HEAD_DIM ** -0.5)).astype(_BF16)
    kv = jnp.dot(hb, w_ref[:, KV_START:KV_END], preferred_element_type=_F32) + b_ref[:, KV_START:KV_END]
    k_ref[...] = _rope(kv[:, 0:KV_WIDTH], cos, sin, first_half).astype(_BF16)
    v_ref[...] = kv[:, KV_WIDTH:].astype(_BF16)
    u0 = KV_END
    u1 = KV_END + SGU_WIDTH
    su_ref[...] = jnp.dot(hb, w_ref[:, u0:u1], preferred_element_type=_F32) + b_ref[:, u0:u1]
    sv_ref[...] = jnp.dot(hb, w_ref[:, u1:IN_WIDTH], preferred_element_type=_F32) + b_ref[:, u1:IN_WIDTH]


def _inproj_call(x2, mod3, g_pre, w_in_b, b_in2, cos_t, sin_t):
    tm = ROW_TILE
    tiles_per_batch = SEQ // tm
    row = lambda i: (i, 0)
    const = lambda i: (0, 0)
    return pl.pallas_call(
        _inproj_kernel,
        out_shape=(jax.ShapeDtypeStruct((TOKENS, ATTN_WIDTH), _BF16),
                   jax.ShapeDtypeStruct((TOKENS, KV_WIDTH), _BF16),
                   jax.ShapeDtypeStruct((TOKENS, KV_WIDTH), _BF16),
                   jax.ShapeDtypeStruct((TOKENS, SGU_WIDTH), _F32),
                   jax.ShapeDtypeStruct((TOKENS, SGU_WIDTH), _F32)),
        grid=(TOKENS // tm,),
        in_specs=[pl.BlockSpec((tm, D_MODEL), row),
                  pl.BlockSpec((1, N_MOD, D_MODEL), lambda i: (i // tiles_per_batch, 0, 0)),
                  pl.BlockSpec((1, D_MODEL), const),
                  pl.BlockSpec((D_MODEL, IN_WIDTH), const),
                  pl.BlockSpec((1, IN_WIDTH), const),
                  pl.BlockSpec((tm, LANES), lambda i: (i % tiles_per_batch, 0)),
                  pl.BlockSpec((tm, LANES), lambda i: (i % tiles_per_batch, 0))],
        out_specs=(pl.BlockSpec((tm, ATTN_WIDTH), row),
                   pl.BlockSpec((tm, KV_WIDTH), row),
                   pl.BlockSpec((tm, KV_WIDTH), row),
                   pl.BlockSpec((tm, SGU_WIDTH), row),
                   pl.BlockSpec((tm, SGU_WIDTH), row)),
        compiler_params=pltpu.CompilerParams(dimension_semantics=("arbitrary",),
                                             vmem_limit_bytes=VMEM_LIMIT),
        name="inproj",
    )(x2, mod3, g_pre, w_in_b, b_in2, cos_t, sin_t)


def _ctxproj_kernel(x_ref, mod_ref, g_ref, w_ref, b_ref, k_ref, v_ref):
    m = mod_ref[0]
    h = _rms(x_ref[...], g_ref[...]) * (1.0 + m[1:2]) + m[0:1]
    kv = jnp.dot(h.astype(_BF16), w_ref[...], preferred_element_type=_F32) + b_ref[...]
    k_ref[...] = kv[:, 0:KV_WIDTH].astype(_BF16)
    v_ref[...] = kv[:, KV_WIDTH:].astype(_BF16)


def _ctxproj_call(ctx2, mod3, g_pre, w_kv_b, b_kv2):
    tm = ROW_TILE
    rows = ctx2.shape[0]
    row = lambda i: (i, 0)
    const = lambda i: (0, 0)
    return pl.pallas_call(
        _ctxproj_kernel,
        out_shape=(jax.ShapeDtypeStruct((rows, KV_WIDTH), _BF16),
                   jax.ShapeDtypeStruct((rows, KV_WIDTH), _BF16)),
        grid=(rows // tm,),
        in_specs=[pl.BlockSpec((tm, D_MODEL), row),
                  pl.BlockSpec((1, N_MOD, D_MODEL), lambda i: (BATCH, 0, 0)),
                  pl.BlockSpec((1, D_MODEL), const),
                  pl.BlockSpec((D_MODEL, 2 * KV_WIDTH), const),
                  pl.BlockSpec((1, 2 * KV_WIDTH), const)],
        out_specs=(pl.BlockSpec((tm, KV_WIDTH), row),
                   pl.BlockSpec((tm, KV_WIDTH), row)),
        compiler_params=pltpu.CompilerParams(dimension_semantics=("arbitrary",),
                                             vmem_limit_bytes=VMEM_LIMIT),
        name="ctxproj",
    )(ctx2, mod3, g_pre, w_kv_b, b_kv2)


def _attn_kernel(sink_ref, q_ref, kp_ref, ko_ref, kn_ref, vp_ref, vo_ref, vn_ref,
                 kc_ref, vc_ref, o_ref):
    n = pl.program_id(1)
    nblk = pl.num_programs(1)
    keys = jnp.concatenate([kp_ref[...], ko_ref[...], kn_ref[...], kc_ref[...]], axis=0)
    vals = jnp.concatenate([vp_ref[...], vo_ref[...], vn_ref[...], vc_ref[...]], axis=0)
    nk = 3 * ATTN_BLOCK + CTX_LEN
    rows2 = 2 * ATTN_BLOCK
    row = lax.broadcasted_iota(jnp.int32, (rows2, nk), 0)
    col = lax.broadcasted_iota(jnp.int32, (rows2, nk), 1)
    qi = row & (ATTN_BLOCK - 1)
    off = col - ATTN_BLOCK - qi
    in_band = (off <= WINDOW) & (off >= -WINDOW)
    in_seq = ((col >= ATTN_BLOCK) | (n > 0)) & ((col < 2 * ATTN_BLOCK) | (n < nblk - 1))
    valid = (col >= 3 * ATTN_BLOCK) | (in_band & in_seq)
    row1 = lax.broadcasted_iota(jnp.int32, (rows2, 1), 0)
    lane_q = lax.broadcasted_iota(jnp.int32, (ATTN_BLOCK, LANES), 1)
    for j in range(ATTN_WIDTH // LANES):
        qg = q_ref[:, j * LANES:(j + 1) * LANES]
        zero = jnp.zeros_like(qg)
        q2 = jnp.concatenate([jnp.where(lane_q < HEAD_DIM, qg, zero),
                              jnp.where(lane_q >= HEAD_DIM, qg, zero)], axis=0)
        s = lax.dot_general(q2, keys, (((1,), (1,)), ((), ())), preferred_element_type=_F32)
        s = jnp.where(valid, s, NEG_BIG)
        sk = jnp.where(row1 < ATTN_BLOCK, sink_ref[2 * j], sink_ref[2 * j + 1])
        m = jnp.maximum(jnp.max(s, axis=-1, keepdims=True), sk)
        p = jnp.exp(s - m)
        denom = jnp.sum(p, axis=-1, keepdims=True) + jnp.exp(sk - m)
        o2 = jnp.dot(p.astype(_BF16), vals, preferred_element_type=_F32) / denom
        og = jnp.where(lane_q < HEAD_DIM, o2[0:ATTN_BLOCK], o2[ATTN_BLOCK:])
        o_ref[:, j * LANES:(j + 1) * LANES] = og.astype(_BF16)


def _attn_call(sink_p, q, k, v, kc, vc):
    nblk = SEQ // ATTN_BLOCK
    own = lambda b, n: (b * nblk + n, 0)
    prev = lambda b, n: (b * nblk + jnp.maximum(n - 1, 0), 0)
    nxt = lambda b, n: (b * nblk + jnp.minimum(n + 1, nblk - 1), 0)
    ctx = lambda b, n: (b, 0)
    kvb = (ATTN_BLOCK, KV_WIDTH)
    return pl.pallas_call(
        _attn_kernel,
        out_shape=jax.ShapeDtypeStruct((TOKENS, ATTN_WIDTH), _BF16),
        grid=(BATCH, nblk),
        in_specs=[pl.BlockSpec(memory_space=pltpu.SMEM),
                  pl.BlockSpec((ATTN_BLOCK, ATTN_WIDTH), own),
                  pl.BlockSpec(kvb, prev), pl.BlockSpec(kvb, own), pl.BlockSpec(kvb, nxt),
                  pl.BlockSpec(kvb, prev), pl.BlockSpec(kvb, own), pl.BlockSpec(kvb, nxt),
                  pl.BlockSpec((CTX_LEN, KV_WIDTH), ctx),
                  pl.BlockSpec((CTX_LEN, KV_WIDTH), ctx)],
        out_specs=pl.BlockSpec((ATTN_BLOCK, ATTN_WIDTH), own),
        compiler_params=pltpu.CompilerParams(dimension_semantics=("arbitrary", "arbitrary"),
                                             vmem_limit_bytes=VMEM_LIMIT),
        name="attn",
    )(sink_p, q, k, k, k, v, v, v, kc, vc)


def _post_kernel(x_ref, ao_ref, su_ref, sv_ref, mod_ref, lng_ref, lnb_ref, ws_ref, bs_ref,
                 ga_ref, gs_ref, woa_ref, wos_ref, bo_ref, gpost_ref, gpre_ref, wr_ref, br_ref,
                 tri_ref,
                 xmid_ref, h2_ref, idx_ref, gate_ref, rank_ref, cnt_ref,
                 mixed_ref, carry_ref):
    tm = x_ref.shape[0]
    m = mod_ref[0]

    @pl.when(pl.program_id(0) == 0)
    def _():
        carry_ref[...] = jnp.zeros_like(carry_ref)

    gv = _gelu_tanh(sv_ref[...])
    mu = jnp.mean(gv, axis=-1, keepdims=True)
    gc = gv - mu
    var = jnp.mean(gc * gc, axis=-1, keepdims=True)
    vb = (gc * lax.rsqrt(var + EPS) * lng_ref[...] + lnb_ref[...]).astype(_BF16)
    lane = lax.broadcasted_iota(jnp.int32, (SGU_CHUNK, LANES), 1)
    for c in range(tm // SGU_CHUNK):
        r0 = c * SGU_CHUNK
        for p in range(SGU_WIDTH // LANES):
            l0 = p * LANES
            r = jnp.dot(ws_ref[p], vb[r0:r0 + SGU_CHUNK, l0:l0 + LANES], preferred_element_type=_F32)
            mixed = jnp.where(lane < SGU_HEAD_DIM, r[0:SGU_CHUNK], r[SGU_CHUNK:])
            mixed_ref[r0:r0 + SGU_CHUNK, l0:l0 + LANES] = mixed + bs_ref[:, l0:l0 + LANES]
    sgu_o = _gelu_tanh(su_ref[...]) * mixed_ref[...]

    oa = _rms(ao_ref[...].astype(_F32), ga_ref[...]).astype(_BF16)
    os_ = _rms(sgu_o, gs_ref[...]).astype(_BF16)
    mix = (jnp.dot(oa, woa_ref[...], preferred_element_type=_F32)
           + jnp.dot(os_, wos_ref[...], preferred_element_type=_F32) + bo_ref[...])
    x_mid = x_ref[...] + m[2:3] * _rms(mix, gpost_ref[...])
    xmid_ref[...] = x_mid
    h2 = _rms(x_mid, gpre_ref[...]) * (1.0 + m[4:5]) + m[3:4]
    h2_ref[...] = h2

    h_hi = h2.astype(_BF16)
    h_lo = (h2 - h_hi.astype(_F32)).astype(_BF16)
    r = (jnp.dot(h_hi, wr_ref[...], preferred_element_type=_F32)
         + jnp.dot(h_lo, wr_ref[...], preferred_element_type=_F32))
    lg = r + pltpu.roll(r, LANES - N_EXPERTS, 1) + br_ref[...]
    lane_r = lax.broadcasted_iota(jnp.int32, lg.shape, 1)
    lane_f = lane_r.astype(_F32)
    tops, hots = [], []
    for _k in range(TOP_K):
        mx = jnp.max(lg, axis=-1, keepdims=True)
        pick = jnp.min(jnp.where(lg == mx, lane_f, float(LANES)), axis=-1, keepdims=True)
        hot = lane_f == pick
        tops.append((mx, pick))
        hots.append(hot)
        lg = jnp.where(hot, 2.0 * NEG_BIG, lg)
    es = [jnp.exp(t[0] - tops[0][0]) for t in tops]
    esum = es[0] + es[1] + es[2] + es[3]
    multi = jnp.zeros(lg.shape, _F32)
    for hot in hots:
        multi = multi + jnp.where(hot, 1.0, 0.0)
    cum = jnp.dot(tri_ref[...], multi.astype(_BF16), preferred_element_type=_F32) + carry_ref[...]
    idx_o = jnp.zeros(lg.shape, _F32)
    gate_o = jnp.zeros(lg.shape, _F32)
    rank_o = jnp.zeros(lg.shape, _F32)
    for kk in range(TOP_K):
        sel = lane_r == kk
        rk = jnp.sum(jnp.where(hots[kk], cum, 0.0), axis=-1, keepdims=True)
        idx_o = jnp.where(sel, tops[kk][1], idx_o)
        gate_o = jnp.where(sel, es[kk] / esum, gate_o)
        rank_o = jnp.where(sel, rk, rank_o)
    idx_ref[...] = idx_o.astype(jnp.int32)
    gate_ref[...] = gate_o
    rank_ref[...] = rank_o.astype(jnp.int32)
    carry_ref[...] += jnp.sum(multi, axis=0, keepdims=True)
    cnt_ref[...] = carry_ref[...].astype(jnp.int32)


def _post_call(x2, ao, su, sv, mod3, lng, lnb, ws2, bsf, ga, gs, woa, wos, bo, gpost, gpre,
               wr, br, tri):
    tm = ROW_TILE
    tiles_per_batch = SEQ // tm
    row = lambda i: (i, 0)
    const = lambda i: (0, 0)
    const3 = lambda i: (0, 0, 0)
    return pl.pallas_call(
        _post_kernel,
        out_shape=(jax.ShapeDtypeStruct((TOKENS, D_MODEL), _F32),
                   jax.ShapeDtypeStruct((TOKENS, D_MODEL), _F32),
                   jax.ShapeDtypeStruct((TOKENS, LANES), jnp.int32),
                   jax.ShapeDtypeStruct((TOKENS, LANES), _F32),
                   jax.ShapeDtypeStruct((TOKENS, LANES), jnp.int32),
                   jax.ShapeDtypeStruct((1, LANES), jnp.int32)),
        grid=(TOKENS // tm,),
        in_specs=[pl.BlockSpec((tm, D_MODEL), row),
                  pl.BlockSpec((tm, ATTN_WIDTH), row),
                  pl.BlockSpec((tm, SGU_WIDTH), row),
                  pl.BlockSpec((tm, SGU_WIDTH), row),
                  pl.BlockSpec((1, N_MOD, D_MODEL), lambda i: (i // tiles_per_batch, 0, 0)),
                  pl.BlockSpec((1, SGU_WIDTH), const),
                  pl.BlockSpec((1, SGU_WIDTH), const),
                  pl.BlockSpec((SGU_WIDTH // LANES, 2 * SGU_CHUNK, SGU_CHUNK), const3),
                  pl.BlockSpec((SGU_CHUNK, SGU_WIDTH), const),
                  pl.BlockSpec((1, ATTN_WIDTH), const),
                  pl.BlockSpec((1, SGU_WIDTH), const),
                  pl.BlockSpec((ATTN_WIDTH, D_MODEL), const),
                  pl.BlockSpec((SGU_WIDTH, D_MODEL), const),
                  pl.BlockSpec((1, D_MODEL), const),
                  pl.BlockSpec((1, D_MODEL), const),
                  pl.BlockSpec((1, D_MODEL), const),
                  pl.BlockSpec((D_MODEL, LANES), const),
                  pl.BlockSpec((1, LANES), const),
                  pl.BlockSpec((tm, tm), const)],
        out_specs=(pl.BlockSpec((tm, D_MODEL), row),
                   pl.BlockSpec((tm, D_MODEL), row),
                   pl.BlockSpec((tm, LANES), row),
                   pl.BlockSpec((tm, LANES), row),
                   pl.BlockSpec((tm, LANES), row),
                   pl.BlockSpec((1, LANES), const)),
        scratch_shapes=[pltpu.VMEM((tm, SGU_WIDTH), _F32),
                        pltpu.VMEM((1, LANES), _F32)],
        compiler_params=pltpu.CompilerParams(dimension_semantics=("arbitrary",),
                                             vmem_limit_bytes=VMEM_LIMIT),
        name="post",
    )(x2, ao, su, sv, mod3, lng, lnb, ws2, bsf, ga, gs, woa, wos, bo, gpost, gpre, wr, br, tri)


def _dispatch_kernel(dest_ref, h2_ref, xs_ref, zero_ref, sem):
    tm = h2_ref.shape[0]
    n_token_steps = TOKENS // tm
    i = pl.program_id(0)

    def scatter_rows(src_ref, rows_per_dest):
        def issue(r, carry):
            for kk in range(TOP_K):
                d = dest_ref[r * TOP_K + kk]
                src_row = r if rows_per_dest == TOP_K else 0
                pltpu.make_async_copy(src_ref.at[pl.ds(src_row, 1)], xs_ref.at[pl.ds(d, 1)],
                                      sem).start(priority=kk % 2)
            return carry

        lax.fori_loop(0, tm, issue, 0)
        for kk in range(TOP_K):
            pltpu.make_async_copy(h2_ref, xs_ref.at[pl.ds(0, tm)], sem).wait()

    @pl.when(i < n_token_steps)
    def _():
        scatter_rows(h2_ref, TOP_K)

    @pl.when(i >= n_token_steps)
    def _():
        zero_ref[...] = jnp.zeros_like(zero_ref)
        scatter_rows(zero_ref, 1)


def _dispatch_call(dest_flat, h2):
    tm = MOVE_TILE
    n_token_steps = TOKENS // tm
    n_steps = dest_flat.shape[0] // (tm * TOP_K)
    return pl.pallas_call(
        _dispatch_kernel,
        out_shape=jax.ShapeDtypeStruct((SORTED_ROWS, D_MODEL), _F32),
        grid=(n_steps,),
        in_specs=[pl.BlockSpec((tm * TOP_K,), lambda i: (i,), memory_space=pltpu.SMEM),
                  pl.BlockSpec((tm, D_MODEL), lambda i: (jnp.minimum(i, n_token_steps - 1), 0))],
        out_specs=pl.BlockSpec(memory_space=pl.ANY),
        scratch_shapes=[pltpu.VMEM((8, D_MODEL), _F32),
                        pltpu.SemaphoreType.DMA],
        compiler_params=pltpu.CompilerParams(dimension_semantics=("arbitrary",),
                                             vmem_limit_bytes=VMEM_LIMIT),
        name="dispatch",
    )(dest_flat, h2)


def _expert_kernel(be_ref, nu_ref, xs_ref, wgu_ref, bgu_ref, wd_ref, bd_ref, y_ref):
    del be_ref

    @pl.when(pl.program_id(0) < nu_ref[0])
    def _():
        x = xs_ref[...].astype(_BF16)
        gu = jnp.dot(x, wgu_ref[0], preferred_element_type=_F32) + bgu_ref[0]
        gate = jnp.minimum(gu[:, 0:D_FF_EXPERT], SWIGLU_LIMIT)
        up = jnp.clip(gu[:, D_FF_EXPERT:], -SWIGLU_LIMIT, SWIGLU_LIMIT)
        act = (up + 1.0) * gate * (1.0 / (1.0 + jnp.exp(-SWIGLU_ALPHA * gate)))
        y_ref[...] = jnp.dot(act.astype(_BF16), wd_ref[0], preferred_element_type=_F32) + bd_ref[0]

    @pl.when(pl.program_id(0) >= nu_ref[0])
    def _():
        y_ref[...] = jnp.zeros_like(y_ref)


def _expert_call(block_expert, n_used, xs, wgu_b, bgu3, wd_b, bd3):
    tb = EXPERT_BLOCK
    live = lambda b, be, nu: (jnp.minimum(b, nu[0] - 1), 0)
    wsel = lambda b, be, nu: (be[b], 0, 0)
    grid_spec = pltpu.PrefetchScalarGridSpec(
        num_scalar_prefetch=2,
        grid=(N_EXPERT_BLOCKS,),
        in_specs=[pl.BlockSpec((tb, D_MODEL), live),
                  pl.BlockSpec((1, D_MODEL, 2 * D_FF_EXPERT), wsel),
                  pl.BlockSpec((1, 1, 2 * D_FF_EXPERT), wsel),
                  pl.BlockSpec((1, D_FF_EXPERT, D_MODEL), wsel),
                  pl.BlockSpec((1, 1, D_MODEL), wsel)],
        out_specs=pl.BlockSpec((tb, D_MODEL), lambda b, be, nu: (b, 0)))
    return pl.pallas_call(
        _expert_kernel,
        out_shape=jax.ShapeDtypeStruct((SORTED_ROWS, D_MODEL), _F32),
        grid_spec=grid_spec,
        compiler_params=pltpu.CompilerParams(dimension_semantics=("arbitrary",),
                                             vmem_limit_bytes=VMEM_LIMIT),
        name="expert",
    )(block_expert, n_used, xs, wgu_b, bgu3, wd_b, bd3)


def _combine_kernel(dest_ref, dest_next_ref, gate_ref, xmid_ref, mod_ref, g_ref, y_ref, o_ref,
                    buf_ref, sem):
    tm = xmid_ref.shape[0]
    m = mod_ref[0]
    i = pl.program_id(0)
    n = pl.num_programs(0)

    def gather_rows(idx_ref, slot):
        def issue(r, carry):
            for kk in range(TOP_K):
                d = idx_ref[r * TOP_K + kk]
                pltpu.make_async_copy(y_ref.at[pl.ds(d, 1)], buf_ref.at[slot, kk, pl.ds(r, 1)],
                                      sem.at[slot]).start(priority=kk % 2)
            return carry

        lax.fori_loop(0, tm, issue, 0)

    @pl.when(i == 0)
    def _():
        gather_rows(dest_ref, 0)

    @pl.when(i + 1 < n)
    def _():
        gather_rows(dest_next_ref, (i + 1) % 2)

    slot = i % 2
    for kk in range(TOP_K):
        pltpu.make_async_copy(y_ref.at[pl.ds(0, tm)], buf_ref.at[slot, kk], sem.at[slot]).wait()
    g = gate_ref[...]
    ffn = buf_ref[slot, 0] * g[:, 0:1]
    for kk in range(1, TOP_K):
        ffn = ffn + buf_ref[slot, kk] * g[:, kk:kk + 1]
    o_ref[...] = xmid_ref[...] + m[5:6] * _rms(ffn, g_ref[...])


def _combine_call(dest_flat, gates, x_mid, mod3, gpost_ffn, y_rows):
    tm = MOVE_TILE
    tiles_per_batch = SEQ // tm
    n_steps = TOKENS // tm
    return pl.pallas_call(
        _combine_kernel,
        out_shape=jax.ShapeDtypeStruct((TOKENS, D_MODEL), _F32),
        grid=(n_steps,),
        in_specs=[pl.BlockSpec((tm * TOP_K,), lambda i: (i,), memory_space=pltpu.SMEM),
                  pl.BlockSpec((tm * TOP_K,), lambda i: (jnp.minimum(i + 1, n_steps - 1),),
                               memory_space=pltpu.SMEM),
                  pl.BlockSpec((tm, LANES), lambda i: (i, 0)),
                  pl.BlockSpec((tm, D_MODEL), lambda i: (i, 0)),
                  pl.BlockSpec((1, N_MOD, D_MODEL), lambda i: (i // tiles_per_batch, 0, 0)),
                  pl.BlockSpec((1, D_MODEL), lambda i: (0, 0)),
                  pl.BlockSpec(memory_space=pl.ANY)],
        out_specs=pl.BlockSpec((tm, D_MODEL), lambda i: (i, 0)),
        scratch_shapes=[pltpu.VMEM((2, TOP_K, tm, D_MODEL), _F32),
                        pltpu.SemaphoreType.DMA((2,))],
        compiler_params=pltpu.CompilerParams(dimension_semantics=("arbitrary",),
                                             vmem_limit_bytes=VMEM_LIMIT),
        name="combine",
    )(dest_flat, dest_flat, gates, x_mid, mod3, gpost_ffn, y_rows)


def _rope_tables():
    pos = jnp.arange(SEQ, dtype=_F32)
    pos_row = jnp.floor(pos / GRID_W)
    pos_col = pos - pos_row * GRID_W
    n_freq = HEAD_DIM // 4
    inv_freq = ROPE_THETA ** (-jnp.arange(n_freq, dtype=_F32) / n_freq)
    d = jnp.arange(LANES) % HEAD_DIM
    f = inv_freq[d % n_freq]
    ang = jnp.where((d < HEAD_DIM // 2)[None, :], pos_row[:, None] * f[None, :], pos_col[:, None] * f[None, :])
    sign = jnp.where((d % (HEAD_DIM // 2)) < n_freq, -1.0, 1.0).astype(_F32)
    return jnp.cos(ang), jnp.sin(ang) * sign[None, :]


def _perm_heads(a, axis):
    shape = a.shape
    a = a.reshape(shape[:axis] + (N_HEADS, HEAD_DIM) + shape[axis + 1:])
    a = jnp.take(a, jnp.array(HEAD_PERM), axis=axis)
    return a.reshape(shape)


def kernel(x, c, ctx, c_ctx, w_ada, b_ada, g_pre_mix, g_post_mix, g_pre_ffn, g_post_ffn, w_in, b_in, attn_sink, sgu_ln_g, sgu_ln_b, sgu_w, sgu_b, g_attn_out, g_sgu_out, w_out, b_out, w_router, b_router, w_gate_up, b_gate_up, w_down, b_down):
    l = 0
    x2 = x.reshape(TOKENS, D_MODEL)
    ctx2 = ctx.reshape(BATCH * CTX_LEN, D_MODEL)

    cc = jnp.zeros((16, D_MODEL), _F32).at[:BATCH].set(c).at[BATCH].set(c_ctx)
    mod = _ada_call(cc, w_ada[l], b_ada[l].reshape(1, -1))
    mod3 = mod.reshape(16, N_MOD, D_MODEL)

    w_in_l = w_in[l]
    b_in_l = b_in[l]
    w_in_p = jnp.concatenate([_perm_heads(w_in_l[:, :ATTN_WIDTH], 1), w_in_l[:, ATTN_WIDTH:]], axis=1)
    b_in_p = jnp.concatenate([_perm_heads(b_in_l[:ATTN_WIDTH], 0), b_in_l[ATTN_WIDTH:]], axis=0)
    cos_t, sin_t = _rope_tables()
    g_pre = g_pre_mix[l].reshape(1, -1)

    q, k, v, su, sv = _inproj_call(x2, mod3, g_pre, w_in_p.astype(_BF16), b_in_p.reshape(1, -1), cos_t, sin_t)
    kc, vc = _ctxproj_call(ctx2, mod3, g_pre, w_in_l[:, KV_START:KV_END].astype(_BF16),
                           b_in_l[KV_START:KV_END].reshape(1, -1))

    sink_p = jnp.take(attn_sink[l], jnp.array(HEAD_PERM))
    ao = _attn_call(sink_p, q, k, v, kc, vc)

    ws2 = sgu_w[l].reshape(SGU_WIDTH // LANES, 2 * SGU_CHUNK, SGU_CHUNK).astype(_BF16)
    bsf = jnp.repeat(sgu_b[l].T, SGU_HEAD_DIM, axis=1)
    w_out_l = w_out[l]
    woa = _perm_heads(w_out_l[:ATTN_WIDTH], 0).astype(_BF16)
    wos = w_out_l[ATTN_WIDTH:].astype(_BF16)
    ga = _perm_heads(g_attn_out[l], 0).reshape(1, -1)
    wr_hi = w_router[l].astype(_BF16)
    wr_lo = (w_router[l] - wr_hi.astype(_F32)).astype(_BF16)
    wr = (jnp.zeros((D_MODEL, LANES), _BF16).at[:, :N_EXPERTS].set(wr_hi)
          .at[:, N_EXPERTS:2 * N_EXPERTS].set(wr_lo))
    br = jnp.full((1, LANES), NEG_BIG, _F32).at[0, :N_EXPERTS].set(b_router[l])
    ii = jnp.arange(ROW_TILE)
    tri = (ii[None, :] < ii[:, None]).astype(_BF16)

    x_mid, h2, idx_o, gate_o, rank_o, cnt_o = _post_call(
        x2, ao, su, sv, mod3, sgu_ln_g[l].reshape(1, -1), sgu_ln_b[l].reshape(1, -1), ws2, bsf,
        ga, g_sgu_out[l].reshape(1, -1), woa, wos, b_out[l].reshape(1, -1),
        g_post_mix[l].reshape(1, -1), g_pre_ffn[l].reshape(1, -1), wr, br, tri)

    counts = cnt_o[0, :N_EXPERTS]
    padded = (counts + EXPERT_BLOCK - 1) // EXPERT_BLOCK * EXPERT_BLOCK
    pad_end = jnp.cumsum(padded)
    pad_start = pad_end - padded
    top_idx = idx_o[:, :TOP_K]
    dest = (pad_start[top_idx] + rank_o[:, :TOP_K]).astype(jnp.int32).reshape(-1)
    block_start = jnp.arange(N_EXPERT_BLOCKS, dtype=jnp.int32) * EXPERT_BLOCK
    block_expert = jnp.minimum(jnp.sum(pad_end[None, :] <= block_start[:, None], axis=1),
                               N_EXPERTS - 1).astype(jnp.int32)
    n_used = (pad_end[-1:] // EXPERT_BLOCK).astype(jnp.int32)

    free = padded - counts
    free_end = jnp.cumsum(free)
    free_start = free_end - free
    j = jnp.arange(SORTED_ROWS - TOKENS * TOP_K, dtype=jnp.int32)
    e_of_j = jnp.minimum(jnp.searchsorted(free_end, j, side="right"), N_EXPERTS - 1)
    unrouted = jnp.where(j < free_end[-1],
                         pad_start[e_of_j] + counts[e_of_j] + (j - free_start[e_of_j]),
                         pad_end[-1] + (j - free_end[-1])).astype(jnp.int32)

    xs = _dispatch_call(jnp.concatenate([dest, unrouted]), h2)
    y_rows = _expert_call(block_expert, n_used, xs,
                          w_gate_up[l].astype(_BF16), b_gate_up[l].reshape(N_EXPERTS, 1, -1),
                          w_down[l].astype(_BF16), b_down[l].reshape(N_EXPERTS, 1, -1))
    out = _combine_call(dest, gate_o, x_mid, mod3, g_post_ffn[l].reshape(1, -1), y_rows)
    return out.reshape(BATCH, SEQ, D_MODEL)
```

```python
import functools

import jax
import jax.numpy as jnp
from jax import lax
from jax.experimental import pallas as pl
from jax.experimental.pallas import tpu as pltpu

D_MODEL = 1024
BATCH = 8
SEQ = 4096
TOKENS = BATCH * SEQ
GRID_W = 64
CTX_LEN = 256
N_HEADS = 8
N_KV_HEADS = 2
HEAD_DIM = 64
ATTN_WIDTH = N_HEADS * HEAD_DIM
KV_WIDTH = N_KV_HEADS * HEAD_DIM
WINDOW = 128
ATTN_BLOCK = 128
SGU_HEADS = 8
SGU_HEAD_DIM = 64
SGU_WIDTH = SGU_HEADS * SGU_HEAD_DIM
SGU_CHUNK = 128
KV_START = ATTN_WIDTH
KV_END = ATTN_WIDTH + 2 * KV_WIDTH
IN_WIDTH = KV_END + 2 * SGU_WIDTH
N_EXPERTS = 32
TOP_K = 4
D_FF_EXPERT = 1024
SWIGLU_LIMIT = 7.0
SWIGLU_ALPHA = 1.702
ROPE_THETA = 10000.0
EPS = 1e-6
N_MOD = 6

LANES = 128
NEG_BIG = -1e30
VMEM_LIMIT = 48 * 1024 * 1024

ROW_TILE = 512
MOVE_TILE = 256
EXPERT_BLOCK = 512
N_EXPERT_BLOCKS = TOKENS * TOP_K // EXPERT_BLOCK + N_EXPERTS
SORTED_ROWS = N_EXPERT_BLOCKS * EXPERT_BLOCK
HEAD_PERM = (0, 4, 1, 5, 2, 6, 3, 7)

_F32 = jnp.float32
_BF16 = jnp.bfloat16


def _rms(x, g):
    ms = jnp.mean(x * x, axis=-1, keepdims=True)
    return x * lax.rsqrt(ms + EPS) * g


ROW_SUB = D_MODEL // LANES


def _store_row_tiles(ref, x):
    n = x.shape[0]
    for cc in range(ROW_SUB):
        ref[pl.ds(cc, n, stride=ROW_SUB), :] = x[:, cc * LANES:(cc + 1) * LANES]


def _load_row_tiles(ref, n):
    return jnp.concatenate([ref[pl.ds(cc, n, stride=ROW_SUB), :] for cc in range(ROW_SUB)], axis=1)


def _gelu_tanh(x):
    c = 0.7978845608028654
    return x * (0.5 * (1.0 + jnp.tanh(c * (x + 0.044715 * (x * x * x)))))


def _ada_kernel(c_ref, w_ref, b_ref, o_ref):
    c = c_ref[...]
    a = c / (1.0 + jnp.exp(-c))
    o_ref[...] = jnp.dot(a, w_ref[...], preferred_element_type=_F32,
                         precision=lax.Precision.HIGHEST) + b_ref[...]


def _ada_call(cc, w_ada, b_ada):
    n = N_MOD * D_MODEL
    tn = 512
    return pl.pallas_call(
        _ada_kernel,
        out_shape=jax.ShapeDtypeStruct((16, n), _F32),
        grid=(n // tn,),
        in_specs=[pl.BlockSpec((16, D_MODEL), lambda j: (0, 0)),
                  pl.BlockSpec((D_MODEL, tn), lambda j: (0, j)),
                  pl.BlockSpec((1, tn), lambda j: (0, j))],
        out_specs=pl.BlockSpec((16, tn), lambda j: (0, j)),
        compiler_params=pltpu.CompilerParams(dimension_semantics=("arbitrary",),
                                             vmem_limit_bytes=VMEM_LIMIT),
        name="ada",
    )(cc, w_ada, b_ada)


def _rope(x, cos, sin_signed, first_half):
    nxt = pltpu.roll(x, LANES - 16, 1)
    prv = pltpu.roll(x, 16, 1)
    return x * cos + jnp.where(first_half, nxt, prv) * sin_signed


def _inproj_kernel(x_ref, mod_ref, g_ref, w_ref, b_ref, cos_ref, sin_ref,
                   q_ref, k_ref, v_ref, su_ref, sv_ref):
    m = mod_ref[0]
    h = _rms(x_ref[...], g_ref[...]) * (1.0 + m[1:2]) + m[0:1]
    hb = h.astype(_BF16)
    cos = cos_ref[...]
    sin = sin_ref[...]
    lane = lax.broadcasted_iota(jnp.int32, cos.shape, 1)
    first_half = (lane & 31) < 16

    q = jnp.dot(hb, w_ref[:, 0:ATTN_WIDTH], preferred_element_type=_F32) + b_ref[:, 0:ATTN_WIDTH]
    for j in range(ATTN_WIDTH // LANES):
        qj = _rope(q[:, j * LANES:(j + 1) * LANES], cos, sin, first_half)
        q_ref[:, j * LANES:(j + 1) * LANES] = (qj * (HEAD_DIM ** -0.5)).astype(_BF16)
    kv = jnp.dot(hb, w_ref[:, KV_START:KV_END], preferred_element_type=_F32) + b_ref[:, KV_START:KV_END]
    k_ref[...] = _rope(kv[:, 0:KV_WIDTH], cos, sin, first_half).astype(_BF16)
    v_ref[...] = kv[:, KV_WIDTH:].astype(_BF16)
    u0 = KV_END
    u1 = KV_END + SGU_WIDTH
    su_ref[...] = jnp.dot(hb, w_ref[:, u0:u1], preferred_element_type=_F32) + b_ref[:, u0:u1]
    sv_ref[...] = jnp.dot(hb, w_ref[:, u1:IN_WIDTH], preferred_element_type=_F32) + b_ref[:, u1:IN_WIDTH]


def _inproj_call(x2, mod3, g_pre, w_in_b, b_in2, cos_t, sin_t):
    tm = ROW_TILE
    tiles_per_batch = SEQ // tm
    row = lambda i: (i, 0)
    const = lambda i: (0, 0)
    return pl.pallas_call(
        _inproj_kernel,
        out_shape=(jax.ShapeDtypeStruct((TOKENS, ATTN_WIDTH), _BF16),
                   jax.ShapeDtypeStruct((TOKENS, KV_WIDTH), _BF16),
                   jax.ShapeDtypeStruct((TOKENS, KV_WIDTH), _BF16),
                   jax.ShapeDtypeStruct((TOKENS, SGU_WIDTH), _F32),
                   jax.ShapeDtypeStruct((TOKENS, SGU_WIDTH), _F32)),
        grid=(TOKENS // tm,),
        in_specs=[pl.BlockSpec((tm, D_MODEL), row),
                  pl.BlockSpec((1, N_MOD, D_MODEL), lambda i: (i // tiles_per_batch, 0, 0)),
                  pl.BlockSpec((1, D_MODEL), const),
                  pl.BlockSpec((D_MODEL, IN_WIDTH), const),
                  pl.BlockSpec((1, IN_WIDTH), const),
                  pl.BlockSpec((tm, LANES), lambda i: (i % tiles_per_batch, 0)),
                  pl.BlockSpec((tm, LANES), lambda i: (i % tiles_per_batch, 0))],
        out_specs=(pl.BlockSpec((tm, ATTN_WIDTH), row),
                   pl.BlockSpec((tm, KV_WIDTH), row),
                   pl.BlockSpec((tm, KV_WIDTH), row),
                   pl.BlockSpec((tm, SGU_WIDTH), row),
                   pl.BlockSpec((tm, SGU_WIDTH), row)),
        compiler_params=pltpu.CompilerParams(dimension_semantics=("arbitrary",),
                                             vmem_limit_bytes=VMEM_LIMIT),
        name="inproj",
    )(x2, mod3, g_pre, w_in_b, b_in2, cos_t, sin_t)


def _ctxproj_kernel(x_ref, mod_ref, g_ref, w_ref, b_ref, k_ref, v_ref):
    m = mod_ref[0]
    h = _rms(x_ref[...], g_ref[...]) * (1.0 + m[1:2]) + m[0:1]
    kv = jnp.dot(h.astype(_BF16), w_ref[...], preferred_element_type=_F32) + b_ref[...]
    k_ref[...] = kv[:, 0:KV_WIDTH].astype(_BF16)
    v_ref[...] = kv[:, KV_WIDTH:].astype(_BF16)


def _ctxproj_call(ctx2, mod3, g_pre, w_kv_b, b_kv2):
    tm = ROW_TILE
    rows = ctx2.shape[0]
    row = lambda i: (i, 0)
    const = lambda i: (0, 0)
    return pl.pallas_call(
        _ctxproj_kernel,
        out_shape=(jax.ShapeDtypeStruct((rows, KV_WIDTH), _BF16),
                   jax.ShapeDtypeStruct((rows, KV_WIDTH), _BF16)),
        grid=(rows // tm,),
        in_specs=[pl.BlockSpec((tm, D_MODEL), row),
                  pl.BlockSpec((1, N_MOD, D_MODEL), lambda i: (BATCH, 0, 0)),
                  pl.BlockSpec((1, D_MODEL), const),
                  pl.BlockSpec((D_MODEL, 2 * KV_WIDTH), const),
                  pl.BlockSpec((1, 2 * KV_WIDTH), const)],
        out_specs=(pl.BlockSpec((tm, KV_WIDTH), row),
                   pl.BlockSpec((tm, KV_WIDTH), row)),
        compiler_params=pltpu.CompilerParams(dimension_semantics=("arbitrary",),
                                             vmem_limit_bytes=VMEM_LIMIT),
        name="ctxproj",
    )(ctx2, mod3, g_pre, w_kv_b, b_kv2)


def _attn_kernel(sink_ref, q_ref, kp_ref, ko_ref, kn_ref, vp_ref, vo_ref, vn_ref,
                 kc_ref, vc_ref, o_ref):
    n = pl.program_id(1)
    nblk = pl.num_programs(1)
    keys = jnp.concatenate([kp_ref[...], ko_ref[...], kn_ref[...], kc_ref[...]], axis=0)
    vals = jnp.concatenate([vp_ref[...], vo_ref[...], vn_ref[...], vc_ref[...]], axis=0)
    nk = 3 * ATTN_BLOCK + CTX_LEN
    rows2 = 2 * ATTN_BLOCK
    row = lax.broadcasted_iota(jnp.int32, (rows2, nk), 0)
    col = lax.broadcasted_iota(jnp.int32, (rows2, nk), 1)
    qi = row & (ATTN_BLOCK - 1)
    off = col - ATTN_BLOCK - qi
    in_band = (off <= WINDOW) & (off >= -WINDOW)
    in_seq = ((col >= ATTN_BLOCK) | (n > 0)) & ((col < 2 * ATTN_BLOCK) | (n < nblk - 1))
    valid = (col >= 3 * ATTN_BLOCK) | (in_band & in_seq)
    row1 = lax.broadcasted_iota(jnp.int32, (rows2, 1), 0)
    lane_q = lax.broadcasted_iota(jnp.int32, (ATTN_BLOCK, LANES), 1)
    for j in range(ATTN_WIDTH // LANES):
        qg = q_ref[:, j * LANES:(j + 1) * LANES]
        zero = jnp.zeros_like(qg)
        q2 = jnp.concatenate([jnp.where(lane_q < HEAD_DIM, qg, zero),
                              jnp.where(lane_q >= HEAD_DIM, qg, zero)], axis=0)
        s = lax.dot_general(q2, keys, (((1,), (1,)), ((), ())), preferred_element_type=_F32)
        s = jnp.where(valid, s, NEG_BIG)
        sk = jnp.where(row1 < ATTN_BLOCK, sink_ref[2 * j], sink_ref[2 * j + 1])
        m = jnp.maximum(jnp.max(s, axis=-1, keepdims=True), sk)
        p = jnp.exp(s - m)
        denom = jnp.sum(p, axis=-1, keepdims=True) + jnp.exp(sk - m)
        o2 = jnp.dot(p.astype(_BF16), vals, preferred_element_type=_F32) / denom
        og = jnp.where(lane_q < HEAD_DIM, o2[0:ATTN_BLOCK], o2[ATTN_BLOCK:])
        o_ref[:, j * LANES:(j + 1) * LANES] = og.astype(_BF16)


def _attn_call(sink_p, q, k, v, kc, vc):
    nblk = SEQ // ATTN_BLOCK
    own = lambda b, n: (b * nblk + n, 0)
    prev = lambda b, n: (b * nblk + jnp.maximum(n - 1, 0), 0)
    nxt = lambda b, n: (b * nblk + jnp.minimum(n + 1, nblk - 1), 0)
    ctx = lambda b, n: (b, 0)
    kvb = (ATTN_BLOCK, KV_WIDTH)
    return pl.pallas_call(
        _attn_kernel,
        out_shape=jax.ShapeDtypeStruct((TOKENS, ATTN_WIDTH), _BF16),
        grid=(BATCH, nblk),
        in_specs=[pl.BlockSpec(memory_space=pltpu.SMEM),
                  pl.BlockSpec((ATTN_BLOCK, ATTN_WIDTH), own),
                  pl.BlockSpec(kvb, prev), pl.BlockSpec(kvb, own), pl.BlockSpec(kvb, nxt),
                  pl.BlockSpec(kvb, prev), pl.BlockSpec(kvb, own), pl.BlockSpec(kvb, nxt),
                  pl.BlockSpec((CTX_LEN, KV_WIDTH), ctx),
                  pl.BlockSpec((CTX_LEN, KV_WIDTH), ctx)],
        out_specs=pl.BlockSpec((ATTN_BLOCK, ATTN_WIDTH), own),
        compiler_params=pltpu.CompilerParams(dimension_semantics=("arbitrary", "arbitrary"),
                                             vmem_limit_bytes=VMEM_LIMIT),
        name="attn",
    )(sink_p, q, k, k, k, v, v, v, kc, vc)


def _post_kernel(x_ref, ao_ref, su_ref, sv_ref, mod_ref, lng_ref, lnb_ref, ws_ref, bs_ref,
                 ga_ref, gs_ref, woa_ref, wos_ref, bo_ref, gpost_ref, gpre_ref, wr_ref, br_ref,
                 tri_ref,
                 xmid_ref, h2_ref, idx_ref, gate_ref, rank_ref, cnt_ref,
                 mixed_ref, carry_ref):
    tm = x_ref.shape[0]
    m = mod_ref[0]

    @pl.when(pl.program_id(0) == 0)
    def _():
        carry_ref[...] = jnp.zeros_like(carry_ref)

    gv = _gelu_tanh(sv_ref[...])
    mu = jnp.mean(gv, axis=-1, keepdims=True)
    gc = gv - mu
    var = jnp.mean(gc * gc, axis=-1, keepdims=True)
    vb = (gc * lax.rsqrt(var + EPS) * lng_ref[...] + lnb_ref[...]).astype(_BF16)
    lane = lax.broadcasted_iota(jnp.int32, (SGU_CHUNK, LANES), 1)
    for c in range(tm // SGU_CHUNK):
        r0 = c * SGU_CHUNK
        for p in range(SGU_WIDTH // LANES):
            l0 = p * LANES
            r = jnp.dot(ws_ref[p], vb[r0:r0 + SGU_CHUNK, l0:l0 + LANES], preferred_element_type=_F32)
            mixed = jnp.where(lane < SGU_HEAD_DIM, r[0:SGU_CHUNK], r[SGU_CHUNK:])
            mixed_ref[r0:r0 + SGU_CHUNK, l0:l0 + LANES] = mixed + bs_ref[:, l0:l0 + LANES]
    sgu_o = _gelu_tanh(su_ref[...]) * mixed_ref[...]

    oa = _rms(ao_ref[...].astype(_F32), ga_ref[...]).astype(_BF16)
    os_ = _rms(sgu_o, gs_ref[...]).astype(_BF16)
    mix = (jnp.dot(oa, woa_ref[...], preferred_element_type=_F32)
           + jnp.dot(os_, wos_ref[...], preferred_element_type=_F32) + bo_ref[...])
    x_mid = x_ref[...] + m[2:3] * _rms(mix, gpost_ref[...])
    xmid_ref[...] = x_mid
    h2 = _rms(x_mid, gpre_ref[...]) * (1.0 + m[4:5]) + m[3:4]
    _store_row_tiles(h2_ref, h2)

    h_hi = h2.astype(_BF16)
    h_lo = (h2 - h_hi.astype(_F32)).astype(_BF16)
    r = (jnp.dot(h_hi, wr_ref[...], preferred_element_type=_F32)
         + jnp.dot(h_lo, wr_ref[...], preferred_element_type=_F32))
    lg = r + pltpu.roll(r, LANES - N_EXPERTS, 1) + br_ref[...]
    lane_r = lax.broadcasted_iota(jnp.int32, lg.shape, 1)
    lane_f = lane_r.astype(_F32)
    tops, hots = [], []
    for _k in range(TOP_K):
        mx = jnp.max(lg, axis=-1, keepdims=True)
        pick = jnp.min(jnp.where(lg == mx, lane_f, float(LANES)), axis=-1, keepdims=True)
        hot = lane_f == pick
        tops.append((mx, pick))
        hots.append(hot)
        lg = jnp.where(hot, 2.0 * NEG_BIG, lg)
    es = [jnp.exp(t[0] - tops[0][0]) for t in tops]
    esum = es[0] + es[1] + es[2] + es[3]
    multi = jnp.zeros(lg.shape, _F32)
    for hot in hots:
        multi = multi + jnp.where(hot, 1.0, 0.0)
    cum = jnp.dot(tri_ref[...], multi.astype(_BF16), preferred_element_type=_F32) + carry_ref[...]
    idx_o = jnp.zeros(lg.shape, _F32)
    gate_o = jnp.zeros(lg.shape, _F32)
    rank_o = jnp.zeros(lg.shape, _F32)
    for kk in range(TOP_K):
        sel = lane_r == kk
        rk = jnp.sum(jnp.where(hots[kk], cum, 0.0), axis=-1, keepdims=True)
        idx_o = jnp.where(sel, tops[kk][1], idx_o)
        gate_o = jnp.where(sel, es[kk] / esum, gate_o)
        rank_o = jnp.where(sel, rk, rank_o)
    idx_ref[...] = idx_o.astype(jnp.int32)
    gate_ref[...] = gate_o
    rank_ref[...] = rank_o.astype(jnp.int32)
    carry_ref[...] += jnp.sum(multi, axis=0, keepdims=True)
    cnt_ref[...] = carry_ref[...].astype(jnp.int32)


def _post_call(x2, ao, su, sv, mod3, lng, lnb, ws2, bsf, ga, gs, woa, wos, bo, gpost, gpre,
               wr, br, tri):
    tm = ROW_TILE
    tiles_per_batch = SEQ // tm
    row = lambda i: (i, 0)
    const = lambda i: (0, 0)
    const3 = lambda i: (0, 0, 0)
    return pl.pallas_call(
        _post_kernel,
        out_shape=(jax.ShapeDtypeStruct((TOKENS, D_MODEL), _F32),
                   jax.ShapeDtypeStruct((TOKENS * ROW_SUB, LANES), _F32),
                   jax.ShapeDtypeStruct((TOKENS, LANES), jnp.int32),
                   jax.ShapeDtypeStruct((TOKENS, LANES), _F32),
                   jax.ShapeDtypeStruct((TOKENS, LANES), jnp.int32),
                   jax.ShapeDtypeStruct((1, LANES), jnp.int32)),
        grid=(TOKENS // tm,),
        in_specs=[pl.BlockSpec((tm, D_MODEL), row),
                  pl.BlockSpec((tm, ATTN_WIDTH), row),
                  pl.BlockSpec((tm, SGU_WIDTH), row),
                  pl.BlockSpec((tm, SGU_WIDTH), row),
                  pl.BlockSpec((1, N_MOD, D_MODEL), lambda i: (i // tiles_per_batch, 0, 0)),
                  pl.BlockSpec((1, SGU_WIDTH), const),
                  pl.BlockSpec((1, SGU_WIDTH), const),
                  pl.BlockSpec((SGU_WIDTH // LANES, 2 * SGU_CHUNK, SGU_CHUNK), const3),
                  pl.BlockSpec((SGU_CHUNK, SGU_WIDTH), const),
                  pl.BlockSpec((1, ATTN_WIDTH), const),
                  pl.BlockSpec((1, SGU_WIDTH), const),
                  pl.BlockSpec((ATTN_WIDTH, D_MODEL), const),
                  pl.BlockSpec((SGU_WIDTH, D_MODEL), const),
                  pl.BlockSpec((1, D_MODEL), const),
                  pl.BlockSpec((1, D_MODEL), const),
                  pl.BlockSpec((1, D_MODEL), const),
                  pl.BlockSpec((D_MODEL, LANES), const),
                  pl.BlockSpec((1, LANES), const),
                  pl.BlockSpec((tm, tm), const)],
        out_specs=(pl.BlockSpec((tm, D_MODEL), row),
                   pl.BlockSpec((tm * ROW_SUB, LANES), row),
                   pl.BlockSpec((tm, LANES), row),
                   pl.BlockSpec((tm, LANES), row),
                   pl.BlockSpec((tm, LANES), row),
                   pl.BlockSpec((1, LANES), const)),
        scratch_shapes=[pltpu.VMEM((tm, SGU_WIDTH), _F32),
                        pltpu.VMEM((1, LANES), _F32)],
        compiler_params=pltpu.CompilerParams(dimension_semantics=("arbitrary",),
                                             vmem_limit_bytes=VMEM_LIMIT),
        name="post",
    )(x2, ao, su, sv, mod3, lng, lnb, ws2, bsf, ga, gs, woa, wos, bo, gpost, gpre, wr, br, tri)


def _dispatch_kernel(dest_ref, h2_ref, xs_ref, zero_ref, sem):
    tm = h2_ref.shape[0] // ROW_SUB
    n_token_steps = TOKENS // tm
    i = pl.program_id(0)

    def scatter_rows(src_ref, rows_per_dest):
        def issue(r, carry):
            for kk in range(TOP_K):
                d = pl.multiple_of(dest_ref[r * TOP_K + kk] * ROW_SUB, ROW_SUB)
                src_row = pl.multiple_of(r * ROW_SUB, ROW_SUB) if rows_per_dest == TOP_K else 0
                pltpu.make_async_copy(src_ref.at[pl.ds(src_row, ROW_SUB)],
                                      xs_ref.at[pl.ds(d, ROW_SUB)], sem).start(priority=kk % 2)
            return carry

        lax.fori_loop(0, tm, issue, 0)
        for kk in range(TOP_K):
            pltpu.make_async_copy(h2_ref, xs_ref.at[pl.ds(0, tm * ROW_SUB)], sem).wait()

    @pl.when(i < n_token_steps)
    def _():
        scatter_rows(h2_ref, TOP_K)

    @pl.when(i >= n_token_steps)
    def _():
        zero_ref[...] = jnp.zeros_like(zero_ref)
        scatter_rows(zero_ref, 1)


def _dispatch_call(dest_flat, h2):
    tm = MOVE_TILE
    n_token_steps = TOKENS // tm
    n_steps = dest_flat.shape[0] // (tm * TOP_K)
    return pl.pallas_call(
        _dispatch_kernel,
        out_shape=jax.ShapeDtypeStruct((SORTED_ROWS * ROW_SUB, LANES), _F32),
        grid=(n_steps,),
        in_specs=[pl.BlockSpec((tm * TOP_K,), lambda i: (i,), memory_space=pltpu.SMEM),
                  pl.BlockSpec((tm * ROW_SUB, LANES),
                               lambda i: (jnp.minimum(i, n_token_steps - 1), 0))],
        out_specs=pl.BlockSpec(memory_space=pl.ANY),
        scratch_shapes=[pltpu.VMEM((ROW_SUB, LANES), _F32),
                        pltpu.SemaphoreType.DMA],
        compiler_params=pltpu.CompilerParams(dimension_semantics=("arbitrary",),
                                             vmem_limit_bytes=VMEM_LIMIT),
        name="dispatch",
    )(dest_flat, h2)


def _expert_kernel(be_ref, nu_ref, xs_ref, wgu_ref, bgu_ref, wd_ref, bd_ref, y_ref):
    del be_ref

    @pl.when(pl.program_id(0) < nu_ref[0])
    def _():
        x = _load_row_tiles(xs_ref, EXPERT_BLOCK).astype(_BF16)
        gu = jnp.dot(x, wgu_ref[0], preferred_element_type=_F32) + bgu_ref[0]
        gate = jnp.minimum(gu[:, 0:D_FF_EXPERT], SWIGLU_LIMIT)
        up = jnp.clip(gu[:, D_FF_EXPERT:], -SWIGLU_LIMIT, SWIGLU_LIMIT)
        act = (up + 1.0) * gate * (1.0 / (1.0 + jnp.exp(-SWIGLU_ALPHA * gate)))
        y = jnp.dot(act.astype(_BF16), wd_ref[0], preferred_element_type=_F32) + bd_ref[0]
        _store_row_tiles(y_ref, y)

    @pl.when(pl.program_id(0) >= nu_ref[0])
    def _():
        y_ref[...] = jnp.zeros_like(y_ref)


def _expert_call(block_expert, n_used, xs, wgu_b, bgu3, wd_b, bd3):
    tb = EXPERT_BLOCK
    live = lambda b, be, nu: (jnp.minimum(b, nu[0] - 1), 0)
    wsel = lambda b, be, nu: (be[b], 0, 0)
    grid_spec = pltpu.PrefetchScalarGridSpec(
        num_scalar_prefetch=2,
        grid=(N_EXPERT_BLOCKS,),
        in_specs=[pl.BlockSpec((tb * ROW_SUB, LANES), live),
                  pl.BlockSpec((1, D_MODEL, 2 * D_FF_EXPERT), wsel),
                  pl.BlockSpec((1, 1, 2 * D_FF_EXPERT), wsel),
                  pl.BlockSpec((1, D_FF_EXPERT, D_MODEL), wsel),
                  pl.BlockSpec((1, 1, D_MODEL), wsel)],
        out_specs=pl.BlockSpec((tb * ROW_SUB, LANES), lambda b, be, nu: (b, 0)))
    return pl.pallas_call(
        _expert_kernel,
        out_shape=jax.ShapeDtypeStruct((SORTED_ROWS * ROW_SUB, LANES), _F32),
        grid_spec=grid_spec,
        compiler_params=pltpu.CompilerParams(dimension_semantics=("arbitrary",),
                                             vmem_limit_bytes=VMEM_LIMIT),
        name="expert",
    )(block_expert, n_used, xs, wgu_b, bgu3, wd_b, bd3)


def _combine_kernel(dest_ref, dest_next_ref, gate_ref, xmid_ref, mod_ref, g_ref, y_ref, o_ref,
                    buf_ref, sem):
    tm = xmid_ref.shape[0]
    m = mod_ref[0]
    i = pl.program_id(0)
    n = pl.num_programs(0)

    def gather_rows(idx_ref, slot):
        def issue(r, carry):
            for kk in range(TOP_K):
                d = pl.multiple_of(idx_ref[r * TOP_K + kk] * ROW_SUB, ROW_SUB)
                pltpu.make_async_copy(
                    y_ref.at[pl.ds(d, ROW_SUB)],
                    buf_ref.at[slot, kk, pl.ds(pl.multiple_of(r * ROW_SUB, ROW_SUB), ROW_SUB)],
                    sem.at[slot]).start(priority=kk % 2)
            return carry

        lax.fori_loop(0, tm, issue, 0)

    @pl.when(i == 0)
    def _():
        gather_rows(dest_ref, 0)

    @pl.when(i + 1 < n)
    def _():
        gather_rows(dest_next_ref, (i + 1) % 2)

    slot = i % 2
    for kk in range(TOP_K):
        pltpu.make_async_copy(y_ref.at[pl.ds(0, tm * ROW_SUB)], buf_ref.at[slot, kk],
                              sem.at[slot]).wait()
    g = gate_ref[...]
    pieces = []
    for cc in range(ROW_SUB):
        piece = buf_ref[slot, 0, pl.ds(cc, tm, stride=ROW_SUB), :] * g[:, 0:1]
        for kk in range(1, TOP_K):
            piece = piece + buf_ref[slot, kk, pl.ds(cc, tm, stride=ROW_SUB), :] * g[:, kk:kk + 1]
        pieces.append(piece)
    ffn = jnp.concatenate(pieces, axis=1)
    o_ref[...] = xmid_ref[...] + m[5:6] * _rms(ffn, g_ref[...])


def _combine_call(dest_flat, gates, x_mid, mod3, gpost_ffn, y_rows):
    tm = MOVE_TILE
    tiles_per_batch = SEQ // tm
    n_steps = TOKENS // tm
    return pl.pallas_call(
        _combine_kernel,
        out_shape=jax.ShapeDtypeStruct((TOKENS, D_MODEL), _F32),
        grid=(n_steps,),
        in_specs=[pl.BlockSpec((tm * TOP_K,), lambda i: (i,), memory_space=pltpu.SMEM),
                  pl.BlockSpec((tm * TOP_K,), lambda i: (jnp.minimum(i + 1, n_steps - 1),),
                               memory_space=pltpu.SMEM),
                  pl.BlockSpec((tm, LANES), lambda i: (i, 0)),
                  pl.BlockSpec((tm, D_MODEL), lambda i: (i, 0)),
                  pl.BlockSpec((1, N_MOD, D_MODEL), lambda i: (i // tiles_per_batch, 0, 0)),
                  pl.BlockSpec((1, D_MODEL), lambda i: (0, 0)),
                  pl.BlockSpec(memory_space=pl.ANY)],
        out_specs=pl.BlockSpec((tm, D_MODEL), lambda i: (i, 0)),
        scratch_shapes=[pltpu.VMEM((2, TOP_K, tm * ROW_SUB, LANES), _F32),
                        pltpu.SemaphoreType.DMA((2,))],
        compiler_params=pltpu.CompilerParams(dimension_semantics=("arbitrary",),
                                             vmem_limit_bytes=VMEM_LIMIT),
        name="combine",
    )(dest_flat, dest_flat, gates, x_mid, mod3, gpost_ffn, y_rows)


def _rope_tables():
    pos = jnp.arange(SEQ, dtype=_F32)
    pos_row = jnp.floor(pos / GRID_W)
    pos_col = pos - pos_row * GRID_W
    n_freq = HEAD_DIM // 4
    inv_freq = ROPE_THETA ** (-jnp.arange(n_freq, dtype=_F32) / n_freq)
    d = jnp.arange(LANES) % HEAD_DIM
    f = inv_freq[d % n_freq]
    ang = jnp.where((d < HEAD_DIM // 2)[None, :], pos_row[:, None] * f[None, :], pos_col[:, None] * f[None, :])
    sign = jnp.where((d % (HEAD_DIM // 2)) < n_freq, -1.0, 1.0).astype(_F32)
    return jnp.cos(ang), jnp.sin(ang) * sign[None, :]


def _perm_heads(a, axis):
    shape = a.shape
    a = a.reshape(shape[:axis] + (N_HEADS, HEAD_DIM) + shape[axis + 1:])
    a = jnp.take(a, jnp.array(HEAD_PERM), axis=axis)
    return a.reshape(shape)


def kernel(x, c, ctx, c_ctx, w_ada, b_ada, g_pre_mix, g_post_mix, g_pre_ffn, g_post_ffn, w_in, b_in, attn_sink, sgu_ln_g, sgu_ln_b, sgu_w, sgu_b, g_attn_out, g_sgu_out, w_out, b_out, w_router, b_router, w_gate_up, b_gate_up, w_down, b_down):
    l = 0
    x2 = x.reshape(TOKENS, D_MODEL)
    ctx2 = ctx.reshape(BATCH * CTX_LEN, D_MODEL)

    cc = jnp.zeros((16, D_MODEL), _F32).at[:BATCH].set(c).at[BATCH].set(c_ctx)
    mod = _ada_call(cc, w_ada[l], b_ada[l].reshape(1, -1))
    mod3 = mod.reshape(16, N_MOD, D_MODEL)

    w_in_l = w_in[l]
    b_in_l = b_in[l]
    w_in_p = jnp.concatenate([_perm_heads(w_in_l[:, :ATTN_WIDTH], 1), w_in_l[:, ATTN_WIDTH:]], axis=1)
    b_in_p = jnp.concatenate([_perm_heads(b_in_l[:ATTN_WIDTH], 0), b_in_l[ATTN_WIDTH:]], axis=0)
    cos_t, sin_t = _rope_tables()
    g_pre = g_pre_mix[l].reshape(1, -1)

    q, k, v, su, sv = _inproj_call(x2, mod3, g_pre, w_in_p.astype(_BF16), b_in_p.reshape(1, -1), cos_t, sin_t)
    kc, vc = _ctxproj_call(ctx2, mod3, g_pre, w_in_l[:, KV_START:KV_END].astype(_BF16),
                           b_in_l[KV_START:KV_END].reshape(1, -1))

    sink_p = jnp.take(attn_sink[l], jnp.array(HEAD_PERM))
    ao = _attn_call(sink_p, q, k, v, kc, vc)

    ws2 = sgu_w[l].reshape(SGU_WIDTH // LANES, 2 * SGU_CHUNK, SGU_CHUNK).astype(_BF16)
    bsf = jnp.repeat(sgu_b[l].T, SGU_HEAD_DIM, axis=1)
    w_out_l = w_out[l]
    woa = _perm_heads(w_out_l[:ATTN_WIDTH], 0).astype(_BF16)
    wos = w_out_l[ATTN_WIDTH:].astype(_BF16)
    ga = _perm_heads(g_attn_out[l], 0).reshape(1, -1)
    wr_hi = w_router[l].astype(_BF16)
    wr_lo = (w_router[l] - wr_hi.astype(_F32)).astype(_BF16)
    wr = (jnp.zeros((D_MODEL, LANES), _BF16).at[:, :N_EXPERTS].set(wr_hi)
          .at[:, N_EXPERTS:2 * N_EXPERTS].set(wr_lo))
    br = jnp.full((1, LANES), NEG_BIG, _F32).at[0, :N_EXPERTS].set(b_router[l])
    ii = jnp.arange(ROW_TILE)
    tri = (ii[None, :] < ii[:, None]).astype(_BF16)

    x_mid, h2, idx_o, gate_o, rank_o, cnt_o = _post_call(
        x2, ao, su, sv, mod3, sgu_ln_g[l].reshape(1, -1), sgu_ln_b[l].reshape(1, -1), ws2, bsf,
        ga, g_sgu_out[l].reshape(1, -1), woa, wos, b_out[l].reshape(1, -1),
        g_post_mix[l].reshape(1, -1), g_pre_ffn[l].reshape(1, -1), wr, br, tri)

    counts = cnt_o[0, :N_EXPERTS]
    padded = (counts + EXPERT_BLOCK - 1) // EXPERT_BLOCK * EXPERT_BLOCK
    pad_end = jnp.cumsum(padded)
    pad_start = pad_end - padded
    top_idx = idx_o[:, :TOP_K]
    dest = (pad_start[top_idx] + rank_o[:, :TOP_K]).astype(jnp.int32).reshape(-1)
    block_start = jnp.arange(N_EXPERT_BLOCKS, dtype=jnp.int32) * EXPERT_BLOCK
    block_expert = jnp.minimum(jnp.sum(pad_end[None, :] <= block_start[:, None], axis=1),
                               N_EXPERTS - 1).astype(jnp.int32)
    n_used = (pad_end[-1:] // EXPERT_BLOCK).astype(jnp.int32)

    free = padded - counts
    free_end = jnp.cumsum(free)
    free_start = free_end - free
    j = jnp.arange(SORTED_ROWS - TOKENS * TOP_K, dtype=jnp.int32)
    e_of_j = jnp.minimum(jnp.sum(free_end[None, :] <= j[:, None], axis=1), N_EXPERTS - 1)
    unrouted = jnp.where(j < free_end[-1],
                         pad_start[e_of_j] + counts[e_of_j] + (j - free_start[e_of_j]),
                         pad_end[-1] + (j - free_end[-1])).astype(jnp.int32)

    xs = _dispatch_call(jnp.concatenate([dest, unrouted]), h2)
    y_rows = _expert_call(block_expert, n_used, xs,
                          w_gate_up[l].astype(_BF16), b_gate_up[l].reshape(N_EXPERTS, 1, -1),
                          w_down[l].astype(_BF16), b_down[l].reshape(N_EXPERTS, 1, -1))
    out = _combine_call(dest, gate_o, x_mid, mod3, g_post_ffn[l].reshape(1, -1), y_rows)
    return out.reshape(BATCH, SEQ, D_MODEL)
```

```python
import functools

import jax
import jax.numpy as jnp
from jax import lax
from jax.experimental import pallas as pl
from jax.experimental.pallas import tpu as pltpu

D_MODEL = 1024
BATCH = 8
SEQ = 4096
TOKENS = BATCH * SEQ
GRID_W = 64
CTX_LEN = 256
N_HEADS = 8
N_KV_HEADS = 2
HEAD_DIM = 64
ATTN_WIDTH = N_HEADS * HEAD_DIM
KV_WIDTH = N_KV_HEADS * HEAD_DIM
WINDOW = 128
ATTN_BLOCK = 128
SGU_HEADS = 8
SGU_HEAD_DIM = 64
SGU_WIDTH = SGU_HEADS * SGU_HEAD_DIM
SGU_CHUNK = 128
KV_START = ATTN_WIDTH
KV_END = ATTN_WIDTH + 2 * KV_WIDTH
IN_WIDTH = KV_END + 2 * SGU_WIDTH
N_EXPERTS = 32
TOP_K = 4
D_FF_EXPERT = 1024
SWIGLU_LIMIT = 7.0
SWIGLU_ALPHA = 1.702
ROPE_THETA = 10000.0
EPS = 1e-6
N_MOD = 6

LANES = 128
NEG_BIG = -1e30
VMEM_LIMIT = 56 * 1024 * 1024

ROW_TILE = 512
MOVE_TILE = 256
EXPERT_BLOCK = 512
N_EXPERT_BLOCKS = TOKENS * TOP_K // EXPERT_BLOCK + N_EXPERTS
SORTED_ROWS = N_EXPERT_BLOCKS * EXPERT_BLOCK
HEAD_PERM = (0, 4, 1, 5, 2, 6, 3, 7)

_F32 = jnp.float32
_BF16 = jnp.bfloat16


def _rms(x, g):
    ms = jnp.mean(x * x, axis=-1, keepdims=True)
    return x * lax.rsqrt(ms + EPS) * g


ROW_SUB = D_MODEL // LANES


def _store_row_tiles(ref, x):
    n = x.shape[0]
    for cc in range(ROW_SUB):
        ref[pl.ds(cc, n, stride=ROW_SUB), :] = x[:, cc * LANES:(cc + 1) * LANES]


def _load_row_tiles(ref, n):
    return jnp.concatenate([ref[pl.ds(cc, n, stride=ROW_SUB), :] for cc in range(ROW_SUB)], axis=1)


def _gelu_tanh(x):
    c = 0.7978845608028654
    return x * (0.5 * (1.0 + jnp.tanh(c * (x + 0.044715 * (x * x * x)))))


def _ada_kernel(c_ref, w_ref, b_ref, o_ref):
    c = c_ref[...]
    a = c / (1.0 + jnp.exp(-c))
    o_ref[...] = jnp.dot(a, w_ref[...], preferred_element_type=_F32,
                         precision=lax.Precision.HIGHEST) + b_ref[...]


def _ada_call(cc, w_ada, b_ada):
    n = N_MOD * D_MODEL
    tn = 512
    return pl.pallas_call(
        _ada_kernel,
        out_shape=jax.ShapeDtypeStruct((16, n), _F32),
        grid=(n // tn,),
        in_specs=[pl.BlockSpec((16, D_MODEL), lambda j: (0, 0)),
                  pl.BlockSpec((D_MODEL, tn), lambda j: (0, j)),
                  pl.BlockSpec((1, tn), lambda j: (0, j))],
        out_specs=pl.BlockSpec((16, tn), lambda j: (0, j)),
        compiler_params=pltpu.CompilerParams(dimension_semantics=("arbitrary",),
                                             vmem_limit_bytes=VMEM_LIMIT),
        name="ada",
    )(cc, w_ada, b_ada)


def _rope(x, cos, sin_signed, first_half):
    nxt = pltpu.roll(x, LANES - 16, 1)
    prv = pltpu.roll(x, 16, 1)
    return x * cos + jnp.where(first_half, nxt, prv) * sin_signed


def _inproj_kernel(x_ref, mod_ref, g_ref, w_ref, b_ref, cos_ref, sin_ref,
                   q_ref, k_ref, v_ref, su_ref, sv_ref):
    m = mod_ref[0]
    h = _rms(x_ref[...], g_ref[...]) * (1.0 + m[1:2]) + m[0:1]
    hb = h.astype(_BF16)
    cos = cos_ref[...]
    sin = sin_ref[...]
    lane = lax.broadcasted_iota(jnp.int32, cos.shape, 1)
    first_half = (lane & 31) < 16

    q = jnp.dot(hb, w_ref[:, 0:ATTN_WIDTH], preferred_element_type=_F32) + b_ref[:, 0:ATTN_WIDTH]
    for j in range(ATTN_WIDTH // LANES):
        qj = _rope(q[:, j * LANES:(j + 1) * LANES], cos, sin, first_half)
        q_ref[:, j * LANES:(j + 1) * LANES] = (qj * (HEAD_DIM ** -0.5)).astype(_BF16)
    kv = jnp.dot(hb, w_ref[:, KV_START:KV_END], preferred_element_type=_F32) + b_ref[:, KV_START:KV_END]
    k_ref[...] = _rope(kv[:, 0:KV_WIDTH], cos, sin, first_half).astype(_BF16)
    v_ref[...] = kv[:, KV_WIDTH:].astype(_BF16)
    u0 = KV_END
    u1 = KV_END + SGU_WIDTH
    su_ref[...] = jnp.dot(hb, w_ref[:, u0:u1], preferred_element_type=_F32) + b_ref[:, u0:u1]
    sv_ref[...] = jnp.dot(hb, w_ref[:, u1:IN_WIDTH], preferred_element_type=_F32) + b_ref[:, u1:IN_WIDTH]


def _inproj_call(x2, mod3, g_pre, w_in_b, b_in2, cos_t, sin_t):
    tm = ROW_TILE
    tiles_per_batch = SEQ // tm
    row = lambda i: (i, 0)
    const = lambda i: (0, 0)
    return pl.pallas_call(
        _inproj_kernel,
        out_shape=(jax.ShapeDtypeStruct((TOKENS, ATTN_WIDTH), _BF16),
                   jax.ShapeDtypeStruct((TOKENS, KV_WIDTH), _BF16),
                   jax.ShapeDtypeStruct((TOKENS, KV_WIDTH), _BF16),
                   jax.ShapeDtypeStruct((TOKENS, SGU_WIDTH), _F32),
                   jax.ShapeDtypeStruct((TOKENS, SGU_WIDTH), _F32)),
        grid=(TOKENS // tm,),
        in_specs=[pl.BlockSpec((tm, D_MODEL), row),
                  pl.BlockSpec((1, N_MOD, D_MODEL), lambda i: (i // tiles_per_batch, 0, 0)),
                  pl.BlockSpec((1, D_MODEL), const),
                  pl.BlockSpec((D_MODEL, IN_WIDTH), const),
                  pl.BlockSpec((1, IN_WIDTH), const),
                  pl.BlockSpec((tm, LANES), lambda i: (i % tiles_per_batch, 0)),
                  pl.BlockSpec((tm, LANES), lambda i: (i % tiles_per_batch, 0))],
        out_specs=(pl.BlockSpec((tm, ATTN_WIDTH), row),
                   pl.BlockSpec((tm, KV_WIDTH), row),
                   pl.BlockSpec((tm, KV_WIDTH), row),
                   pl.BlockSpec((tm, SGU_WIDTH), row),
                   pl.BlockSpec((tm, SGU_WIDTH), row)),
        compiler_params=pltpu.CompilerParams(dimension_semantics=("arbitrary",),
                                             vmem_limit_bytes=VMEM_LIMIT),
        name="inproj",
    )(x2, mod3, g_pre, w_in_b, b_in2, cos_t, sin_t)


def _ctxproj_kernel(x_ref, mod_ref, g_ref, w_ref, b_ref, k_ref, v_ref):
    m = mod_ref[0]
    h = _rms(x_ref[...], g_ref[...]) * (1.0 + m[1:2]) + m[0:1]
    kv = jnp.dot(h.astype(_BF16), w_ref[...], preferred_element_type=_F32) + b_ref[...]
    k_ref[...] = kv[:, 0:KV_WIDTH].astype(_BF16)
    v_ref[...] = kv[:, KV_WIDTH:].astype(_BF16)


def _ctxproj_call(ctx2, mod3, g_pre, w_kv_b, b_kv2):
    tm = ROW_TILE
    rows = ctx2.shape[0]
    row = lambda i: (i, 0)
    const = lambda i: (0, 0)
    return pl.pallas_call(
        _ctxproj_kernel,
        out_shape=(jax.ShapeDtypeStruct((rows, KV_WIDTH), _BF16),
                   jax.ShapeDtypeStruct((rows, KV_WIDTH), _BF16)),
        grid=(rows // tm,),
        in_specs=[pl.BlockSpec((tm, D_MODEL), row),
                  pl.BlockSpec((1, N_MOD, D_MODEL), lambda i: (BATCH, 0, 0)),
                  pl.BlockSpec((1, D_MODEL), const),
                  pl.BlockSpec((D_MODEL, 2 * KV_WIDTH), const),
                  pl.BlockSpec((1, 2 * KV_WIDTH), const)],
        out_specs=(pl.BlockSpec((tm, KV_WIDTH), row),
                   pl.BlockSpec((tm, KV_WIDTH), row)),
        compiler_params=pltpu.CompilerParams(dimension_semantics=("arbitrary",),
                                             vmem_limit_bytes=VMEM_LIMIT),
        name="ctxproj",
    )(ctx2, mod3, g_pre, w_kv_b, b_kv2)


def _attn_kernel(sink_ref, q_ref, kp_ref, ko_ref, kn_ref, vp_ref, vo_ref, vn_ref,
                 kc_ref, vc_ref, o_ref):
    n = pl.program_id(1)
    nblk = pl.num_programs(1)
    keys = jnp.concatenate([kp_ref[...], ko_ref[...], kn_ref[...], kc_ref[...]], axis=0)
    vals = jnp.concatenate([vp_ref[...], vo_ref[...], vn_ref[...], vc_ref[...]], axis=0)
    nk = 3 * ATTN_BLOCK + CTX_LEN
    rows2 = 2 * ATTN_BLOCK
    row = lax.broadcasted_iota(jnp.int32, (rows2, nk), 0)
    col = lax.broadcasted_iota(jnp.int32, (rows2, nk), 1)
    qi = row & (ATTN_BLOCK - 1)
    off = col - ATTN_BLOCK - qi
    in_band = (off <= WINDOW) & (off >= -WINDOW)
    in_seq = ((col >= ATTN_BLOCK) | (n > 0)) & ((col < 2 * ATTN_BLOCK) | (n < nblk - 1))
    valid = (col >= 3 * ATTN_BLOCK) | (in_band & in_seq)
    row1 = lax.broadcasted_iota(jnp.int32, (rows2, 1), 0)
    lane_q = lax.broadcasted_iota(jnp.int32, (ATTN_BLOCK, LANES), 1)
    for j in range(ATTN_WIDTH // LANES):
        qg = q_ref[:, j * LANES:(j + 1) * LANES]
        zero = jnp.zeros_like(qg)
        q2 = jnp.concatenate([jnp.where(lane_q < HEAD_DIM, qg, zero),
                              jnp.where(lane_q >= HEAD_DIM, qg, zero)], axis=0)
        s = lax.dot_general(q2, keys, (((1,), (1,)), ((), ())), preferred_element_type=_F32)
        s = jnp.where(valid, s, NEG_BIG)
        sk = jnp.where(row1 < ATTN_BLOCK, sink_ref[2 * j], sink_ref[2 * j + 1])
        m = jnp.maximum(jnp.max(s, axis=-1, keepdims=True), sk)
        p = jnp.exp(s - m)
        denom = jnp.sum(p, axis=-1, keepdims=True) + jnp.exp(sk - m)
        o2 = jnp.dot(p.astype(_BF16), vals, preferred_element_type=_F32) / denom
        og = jnp.where(lane_q < HEAD_DIM, o2[0:ATTN_BLOCK], o2[ATTN_BLOCK:])
        o_ref[:, j * LANES:(j + 1) * LANES] = og.astype(_BF16)


def _attn_call(sink_p, q, k, v, kc, vc):
    nblk = SEQ // ATTN_BLOCK
    own = lambda b, n: (b * nblk + n, 0)
    prev = lambda b, n: (b * nblk + jnp.maximum(n - 1, 0), 0)
    nxt = lambda b, n: (b * nblk + jnp.minimum(n + 1, nblk - 1), 0)
    ctx = lambda b, n: (b, 0)
    kvb = (ATTN_BLOCK, KV_WIDTH)
    return pl.pallas_call(
        _attn_kernel,
        out_shape=jax.ShapeDtypeStruct((TOKENS, ATTN_WIDTH), _BF16),
        grid=(BATCH, nblk),
        in_specs=[pl.BlockSpec(memory_space=pltpu.SMEM),
                  pl.BlockSpec((ATTN_BLOCK, ATTN_WIDTH), own),
                  pl.BlockSpec(kvb, prev), pl.BlockSpec(kvb, own), pl.BlockSpec(kvb, nxt),
                  pl.BlockSpec(kvb, prev), pl.BlockSpec(kvb, own), pl.BlockSpec(kvb, nxt),
                  pl.BlockSpec((CTX_LEN, KV_WIDTH), ctx),
                  pl.BlockSpec((CTX_LEN, KV_WIDTH), ctx)],
        out_specs=pl.BlockSpec((ATTN_BLOCK, ATTN_WIDTH), own),
        compiler_params=pltpu.CompilerParams(dimension_semantics=("arbitrary", "arbitrary"),
                                             vmem_limit_bytes=VMEM_LIMIT),
        name="attn",
    )(sink_p, q, k, k, k, v, v, v, kc, vc)


def _post_kernel(x_ref, ao_ref, su_ref, sv_ref, mod_ref, lng_ref, lnb_ref, ws_ref, bs_ref,
                 ga_ref, gs_ref, woa_ref, wos_ref, bo_ref, gpost_ref, gpre_ref, wr_ref, br_ref,
                 tri_ref,
                 xmid_ref, h2_ref, idx_ref, gate_ref, rank_ref, cnt_ref,
                 mixed_ref, carry_ref):
    tm = x_ref.shape[0]
    m = mod_ref[0]

    @pl.when(pl.program_id(0) == 0)
    def _():
        carry_ref[...] = jnp.zeros_like(carry_ref)

    gv = _gelu_tanh(sv_ref[...])
    mu = jnp.mean(gv, axis=-1, keepdims=True)
    gc = gv - mu
    var = jnp.mean(gc * gc, axis=-1, keepdims=True)
    vb = (gc * lax.rsqrt(var + EPS) * lng_ref[...] + lnb_ref[...]).astype(_BF16)
    lane = lax.broadcasted_iota(jnp.int32, (SGU_CHUNK, LANES), 1)
    for c in range(tm // SGU_CHUNK):
        r0 = c * SGU_CHUNK
        for p in range(SGU_WIDTH // LANES):
            l0 = p * LANES
            r = jnp.dot(ws_ref[p], vb[r0:r0 + SGU_CHUNK, l0:l0 + LANES], preferred_element_type=_F32)
            mixed = jnp.where(lane < SGU_HEAD_DIM, r[0:SGU_CHUNK], r[SGU_CHUNK:])
            mixed_ref[r0:r0 + SGU_CHUNK, l0:l0 + LANES] = mixed + bs_ref[:, l0:l0 + LANES]
    sgu_o = _gelu_tanh(su_ref[...]) * mixed_ref[...]

    oa = _rms(ao_ref[...].astype(_F32), ga_ref[...]).astype(_BF16)
    os_ = _rms(sgu_o, gs_ref[...]).astype(_BF16)
    mix = (jnp.dot(oa, woa_ref[...], preferred_element_type=_F32)
           + jnp.dot(os_, wos_ref[...], preferred_element_type=_F32) + bo_ref[...])
    x_mid = x_ref[...] + m[2:3] * _rms(mix, gpost_ref[...])
    xmid_ref[...] = x_mid
    h2 = _rms(x_mid, gpre_ref[...]) * (1.0 + m[4:5]) + m[3:4]
    _store_row_tiles(h2_ref, h2)

    h_hi = h2.astype(_BF16)
    h_lo = (h2 - h_hi.astype(_F32)).astype(_BF16)
    r = (jnp.dot(h_hi, wr_ref[...], preferred_element_type=_F32)
         + jnp.dot(h_lo, wr_ref[...], preferred_element_type=_F32))
    lg = r + pltpu.roll(r, LANES - N_EXPERTS, 1) + br_ref[...]
    lane_r = lax.broadcasted_iota(jnp.int32, lg.shape, 1)
    lane_f = lane_r.astype(_F32)
    tops, hots = [], []
    for _k in range(TOP_K):
        mx = jnp.max(lg, axis=-1, keepdims=True)
        pick = jnp.min(jnp.where(lg == mx, lane_f, float(LANES)), axis=-1, keepdims=True)
        hot = lane_f == pick
        tops.append((mx, pick))
        hots.append(hot)
        lg = jnp.where(hot, 2.0 * NEG_BIG, lg)
    es = [jnp.exp(t[0] - tops[0][0]) for t in tops]
    esum = es[0] + es[1] + es[2] + es[3]
    multi = jnp.zeros(lg.shape, _F32)
    for hot in hots:
        multi = multi + jnp.where(hot, 1.0, 0.0)
    cum = jnp.dot(tri_ref[...], multi.astype(_BF16), preferred_element_type=_F32) + carry_ref[...]
    idx_o = jnp.zeros(lg.shape, _F32)
    gate_o = jnp.zeros(lg.shape, _F32)
    rank_o = jnp.zeros(lg.shape, _F32)
    for kk in range(TOP_K):
        sel = lane_r == kk
        rk = jnp.sum(jnp.where(hots[kk], cum, 0.0), axis=-1, keepdims=True)
        idx_o = jnp.where(sel, tops[kk][1], idx_o)
        gate_o = jnp.where(sel, es[kk] / esum, gate_o)
        rank_o = jnp.where(sel, rk, rank_o)
    idx_ref[...] = idx_o.astype(jnp.int32)
    gate_ref[...] = gate_o
    rank_ref[...] = rank_o.astype(jnp.int32)
    carry_ref[...] += jnp.sum(multi, axis=0, keepdims=True)
    cnt_ref[...] = carry_ref[...].astype(jnp.int32)


def _post_call(x2, ao, su, sv, mod3, lng, lnb, ws2, bsf, ga, gs, woa, wos, bo, gpost, gpre,
               wr, br, tri):
    tm = ROW_TILE
    tiles_per_batch = SEQ // tm
    row = lambda i: (i, 0)
    const = lambda i: (0, 0)
    const3 = lambda i: (0, 0, 0)
    return pl.pallas_call(
        _post_kernel,
        out_shape=(jax.ShapeDtypeStruct((TOKENS, D_MODEL), _F32),
                   jax.ShapeDtypeStruct((TOKENS * ROW_SUB, LANES), _F32),
                   jax.ShapeDtypeStruct((TOKENS, LANES), jnp.int32),
                   jax.ShapeDtypeStruct((TOKENS, LANES), _F32),
                   jax.ShapeDtypeStruct((TOKENS, LANES), jnp.int32),
                   jax.ShapeDtypeStruct((1, LANES), jnp.int32)),
        grid=(TOKENS // tm,),
        in_specs=[pl.BlockSpec((tm, D_MODEL), row),
                  pl.BlockSpec((tm, ATTN_WIDTH), row),
                  pl.BlockSpec((tm, SGU_WIDTH), row),
                  pl.BlockSpec((tm, SGU_WIDTH), row),
                  pl.BlockSpec((1, N_MOD, D_MODEL), lambda i: (i // tiles_per_batch, 0, 0)),
                  pl.BlockSpec((1, SGU_WIDTH), const),
                  pl.BlockSpec((1, SGU_WIDTH), const),
                  pl.BlockSpec((SGU_WIDTH // LANES, 2 * SGU_CHUNK, SGU_CHUNK), const3),
                  pl.BlockSpec((SGU_CHUNK, SGU_WIDTH), const),
                  pl.BlockSpec((1, ATTN_WIDTH), const),
                  pl.BlockSpec((1, SGU_WIDTH), const),
                  pl.BlockSpec((ATTN_WIDTH, D_MODEL), const),
                  pl.BlockSpec((SGU_WIDTH, D_MODEL), const),
                  pl.BlockSpec((1, D_MODEL), const),
                  pl.BlockSpec((1, D_MODEL), const),
                  pl.BlockSpec((1, D_MODEL), const),
                  pl.BlockSpec((D_MODEL, LANES), const),
                  pl.BlockSpec((1, LANES), const),
                  pl.BlockSpec((tm, tm), const)],
        out_specs=(pl.BlockSpec((tm, D_MODEL), row),
                   pl.BlockSpec((tm * ROW_SUB, LANES), row),
                   pl.BlockSpec((tm, LANES), row),
                   pl.BlockSpec((tm, LANES), row),
                   pl.BlockSpec((tm, LANES), row),
                   pl.BlockSpec((1, LANES), const)),
        scratch_shapes=[pltpu.VMEM((tm, SGU_WIDTH), _F32),
                        pltpu.VMEM((1, LANES), _F32)],
        compiler_params=pltpu.CompilerParams(dimension_semantics=("arbitrary",),
                                             vmem_limit_bytes=VMEM_LIMIT),
        name="post",
    )(x2, ao, su, sv, mod3, lng, lnb, ws2, bsf, ga, gs, woa, wos, bo, gpost, gpre, wr, br, tri)


def _dispatch_kernel(dest_ref, h2_ref, xs_ref, zero_ref, sem):
    tm = h2_ref.shape[0] // ROW_SUB
    n_token_steps = TOKENS // tm
    i = pl.program_id(0)

    def scatter_rows(src_ref, rows_per_dest):
        def issue(r, carry):
            for kk in range(TOP_K):
                d = pl.multiple_of(dest_ref[r * TOP_K + kk] * ROW_SUB, ROW_SUB)
                src_row = pl.multiple_of(r * ROW_SUB, ROW_SUB) if rows_per_dest == TOP_K else 0
                pltpu.make_async_copy(src_ref.at[pl.ds(src_row, ROW_SUB)],
                                      xs_ref.at[pl.ds(d, ROW_SUB)], sem).start(priority=kk % 2)
            return carry

        lax.fori_loop(0, tm, issue, 0)
        for kk in range(TOP_K):
            pltpu.make_async_copy(h2_ref, xs_ref.at[pl.ds(0, tm * ROW_SUB)], sem).wait()

    @pl.when(i < n_token_steps)
    def _():
        scatter_rows(h2_ref, TOP_K)

    @pl.when(i >= n_token_steps)
    def _():
        zero_ref[...] = jnp.zeros_like(zero_ref)
        scatter_rows(zero_ref, 1)


def _dispatch_call(dest_flat, h2):
    tm = MOVE_TILE
    n_token_steps = TOKENS // tm
    n_steps = dest_flat.shape[0] // (tm * TOP_K)
    return pl.pallas_call(
        _dispatch_kernel,
        out_shape=jax.ShapeDtypeStruct((SORTED_ROWS * ROW_SUB, LANES), _F32),
        grid=(n_steps,),
        in_specs=[pl.BlockSpec((tm * TOP_K,), lambda i: (i,), memory_space=pltpu.SMEM),
                  pl.BlockSpec((tm * ROW_SUB, LANES),
                               lambda i: (jnp.minimum(i, n_token_steps - 1), 0))],
        out_specs=pl.BlockSpec(memory_space=pl.ANY),
        scratch_shapes=[pltpu.VMEM((ROW_SUB, LANES), _F32),
                        pltpu.SemaphoreType.DMA],
        compiler_params=pltpu.CompilerParams(dimension_semantics=("arbitrary",),
                                             vmem_limit_bytes=VMEM_LIMIT),
        name="dispatch",
    )(dest_flat, h2)


def _expert_kernel(be_ref, nu_ref, xs_ref, wgu_ref, bgu_ref, wd_ref, bd_ref, y_ref):
    del be_ref

    @pl.when(pl.program_id(0) < nu_ref[0])
    def _():
        x = _load_row_tiles(xs_ref, EXPERT_BLOCK).astype(_BF16)
        gu = jnp.dot(x, wgu_ref[0].astype(_BF16), preferred_element_type=_F32) + bgu_ref[0]
        gate = jnp.minimum(gu[:, 0:D_FF_EXPERT], SWIGLU_LIMIT)
        up = jnp.clip(gu[:, D_FF_EXPERT:], -SWIGLU_LIMIT, SWIGLU_LIMIT)
        act = (up + 1.0) * gate * (1.0 / (1.0 + jnp.exp(-SWIGLU_ALPHA * gate)))
        y = jnp.dot(act.astype(_BF16), wd_ref[0].astype(_BF16), preferred_element_type=_F32) + bd_ref[0]
        _store_row_tiles(y_ref, y)

    @pl.when(pl.program_id(0) >= nu_ref[0])
    def _():
        y_ref[...] = jnp.zeros_like(y_ref)


def _expert_call(block_expert, n_used, xs, wgu_b, bgu3, wd_b, bd3):
    tb = EXPERT_BLOCK
    live = lambda b, be, nu: (jnp.minimum(b, nu[0] - 1), 0)
    wsel = lambda b, be, nu: (be[b], 0, 0)
    grid_spec = pltpu.PrefetchScalarGridSpec(
        num_scalar_prefetch=2,
        grid=(N_EXPERT_BLOCKS,),
        in_specs=[pl.BlockSpec((tb * ROW_SUB, LANES), live),
                  pl.BlockSpec((1, D_MODEL, 2 * D_FF_EXPERT), wsel),
                  pl.BlockSpec((1, 1, 2 * D_FF_EXPERT), wsel),
                  pl.BlockSpec((1, D_FF_EXPERT, D_MODEL), wsel),
                  pl.BlockSpec((1, 1, D_MODEL), wsel)],
        out_specs=pl.BlockSpec((tb * ROW_SUB, LANES), lambda b, be, nu: (b, 0)))
    return pl.pallas_call(
        _expert_kernel,
        out_shape=jax.ShapeDtypeStruct((SORTED_ROWS * ROW_SUB, LANES), _F32),
        grid_spec=grid_spec,
        compiler_params=pltpu.CompilerParams(dimension_semantics=("arbitrary",),
                                             vmem_limit_bytes=VMEM_LIMIT),
        name="expert",
    )(block_expert, n_used, xs, wgu_b, bgu3, wd_b, bd3)


def _combine_kernel(dest_ref, dest_next_ref, gate_ref, xmid_ref, mod_ref, g_ref, y_ref, o_ref,
                    buf_ref, sem):
    tm = xmid_ref.shape[0]
    m = mod_ref[0]
    i = pl.program_id(0)
    n = pl.num_programs(0)

    def gather_rows(idx_ref, slot):
        def issue(r, carry):
            for kk in range(TOP_K):
                d = pl.multiple_of(idx_ref[r * TOP_K + kk] * ROW_SUB, ROW_SUB)
                pltpu.make_async_copy(
                    y_ref.at[pl.ds(d, ROW_SUB)],
                    buf_ref.at[slot, kk, pl.ds(pl.multiple_of(r * ROW_SUB, ROW_SUB), ROW_SUB)],
                    sem.at[slot]).start(priority=kk % 2)
            return carry

        lax.fori_loop(0, tm, issue, 0)

    @pl.when(i == 0)
    def _():
        gather_rows(dest_ref, 0)

    @pl.when(i + 1 < n)
    def _():
        gather_rows(dest_next_ref, (i + 1) % 2)

    slot = i % 2
    for kk in range(TOP_K):
        pltpu.make_async_copy(y_ref.at[pl.ds(0, tm * ROW_SUB)], buf_ref.at[slot, kk],
                              sem.at[slot]).wait()
    g = gate_ref[...]
    pieces = []
    for cc in range(ROW_SUB):
        piece = buf_ref[slot, 0, pl.ds(cc, tm, stride=ROW_SUB), :] * g[:, 0:1]
        for kk in range(1, TOP_K):
            piece = piece + buf_ref[slot, kk, pl.ds(cc, tm, stride=ROW_SUB), :] * g[:, kk:kk + 1]
        pieces.append(piece)
    ffn = jnp.concatenate(pieces, axis=1)
    o_ref[...] = xmid_ref[...] + m[5:6] * _rms(ffn, g_ref[...])


def _combine_call(dest_flat, gates, x_mid, mod3, gpost_ffn, y_rows):
    tm = MOVE_TILE
    tiles_per_batch = SEQ // tm
    n_steps = TOKENS // tm
    return pl.pallas_call(
        _combine_kernel,
        out_shape=jax.ShapeDtypeStruct((TOKENS, D_MODEL), _F32),
        grid=(n_steps,),
        in_specs=[pl.BlockSpec((tm * TOP_K,), lambda i: (i,), memory_space=pltpu.SMEM),
                  pl.BlockSpec((tm * TOP_K,), lambda i: (jnp.minimum(i + 1, n_steps - 1),),
                               memory_space=pltpu.SMEM),
                  pl.BlockSpec((tm, LANES), lambda i: (i, 0)),
                  pl.BlockSpec((tm, D_MODEL), lambda i: (i, 0)),
                  pl.BlockSpec((1, N_MOD, D_MODEL), lambda i: (i // tiles_per_batch, 0, 0)),
                  pl.BlockSpec((1, D_MODEL), lambda i: (0, 0)),
                  pl.BlockSpec(memory_space=pl.ANY)],
        out_specs=pl.BlockSpec((tm, D_MODEL), lambda i: (i, 0)),
        scratch_shapes=[pltpu.VMEM((2, TOP_K, tm * ROW_SUB, LANES), _F32),
                        pltpu.SemaphoreType.DMA((2,))],
        compiler_params=pltpu.CompilerParams(dimension_semantics=("arbitrary",),
                                             vmem_limit_bytes=VMEM_LIMIT),
        name="combine",
    )(dest_flat, dest_flat, gates, x_mid, mod3, gpost_ffn, y_rows)


def _rope_tables():
    pos = jnp.arange(SEQ, dtype=_F32)
    pos_row = jnp.floor(pos / GRID_W)
    pos_col = pos - pos_row * GRID_W
    n_freq = HEAD_DIM // 4
    inv_freq = ROPE_THETA ** (-jnp.arange(n_freq, dtype=_F32) / n_freq)
    d = jnp.arange(LANES) % HEAD_DIM
    f = inv_freq[d % n_freq]
    ang = jnp.where((d < HEAD_DIM // 2)[None, :], pos_row[:, None] * f[None, :], pos_col[:, None] * f[None, :])
    sign = jnp.where((d % (HEAD_DIM // 2)) < n_freq, -1.0, 1.0).astype(_F32)
    return jnp.cos(ang), jnp.sin(ang) * sign[None, :]


def _perm_heads(a, axis):
    shape = a.shape
    a = a.reshape(shape[:axis] + (N_HEADS, HEAD_DIM) + shape[axis + 1:])
    a = jnp.take(a, jnp.array(HEAD_PERM), axis=axis)
    return a.reshape(shape)


def kernel(x, c, ctx, c_ctx, w_ada, b_ada, g_pre_mix, g_post_mix, g_pre_ffn, g_post_ffn, w_in, b_in, attn_sink, sgu_ln_g, sgu_ln_b, sgu_w, sgu_b, g_attn_out, g_sgu_out, w_out, b_out, w_router, b_router, w_gate_up, b_gate_up, w_down, b_down):
    l = 0
    x2 = x.reshape(TOKENS, D_MODEL)
    ctx2 = ctx.reshape(BATCH * CTX_LEN, D_MODEL)

    cc = jnp.zeros((16, D_MODEL), _F32).at[:BATCH].set(c).at[BATCH].set(c_ctx)
    mod = _ada_call(cc, w_ada[l], b_ada[l].reshape(1, -1))
    mod3 = mod.reshape(16, N_MOD, D_MODEL)

    w_in_l = w_in[l]
    b_in_l = b_in[l]
    w_in_p = jnp.concatenate([_perm_heads(w_in_l[:, :ATTN_WIDTH], 1), w_in_l[:, ATTN_WIDTH:]], axis=1)
    b_in_p = jnp.concatenate([_perm_heads(b_in_l[:ATTN_WIDTH], 0), b_in_l[ATTN_WIDTH:]], axis=0)
    cos_t, sin_t = _rope_tables()
    g_pre = g_pre_mix[l].reshape(1, -1)

    q, k, v, su, sv = _inproj_call(x2, mod3, g_pre, w_in_p.astype(_BF16), b_in_p.reshape(1, -1), cos_t, sin_t)
    kc, vc = _ctxproj_call(ctx2, mod3, g_pre, w_in_l[:, KV_START:KV_END].astype(_BF16),
                           b_in_l[KV_START:KV_END].reshape(1, -1))

    sink_p = jnp.take(attn_sink[l], jnp.array(HEAD_PERM))
    ao = _attn_call(sink_p, q, k, v, kc, vc)

    ws2 = sgu_w[l].reshape(SGU_WIDTH // LANES, 2 * SGU_CHUNK, SGU_CHUNK).astype(_BF16)
    bsf = jnp.repeat(sgu_b[l].T, SGU_HEAD_DIM, axis=1)
    w_out_l = w_out[l]
    woa = _perm_heads(w_out_l[:ATTN_WIDTH], 0).astype(_BF16)
    wos = w_out_l[ATTN_WIDTH:].astype(_BF16)
    ga = _perm_heads(g_attn_out[l], 0).reshape(1, -1)
    wr_hi = w_router[l].astype(_BF16)
    wr_lo = (w_router[l] - wr_hi.astype(_F32)).astype(_BF16)
    wr = (jnp.zeros((D_MODEL, LANES), _BF16).at[:, :N_EXPERTS].set(wr_hi)
          .at[:, N_EXPERTS:2 * N_EXPERTS].set(wr_lo))
    br = jnp.full((1, LANES), NEG_BIG, _F32).at[0, :N_EXPERTS].set(b_router[l])
    ii = jnp.arange(ROW_TILE)
    tri = (ii[None, :] < ii[:, None]).astype(_BF16)

    x_mid, h2, idx_o, gate_o, rank_o, cnt_o = _post_call(
        x2, ao, su, sv, mod3, sgu_ln_g[l].reshape(1, -1), sgu_ln_b[l].reshape(1, -1), ws2, bsf,
        ga, g_sgu_out[l].reshape(1, -1), woa, wos, b_out[l].reshape(1, -1),
        g_post_mix[l].reshape(1, -1), g_pre_ffn[l].reshape(1, -1), wr, br, tri)

    counts = cnt_o[0, :N_EXPERTS]
    padded = (counts + EXPERT_BLOCK - 1) // EXPERT_BLOCK * EXPERT_BLOCK
    pad_end = jnp.cumsum(padded)
    pad_start = pad_end - padded
    top_idx = idx_o[:, :TOP_K]
    dest = (pad_start[top_idx] + rank_o[:, :TOP_K]).astype(jnp.int32).reshape(-1)
    block_start = jnp.arange(N_EXPERT_BLOCKS, dtype=jnp.int32) * EXPERT_BLOCK
    block_expert = jnp.minimum(jnp.sum(pad_end[None, :] <= block_start[:, None], axis=1),
                               N_EXPERTS - 1).astype(jnp.int32)
    n_used = (pad_end[-1:] // EXPERT_BLOCK).astype(jnp.int32)

    free = padded - counts
    free_end = jnp.cumsum(free)
    free_start = free_end - free
    j = jnp.arange(SORTED_ROWS - TOKENS * TOP_K, dtype=jnp.int32)
    e_of_j = jnp.minimum(jnp.sum(free_end[:, None] <= j[None, :], axis=0), N_EXPERTS - 1)
    unrouted = jnp.where(j < free_end[-1],
                         pad_start[e_of_j] + counts[e_of_j] + (j - free_start[e_of_j]),
                         pad_end[-1] + (j - free_end[-1])).astype(jnp.int32)

    xs = _dispatch_call(jnp.concatenate([dest, unrouted]), h2)
    y_rows = _expert_call(block_expert, n_used, xs,
                          w_gate_up[l], b_gate_up[l].reshape(N_EXPERTS, 1, -1),
                          w_down[l], b_down[l].reshape(N_EXPERTS, 1, -1))
    out = _combine_call(dest, gate_o, x_mid, mod3, g_post_ffn[l].reshape(1, -1), y_rows)
    return out.reshape(BATCH, SEQ, D_MODEL)
```

```python
import functools

import jax
import jax.numpy as jnp
from jax import lax
from jax.experimental import pallas as pl
from jax.experimental.pallas import tpu as pltpu

D_MODEL = 1024
BATCH = 8
SEQ = 4096
TOKENS = BATCH * SEQ
GRID_W = 64
CTX_LEN = 256
N_HEADS = 8
N_KV_HEADS = 2
HEAD_DIM = 64
ATTN_WIDTH = N_HEADS * HEAD_DIM
KV_WIDTH = N_KV_HEADS * HEAD_DIM
WINDOW = 128
ATTN_BLOCK = 128
SGU_HEADS = 8
SGU_HEAD_DIM = 64
SGU_WIDTH = SGU_HEADS * SGU_HEAD_DIM
SGU_CHUNK = 128
KV_START = ATTN_WIDTH
KV_END = ATTN_WIDTH + 2 * KV_WIDTH
IN_WIDTH = KV_END + 2 * SGU_WIDTH
N_EXPERTS = 32
TOP_K = 4
D_FF_EXPERT = 1024
SWIGLU_LIMIT = 7.0
SWIGLU_ALPHA = 1.702
ROPE_THETA = 10000.0
EPS = 1e-6
N_MOD = 6

LANES = 128
NEG_BIG = -1e30
VMEM_LIMIT = 56 * 1024 * 1024

ROW_TILE = 512
MOVE_TILE = 256
EXPERT_BLOCK = 512
N_EXPERT_BLOCKS = TOKENS * TOP_K // EXPERT_BLOCK + N_EXPERTS
SORTED_ROWS = N_EXPERT_BLOCKS * EXPERT_BLOCK
HEAD_PERM = (0, 4, 1, 5, 2, 6, 3, 7)

_F32 = jnp.float32
_BF16 = jnp.bfloat16


def _rms(x, g):
    ms = jnp.mean(x * x, axis=-1, keepdims=True)
    return x * lax.rsqrt(ms + EPS) * g


ROW_SUB = D_MODEL // LANES


def _store_row_tiles(ref, x):
    n = x.shape[0]
    for cc in range(ROW_SUB):
        ref[pl.ds(cc, n, stride=ROW_SUB), :] = x[:, cc * LANES:(cc + 1) * LANES]


def _load_row_tiles(ref, n):
    return jnp.concatenate([ref[pl.ds(cc, n, stride=ROW_SUB), :] for cc in range(ROW_SUB)], axis=1)


def _gelu_tanh(x):
    c = 0.7978845608028654
    return x * (0.5 * (1.0 + jnp.tanh(c * (x + 0.044715 * (x * x * x)))))


def _ada_kernel(c_ref, w_ref, b_ref, o_ref):
    c = c_ref[...]
    a = c / (1.0 + jnp.exp(-c))
    o_ref[...] = jnp.dot(a, w_ref[...], preferred_element_type=_F32,
                         precision=lax.Precision.HIGHEST) + b_ref[...]


def _ada_call(cc, w_ada, b_ada):
    n = N_MOD * D_MODEL
    tn = 512
    return pl.pallas_call(
        _ada_kernel,
        out_shape=jax.ShapeDtypeStruct((16, n), _F32),
        grid=(n // tn,),
        in_specs=[pl.BlockSpec((16, D_MODEL), lambda j: (0, 0)),
                  pl.BlockSpec((D_MODEL, tn), lambda j: (0, j)),
                  pl.BlockSpec((1, tn), lambda j: (0, j))],
        out_specs=pl.BlockSpec((16, tn), lambda j: (0, j)),
        compiler_params=pltpu.CompilerParams(dimension_semantics=("arbitrary",),
                                             vmem_limit_bytes=VMEM_LIMIT),
        name="ada",
    )(cc, w_ada, b_ada)


def _rope(x, cos, sin_signed, first_half):
    nxt = pltpu.roll(x, LANES - 16, 1)
    prv = pltpu.roll(x, 16, 1)
    return x * cos + jnp.where(first_half, nxt, prv) * sin_signed


def _inproj_kernel(x_ref, mod_ref, g_ref, w_ref, b_ref, cos_ref, sin_ref,
                   q_ref, k_ref, v_ref, su_ref, sv_ref):
    m = mod_ref[0]
    h = _rms(x_ref[...], g_ref[...]) * (1.0 + m[1:2]) + m[0:1]
    hb = h.astype(_BF16)
    cos = cos_ref[...]
    sin = sin_ref[...]
    lane = lax.broadcasted_iota(jnp.int32, cos.shape, 1)
    first_half = (lane & 31) < 16

    q = jnp.dot(hb, w_ref[:, 0:ATTN_WIDTH], preferred_element_type=_F32) + b_ref[:, 0:ATTN_WIDTH]
    for j in range(ATTN_WIDTH // LANES):
        qj = _rope(q[:, j * LANES:(j + 1) * LANES], cos, sin, first_half)
        q_ref[:, j * LANES:(j + 1) * LANES] = (qj * (HEAD_DIM ** -0.5)).astype(_BF16)
    kv = jnp.dot(hb, w_ref[:, KV_START:KV_END], preferred_element_type=_F32) + b_ref[:, KV_START:KV_END]
    k_ref[...] = _rope(kv[:, 0:KV_WIDTH], cos, sin, first_half).astype(_BF16)
    v_ref[...] = kv[:, KV_WIDTH:].astype(_BF16)
    u0 = KV_END
    u1 = KV_END + SGU_WIDTH
    su_ref[...] = jnp.dot(hb, w_ref[:, u0:u1], preferred_element_type=_F32) + b_ref[:, u0:u1]
    sv_ref[...] = jnp.dot(hb, w_ref[:, u1:IN_WIDTH], preferred_element_type=_F32) + b_ref[:, u1:IN_WIDTH]


def _inproj_call(x2, mod3, g_pre, w_in_b, b_in2, cos_t, sin_t):
    tm = ROW_TILE
    tiles_per_batch = SEQ // tm
    row = lambda i: (i, 0)
    const = lambda i: (0, 0)
    return pl.pallas_call(
        _inproj_kernel,
        out_shape=(jax.ShapeDtypeStruct((TOKENS, ATTN_WIDTH), _BF16),
                   jax.ShapeDtypeStruct((TOKENS, KV_WIDTH), _BF16),
                   jax.ShapeDtypeStruct((TOKENS, KV_WIDTH), _BF16),
                   jax.ShapeDtypeStruct((TOKENS, SGU_WIDTH), _F32),
                   jax.ShapeDtypeStruct((TOKENS, SGU_WIDTH), _F32)),
        grid=(TOKENS // tm,),
        in_specs=[pl.BlockSpec((tm, D_MODEL), row),
                  pl.BlockSpec((1, N_MOD, D_MODEL), lambda i: (i // tiles_per_batch, 0, 0)),
                  pl.BlockSpec((1, D_MODEL), const),
                  pl.BlockSpec((D_MODEL, IN_WIDTH), const),
                  pl.BlockSpec((1, IN_WIDTH), const),
                  pl.BlockSpec((tm, LANES), lambda i: (i % tiles_per_batch, 0)),
                  pl.BlockSpec((tm, LANES), lambda i: (i % tiles_per_batch, 0))],
        out_specs=(pl.BlockSpec((tm, ATTN_WIDTH), row),
                   pl.BlockSpec((tm, KV_WIDTH), row),
                   pl.BlockSpec((tm, KV_WIDTH), row),
                   pl.BlockSpec((tm, SGU_WIDTH), row),
                   pl.BlockSpec((tm, SGU_WIDTH), row)),
        compiler_params=pltpu.CompilerParams(dimension_semantics=("arbitrary",),
                                             vmem_limit_bytes=VMEM_LIMIT),
        name="inproj",
    )(x2, mod3, g_pre, w_in_b, b_in2, cos_t, sin_t)


def _ctxproj_kernel(x_ref, mod_ref, g_ref, w_ref, b_ref, k_ref, v_ref):
    m = mod_ref[0]
    h = _rms(x_ref[...], g_ref[...]) * (1.0 + m[1:2]) + m[0:1]
    kv = jnp.dot(h.astype(_BF16), w_ref[...], preferred_element_type=_F32) + b_ref[...]
    k_ref[...] = kv[:, 0:KV_WIDTH].astype(_BF16)
    v_ref[...] = kv[:, KV_WIDTH:].astype(_BF16)


def _ctxproj_call(ctx2, mod3, g_pre, w_kv_b, b_kv2):
    tm = ROW_TILE
    rows = ctx2.shape[0]
    row = lambda i: (i, 0)
    const = lambda i: (0, 0)
    return pl.pallas_call(
        _ctxproj_kernel,
        out_shape=(jax.ShapeDtypeStruct((rows, KV_WIDTH), _BF16),
                   jax.ShapeDtypeStruct((rows, KV_WIDTH), _BF16)),
        grid=(rows // tm,),
        in_specs=[pl.BlockSpec((tm, D_MODEL), row),
                  pl.BlockSpec((1, N_MOD, D_MODEL), lambda i: (BATCH, 0, 0)),
                  pl.BlockSpec((1, D_MODEL), const),
                  pl.BlockSpec((D_MODEL, 2 * KV_WIDTH), const),
                  pl.BlockSpec((1, 2 * KV_WIDTH), const)],
        out_specs=(pl.BlockSpec((tm, KV_WIDTH), row),
                   pl.BlockSpec((tm, KV_WIDTH), row)),
        compiler_params=pltpu.CompilerParams(dimension_semantics=("arbitrary",),
                                             vmem_limit_bytes=VMEM_LIMIT),
        name="ctxproj",
    )(ctx2, mod3, g_pre, w_kv_b, b_kv2)


def _attn_kernel(sink_ref, q_ref, kp_ref, km_ref, kn_ref, vp_ref, vm_ref, vn_ref,
                 kc_ref, vc_ref, o_ref):
    n = pl.program_id(1)
    nstep = pl.num_programs(1)
    blk = ATTN_BLOCK
    rows2 = 2 * blk
    r = lax.broadcasted_iota(jnp.int32, (rows2, blk), 0) & (blk - 1)
    c = lax.broadcasted_iota(jnp.int32, (rows2, blk), 1)
    tri_prev = c >= r
    tri_next = c <= r
    row1 = lax.broadcasted_iota(jnp.int32, (rows2, 1), 0)
    lane_q = lax.broadcasted_iota(jnp.int32, (blk, LANES), 1)
    k_mid = km_ref[...]
    v_mid = vm_ref[...]
    pieces = [(kp_ref[...], k_mid[0:blk], k_mid[blk:], vp_ref[...], v_mid[0:blk], v_mid[blk:],
               tri_prev & (n > 0), tri_next),
              (k_mid[0:blk], k_mid[blk:], kn_ref[...], v_mid[0:blk], v_mid[blk:], vn_ref[...],
               tri_prev, tri_next & (n < nstep - 1))]
    n_grp = ATTN_WIDTH // LANES
    for t, (ka, kb, kc_, va, vb, vc_, mask_a, mask_c) in enumerate(pieces):
        keys = jnp.concatenate([ka, kb, kc_, kc_ref[...]], axis=0)
        vals = jnp.concatenate([va, vb, vc_, vc_ref[...]], axis=0)
        q_rows = []
        for j in range(n_grp):
            qg = q_ref[t * blk:(t + 1) * blk, j * LANES:(j + 1) * LANES]
            zero = jnp.zeros_like(qg)
            q_rows += [jnp.where(lane_q < HEAD_DIM, qg, zero), jnp.where(lane_q >= HEAD_DIM, qg, zero)]
        s_all = lax.dot_general(jnp.concatenate(q_rows, axis=0), keys, (((1,), (1,)), ((), ())),
                                preferred_element_type=_F32)
        p_rows, denoms = [], []
        for j in range(n_grp):
            s = s_all[j * rows2:(j + 1) * rows2]
            s = jnp.concatenate([jnp.where(mask_a, s[:, 0:blk], NEG_BIG), s[:, blk:2 * blk],
                                 jnp.where(mask_c, s[:, 2 * blk:3 * blk], NEG_BIG), s[:, 3 * blk:]],
                                axis=1)
            sk = jnp.where(row1 < blk, sink_ref[2 * j], sink_ref[2 * j + 1])
            m = jnp.maximum(jnp.max(s, axis=-1, keepdims=True), sk)
            p = jnp.exp(s - m)
            denoms.append(jnp.sum(p, axis=-1, keepdims=True) + jnp.exp(sk - m))
            p_rows.append(p.astype(_BF16))
        o_all = jnp.dot(jnp.concatenate(p_rows, axis=0), vals, preferred_element_type=_F32)
        for j in range(n_grp):
            o2 = o_all[j * rows2:(j + 1) * rows2] / denoms[j]
            og = jnp.where(lane_q < HEAD_DIM, o2[0:blk], o2[blk:])
            o_ref[t * blk:(t + 1) * blk, j * LANES:(j + 1) * LANES] = og.astype(_BF16)


def _attn_call(sink_p, q, k, v, kc, vc):
    nblk = SEQ // ATTN_BLOCK
    nstep = nblk // 2
    own = lambda b, n: (b * nstep + n, 0)
    prev = lambda b, n: (b * nblk + jnp.maximum(2 * n - 1, 0), 0)
    nxt = lambda b, n: (b * nblk + jnp.minimum(2 * n + 2, nblk - 1), 0)
    ctx = lambda b, n: (b, 0)
    kv1 = (ATTN_BLOCK, KV_WIDTH)
    kv2 = (2 * ATTN_BLOCK, KV_WIDTH)
    return pl.pallas_call(
        _attn_kernel,
        out_shape=jax.ShapeDtypeStruct((TOKENS, ATTN_WIDTH), _BF16),
        grid=(BATCH, nstep),
        in_specs=[pl.BlockSpec(memory_space=pltpu.SMEM),
                  pl.BlockSpec((2 * ATTN_BLOCK, ATTN_WIDTH), own),
                  pl.BlockSpec(kv1, prev), pl.BlockSpec(kv2, own), pl.BlockSpec(kv1, nxt),
                  pl.BlockSpec(kv1, prev), pl.BlockSpec(kv2, own), pl.BlockSpec(kv1, nxt),
                  pl.BlockSpec((CTX_LEN, KV_WIDTH), ctx),
                  pl.BlockSpec((CTX_LEN, KV_WIDTH), ctx)],
        out_specs=pl.BlockSpec((2 * ATTN_BLOCK, ATTN_WIDTH), own),
        compiler_params=pltpu.CompilerParams(dimension_semantics=("arbitrary", "arbitrary"),
                                             vmem_limit_bytes=VMEM_LIMIT),
        name="attn",
    )(sink_p, q, k, k, k, v, v, v, kc, vc)


def _post_kernel(x_ref, ao_ref, su_ref, sv_ref, mod_ref, lng_ref, lnb_ref, ws_ref, bs_ref,
                 ga_ref, gs_ref, woa_ref, wos_ref, bo_ref, gpost_ref, gpre_ref, wr_ref, br_ref,
                 tri_ref,
                 xmid_ref, h2_ref, idx_ref, gate_ref, rank_ref, cnt_ref,
                 mixed_ref, carry_ref):
    tm = x_ref.shape[0]
    m = mod_ref[0]

    @pl.when(pl.program_id(0) == 0)
    def _():
        carry_ref[...] = jnp.zeros_like(carry_ref)

    gv = _gelu_tanh(sv_ref[...])
    mu = jnp.mean(gv, axis=-1, keepdims=True)
    gc = gv - mu
    var = jnp.mean(gc * gc, axis=-1, keepdims=True)
    vb = (gc * lax.rsqrt(var + EPS) * lng_ref[...] + lnb_ref[...]).astype(_BF16)
    lane = lax.broadcasted_iota(jnp.int32, (SGU_CHUNK, LANES), 1)
    for c in range(tm // SGU_CHUNK):
        r0 = c * SGU_CHUNK
        for p in range(SGU_WIDTH // LANES):
            l0 = p * LANES
            r = jnp.dot(ws_ref[p], vb[r0:r0 + SGU_CHUNK, l0:l0 + LANES], preferred_element_type=_F32)
            mixed = jnp.where(lane < SGU_HEAD_DIM, r[0:SGU_CHUNK], r[SGU_CHUNK:])
            mixed_ref[r0:r0 + SGU_CHUNK, l0:l0 + LANES] = mixed + bs_ref[:, l0:l0 + LANES]
    sgu_o = _gelu_tanh(su_ref[...]) * mixed_ref[...]

    oa = _rms(ao_ref[...].astype(_F32), ga_ref[...]).astype(_BF16)
    os_ = _rms(sgu_o, gs_ref[...]).astype(_BF16)
    mix = (jnp.dot(oa, woa_ref[...], preferred_element_type=_F32)
           + jnp.dot(os_, wos_ref[...], preferred_element_type=_F32) + bo_ref[...])
    x_mid = x_ref[...] + m[2:3] * _rms(mix, gpost_ref[...])
    xmid_ref[...] = x_mid
    h2 = _rms(x_mid, gpre_ref[...]) * (1.0 + m[4:5]) + m[3:4]
    _store_row_tiles(h2_ref, h2)

    h_hi = h2.astype(_BF16)
    h_lo = (h2 - h_hi.astype(_F32)).astype(_BF16)
    r = (jnp.dot(h_hi, wr_ref[...], preferred_element_type=_F32)
         + jnp.dot(h_lo, wr_ref[...], preferred_element_type=_F32))
    lg = r + pltpu.roll(r, LANES - N_EXPERTS, 1) + br_ref[...]
    lane_r = lax.broadcasted_iota(jnp.int32, lg.shape, 1)
    lane_f = lane_r.astype(_F32)
    tops, hots = [], []
    for _k in range(TOP_K):
        mx = jnp.max(lg, axis=-1, keepdims=True)
        pick = jnp.min(jnp.where(lg == mx, lane_f, float(LANES)), axis=-1, keepdims=True)
        hot = lane_f == pick
        tops.append((mx, pick))
        hots.append(hot)
        lg = jnp.where(hot, 2.0 * NEG_BIG, lg)
    es = [jnp.exp(t[0] - tops[0][0]) for t in tops]
    esum = es[0] + es[1] + es[2] + es[3]
    multi = jnp.zeros(lg.shape, _F32)
    for hot in hots:
        multi = multi + jnp.where(hot, 1.0, 0.0)
    cum = jnp.dot(tri_ref[...], multi.astype(_BF16), preferred_element_type=_F32) + carry_ref[...]
    idx_o = jnp.zeros(lg.shape, _F32)
    gate_o = jnp.zeros(lg.shape, _F32)
    rank_o = jnp.zeros(lg.shape, _F32)
    for kk in range(TOP_K):
        sel = lane_r == kk
        rk = jnp.sum(jnp.where(hots[kk], cum, 0.0), axis=-1, keepdims=True)
        idx_o = jnp.where(sel, tops[kk][1], idx_o)
        gate_o = jnp.where(sel, es[kk] / esum, gate_o)
        rank_o = jnp.where(sel, rk, rank_o)
    idx_ref[...] = idx_o.astype(jnp.int32)
    gate_ref[...] = gate_o
    rank_ref[...] = rank_o.astype(jnp.int32)
    carry_ref[...] += jnp.sum(multi, axis=0, keepdims=True)
    cnt_ref[...] = carry_ref[...].astype(jnp.int32)


def _post_call(x2, ao, su, sv, mod3, lng, lnb, ws2, bsf, ga, gs, woa, wos, bo, gpost, gpre,
               wr, br, tri):
    tm = ROW_TILE
    tiles_per_batch = SEQ // tm
    row = lambda i: (i, 0)
    const = lambda i: (0, 0)
    const3 = lambda i: (0, 0, 0)
    return pl.pallas_call(
        _post_kernel,
        out_shape=(jax.ShapeDtypeStruct((TOKENS, D_MODEL), _F32),
                   jax.ShapeDtypeStruct((TOKENS * ROW_SUB, LANES), _F32),
                   jax.ShapeDtypeStruct((TOKENS, LANES), jnp.int32),
                   jax.ShapeDtypeStruct((TOKENS, LANES), _F32),
                   jax.ShapeDtypeStruct((TOKENS, LANES), jnp.int32),
                   jax.ShapeDtypeStruct((1, LANES), jnp.int32)),
        grid=(TOKENS // tm,),
        in_specs=[pl.BlockSpec((tm, D_MODEL), row),
                  pl.BlockSpec((tm, ATTN_WIDTH), row),
                  pl.BlockSpec((tm, SGU_WIDTH), row),
                  pl.BlockSpec((tm, SGU_WIDTH), row),
                  pl.BlockSpec((1, N_MOD, D_MODEL), lambda i: (i // tiles_per_batch, 0, 0)),
                  pl.BlockSpec((1, SGU_WIDTH), const),
                  pl.BlockSpec((1, SGU_WIDTH), const),
                  pl.BlockSpec((SGU_WIDTH // LANES, 2 * SGU_CHUNK, SGU_CHUNK), const3),
                  pl.BlockSpec((SGU_CHUNK, SGU_WIDTH), const),
                  pl.BlockSpec((1, ATTN_WIDTH), const),
                  pl.BlockSpec((1, SGU_WIDTH), const),
                  pl.BlockSpec((ATTN_WIDTH, D_MODEL), const),
                  pl.BlockSpec((SGU_WIDTH, D_MODEL), const),
                  pl.BlockSpec((1, D_MODEL), const),
                  pl.BlockSpec((1, D_MODEL), const),
                  pl.BlockSpec((1, D_MODEL), const),
                  pl.BlockSpec((D_MODEL, LANES), const),
                  pl.BlockSpec((1, LANES), const),
                  pl.BlockSpec((tm, tm), const)],
        out_specs=(pl.BlockSpec((tm, D_MODEL), row),
                   pl.BlockSpec((tm * ROW_SUB, LANES), row),
                   pl.BlockSpec((tm, LANES), row),
                   pl.BlockSpec((tm, LANES), row),
                   pl.BlockSpec((tm, LANES), row),
                   pl.BlockSpec((1, LANES), const)),
        scratch_shapes=[pltpu.VMEM((tm, SGU_WIDTH), _F32),
                        pltpu.VMEM((1, LANES), _F32)],
        compiler_params=pltpu.CompilerParams(dimension_semantics=("arbitrary",),
                                             vmem_limit_bytes=VMEM_LIMIT),
        name="post",
    )(x2, ao, su, sv, mod3, lng, lnb, ws2, bsf, ga, gs, woa, wos, bo, gpost, gpre, wr, br, tri)


def _dispatch_kernel(dest_ref, h2_ref, xs_ref, zero_ref, sem):
    tm = h2_ref.shape[0] // ROW_SUB
    n_token_steps = TOKENS // tm
    i = pl.program_id(0)

    def scatter_rows(src_ref, rows_per_dest):
        def issue(r, carry):
            for kk in range(TOP_K):
                d = pl.multiple_of(dest_ref[r * TOP_K + kk] * ROW_SUB, ROW_SUB)
                src_row = pl.multiple_of(r * ROW_SUB, ROW_SUB) if rows_per_dest == TOP_K else 0
                pltpu.make_async_copy(src_ref.at[pl.ds(src_row, ROW_SUB)],
                                      xs_ref.at[pl.ds(d, ROW_SUB)], sem).start(priority=kk % 2)
            return carry

        lax.fori_loop(0, tm, issue, 0, unroll=8)
        for kk in range(TOP_K):
            pltpu.make_async_copy(h2_ref, xs_ref.at[pl.ds(0, tm * ROW_SUB)], sem).wait()

    @pl.when(i < n_token_steps)
    def _():
        scatter_rows(h2_ref, TOP_K)

    @pl.when(i >= n_token_steps)
    def _():
        zero_ref[...] = jnp.zeros_like(zero_ref)
        scatter_rows(zero_ref, 1)


def _dispatch_call(dest_flat, h2):
    tm = MOVE_TILE
    n_token_steps = TOKENS // tm
    n_steps = dest_flat.shape[0] // (tm * TOP_K)
    return pl.pallas_call(
        _dispatch_kernel,
        out_shape=jax.ShapeDtypeStruct((SORTED_ROWS * ROW_SUB, LANES), _F32),
        grid=(n_steps,),
        in_specs=[pl.BlockSpec((tm * TOP_K,), lambda i: (i,), memory_space=pltpu.SMEM),
                  pl.BlockSpec((tm * ROW_SUB, LANES),
                               lambda i: (jnp.minimum(i, n_token_steps - 1), 0))],
        out_specs=pl.BlockSpec(memory_space=pl.ANY),
        scratch_shapes=[pltpu.VMEM((ROW_SUB, LANES), _F32),
                        pltpu.SemaphoreType.DMA],
        compiler_params=pltpu.CompilerParams(dimension_semantics=("arbitrary",),
                                             vmem_limit_bytes=VMEM_LIMIT),
        name="dispatch",
    )(dest_flat, h2)


def _expert_kernel(be_ref, nu_ref, xs_ref, wgu_ref, bgu_ref, wd_ref, bd_ref, y_ref):
    del be_ref

    @pl.when(pl.program_id(0) < nu_ref[0])
    def _():
        x = _load_row_tiles(xs_ref, EXPERT_BLOCK).astype(_BF16)
        gu = jnp.dot(x, wgu_ref[0].astype(_BF16), preferred_element_type=_F32) + bgu_ref[0]
        gate = jnp.minimum(gu[:, 0:D_FF_EXPERT], SWIGLU_LIMIT)
        up = jnp.clip(gu[:, D_FF_EXPERT:], -SWIGLU_LIMIT, SWIGLU_LIMIT)
        act = (up + 1.0) * gate * (1.0 / (1.0 + jnp.exp(-SWIGLU_ALPHA * gate)))
        y = jnp.dot(act.astype(_BF16), wd_ref[0].astype(_BF16), preferred_element_type=_F32) + bd_ref[0]
        _store_row_tiles(y_ref, y)

    @pl.when(pl.program_id(0) >= nu_ref[0])
    def _():
        y_ref[...] = jnp.zeros_like(y_ref)


def _expert_call(block_expert, n_used, xs, wgu_b, bgu3, wd_b, bd3):
    tb = EXPERT_BLOCK
    live = lambda b, be, nu: (jnp.minimum(b, nu[0] - 1), 0)
    wsel = lambda b, be, nu: (be[b], 0, 0)
    grid_spec = pltpu.PrefetchScalarGridSpec(
        num_scalar_prefetch=2,
        grid=(N_EXPERT_BLOCKS,),
        in_specs=[pl.BlockSpec((tb * ROW_SUB, LANES), live),
                  pl.BlockSpec((1, D_MODEL, 2 * D_FF_EXPERT), wsel),
                  pl.BlockSpec((1, 1, 2 * D_FF_EXPERT), wsel),
                  pl.BlockSpec((1, D_FF_EXPERT, D_MODEL), wsel),
                  pl.BlockSpec((1, 1, D_MODEL), wsel)],
        out_specs=pl.BlockSpec((tb * ROW_SUB, LANES), lambda b, be, nu: (b, 0)))
    return pl.pallas_call(
        _expert_kernel,
        out_shape=jax.ShapeDtypeStruct((SORTED_ROWS * ROW_SUB, LANES), _F32),
        grid_spec=grid_spec,
        compiler_params=pltpu.CompilerParams(dimension_semantics=("arbitrary",),
                                             vmem_limit_bytes=VMEM_LIMIT),
        name="expert",
    )(block_expert, n_used, xs, wgu_b, bgu3, wd_b, bd3)


def _combine_kernel(dest_ref, dest_next_ref, gate_ref, xmid_ref, mod_ref, g_ref, y_ref, o_ref,
                    buf_ref, sem):
    tm = xmid_ref.shape[0]
    m = mod_ref[0]
    i = pl.program_id(0)
    n = pl.num_programs(0)

    def gather_rows(idx_ref, slot):
        def issue(r, carry):
            for kk in range(TOP_K):
                d = pl.multiple_of(idx_ref[r * TOP_K + kk] * ROW_SUB, ROW_SUB)
                pltpu.make_async_copy(
                    y_ref.at[pl.ds(d, ROW_SUB)],
                    buf_ref.at[slot, kk, pl.ds(pl.multiple_of(r * ROW_SUB, ROW_SUB), ROW_SUB)],
                    sem.at[slot]).start(priority=kk % 2)
            return carry

        lax.fori_loop(0, tm, issue, 0, unroll=8)

    @pl.when(i == 0)
    def _():
        gather_rows(dest_ref, 0)

    @pl.when(i + 1 < n)
    def _():
        gather_rows(dest_next_ref, (i + 1) % 2)

    slot = i % 2
    for kk in range(TOP_K):
        pltpu.make_async_copy(y_ref.at[pl.ds(0, tm * ROW_SUB)], buf_ref.at[slot, kk],
                              sem.at[slot]).wait()
    g = gate_ref[...]
    pieces = []
    for cc in range(ROW_SUB):
        piece = buf_ref[slot, 0, pl.ds(cc, tm, stride=ROW_SUB), :] * g[:, 0:1]
        for kk in range(1, TOP_K):
            piece = piece + buf_ref[slot, kk, pl.ds(cc, tm, stride=ROW_SUB), :] * g[:, kk:kk + 1]
        pieces.append(piece)
    ffn = jnp.concatenate(pieces, axis=1)
    o_ref[...] = xmid_ref[...] + m[5:6] * _rms(ffn, g_ref[...])


def _combine_call(dest_flat, gates, x_mid, mod3, gpost_ffn, y_rows):
    tm = MOVE_TILE
    tiles_per_batch = SEQ // tm
    n_steps = TOKENS // tm
    return pl.pallas_call(
        _combine_kernel,
        out_shape=jax.ShapeDtypeStruct((TOKENS, D_MODEL), _F32),
        grid=(n_steps,),
        in_specs=[pl.BlockSpec((tm * TOP_K,), lambda i: (i,), memory_space=pltpu.SMEM),
                  pl.BlockSpec((tm * TOP_K,), lambda i: (jnp.minimum(i + 1, n_steps - 1),),
                               memory_space=pltpu.SMEM),
                  pl.BlockSpec((tm, LANES), lambda i: (i, 0)),
                  pl.BlockSpec((tm, D_MODEL), lambda i: (i, 0)),
                  pl.BlockSpec((1, N_MOD, D_MODEL), lambda i: (i // tiles_per_batch, 0, 0)),
                  pl.BlockSpec((1, D_MODEL), lambda i: (0, 0)),
                  pl.BlockSpec(memory_space=pl.ANY)],
        out_specs=pl.BlockSpec((tm, D_MODEL), lambda i: (i, 0)),
        scratch_shapes=[pltpu.VMEM((2, TOP_K, tm * ROW_SUB, LANES), _F32),
                        pltpu.SemaphoreType.DMA((2,))],
        compiler_params=pltpu.CompilerParams(dimension_semantics=("arbitrary",),
                                             vmem_limit_bytes=VMEM_LIMIT),
        name="combine",
    )(dest_flat, dest_flat, gates, x_mid, mod3, gpost_ffn, y_rows)


def _rope_tables():
    pos = jnp.arange(SEQ, dtype=_F32)
    pos_row = jnp.floor(pos / GRID_W)
    pos_col = pos - pos_row * GRID_W
    n_freq = HEAD_DIM // 4
    inv_freq = ROPE_THETA ** (-jnp.arange(n_freq, dtype=_F32) / n_freq)
    d = jnp.arange(LANES) % HEAD_DIM
    f = inv_freq[d % n_freq]
    ang = jnp.where((d < HEAD_DIM // 2)[None, :], pos_row[:, None] * f[None, :], pos_col[:, None] * f[None, :])
    sign = jnp.where((d % (HEAD_DIM // 2)) < n_freq, -1.0, 1.0).astype(_F32)
    return jnp.cos(ang), jnp.sin(ang) * sign[None, :]


def _perm_heads(a, axis):
    shape = a.shape
    a = a.reshape(shape[:axis] + (N_HEADS, HEAD_DIM) + shape[axis + 1:])
    a = jnp.take(a, jnp.array(HEAD_PERM), axis=axis)
    return a.reshape(shape)


def kernel(x, c, ctx, c_ctx, w_ada, b_ada, g_pre_mix, g_post_mix, g_pre_ffn, g_post_ffn, w_in, b_in, attn_sink, sgu_ln_g, sgu_ln_b, sgu_w, sgu_b, g_attn_out, g_sgu_out, w_out, b_out, w_router, b_router, w_gate_up, b_gate_up, w_down, b_down):
    l = 0
    x2 = x.reshape(TOKENS, D_MODEL)
    ctx2 = ctx.reshape(BATCH * CTX_LEN, D_MODEL)

    cc = jnp.zeros((16, D_MODEL), _F32).at[:BATCH].set(c).at[BATCH].set(c_ctx)
    mod = _ada_call(cc, w_ada[l], b_ada[l].reshape(1, -1))
    mod3 = mod.reshape(16, N_MOD, D_MODEL)

    w_in_l = w_in[l]
    b_in_l = b_in[l]
    w_in_p = jnp.concatenate([_perm_heads(w_in_l[:, :ATTN_WIDTH], 1), w_in_l[:, ATTN_WIDTH:]], axis=1)
    b_in_p = jnp.concatenate([_perm_heads(b_in_l[:ATTN_WIDTH], 0), b_in_l[ATTN_WIDTH:]], axis=0)
    cos_t, sin_t = _rope_tables()
    g_pre = g_pre_mix[l].reshape(1, -1)

    q, k, v, su, sv = _inproj_call(x2, mod3, g_pre, w_in_p.astype(_BF16), b_in_p.reshape(1, -1), cos_t, sin_t)
    kc, vc = _ctxproj_call(ctx2, mod3, g_pre, w_in_l[:, KV_START:KV_END].astype(_BF16),
                           b_in_l[KV_START:KV_END].reshape(1, -1))

    sink_p = jnp.take(attn_sink[l], jnp.array(HEAD_PERM))
    ao = _attn_call(sink_p, q, k, v, kc, vc)

    ws2 = sgu_w[l].reshape(SGU_WIDTH // LANES, 2 * SGU_CHUNK, SGU_CHUNK).astype(_BF16)
    bsf = jnp.repeat(sgu_b[l].T, SGU_HEAD_DIM, axis=1)
    w_out_l = w_out[l]
    woa = _perm_heads(w_out_l[:ATTN_WIDTH], 0).astype(_BF16)
    wos = w_out_l[ATTN_WIDTH:].astype(_BF16)
    ga = _perm_heads(g_attn_out[l], 0).reshape(1, -1)
    wr_hi = w_router[l].astype(_BF16)
    wr_lo = (w_router[l] - wr_hi.astype(_F32)).astype(_BF16)
    wr = (jnp.zeros((D_MODEL, LANES), _BF16).at[:, :N_EXPERTS].set(wr_hi)
          .at[:, N_EXPERTS:2 * N_EXPERTS].set(wr_lo))
    br = jnp.full((1, LANES), NEG_BIG, _F32).at[0, :N_EXPERTS].set(b_router[l])
    ii = jnp.arange(ROW_TILE)
    tri = (ii[None, :] < ii[:, None]).astype(_BF16)

    x_mid, h2, idx_o, gate_o, rank_o, cnt_o = _post_call(
        x2, ao, su, sv, mod3, sgu_ln_g[l].reshape(1, -1), sgu_ln_b[l].reshape(1, -1), ws2, bsf,
        ga, g_sgu_out[l].reshape(1, -1), woa, wos, b_out[l].reshape(1, -1),
        g_post_mix[l].reshape(1, -1), g_pre_ffn[l].reshape(1, -1), wr, br, tri)

    counts = cnt_o[0, :N_EXPERTS]
    padded = (counts + EXPERT_BLOCK - 1) // EXPERT_BLOCK * EXPERT_BLOCK
    pad_end = jnp.cumsum(padded)
    pad_start = pad_end - padded
    top_idx = idx_o[:, :TOP_K]
    dest = (pad_start[top_idx] + rank_o[:, :TOP_K]).astype(jnp.int32).reshape(-1)
    block_start = jnp.arange(N_EXPERT_BLOCKS, dtype=jnp.int32) * EXPERT_BLOCK
    block_expert = jnp.minimum(jnp.sum(pad_end[None, :] <= block_start[:, None], axis=1),
                               N_EXPERTS - 1).astype(jnp.int32)
    n_used = (pad_end[-1:] // EXPERT_BLOCK).astype(jnp.int32)

    free = padded - counts
    free_end = jnp.cumsum(free)
    free_start = free_end - free
    j = jnp.arange(SORTED_ROWS - TOKENS * TOP_K, dtype=jnp.int32)
    e_of_j = jnp.minimum(jnp.sum(free_end[:, None] <= j[None, :], axis=0), N_EXPERTS - 1)
    unrouted = jnp.where(j < free_end[-1],
                         pad_start[e_of_j] + counts[e_of_j] + (j - free_start[e_of_j]),
                         pad_end[-1] + (j - free_end[-1])).astype(jnp.int32)

    xs = _dispatch_call(jnp.concatenate([dest, unrouted]), h2)
    y_rows = _expert_call(block_expert, n_used, xs,
                          w_gate_up[l], b_gate_up[l].reshape(N_EXPERTS, 1, -1),
                          w_down[l], b_down[l].reshape(N_EXPERTS, 1, -1))
    out = _combine_call(dest, gate_o, x_mid, mod3, g_post_ffn[l].reshape(1, -1), y_rows)
    return out.reshape(BATCH, SEQ, D_MODEL)
```

```python
import functools

import jax
import jax.numpy as jnp
from jax import lax
from jax.experimental import pallas as pl
from jax.experimental.pallas import tpu as pltpu

D_MODEL = 1024
BATCH = 8
SEQ = 4096
TOKENS = BATCH * SEQ
GRID_W = 64
CTX_LEN = 256
N_HEADS = 8
N_KV_HEADS = 2
HEAD_DIM = 64
ATTN_WIDTH = N_HEADS * HEAD_DIM
KV_WIDTH = N_KV_HEADS * HEAD_DIM
WINDOW = 128
ATTN_BLOCK = 128
SGU_HEADS = 8
SGU_HEAD_DIM = 64
SGU_WIDTH = SGU_HEADS * SGU_HEAD_DIM
SGU_CHUNK = 128
KV_START = ATTN_WIDTH
KV_END = ATTN_WIDTH + 2 * KV_WIDTH
IN_WIDTH = KV_END + 2 * SGU_WIDTH
N_EXPERTS = 32
TOP_K = 4
D_FF_EXPERT = 1024
SWIGLU_LIMIT = 7.0
SWIGLU_ALPHA = 1.702
ROPE_THETA = 10000.0
EPS = 1e-6
N_MOD = 6

LANES = 128
NEG_BIG = -1e30
VMEM_LIMIT = 56 * 1024 * 1024

ROW_TILE = 512
MOVE_TILE = 256
EXPERT_BLOCK = 512
N_EXPERT_BLOCKS = TOKENS * TOP_K // EXPERT_BLOCK + N_EXPERTS
SORTED_ROWS = N_EXPERT_BLOCKS * EXPERT_BLOCK
HEAD_PERM = (0, 4, 1, 5, 2, 6, 3, 7)

_F32 = jnp.float32
_BF16 = jnp.bfloat16


def _rms(x, g):
    ms = jnp.mean(x * x, axis=-1, keepdims=True)
    return x * lax.rsqrt(ms + EPS) * g


ROW_SUB = D_MODEL // LANES


def _store_row_tiles(ref, x):
    n = x.shape[0]
    for cc in range(ROW_SUB):
        ref[pl.ds(cc, n, stride=ROW_SUB), :] = x[:, cc * LANES:(cc + 1) * LANES]


def _load_row_tiles(ref, n):
    return jnp.concatenate([ref[pl.ds(cc, n, stride=ROW_SUB), :] for cc in range(ROW_SUB)], axis=1)


def _gelu_tanh(x):
    c = 0.7978845608028654
    return x * (0.5 * (1.0 + jnp.tanh(c * (x + 0.044715 * (x * x * x)))))


def _ada_kernel(c_ref, w_ref, b_ref, o_ref):
    c = c_ref[...]
    a = c / (1.0 + jnp.exp(-c))
    o_ref[...] = jnp.dot(a, w_ref[...], preferred_element_type=_F32,
                         precision=lax.Precision.HIGHEST) + b_ref[...]


def _ada_call(cc, w_ada, b_ada):
    n = N_MOD * D_MODEL
    tn = 512
    return pl.pallas_call(
        _ada_kernel,
        out_shape=jax.ShapeDtypeStruct((16, n), _F32),
        grid=(n // tn,),
        in_specs=[pl.BlockSpec((16, D_MODEL), lambda j: (0, 0)),
                  pl.BlockSpec((D_MODEL, tn), lambda j: (0, j)),
                  pl.BlockSpec((1, tn), lambda j: (0, j))],
        out_specs=pl.BlockSpec((16, tn), lambda j: (0, j)),
        compiler_params=pltpu.CompilerParams(dimension_semantics=("arbitrary",),
                                             vmem_limit_bytes=VMEM_LIMIT),
        name="ada",
    )(cc, w_ada, b_ada)


def _rope(x, cos, sin_signed, first_half):
    nxt = pltpu.roll(x, LANES - 16, 1)
    prv = pltpu.roll(x, 16, 1)
    return x * cos + jnp.where(first_half, nxt, prv) * sin_signed


def _inproj_kernel(x_ref, mod_ref, g_ref, w_ref, b_ref, cos_ref, sin_ref,
                   q_ref, k_ref, v_ref, su_ref, sv_ref):
    m = mod_ref[0]
    h = _rms(x_ref[...], g_ref[...]) * (1.0 + m[1:2]) + m[0:1]
    hb = h.astype(_BF16)
    cos = cos_ref[...]
    sin = sin_ref[...]
    lane = lax.broadcasted_iota(jnp.int32, cos.shape, 1)
    first_half = (lane & 31) < 16

    q = jnp.dot(hb, w_ref[:, 0:ATTN_WIDTH], preferred_element_type=_F32) + b_ref[:, 0:ATTN_WIDTH]
    for j in range(ATTN_WIDTH // LANES):
        qj = _rope(q[:, j * LANES:(j + 1) * LANES], cos, sin, first_half)
        q_ref[:, j * LANES:(j + 1) * LANES] = (qj * (HEAD_DIM ** -0.5)).astype(_BF16)
    kv = jnp.dot(hb, w_ref[:, KV_START:KV_END], preferred_element_type=_F32) + b_ref[:, KV_START:KV_END]
    k_ref[...] = _rope(kv[:, 0:KV_WIDTH], cos, sin, first_half).astype(_BF16)
    v_ref[...] = kv[:, KV_WIDTH:].astype(_BF16)
    u0 = KV_END
    u1 = KV_END + SGU_WIDTH
    su_ref[...] = jnp.dot(hb, w_ref[:, u0:u1], preferred_element_type=_F32) + b_ref[:, u0:u1]
    sv_ref[...] = jnp.dot(hb, w_ref[:, u1:IN_WIDTH], preferred_element_type=_F32) + b_ref[:, u1:IN_WIDTH]


def _inproj_call(x2, mod3, g_pre, w_in_b, b_in2, cos_t, sin_t):
    tm = ROW_TILE
    tiles_per_batch = SEQ // tm
    row = lambda i: (i, 0)
    const = lambda i: (0, 0)
    return pl.pallas_call(
        _inproj_kernel,
        out_shape=(jax.ShapeDtypeStruct((TOKENS, ATTN_WIDTH), _BF16),
                   jax.ShapeDtypeStruct((TOKENS, KV_WIDTH), _BF16),
                   jax.ShapeDtypeStruct((TOKENS, KV_WIDTH), _BF16),
                   jax.ShapeDtypeStruct((TOKENS, SGU_WIDTH), _F32),
                   jax.ShapeDtypeStruct((TOKENS, SGU_WIDTH), _F32)),
        grid=(TOKENS // tm,),
        in_specs=[pl.BlockSpec((tm, D_MODEL), row),
                  pl.BlockSpec((1, N_MOD, D_MODEL), lambda i: (i // tiles_per_batch, 0, 0)),
                  pl.BlockSpec((1, D_MODEL), const),
                  pl.BlockSpec((D_MODEL, IN_WIDTH), const),
                  pl.BlockSpec((1, IN_WIDTH), const),
                  pl.BlockSpec((tm, LANES), lambda i: (i % tiles_per_batch, 0)),
                  pl.BlockSpec((tm, LANES), lambda i: (i % tiles_per_batch, 0))],
        out_specs=(pl.BlockSpec((tm, ATTN_WIDTH), row),
                   pl.BlockSpec((tm, KV_WIDTH), row),
                   pl.BlockSpec((tm, KV_WIDTH), row),
                   pl.BlockSpec((tm, SGU_WIDTH), row),
                   pl.BlockSpec((tm, SGU_WIDTH), row)),
        compiler_params=pltpu.CompilerParams(dimension_semantics=("arbitrary",),
                                             vmem_limit_bytes=VMEM_LIMIT),
        name="inproj",
    )(x2, mod3, g_pre, w_in_b, b_in2, cos_t, sin_t)


def _ctxproj_kernel(x_ref, mod_ref, g_ref, w_ref, b_ref, k_ref, v_ref):
    m = mod_ref[0]
    h = _rms(x_ref[...], g_ref[...]) * (1.0 + m[1:2]) + m[0:1]
    kv = jnp.dot(h.astype(_BF16), w_ref[...], preferred_element_type=_F32) + b_ref[...]
    k_ref[...] = kv[:, 0:KV_WIDTH].astype(_BF16)
    v_ref[...] = kv[:, KV_WIDTH:].astype(_BF16)


def _ctxproj_call(ctx2, mod3, g_pre, w_kv_b, b_kv2):
    tm = ROW_TILE
    rows = ctx2.shape[0]
    row = lambda i: (i, 0)
    const = lambda i: (0, 0)
    return pl.pallas_call(
        _ctxproj_kernel,
        out_shape=(jax.ShapeDtypeStruct((rows, KV_WIDTH), _BF16),
                   jax.ShapeDtypeStruct((rows, KV_WIDTH), _BF16)),
        grid=(rows // tm,),
        in_specs=[pl.BlockSpec((tm, D_MODEL), row),
                  pl.BlockSpec((1, N_MOD, D_MODEL), lambda i: (BATCH, 0, 0)),
                  pl.BlockSpec((1, D_MODEL), const),
                  pl.BlockSpec((D_MODEL, 2 * KV_WIDTH), const),
                  pl.BlockSpec((1, 2 * KV_WIDTH), const)],
        out_specs=(pl.BlockSpec((tm, KV_WIDTH), row),
                   pl.BlockSpec((tm, KV_WIDTH), row)),
        compiler_params=pltpu.CompilerParams(dimension_semantics=("arbitrary",),
                                             vmem_limit_bytes=VMEM_LIMIT),
        name="ctxproj",
    )(ctx2, mod3, g_pre, w_kv_b, b_kv2)


def _attn_kernel(sink_ref, q_ref, kp_ref, km_ref, kn_ref, vp_ref, vm_ref, vn_ref,
                 kc_ref, vc_ref, o_ref):
    n = pl.program_id(1)
    nstep = pl.num_programs(1)
    blk = ATTN_BLOCK
    rows2 = 2 * blk
    r = lax.broadcasted_iota(jnp.int32, (rows2, blk), 0) & (blk - 1)
    c = lax.broadcasted_iota(jnp.int32, (rows2, blk), 1)
    tri_prev = c >= r
    tri_next = c <= r
    row1 = lax.broadcasted_iota(jnp.int32, (rows2, 1), 0)
    lane_q = lax.broadcasted_iota(jnp.int32, (blk, LANES), 1)
    k_mid = km_ref[...]
    v_mid = vm_ref[...]
    pieces = [(kp_ref[...], k_mid[0:blk], k_mid[blk:], vp_ref[...], v_mid[0:blk], v_mid[blk:],
               tri_prev & (n > 0), tri_next),
              (k_mid[0:blk], k_mid[blk:], kn_ref[...], v_mid[0:blk], v_mid[blk:], vn_ref[...],
               tri_prev, tri_next & (n < nstep - 1))]
    n_grp = ATTN_WIDTH // LANES
    for t, (ka, kb, kc_, va, vb, vc_, mask_a, mask_c) in enumerate(pieces):
        keys = jnp.concatenate([ka, kb, kc_, kc_ref[...]], axis=0)
        vals = jnp.concatenate([va, vb, vc_, vc_ref[...]], axis=0)
        q_rows = []
        for j in range(n_grp):
            qg = q_ref[t * blk:(t + 1) * blk, j * LANES:(j + 1) * LANES]
            zero = jnp.zeros_like(qg)
            q_rows += [jnp.where(lane_q < HEAD_DIM, qg, zero), jnp.where(lane_q >= HEAD_DIM, qg, zero)]
        s_all = lax.dot_general(jnp.concatenate(q_rows, axis=0), keys, (((1,), (1,)), ((), ())),
                                preferred_element_type=_F32)
        p_rows, denoms = [], []
        for j in range(n_grp):
            s = s_all[j * rows2:(j + 1) * rows2]
            s = jnp.concatenate([jnp.where(mask_a, s[:, 0:blk], NEG_BIG), s[:, blk:2 * blk],
                                 jnp.where(mask_c, s[:, 2 * blk:3 * blk], NEG_BIG), s[:, 3 * blk:]],
                                axis=1)
            sk = jnp.where(row1 < blk, sink_ref[2 * j], sink_ref[2 * j + 1])
            m = jnp.maximum(jnp.max(s, axis=-1, keepdims=True), sk)
            p = jnp.exp(s - m)
            denoms.append(jnp.sum(p, axis=-1, keepdims=True) + jnp.exp(sk - m))
            p_rows.append(p.astype(_BF16))
        o_all = jnp.dot(jnp.concatenate(p_rows, axis=0), vals, preferred_element_type=_F32)
        for j in range(n_grp):
            o2 = o_all[j * rows2:(j + 1) * rows2] / denoms[j]
            og = jnp.where(lane_q < HEAD_DIM, o2[0:blk], o2[blk:])
            o_ref[t * blk:(t + 1) * blk, j * LANES:(j + 1) * LANES] = og.astype(_BF16)


def _attn_call(sink_p, q, k, v, kc, vc):
    nblk = SEQ // ATTN_BLOCK
    nstep = nblk // 2
    own = lambda b, n: (b * nstep + n, 0)
    prev = lambda b, n: (b * nblk + jnp.maximum(2 * n - 1, 0), 0)
    nxt = lambda b, n: (b * nblk + jnp.minimum(2 * n + 2, nblk - 1), 0)
    ctx = lambda b, n: (b, 0)
    kv1 = (ATTN_BLOCK, KV_WIDTH)
    kv2 = (2 * ATTN_BLOCK, KV_WIDTH)
    return pl.pallas_call(
        _attn_kernel,
        out_shape=jax.ShapeDtypeStruct((TOKENS, ATTN_WIDTH), _BF16),
        grid=(BATCH, nstep),
        in_specs=[pl.BlockSpec(memory_space=pltpu.SMEM),
                  pl.BlockSpec((2 * ATTN_BLOCK, ATTN_WIDTH), own),
                  pl.BlockSpec(kv1, prev), pl.BlockSpec(kv2, own), pl.BlockSpec(kv1, nxt),
                  pl.BlockSpec(kv1, prev), pl.BlockSpec(kv2, own), pl.BlockSpec(kv1, nxt),
                  pl.BlockSpec((CTX_LEN, KV_WIDTH), ctx),
                  pl.BlockSpec((CTX_LEN, KV_WIDTH), ctx)],
        out_specs=pl.BlockSpec((2 * ATTN_BLOCK, ATTN_WIDTH), own),
        compiler_params=pltpu.CompilerParams(dimension_semantics=("arbitrary", "arbitrary"),
                                             vmem_limit_bytes=VMEM_LIMIT),
        name="attn",
    )(sink_p, q, k, k, k, v, v, v, kc, vc)


def _post_kernel(x_ref, ao_ref, su_ref, sv_ref, mod_ref, lng_ref, lnb_ref, ws_ref, bs_ref,
                 ga_ref, gs_ref, woa_ref, wos_ref, bo_ref, gpost_ref, gpre_ref, wr_ref, br_ref,
                 tri_ref,
                 xmid_ref, h2_ref, idx_ref, gate_ref, rank_ref, cnt_ref,
                 mixed_ref, carry_ref):
    tm = x_ref.shape[0]
    m = mod_ref[0]

    @pl.when(pl.program_id(0) == 0)
    def _():
        carry_ref[...] = jnp.zeros_like(carry_ref)

    gv = _gelu_tanh(sv_ref[...])
    mu = jnp.mean(gv, axis=-1, keepdims=True)
    gc = gv - mu
    var = jnp.mean(gc * gc, axis=-1, keepdims=True)
    vb = (gc * lax.rsqrt(var + EPS) * lng_ref[...] + lnb_ref[...]).astype(_BF16)
    lane = lax.broadcasted_iota(jnp.int32, (SGU_CHUNK, LANES), 1)
    for c in range(tm // SGU_CHUNK):
        r0 = c * SGU_CHUNK
        for p in range(SGU_WIDTH // LANES):
            l0 = p * LANES
            r = jnp.dot(ws_ref[p], vb[r0:r0 + SGU_CHUNK, l0:l0 + LANES], preferred_element_type=_F32)
            mixed = jnp.where(lane < SGU_HEAD_DIM, r[0:SGU_CHUNK], r[SGU_CHUNK:])
            mixed_ref[r0:r0 + SGU_CHUNK, l0:l0 + LANES] = mixed + bs_ref[:, l0:l0 + LANES]
    sgu_o = _gelu_tanh(su_ref[...]) * mixed_ref[...]

    oa = _rms(ao_ref[...].astype(_F32), ga_ref[...]).astype(_BF16)
    os_ = _rms(sgu_o, gs_ref[...]).astype(_BF16)
    mix = (jnp.dot(oa, woa_ref[...], preferred_element_type=_F32)
           + jnp.dot(os_, wos_ref[...], preferred_element_type=_F32) + bo_ref[...])
    x_mid = x_ref[...] + m[2:3] * _rms(mix, gpost_ref[...])
    xmid_ref[...] = x_mid
    h2 = _rms(x_mid, gpre_ref[...]) * (1.0 + m[4:5]) + m[3:4]
    _store_row_tiles(h2_ref, h2)

    h_hi = h2.astype(_BF16)
    h_lo = (h2 - h_hi.astype(_F32)).astype(_BF16)
    r = (jnp.dot(h_hi, wr_ref[...], preferred_element_type=_F32)
         + jnp.dot(h_lo, wr_ref[...], preferred_element_type=_F32))
    lg = r + pltpu.roll(r, LANES - N_EXPERTS, 1) + br_ref[...]
    lane_r = lax.broadcasted_iota(jnp.int32, lg.shape, 1)
    lane_f = lane_r.astype(_F32)
    tops, hots = [], []
    for _k in range(TOP_K):
        mx = jnp.max(lg, axis=-1, keepdims=True)
        pick = jnp.min(jnp.where(lg == mx, lane_f, float(LANES)), axis=-1, keepdims=True)
        hot = lane_f == pick
        tops.append((mx, pick))
        hots.append(hot)
        lg = jnp.where(hot, 2.0 * NEG_BIG, lg)
    es = [jnp.exp(t[0] - tops[0][0]) for t in tops]
    esum = es[0] + es[1] + es[2] + es[3]
    multi = jnp.zeros(lg.shape, _F32)
    for hot in hots:
        multi = multi + jnp.where(hot, 1.0, 0.0)
    cum = jnp.dot(tri_ref[...], multi.astype(_BF16), preferred_element_type=_F32) + carry_ref[...]
    tok_per_row = LANES // TOP_K
    row_r = lax.broadcasted_iota(jnp.int32, lg.shape, 0)
    lane_base = (row_r & (tok_per_row - 1)) * TOP_K
    gate_o = jnp.zeros(lg.shape, _F32)
    idx_e = jnp.zeros(lg.shape, _F32)
    rank_hi_e = jnp.zeros(lg.shape, _F32)
    rank_lo_e = jnp.zeros(lg.shape, _F32)
    for kk in range(TOP_K):
        rk = jnp.sum(jnp.where(hots[kk], cum, 0.0), axis=-1, keepdims=True)
        rk_hi = jnp.floor(rk * (1.0 / 256.0))
        here = lane_r == lane_base + kk
        gate_o = jnp.where(lane_r == kk, es[kk] / esum, gate_o)
        idx_e = jnp.where(here, tops[kk][1], idx_e)
        rank_hi_e = jnp.where(here, rk_hi, rank_hi_e)
        rank_lo_e = jnp.where(here, rk - 256.0 * rk_hi, rank_lo_e)
    fr = lax.broadcasted_iota(jnp.int32, (tm // tok_per_row, tm), 0)
    fc = lax.broadcasted_iota(jnp.int32, (tm // tok_per_row, tm), 1)
    fold = jnp.where(lax.shift_right_logical(fc, tok_per_row.bit_length() - 1) == fr, 1.0, 0.0).astype(_BF16)
    idx_ref[...] = jnp.dot(fold, idx_e.astype(_BF16), preferred_element_type=_F32).astype(jnp.int32)
    rank_ref[...] = (256.0 * jnp.dot(fold, rank_hi_e.astype(_BF16), preferred_element_type=_F32)
                     + jnp.dot(fold, rank_lo_e.astype(_BF16), preferred_element_type=_F32)).astype(jnp.int32)
    gate_ref[...] = gate_o
    carry_ref[...] += jnp.sum(multi, axis=0, keepdims=True)
    cnt_ref[...] = carry_ref[...].astype(jnp.int32)


def _post_call(x2, ao, su, sv, mod3, lng, lnb, ws2, bsf, ga, gs, woa, wos, bo, gpost, gpre,
               wr, br, tri):
    tm = ROW_TILE
    tiles_per_batch = SEQ // tm
    row = lambda i: (i, 0)
    const = lambda i: (0, 0)
    const3 = lambda i: (0, 0, 0)
    return pl.pallas_call(
        _post_kernel,
        out_shape=(jax.ShapeDtypeStruct((TOKENS, D_MODEL), _F32),
                   jax.ShapeDtypeStruct((TOKENS * ROW_SUB, LANES), _F32),
                   jax.ShapeDtypeStruct((TOKENS * TOP_K // LANES, LANES), jnp.int32),
                   jax.ShapeDtypeStruct((TOKENS, LANES), _F32),
                   jax.ShapeDtypeStruct((TOKENS * TOP_K // LANES, LANES), jnp.int32),
                   jax.ShapeDtypeStruct((1, LANES), jnp.int32)),
        grid=(TOKENS // tm,),
        in_specs=[pl.BlockSpec((tm, D_MODEL), row),
                  pl.BlockSpec((tm, ATTN_WIDTH), row),
                  pl.BlockSpec((tm, SGU_WIDTH), row),
                  pl.BlockSpec((tm, SGU_WIDTH), row),
                  pl.BlockSpec((1, N_MOD, D_MODEL), lambda i: (i // tiles_per_batch, 0, 0)),
                  pl.BlockSpec((1, SGU_WIDTH), const),
                  pl.BlockSpec((1, SGU_WIDTH), const),
                  pl.BlockSpec((SGU_WIDTH // LANES, 2 * SGU_CHUNK, SGU_CHUNK), const3),
                  pl.BlockSpec((SGU_CHUNK, SGU_WIDTH), const),
                  pl.BlockSpec((1, ATTN_WIDTH), const),
                  pl.BlockSpec((1, SGU_WIDTH), const),
                  pl.BlockSpec((ATTN_WIDTH, D_MODEL), const),
                  pl.BlockSpec((SGU_WIDTH, D_MODEL), const),
                  pl.BlockSpec((1, D_MODEL), const),
                  pl.BlockSpec((1, D_MODEL), const),
                  pl.BlockSpec((1, D_MODEL), const),
                  pl.BlockSpec((D_MODEL, LANES), const),
                  pl.BlockSpec((1, LANES), const),
                  pl.BlockSpec((tm, tm), const)],
        out_specs=(pl.BlockSpec((tm, D_MODEL), row),
                   pl.BlockSpec((tm * ROW_SUB, LANES), row),
                   pl.BlockSpec((tm * TOP_K // LANES, LANES), row),
                   pl.BlockSpec((tm, LANES), row),
                   pl.BlockSpec((tm * TOP_K // LANES, LANES), row),
                   pl.BlockSpec((1, LANES), const)),
        scratch_shapes=[pltpu.VMEM((tm, SGU_WIDTH), _F32),
                        pltpu.VMEM((1, LANES), _F32)],
        compiler_params=pltpu.CompilerParams(dimension_semantics=("arbitrary",),
                                             vmem_limit_bytes=VMEM_LIMIT),
        name="post",
    )(x2, ao, su, sv, mod3, lng, lnb, ws2, bsf, ga, gs, woa, wos, bo, gpost, gpre, wr, br, tri)


def _dispatch_kernel(fs_ref, fl_ref, nu_ref, dest_ref, h2_ref, xs_ref, zero_ref, sem, zsem):
    tm = h2_ref.shape[0] // ROW_SUB
    n_token_steps = TOKENS // tm
    i = pl.program_id(0)

    @pl.when(i < n_token_steps)
    def _():
        def issue(r, carry):
            for kk in range(TOP_K):
                d = pl.multiple_of(dest_ref[r * TOP_K + kk] * ROW_SUB, ROW_SUB)
                pltpu.make_async_copy(h2_ref.at[pl.ds(pl.multiple_of(r * ROW_SUB, ROW_SUB), ROW_SUB)],
                                      xs_ref.at[pl.ds(d, ROW_SUB)], sem).start(priority=kk % 2)
            return carry

        lax.fori_loop(0, tm, issue, 0, unroll=8)
        for kk in range(TOP_K):
            pltpu.make_async_copy(h2_ref, xs_ref.at[pl.ds(0, tm * ROW_SUB)], sem).wait()

    @pl.when(i == n_token_steps)
    def _():
        zero_ref[...] = jnp.zeros_like(zero_ref)
        block_rows = EXPERT_BLOCK * ROW_SUB

        def pad_run(e, wait):
            pos = fs_ref[e]
            length = fl_ref[e]
            for bit in reversed(range(EXPERT_BLOCK.bit_length() - 1)):
                size = 1 << bit
                take = length & size

                @pl.when(take != 0)
                def _():
                    cp = pltpu.make_async_copy(
                        zero_ref.at[pl.ds(0, size * ROW_SUB)],
                        xs_ref.at[pl.ds(pl.multiple_of(pos * ROW_SUB, ROW_SUB), size * ROW_SUB)], zsem)
                    if wait:
                        cp.wait()
                    else:
                        cp.start()

                pos = pos + take

        def tail_block(blk, wait):
            cp = pltpu.make_async_copy(
                zero_ref, xs_ref.at[pl.ds(pl.multiple_of(blk * block_rows, block_rows), block_rows)], zsem)
            if wait:
                cp.wait()
            else:
                cp.start()

        for wait in (False, True):
            lax.fori_loop(0, N_EXPERTS, lambda e, c, w=wait: (pad_run(e, w), c)[1], 0)
            lax.fori_loop(nu_ref[0], N_EXPERT_BLOCKS, lambda blk, c, w=wait: (tail_block(blk, w), c)[1], 0)


def _dispatch_call(fill_start, fill_len, n_used, dest_flat, h2):
    tm = MOVE_TILE
    n_token_steps = TOKENS // tm
    grid_spec = pltpu.PrefetchScalarGridSpec(
        num_scalar_prefetch=3,
        grid=(n_token_steps + 1,),
        in_specs=[pl.BlockSpec((tm * TOP_K,), lambda i, fs, fl, nu: (jnp.minimum(i, n_token_steps - 1),),
                               memory_space=pltpu.SMEM),
                  pl.BlockSpec((tm * ROW_SUB, LANES),
                               lambda i, fs, fl, nu: (jnp.minimum(i, n_token_steps - 1), 0))],
        out_specs=pl.BlockSpec(memory_space=pl.ANY),
        scratch_shapes=[pltpu.VMEM((EXPERT_BLOCK * ROW_SUB, LANES), _F32),
                        pltpu.SemaphoreType.DMA,
                        pltpu.SemaphoreType.DMA])
    return pl.pallas_call(
        _dispatch_kernel,
        out_shape=jax.ShapeDtypeStruct((SORTED_ROWS * ROW_SUB, LANES), _F32),
        grid_spec=grid_spec,
        compiler_params=pltpu.CompilerParams(dimension_semantics=("arbitrary",),
                                             vmem_limit_bytes=VMEM_LIMIT),
        name="dispatch",
    )(fill_start, fill_len, n_used, dest_flat, h2)


def _expert_kernel(be_ref, slot_ref, first_ref, nxt_ref, nu_ref,
                   xs_ref, wgu_hbm, bgu_ref, wd_hbm, bd_ref, y_ref, wgu_buf, wd_buf, sem):
    b = pl.program_id(0)

    def weight_copies(e, slot):
        return (pltpu.make_async_copy(wgu_hbm.at[e], wgu_buf.at[slot], sem.at[0, slot]),
                pltpu.make_async_copy(wd_hbm.at[e], wd_buf.at[slot], sem.at[1, slot]))

    @pl.when(b < nu_ref[0])
    def _():
        e = be_ref[b]
        slot = slot_ref[b]

        @pl.when(b == 0)
        def _():
            for cp in weight_copies(e, slot):
                cp.start()

        @pl.when(first_ref[b] == 1)
        def _():
            for cp in weight_copies(e, slot):
                cp.wait()

            @pl.when(nxt_ref[b] < N_EXPERTS)
            def _():
                for cp in weight_copies(nxt_ref[b], 1 - slot):
                    cp.start()

        x = _load_row_tiles(xs_ref, EXPERT_BLOCK).astype(_BF16)
        gu = jnp.dot(x, wgu_buf[slot].astype(_BF16), preferred_element_type=_F32) + bgu_ref[0]
        gate = jnp.minimum(gu[:, 0:D_FF_EXPERT], SWIGLU_LIMIT)
        up = jnp.clip(gu[:, D_FF_EXPERT:], -SWIGLU_LIMIT, SWIGLU_LIMIT)
        act = (up + 1.0) * gate * (1.0 / (1.0 + jnp.exp(-SWIGLU_ALPHA * gate)))
        y = jnp.dot(act.astype(_BF16), wd_buf[slot].astype(_BF16), preferred_element_type=_F32) + bd_ref[0]
        _store_row_tiles(y_ref, y)

    @pl.when(b >= nu_ref[0])
    def _():
        y_ref[...] = jnp.zeros_like(y_ref)


def _expert_call(block_expert, block_slot, block_first, block_next, n_used, xs, wgu, bgu3, wd, bd3):
    tb = EXPERT_BLOCK
    live = lambda b, be, sl, fi, nx, nu: (jnp.minimum(b, nu[0] - 1), 0)
    bsel = lambda b, be, sl, fi, nx, nu: (be[b], 0, 0)
    grid_spec = pltpu.PrefetchScalarGridSpec(
        num_scalar_prefetch=5,
        grid=(N_EXPERT_BLOCKS,),
        in_specs=[pl.BlockSpec((tb * ROW_SUB, LANES), live),
                  pl.BlockSpec(memory_space=pl.ANY),
                  pl.BlockSpec((1, 1, 2 * D_FF_EXPERT), bsel),
                  pl.BlockSpec(memory_space=pl.ANY),
                  pl.BlockSpec((1, 1, D_MODEL), bsel)],
        out_specs=pl.BlockSpec((tb * ROW_SUB, LANES), lambda b, be, sl, fi, nx, nu: (b, 0)),
        scratch_shapes=[pltpu.VMEM((2, D_MODEL, 2 * D_FF_EXPERT), _F32),
                        pltpu.VMEM((2, D_FF_EXPERT, D_MODEL), _F32),
                        pltpu.SemaphoreType.DMA((2, 2))])
    return pl.pallas_call(
        _expert_kernel,
        out_shape=jax.ShapeDtypeStruct((SORTED_ROWS * ROW_SUB, LANES), _F32),
        grid_spec=grid_spec,
        compiler_params=pltpu.CompilerParams(dimension_semantics=("arbitrary",),
                                             vmem_limit_bytes=VMEM_LIMIT),
        name="expert",
    )(block_expert, block_slot, block_first, block_next, n_used, xs, wgu, bgu3, wd, bd3)


def _combine_kernel(dest_ref, dest_next_ref, gate_ref, xmid_ref, mod_ref, g_ref, y_ref, o_ref,
                    buf_ref, sem):
    tm = xmid_ref.shape[0]
    m = mod_ref[0]
    i = pl.program_id(0)
    n = pl.num_programs(0)

    def gather_rows(idx_ref, slot):
        def issue(r, carry):
            for kk in range(TOP_K):
                d = pl.multiple_of(idx_ref[r * TOP_K + kk] * ROW_SUB, ROW_SUB)
                pltpu.make_async_copy(
                    y_ref.at[pl.ds(d, ROW_SUB)],
                    buf_ref.at[slot, kk, pl.ds(pl.multiple_of(r * ROW_SUB, ROW_SUB), ROW_SUB)],
                    sem.at[slot]).start(priority=kk % 2)
            return carry

        lax.fori_loop(0, tm, issue, 0, unroll=8)

    @pl.when(i == 0)
    def _():
        gather_rows(dest_ref, 0)

    @pl.when(i + 1 < n)
    def _():
        gather_rows(dest_next_ref, (i + 1) % 2)

    slot = i % 2
    for kk in range(TOP_K):
        pltpu.make_async_copy(y_ref.at[pl.ds(0, tm * ROW_SUB)], buf_ref.at[slot, kk],
                              sem.at[slot]).wait()
    g = gate_ref[...]
    pieces = []
    for cc in range(ROW_SUB):
        piece = buf_ref[slot, 0, pl.ds(cc, tm, stride=ROW_SUB), :] * g[:, 0:1]
        for kk in range(1, TOP_K):
            piece = piece + buf_ref[slot, kk, pl.ds(cc, tm, stride=ROW_SUB), :] * g[:, kk:kk + 1]
        pieces.append(piece)
    ffn = jnp.concatenate(pieces, axis=1)
    o_ref[...] = xmid_ref[...] + m[5:6] * _rms(ffn, g_ref[...])


def _combine_call(dest_flat, gates, x_mid, mod3, gpost_ffn, y_rows):
    tm = MOVE_TILE
    tiles_per_batch = SEQ // tm
    n_steps = TOKENS // tm
    return pl.pallas_call(
        _combine_kernel,
        out_shape=jax.ShapeDtypeStruct((TOKENS, D_MODEL), _F32),
        grid=(n_steps,),
        in_specs=[pl.BlockSpec((tm * TOP_K,), lambda i: (i,), memory_space=pltpu.SMEM),
                  pl.BlockSpec((tm * TOP_K,), lambda i: (jnp.minimum(i + 1, n_steps - 1),),
                               memory_space=pltpu.SMEM),
                  pl.BlockSpec((tm, LANES), lambda i: (i, 0)),
                  pl.BlockSpec((tm, D_MODEL), lambda i: (i, 0)),
                  pl.BlockSpec((1, N_MOD, D_MODEL), lambda i: (i // tiles_per_batch, 0, 0)),
                  pl.BlockSpec((1, D_MODEL), lambda i: (0, 0)),
                  pl.BlockSpec(memory_space=pl.ANY)],
        out_specs=pl.BlockSpec((tm, D_MODEL), lambda i: (i, 0)),
        scratch_shapes=[pltpu.VMEM((2, TOP_K, tm * ROW_SUB, LANES), _F32),
                        pltpu.SemaphoreType.DMA((2,))],
        compiler_params=pltpu.CompilerParams(dimension_semantics=("arbitrary",),
                                             vmem_limit_bytes=VMEM_LIMIT),
        name="combine",
    )(dest_flat, dest_flat, gates, x_mid, mod3, gpost_ffn, y_rows)


def _rope_tables():
    pos = jnp.arange(SEQ, dtype=_F32)
    pos_row = jnp.floor(pos / GRID_W)
    pos_col = pos - pos_row * GRID_W
    n_freq = HEAD_DIM // 4
    inv_freq = ROPE_THETA ** (-jnp.arange(n_freq, dtype=_F32) / n_freq)
    d = jnp.arange(LANES) % HEAD_DIM
    f = inv_freq[d % n_freq]
    ang = jnp.where((d < HEAD_DIM // 2)[None, :], pos_row[:, None] * f[None, :], pos_col[:, None] * f[None, :])
    sign = jnp.where((d % (HEAD_DIM // 2)) < n_freq, -1.0, 1.0).astype(_F32)
    return jnp.cos(ang), jnp.sin(ang) * sign[None, :]


def _perm_heads(a, axis):
    shape = a.shape
    a = a.reshape(shape[:axis] + (N_HEADS, HEAD_DIM) + shape[axis + 1:])
    a = jnp.take(a, jnp.array(HEAD_PERM), axis=axis)
    return a.reshape(shape)


def kernel(x, c, ctx, c_ctx, w_ada, b_ada, g_pre_mix, g_post_mix, g_pre_ffn, g_post_ffn, w_in, b_in, attn_sink, sgu_ln_g, sgu_ln_b, sgu_w, sgu_b, g_attn_out, g_sgu_out, w_out, b_out, w_router, b_router, w_gate_up, b_gate_up, w_down, b_down):
    l = 0
    x2 = x.reshape(TOKENS, D_MODEL)
    ctx2 = ctx.reshape(BATCH * CTX_LEN, D_MODEL)

    cc = jnp.zeros((16, D_MODEL), _F32).at[:BATCH].set(c).at[BATCH].set(c_ctx)
    mod = _ada_call(cc, w_ada[l], b_ada[l].reshape(1, -1))
    mod3 = mod.reshape(16, N_MOD, D_MODEL)

    w_in_l = w_in[l]
    b_in_l = b_in[l]
    w_in_p = jnp.concatenate([_perm_heads(w_in_l[:, :ATTN_WIDTH], 1), w_in_l[:, ATTN_WIDTH:]], axis=1)
    b_in_p = jnp.concatenate([_perm_heads(b_in_l[:ATTN_WIDTH], 0), b_in_l[ATTN_WIDTH:]], axis=0)
    cos_t, sin_t = _rope_tables()
    g_pre = g_pre_mix[l].reshape(1, -1)

    q, k, v, su, sv = _inproj_call(x2, mod3, g_pre, w_in_p.astype(_BF16), b_in_p.reshape(1, -1), cos_t, sin_t)
    kc, vc = _ctxproj_call(ctx2, mod3, g_pre, w_in_l[:, KV_START:KV_END].astype(_BF16),
                           b_in_l[KV_START:KV_END].reshape(1, -1))

    sink_p = jnp.take(attn_sink[l], jnp.array(HEAD_PERM))
    ao = _attn_call(sink_p, q, k, v, kc, vc)

    ws2 = sgu_w[l].reshape(SGU_WIDTH // LANES, 2 * SGU_CHUNK, SGU_CHUNK).astype(_BF16)
    bsf = jnp.repeat(sgu_b[l].T, SGU_HEAD_DIM, axis=1)
    w_out_l = w_out[l]
    woa = _perm_heads(w_out_l[:ATTN_WIDTH], 0).astype(_BF16)
    wos = w_out_l[ATTN_WIDTH:].astype(_BF16)
    ga = _perm_heads(g_attn_out[l], 0).reshape(1, -1)
    wr_hi = w_router[l].astype(_BF16)
    wr_lo = (w_router[l] - wr_hi.astype(_F32)).astype(_BF16)
    wr = (jnp.zeros((D_MODEL, LANES), _BF16).at[:, :N_EXPERTS].set(wr_hi)
          .at[:, N_EXPERTS:2 * N_EXPERTS].set(wr_lo))
    br = jnp.full((1, LANES), NEG_BIG, _F32).at[0, :N_EXPERTS].set(b_router[l])
    ii = jnp.arange(ROW_TILE)
    tri = (ii[None, :] < ii[:, None]).astype(_BF16)

    x_mid, h2, idx_o, gate_o, rank_o, cnt_o = _post_call(
        x2, ao, su, sv, mod3, sgu_ln_g[l].reshape(1, -1), sgu_ln_b[l].reshape(1, -1), ws2, bsf,
        ga, g_sgu_out[l].reshape(1, -1), woa, wos, b_out[l].reshape(1, -1),
        g_post_mix[l].reshape(1, -1), g_pre_ffn[l].reshape(1, -1), wr, br, tri)

    counts = cnt_o[0, :N_EXPERTS]
    padded = (counts + EXPERT_BLOCK - 1) // EXPERT_BLOCK * EXPERT_BLOCK
    pad_end = jnp.cumsum(padded)
    pad_start = pad_end - padded
    dest = (pad_start[idx_o] + rank_o).astype(jnp.int32).reshape(-1)
    block_start = jnp.arange(N_EXPERT_BLOCKS, dtype=jnp.int32) * EXPERT_BLOCK
    block_expert = jnp.minimum(jnp.sum(pad_end[None, :] <= block_start[:, None], axis=1),
                               N_EXPERTS - 1).astype(jnp.int32)
    n_used = (pad_end[-1:] // EXPERT_BLOCK).astype(jnp.int32)

    xs = _dispatch_call((pad_start + counts).astype(jnp.int32), (padded - counts).astype(jnp.int32),
                        n_used, dest, h2)

    used = padded > 0
    e_ids = jnp.arange(N_EXPERTS, dtype=jnp.int32)
    cand = jnp.where(used, e_ids, N_EXPERTS)
    later_min = lax.cummin(cand[::-1])[::-1]
    next_used = jnp.concatenate([later_min[1:], jnp.full((1,), N_EXPERTS, jnp.int32)])
    ordinal = jnp.cumsum(used.astype(jnp.int32)) - 1
    block_slot = (ordinal[block_expert] & 1).astype(jnp.int32)
    block_first = (block_start == pad_start[block_expert]).astype(jnp.int32)
    block_next = next_used[block_expert].astype(jnp.int32)
    y_rows = _expert_call(block_expert, block_slot, block_first, block_next, n_used, xs,
                          w_gate_up[l], b_gate_up[l].reshape(N_EXPERTS, 1, -1),
                          w_down[l], b_down[l].reshape(N_EXPERTS, 1, -1))
    out = _combine_call(dest, gate_o, x_mid, mod3, g_post_ffn[l].reshape(1, -1), y_rows)
    return out.reshape(BATCH, SEQ, D_MODEL)
```

```python
import functools

import jax
import jax.numpy as jnp
from jax import lax
from jax.experimental import pallas as pl
from jax.experimental.pallas import tpu as pltpu

D_MODEL = 1024
BATCH = 8
SEQ = 4096
TOKENS = BATCH * SEQ
GRID_W = 64
CTX_LEN = 256
N_HEADS = 8
N_KV_HEADS = 2
HEAD_DIM = 64
ATTN_WIDTH = N_HEADS * HEAD_DIM
KV_WIDTH = N_KV_HEADS * HEAD_DIM
WINDOW = 128
ATTN_BLOCK = 128
SGU_HEADS = 8
SGU_HEAD_DIM = 64
SGU_WIDTH = SGU_HEADS * SGU_HEAD_DIM
SGU_CHUNK = 128
KV_START = ATTN_WIDTH
KV_END = ATTN_WIDTH + 2 * KV_WIDTH
IN_WIDTH = KV_END + 2 * SGU_WIDTH
N_EXPERTS = 32
TOP_K = 4
D_FF_EXPERT = 1024
SWIGLU_LIMIT = 7.0
SWIGLU_ALPHA = 1.702
ROPE_THETA = 10000.0
EPS = 1e-6
N_MOD = 6

LANES = 128
NEG_BIG = -1e30
VMEM_LIMIT = 56 * 1024 * 1024

ROW_TILE = 512
MOVE_TILE = 512
EXPERT_BLOCK = 512
N_EXPERT_BLOCKS = TOKENS * TOP_K // EXPERT_BLOCK + N_EXPERTS
SORTED_ROWS = N_EXPERT_BLOCKS * EXPERT_BLOCK
HEAD_PERM = (0, 4, 1, 5, 2, 6, 3, 7)

_F32 = jnp.float32
_BF16 = jnp.bfloat16


def _rms(x, g):
    ms = jnp.mean(x * x, axis=-1, keepdims=True)
    return x * lax.rsqrt(ms + EPS) * g


ROW_SUB = D_MODEL // LANES


def _store_row_tiles(ref, x):
    n = x.shape[0]
    for cc in range(ROW_SUB):
        ref[pl.ds(cc, n, stride=ROW_SUB), :] = x[:, cc * LANES:(cc + 1) * LANES]


def _load_row_tiles(ref, n):
    return jnp.concatenate([ref[pl.ds(cc, n, stride=ROW_SUB), :] for cc in range(ROW_SUB)], axis=1)


def _gelu_tanh(x):
    c = 0.7978845608028654
    return x * (0.5 * (1.0 + jnp.tanh(c * (x + 0.044715 * (x * x * x)))))


def _ada_kernel(c_ref, w_ref, b_ref, o_ref):
    c = c_ref[...]
    a = c / (1.0 + jnp.exp(-c))
    o_ref[...] = jnp.dot(a, w_ref[...], preferred_element_type=_F32,
                         precision=lax.Precision.HIGHEST) + b_ref[...]


def _ada_call(cc, w_ada, b_ada):
    n = N_MOD * D_MODEL
    tn = 512
    return pl.pallas_call(
        _ada_kernel,
        out_shape=jax.ShapeDtypeStruct((16, n), _F32),
        grid=(n // tn,),
        in_specs=[pl.BlockSpec((16, D_MODEL), lambda j: (0, 0)),
                  pl.BlockSpec((D_MODEL, tn), lambda j: (0, j)),
                  pl.BlockSpec((1, tn), lambda j: (0, j))],
        out_specs=pl.BlockSpec((16, tn), lambda j: (0, j)),
        compiler_params=pltpu.CompilerParams(dimension_semantics=("arbitrary",),
                                             vmem_limit_bytes=VMEM_LIMIT),
        name="ada",
    )(cc, w_ada, b_ada)


def _rope(x, cos, sin_signed, first_half):
    nxt = pltpu.roll(x, LANES - 16, 1)
    prv = pltpu.roll(x, 16, 1)
    return x * cos + jnp.where(first_half, nxt, prv) * sin_signed


def _inproj_kernel(x_ref, mod_ref, g_ref, w_ref, b_ref, cos_ref, sin_ref,
                   q_ref, k_ref, v_ref, su_ref, sv_ref):
    m = mod_ref[0]
    h = _rms(x_ref[...], g_ref[...]) * (1.0 + m[1:2]) + m[0:1]
    hb = h.astype(_BF16)
    cos = cos_ref[...]
    sin = sin_ref[...]
    lane = lax.broadcasted_iota(jnp.int32, cos.shape, 1)
    first_half = (lane & 31) < 16

    q = jnp.dot(hb, w_ref[:, 0:ATTN_WIDTH], preferred_element_type=_F32) + b_ref[:, 0:ATTN_WIDTH]
    for j in range(ATTN_WIDTH // LANES):
        qj = _rope(q[:, j * LANES:(j + 1) * LANES], cos, sin, first_half)
        q_ref[:, j * LANES:(j + 1) * LANES] = (qj * (HEAD_DIM ** -0.5)).astype(_BF16)
    kv = jnp.dot(hb, w_ref[:, KV_START:KV_END], preferred_element_type=_F32) + b_ref[:, KV_START:KV_END]
    k_ref[...] = _rope(kv[:, 0:KV_WIDTH], cos, sin, first_half).astype(_BF16)
    v_ref[...] = kv[:, KV_WIDTH:].astype(_BF16)
    u0 = KV_END
    u1 = KV_END + SGU_WIDTH
    su_ref[...] = jnp.dot(hb, w_ref[:, u0:u1], preferred_element_type=_F32) + b_ref[:, u0:u1]
    sv_ref[...] = jnp.dot(hb, w_ref[:, u1:IN_WIDTH], preferred_element_type=_F32) + b_ref[:, u1:IN_WIDTH]


def _inproj_call(x2, mod3, g_pre, w_in_b, b_in2, cos_t, sin_t):
    tm = ROW_TILE
    tiles_per_batch = SEQ // tm
    row = lambda i: (i, 0)
    const = lambda i: (0, 0)
    return pl.pallas_call(
        _inproj_kernel,
        out_shape=(jax.ShapeDtypeStruct((TOKENS, ATTN_WIDTH), _BF16),
                   jax.ShapeDtypeStruct((TOKENS, KV_WIDTH), _BF16),
                   jax.ShapeDtypeStruct((TOKENS, KV_WIDTH), _BF16),
                   jax.ShapeDtypeStruct((TOKENS, SGU_WIDTH), _F32),
                   jax.ShapeDtypeStruct((TOKENS, SGU_WIDTH), _F32)),
        grid=(TOKENS // tm,),
        in_specs=[pl.BlockSpec((tm, D_MODEL), row),
                  pl.BlockSpec((1, N_MOD, D_MODEL), lambda i: (i // tiles_per_batch, 0, 0)),
                  pl.BlockSpec((1, D_MODEL), const),
                  pl.BlockSpec((D_MODEL, IN_WIDTH), const),
                  pl.BlockSpec((1, IN_WIDTH), const),
                  pl.BlockSpec((tm, LANES), lambda i: (i % tiles_per_batch, 0)),
                  pl.BlockSpec((tm, LANES), lambda i: (i % tiles_per_batch, 0))],
        out_specs=(pl.BlockSpec((tm, ATTN_WIDTH), row),
                   pl.BlockSpec((tm, KV_WIDTH), row),
                   pl.BlockSpec((tm, KV_WIDTH), row),
                   pl.BlockSpec((tm, SGU_WIDTH), row),
                   pl.BlockSpec((tm, SGU_WIDTH), row)),
        compiler_params=pltpu.CompilerParams(dimension_semantics=("arbitrary",),
                                             vmem_limit_bytes=VMEM_LIMIT),
        name="inproj",
    )(x2, mod3, g_pre, w_in_b, b_in2, cos_t, sin_t)


def _ctxproj_kernel(x_ref, mod_ref, g_ref, w_ref, b_ref, k_ref, v_ref):
    m = mod_ref[0]
    h = _rms(x_ref[...], g_ref[...]) * (1.0 + m[1:2]) + m[0:1]
    kv = jnp.dot(h.astype(_BF16), w_ref[...], preferred_element_type=_F32) + b_ref[...]
    k_ref[...] = kv[:, 0:KV_WIDTH].astype(_BF16)
    v_ref[...] = kv[:, KV_WIDTH:].astype(_BF16)


def _ctxproj_call(ctx2, mod3, g_pre, w_kv_b, b_kv2):
    tm = ROW_TILE
    rows = ctx2.shape[0]
    row = lambda i: (i, 0)
    const = lambda i: (0, 0)
    return pl.pallas_call(
        _ctxproj_kernel,
        out_shape=(jax.ShapeDtypeStruct((rows, KV_WIDTH), _BF16),
                   jax.ShapeDtypeStruct((rows, KV_WIDTH), _BF16)),
        grid=(rows // tm,),
        in_specs=[pl.BlockSpec((tm, D_MODEL), row),
                  pl.BlockSpec((1, N_MOD, D_MODEL), lambda i: (BATCH, 0, 0)),
                  pl.BlockSpec((1, D_MODEL), const),
                  pl.BlockSpec((D_MODEL, 2 * KV_WIDTH), const),
                  pl.BlockSpec((1, 2 * KV_WIDTH), const)],
        out_specs=(pl.BlockSpec((tm, KV_WIDTH), row),
                   pl.BlockSpec((tm, KV_WIDTH), row)),
        compiler_params=pltpu.CompilerParams(dimension_semantics=("arbitrary",),
                                             vmem_limit_bytes=VMEM_LIMIT),
        name="ctxproj",
    )(ctx2, mod3, g_pre, w_kv_b, b_kv2)


def _attn_kernel(sink_ref, q_ref, kp_ref, km_ref, kn_ref, vp_ref, vm_ref, vn_ref,
                 kc_ref, vc_ref, o_ref):
    n = pl.program_id(1)
    nstep = pl.num_programs(1)
    blk = ATTN_BLOCK
    rows2 = 2 * blk
    r = lax.broadcasted_iota(jnp.int32, (rows2, blk), 0) & (blk - 1)
    c = lax.broadcasted_iota(jnp.int32, (rows2, blk), 1)
    tri_prev = c >= r
    tri_next = c <= r
    row1 = lax.broadcasted_iota(jnp.int32, (rows2, 1), 0)
    lane_q = lax.broadcasted_iota(jnp.int32, (blk, LANES), 1)
    k_mid = km_ref[...]
    v_mid = vm_ref[...]
    pieces = [(kp_ref[...], k_mid[0:blk], k_mid[blk:], vp_ref[...], v_mid[0:blk], v_mid[blk:],
               tri_prev & (n > 0), tri_next),
              (k_mid[0:blk], k_mid[blk:], kn_ref[...], v_mid[0:blk], v_mid[blk:], vn_ref[...],
               tri_prev, tri_next & (n < nstep - 1))]
    n_grp = ATTN_WIDTH // LANES
    for t, (ka, kb, kc_, va, vb, vc_, mask_a, mask_c) in enumerate(pieces):
        keys = jnp.concatenate([ka, kb, kc_, kc_ref[...]], axis=0)
        vals = jnp.concatenate([va, vb, vc_, vc_ref[...]], axis=0)
        q_rows = []
        for j in range(n_grp):
            qg = q_ref[t * blk:(t + 1) * blk, j * LANES:(j + 1) * LANES]
            zero = jnp.zeros_like(qg)
            q_rows += [jnp.where(lane_q < HEAD_DIM, qg, zero), jnp.where(lane_q >= HEAD_DIM, qg, zero)]
        s_all = lax.dot_general(jnp.concatenate(q_rows, axis=0), keys, (((1,), (1,)), ((), ())),
                                preferred_element_type=_F32)
        p_rows, denoms = [], []
        for j in range(n_grp):
            s = s_all[j * rows2:(j + 1) * rows2]
            s = jnp.concatenate([jnp.where(mask_a, s[:, 0:blk], NEG_BIG), s[:, blk:2 * blk],
                                 jnp.where(mask_c, s[:, 2 * blk:3 * blk], NEG_BIG), s[:, 3 * blk:]],
                                axis=1)
            sk = jnp.where(row1 < blk, sink_ref[2 * j], sink_ref[2 * j + 1])
            m = jnp.maximum(jnp.max(s, axis=-1, keepdims=True), sk)
            p = jnp.exp(s - m)
            denoms.append(jnp.sum(p, axis=-1, keepdims=True) + jnp.exp(sk - m))
            p_rows.append(p.astype(_BF16))
        o_all = jnp.dot(jnp.concatenate(p_rows, axis=0), vals, preferred_element_type=_F32)
        for j in range(n_grp):
            o2 = o_all[j * rows2:(j + 1) * rows2] / denoms[j]
            og = jnp.where(lane_q < HEAD_DIM, o2[0:blk], o2[blk:])
            o_ref[t * blk:(t + 1) * blk, j * LANES:(j + 1) * LANES] = og.astype(_BF16)


def _attn_call(sink_p, q, k, v, kc, vc):
    nblk = SEQ // ATTN_BLOCK
    nstep = nblk // 2
    own = lambda b, n: (b * nstep + n, 0)
    prev = lambda b, n: (b * nblk + jnp.maximum(2 * n - 1, 0), 0)
    nxt = lambda b, n: (b * nblk + jnp.minimum(2 * n + 2, nblk - 1), 0)
    ctx = lambda b, n: (b, 0)
    kv1 = (ATTN_BLOCK, KV_WIDTH)
    kv2 = (2 * ATTN_BLOCK, KV_WIDTH)
    return pl.pallas_call(
        _attn_kernel,
        out_shape=jax.ShapeDtypeStruct((TOKENS, ATTN_WIDTH), _BF16),
        grid=(BATCH, nstep),
        in_specs=[pl.BlockSpec(memory_space=pltpu.SMEM),
                  pl.BlockSpec((2 * ATTN_BLOCK, ATTN_WIDTH), own),
                  pl.BlockSpec(kv1, prev), pl.BlockSpec(kv2, own), pl.BlockSpec(kv1, nxt),
                  pl.BlockSpec(kv1, prev), pl.BlockSpec(kv2, own), pl.BlockSpec(kv1, nxt),
                  pl.BlockSpec((CTX_LEN, KV_WIDTH), ctx),
                  pl.BlockSpec((CTX_LEN, KV_WIDTH), ctx)],
        out_specs=pl.BlockSpec((2 * ATTN_BLOCK, ATTN_WIDTH), own),
        compiler_params=pltpu.CompilerParams(dimension_semantics=("arbitrary", "arbitrary"),
                                             vmem_limit_bytes=VMEM_LIMIT),
        name="attn",
    )(sink_p, q, k, k, k, v, v, v, kc, vc)


def _post_kernel(x_ref, ao_ref, su_ref, sv_ref, mod_ref, lng_ref, lnb_ref, ws_ref, bs_ref,
                 ga_ref, gs_ref, woa_ref, wos_ref, bo_ref, gpost_ref, gpre_ref, wr_ref, br_ref,
                 tri_ref,
                 xmid_ref, h2_ref, idx_ref, gate_ref, rank_ref, cnt_ref,
                 mixed_ref, carry_ref):
    tm = x_ref.shape[0]
    m = mod_ref[0]

    @pl.when(pl.program_id(0) == 0)
    def _():
        carry_ref[...] = jnp.zeros_like(carry_ref)

    gv = _gelu_tanh(sv_ref[...])
    mu = jnp.mean(gv, axis=-1, keepdims=True)
    gc = gv - mu
    var = jnp.mean(gc * gc, axis=-1, keepdims=True)
    vb = (gc * lax.rsqrt(var + EPS) * lng_ref[...] + lnb_ref[...]).astype(_BF16)
    lane = lax.broadcasted_iota(jnp.int32, (SGU_CHUNK, LANES), 1)
    for c in range(tm // SGU_CHUNK):
        r0 = c * SGU_CHUNK
        for p in range(SGU_WIDTH // LANES):
            l0 = p * LANES
            r = jnp.dot(ws_ref[p], vb[r0:r0 + SGU_CHUNK, l0:l0 + LANES], preferred_element_type=_F32)
            mixed = jnp.where(lane < SGU_HEAD_DIM, r[0:SGU_CHUNK], r[SGU_CHUNK:])
            mixed_ref[r0:r0 + SGU_CHUNK, l0:l0 + LANES] = mixed + bs_ref[:, l0:l0 + LANES]
    sgu_o = _gelu_tanh(su_ref[...]) * mixed_ref[...]

    oa = _rms(ao_ref[...].astype(_F32), ga_ref[...]).astype(_BF16)
    os_ = _rms(sgu_o, gs_ref[...]).astype(_BF16)
    mix = (jnp.dot(oa, woa_ref[...], preferred_element_type=_F32)
           + jnp.dot(os_, wos_ref[...], preferred_element_type=_F32) + bo_ref[...])
    x_mid = x_ref[...] + m[2:3] * _rms(mix, gpost_ref[...])
    xmid_ref[...] = x_mid
    h2 = _rms(x_mid, gpre_ref[...]) * (1.0 + m[4:5]) + m[3:4]
    _store_row_tiles(h2_ref, h2)

    h_hi = h2.astype(_BF16)
    h_lo = (h2 - h_hi.astype(_F32)).astype(_BF16)
    r = (jnp.dot(h_hi, wr_ref[...], preferred_element_type=_F32)
         + jnp.dot(h_lo, wr_ref[...], preferred_element_type=_F32))
    lg = r + pltpu.roll(r, LANES - N_EXPERTS, 1) + br_ref[...]
    lane_r = lax.broadcasted_iota(jnp.int32, lg.shape, 1)
    lane_f = lane_r.astype(_F32)
    tops, hots = [], []
    for _k in range(TOP_K):
        mx = jnp.max(lg, axis=-1, keepdims=True)
        pick = jnp.min(jnp.where(lg == mx, lane_f, float(LANES)), axis=-1, keepdims=True)
        hot = lane_f == pick
        tops.append((mx, pick))
        hots.append(hot)
        lg = jnp.where(hot, 2.0 * NEG_BIG, lg)
    es = [jnp.exp(t[0] - tops[0][0]) for t in tops]
    esum = es[0] + es[1] + es[2] + es[3]
    multi = jnp.zeros(lg.shape, _F32)
    for hot in hots:
        multi = multi + jnp.where(hot, 1.0, 0.0)
    cum = jnp.dot(tri_ref[...], multi.astype(_BF16), preferred_element_type=_F32) + carry_ref[...]
    tok_per_row = LANES // TOP_K
    row_r = lax.broadcasted_iota(jnp.int32, lg.shape, 0)
    lane_base = (row_r & (tok_per_row - 1)) * TOP_K
    gate_o = jnp.zeros(lg.shape, _F32)
    idx_e = jnp.zeros(lg.shape, _F32)
    rank_hi_e = jnp.zeros(lg.shape, _F32)
    rank_lo_e = jnp.zeros(lg.shape, _F32)
    for kk in range(TOP_K):
        rk = jnp.sum(jnp.where(hots[kk], cum, 0.0), axis=-1, keepdims=True)
        rk_hi = jnp.floor(rk * (1.0 / 256.0))
        here = lane_r == lane_base + kk
        gate_o = jnp.where(lane_r == kk, es[kk] / esum, gate_o)
        idx_e = jnp.where(here, tops[kk][1], idx_e)
        rank_hi_e = jnp.where(here, rk_hi, rank_hi_e)
        rank_lo_e = jnp.where(here, rk - 256.0 * rk_hi, rank_lo_e)
    fr = lax.broadcasted_iota(jnp.int32, (tm // tok_per_row, tm), 0)
    fc = lax.broadcasted_iota(jnp.int32, (tm // tok_per_row, tm), 1)
    fold = jnp.where(lax.shift_right_logical(fc, tok_per_row.bit_length() - 1) == fr, 1.0, 0.0).astype(_BF16)
    idx_ref[...] = jnp.dot(fold, idx_e.astype(_BF16), preferred_element_type=_F32).astype(jnp.int32)
    rank_ref[...] = (256.0 * jnp.dot(fold, rank_hi_e.astype(_BF16), preferred_element_type=_F32)
                     + jnp.dot(fold, rank_lo_e.astype(_BF16), preferred_element_type=_F32)).astype(jnp.int32)
    gate_ref[...] = gate_o
    carry_ref[...] += jnp.sum(multi, axis=0, keepdims=True)
    cnt_ref[...] = carry_ref[...].astype(jnp.int32)


def _post_call(x2, ao, su, sv, mod3, lng, lnb, ws2, bsf, ga, gs, woa, wos, bo, gpost, gpre,
               wr, br, tri):
    tm = ROW_TILE
    tiles_per_batch = SEQ // tm
    row = lambda i: (i, 0)
    const = lambda i: (0, 0)
    const3 = lambda i: (0, 0, 0)
    return pl.pallas_call(
        _post_kernel,
        out_shape=(jax.ShapeDtypeStruct((TOKENS, D_MODEL), _F32),
                   jax.ShapeDtypeStruct((TOKENS * ROW_SUB, LANES), _F32),
                   jax.ShapeDtypeStruct((TOKENS * TOP_K // LANES, LANES), jnp.int32),
                   jax.ShapeDtypeStruct((TOKENS, LANES), _F32),
                   jax.ShapeDtypeStruct((TOKENS * TOP_K // LANES, LANES), jnp.int32),
                   jax.ShapeDtypeStruct((1, LANES), jnp.int32)),
        grid=(TOKENS // tm,),
        in_specs=[pl.BlockSpec((tm, D_MODEL), row),
                  pl.BlockSpec((tm, ATTN_WIDTH), row),
                  pl.BlockSpec((tm, SGU_WIDTH), row),
                  pl.BlockSpec((tm, SGU_WIDTH), row),
                  pl.BlockSpec((1, N_MOD, D_MODEL), lambda i: (i // tiles_per_batch, 0, 0)),
                  pl.BlockSpec((1, SGU_WIDTH), const),
                  pl.BlockSpec((1, SGU_WIDTH), const),
                  pl.BlockSpec((SGU_WIDTH // LANES, 2 * SGU_CHUNK, SGU_CHUNK), const3),
                  pl.BlockSpec((SGU_CHUNK, SGU_WIDTH), const),
                  pl.BlockSpec((1, ATTN_WIDTH), const),
                  pl.BlockSpec((1, SGU_WIDTH), const),
                  pl.BlockSpec((ATTN_WIDTH, D_MODEL), const),
                  pl.BlockSpec((SGU_WIDTH, D_MODEL), const),
                  pl.BlockSpec((1, D_MODEL), const),
                  pl.BlockSpec((1, D_MODEL), const),
                  pl.BlockSpec((1, D_MODEL), const),
                  pl.BlockSpec((D_MODEL, LANES), const),
                  pl.BlockSpec((1, LANES), const),
                  pl.BlockSpec((tm, tm), const)],
        out_specs=(pl.BlockSpec((tm, D_MODEL), row),
                   pl.BlockSpec((tm * ROW_SUB, LANES), row),
                   pl.BlockSpec((tm * TOP_K // LANES, LANES), row),
                   pl.BlockSpec((tm, LANES), row),
                   pl.BlockSpec((tm * TOP_K // LANES, LANES), row),
                   pl.BlockSpec((1, LANES), const)),
        scratch_shapes=[pltpu.VMEM((tm, SGU_WIDTH), _F32),
                        pltpu.VMEM((1, LANES), _F32)],
        compiler_params=pltpu.CompilerParams(dimension_semantics=("arbitrary",),
                                             vmem_limit_bytes=VMEM_LIMIT),
        name="post",
    )(x2, ao, su, sv, mod3, lng, lnb, ws2, bsf, ga, gs, woa, wos, bo, gpost, gpre, wr, br, tri)


def _dispatch_kernel(fs_ref, fl_ref, nu_ref, dest_ref, h2_ref, xs_ref, zero_ref, sem, zsem):
    tm = h2_ref.shape[0] // ROW_SUB
    n_token_steps = TOKENS // tm
    i = pl.program_id(0)

    @pl.when(i < n_token_steps)
    def _():
        def issue(r, carry):
            for kk in range(TOP_K):
                d = pl.multiple_of(dest_ref[r * TOP_K + kk] * ROW_SUB, ROW_SUB)
                pltpu.make_async_copy(h2_ref.at[pl.ds(pl.multiple_of(r * ROW_SUB, ROW_SUB), ROW_SUB)],
                                      xs_ref.at[pl.ds(d, ROW_SUB)], sem).start(priority=kk % 2)
            return carry

        lax.fori_loop(0, tm, issue, 0, unroll=8)
        for kk in range(TOP_K):
            pltpu.make_async_copy(h2_ref, xs_ref.at[pl.ds(0, tm * ROW_SUB)], sem).wait()

    @pl.when(i == n_token_steps)
    def _():
        zero_ref[...] = jnp.zeros_like(zero_ref)
        block_rows = EXPERT_BLOCK * ROW_SUB

        def pad_run(e, wait):
            pos = fs_ref[e]
            length = fl_ref[e]
            for bit in reversed(range(EXPERT_BLOCK.bit_length() - 1)):
                size = 1 << bit
                take = length & size

                @pl.when(take != 0)
                def _():
                    cp = pltpu.make_async_copy(
                        zero_ref.at[pl.ds(0, size * ROW_SUB)],
                        xs_ref.at[pl.ds(pl.multiple_of(pos * ROW_SUB, ROW_SUB), size * ROW_SUB)], zsem)
                    if wait:
                        cp.wait()
                    else:
                        cp.start()

                pos = pos + take

        def tail_block(blk, wait):
            cp = pltpu.make_async_copy(
                zero_ref, xs_ref.at[pl.ds(pl.multiple_of(blk * block_rows, block_rows), block_rows)], zsem)
            if wait:
                cp.wait()
            else:
                cp.start()

        for wait in (False, True):
            lax.fori_loop(0, N_EXPERTS, lambda e, c, w=wait: (pad_run(e, w), c)[1], 0)
            lax.fori_loop(nu_ref[0], N_EXPERT_BLOCKS, lambda blk, c, w=wait: (tail_block(blk, w), c)[1], 0)


def _dispatch_call(fill_start, fill_len, n_used, dest_flat, h2):
    tm = MOVE_TILE
    n_token_steps = TOKENS // tm
    grid_spec = pltpu.PrefetchScalarGridSpec(
        num_scalar_prefetch=3,
        grid=(n_token_steps + 1,),
        in_specs=[pl.BlockSpec((tm * TOP_K,), lambda i, fs, fl, nu: (jnp.minimum(i, n_token_steps - 1),),
                               memory_space=pltpu.SMEM),
                  pl.BlockSpec((tm * ROW_SUB, LANES),
                               lambda i, fs, fl, nu: (jnp.minimum(i, n_token_steps - 1), 0))],
        out_specs=pl.BlockSpec(memory_space=pl.ANY),
        scratch_shapes=[pltpu.VMEM((EXPERT_BLOCK * ROW_SUB, LANES), _F32),
                        pltpu.SemaphoreType.DMA,
                        pltpu.SemaphoreType.DMA])
    return pl.pallas_call(
        _dispatch_kernel,
        out_shape=jax.ShapeDtypeStruct((SORTED_ROWS * ROW_SUB, LANES), _F32),
        grid_spec=grid_spec,
        compiler_params=pltpu.CompilerParams(dimension_semantics=("arbitrary",),
                                             vmem_limit_bytes=VMEM_LIMIT),
        name="dispatch",
    )(fill_start, fill_len, n_used, dest_flat, h2)


def _expert_kernel(be_ref, slot_ref, first_ref, nxt_ref, nu_ref,
                   xs_ref, wgu_hbm, bgu_ref, wd_hbm, bd_ref, y_ref, wgu_buf, wd_buf, wgu_bf, wd_bf, sem):
    b = pl.program_id(0)

    def weight_copies(e, slot):
        return (pltpu.make_async_copy(wgu_hbm.at[e], wgu_buf.at[slot], sem.at[0, slot]),
                pltpu.make_async_copy(wd_hbm.at[e], wd_buf.at[slot], sem.at[1, slot]))

    @pl.when(b < nu_ref[0])
    def _():
        e = be_ref[b]
        slot = slot_ref[b]

        @pl.when(b == 0)
        def _():
            for cp in weight_copies(e, slot):
                cp.start()

        @pl.when(first_ref[b] == 1)
        def _():
            for cp in weight_copies(e, slot):
                cp.wait()

            @pl.when(nxt_ref[b] < N_EXPERTS)
            def _():
                for cp in weight_copies(nxt_ref[b], 1 - slot):
                    cp.start()

            wgu_bf[...] = wgu_buf[slot].astype(_BF16)
            wd_bf[...] = wd_buf[slot].astype(_BF16)

        x = _load_row_tiles(xs_ref, EXPERT_BLOCK).astype(_BF16)
        gu = jnp.dot(x, wgu_bf[...], preferred_element_type=_F32) + bgu_ref[0]
        gate = jnp.minimum(gu[:, 0:D_FF_EXPERT], SWIGLU_LIMIT)
        up = jnp.clip(gu[:, D_FF_EXPERT:], -SWIGLU_LIMIT, SWIGLU_LIMIT)
        act = (up + 1.0) * gate * (1.0 / (1.0 + jnp.exp(-SWIGLU_ALPHA * gate)))
        y = jnp.dot(act.astype(_BF16), wd_bf[...], preferred_element_type=_F32) + bd_ref[0]
        _store_row_tiles(y_ref, y)

    @pl.when(b >= nu_ref[0])
    def _():
        y_ref[...] = jnp.zeros_like(y_ref)


def _expert_call(block_expert, block_slot, block_first, block_next, n_used, xs, wgu, bgu3, wd, bd3):
    tb = EXPERT_BLOCK
    live = lambda b, be, sl, fi, nx, nu: (jnp.minimum(b, nu[0] - 1), 0)
    bsel = lambda b, be, sl, fi, nx, nu: (be[b], 0, 0)
    grid_spec = pltpu.PrefetchScalarGridSpec(
        num_scalar_prefetch=5,
        grid=(N_EXPERT_BLOCKS,),
        in_specs=[pl.BlockSpec((tb * ROW_SUB, LANES), live),
                  pl.BlockSpec(memory_space=pl.ANY),
                  pl.BlockSpec((1, 1, 2 * D_FF_EXPERT), bsel),
                  pl.BlockSpec(memory_space=pl.ANY),
                  pl.BlockSpec((1, 1, D_MODEL), bsel)],
        out_specs=pl.BlockSpec((tb * ROW_SUB, LANES), lambda b, be, sl, fi, nx, nu: (b, 0)),
        scratch_shapes=[pltpu.VMEM((2, D_MODEL, 2 * D_FF_EXPERT), _F32),
                        pltpu.VMEM((2, D_FF_EXPERT, D_MODEL), _F32),
                        pltpu.VMEM((D_MODEL, 2 * D_FF_EXPERT), _BF16),
                        pltpu.VMEM((D_FF_EXPERT, D_MODEL), _BF16),
                        pltpu.SemaphoreType.DMA((2, 2))])
    return pl.pallas_call(
        _expert_kernel,
        out_shape=jax.ShapeDtypeStruct((SORTED_ROWS * ROW_SUB, LANES), _F32),
        grid_spec=grid_spec,
        compiler_params=pltpu.CompilerParams(dimension_semantics=("arbitrary",),
                                             vmem_limit_bytes=VMEM_LIMIT),
        name="expert",
    )(block_expert, block_slot, block_first, block_next, n_used, xs, wgu, bgu3, wd, bd3)


def _combine_kernel(dest_ref, dest_next_ref, gate_ref, xmid_ref, mod_ref, g_ref, y_ref, o_ref,
                    buf_ref, sem):
    tm = xmid_ref.shape[0]
    m = mod_ref[0]
    i = pl.program_id(0)
    n = pl.num_programs(0)

    def gather_rows(idx_ref, slot):
        def issue(r, carry):
            for kk in range(TOP_K):
                d = pl.multiple_of(idx_ref[r * TOP_K + kk] * ROW_SUB, ROW_SUB)
                pltpu.make_async_copy(
                    y_ref.at[pl.ds(d, ROW_SUB)],
                    buf_ref.at[slot, kk, pl.ds(pl.multiple_of(r * ROW_SUB, ROW_SUB), ROW_SUB)],
                    sem.at[slot]).start(priority=kk % 2)
            return carry

        lax.fori_loop(0, tm, issue, 0, unroll=8)

    @pl.when(i == 0)
    def _():
        gather_rows(dest_ref, 0)

    @pl.when(i + 1 < n)
    def _():
        gather_rows(dest_next_ref, (i + 1) % 2)

    slot = i % 2
    for kk in range(TOP_K):
        pltpu.make_async_copy(y_ref.at[pl.ds(0, tm * ROW_SUB)], buf_ref.at[slot, kk],
                              sem.at[slot]).wait()
    g = gate_ref[...]
    pieces = []
    for cc in range(ROW_SUB):
        piece = buf_ref[slot, 0, pl.ds(cc, tm, stride=ROW_SUB), :] * g[:, 0:1]
        for kk in range(1, TOP_K):
            piece = piece + buf_ref[slot, kk, pl.ds(cc, tm, stride=ROW_SUB), :] * g[:, kk:kk + 1]
        pieces.append(piece)
    ffn = jnp.concatenate(pieces, axis=1)
    o_ref[...] = xmid_ref[...] + m[5:6] * _rms(ffn, g_ref[...])


def _combine_call(dest_flat, gates, x_mid, mod3, gpost_ffn, y_rows):
    tm = MOVE_TILE
    tiles_per_batch = SEQ // tm
    n_steps = TOKENS // tm
    return pl.pallas_call(
        _combine_kernel,
        out_shape=jax.ShapeDtypeStruct((TOKENS, D_MODEL), _F32),
        grid=(n_steps,),
        in_specs=[pl.BlockSpec((tm * TOP_K,), lambda i: (i,), memory_space=pltpu.SMEM),
                  pl.BlockSpec((tm * TOP_K,), lambda i: (jnp.minimum(i + 1, n_steps - 1),),
                               memory_space=pltpu.SMEM),
                  pl.BlockSpec((tm, LANES), lambda i: (i, 0)),
                  pl.BlockSpec((tm, D_MODEL), lambda i: (i, 0)),
                  pl.BlockSpec((1, N_MOD, D_MODEL), lambda i: (i // tiles_per_batch, 0, 0)),
                  pl.BlockSpec((1, D_MODEL), lambda i: (0, 0)),
                  pl.BlockSpec(memory_space=pl.ANY)],
        out_specs=pl.BlockSpec((tm, D_MODEL), lambda i: (i, 0)),
        scratch_shapes=[pltpu.VMEM((2, TOP_K, tm * ROW_SUB, LANES), _F32),
                        pltpu.SemaphoreType.DMA((2,))],
        compiler_params=pltpu.CompilerParams(dimension_semantics=("arbitrary",),
                                             vmem_limit_bytes=VMEM_LIMIT),
        name="combine",
    )(dest_flat, dest_flat, gates, x_mid, mod3, gpost_ffn, y_rows)


def _rope_tables():
    pos = jnp.arange(SEQ, dtype=_F32)
    pos_row = jnp.floor(pos / GRID_W)
    pos_col = pos - pos_row * GRID_W
    n_freq = HEAD_DIM // 4
    inv_freq = ROPE_THETA ** (-jnp.arange(n_freq, dtype=_F32) / n_freq)
    d = jnp.arange(LANES) % HEAD_DIM
    f = inv_freq[d % n_freq]
    ang = jnp.where((d < HEAD_DIM // 2)[None, :], pos_row[:, None] * f[None, :], pos_col[:, None] * f[None, :])
    sign = jnp.where((d % (HEAD_DIM // 2)) < n_freq, -1.0, 1.0).astype(_F32)
    return jnp.cos(ang), jnp.sin(ang) * sign[None, :]


def _perm_heads(a, axis):
    shape = a.shape
    a = a.reshape(shape[:axis] + (N_HEADS, HEAD_DIM) + shape[axis + 1:])
    a = jnp.take(a, jnp.array(HEAD_PERM), axis=axis)
    return a.reshape(shape)


def kernel(x, c, ctx, c_ctx, w_ada, b_ada, g_pre_mix, g_post_mix, g_pre_ffn, g_post_ffn, w_in, b_in, attn_sink, sgu_ln_g, sgu_ln_b, sgu_w, sgu_b, g_attn_out, g_sgu_out, w_out, b_out, w_router, b_router, w_gate_up, b_gate_up, w_down, b_down):
    l = 0
    x2 = x.reshape(TOKENS, D_MODEL)
    ctx2 = ctx.reshape(BATCH * CTX_LEN, D_MODEL)

    cc = jnp.zeros((16, D_MODEL), _F32).at[:BATCH].set(c).at[BATCH].set(c_ctx)
    mod = _ada_call(cc, w_ada[l], b_ada[l].reshape(1, -1))
    mod3 = mod.reshape(16, N_MOD, D_MODEL)

    w_in_l = w_in[l]
    b_in_l = b_in[l]
    w_in_p = jnp.concatenate([_perm_heads(w_in_l[:, :ATTN_WIDTH], 1), w_in_l[:, ATTN_WIDTH:]], axis=1)
    b_in_p = jnp.concatenate([_perm_heads(b_in_l[:ATTN_WIDTH], 0), b_in_l[ATTN_WIDTH:]], axis=0)
    cos_t, sin_t = _rope_tables()
    g_pre = g_pre_mix[l].reshape(1, -1)

    q, k, v, su, sv = _inproj_call(x2, mod3, g_pre, w_in_p.astype(_BF16), b_in_p.reshape(1, -1), cos_t, sin_t)
    kc, vc = _ctxproj_call(ctx2, mod3, g_pre, w_in_l[:, KV_START:KV_END].astype(_BF16),
                           b_in_l[KV_START:KV_END].reshape(1, -1))

    sink_p = jnp.take(attn_sink[l], jnp.array(HEAD_PERM))
    ao = _attn_call(sink_p, q, k, v, kc, vc)

    ws2 = sgu_w[l].reshape(SGU_WIDTH // LANES, 2 * SGU_CHUNK, SGU_CHUNK).astype(_BF16)
    bsf = jnp.repeat(sgu_b[l].T, SGU_HEAD_DIM, axis=1)
    w_out_l = w_out[l]
    woa = _perm_heads(w_out_l[:ATTN_WIDTH], 0).astype(_BF16)
    wos = w_out_l[ATTN_WIDTH:].astype(_BF16)
    ga = _perm_heads(g_attn_out[l], 0).reshape(1, -1)
    wr_hi = w_router[l].astype(_BF16)
    wr_lo = (w_router[l] - wr_hi.astype(_F32)).astype(_BF16)
    wr = (jnp.zeros((D_MODEL, LANES), _BF16).at[:, :N_EXPERTS].set(wr_hi)
          .at[:, N_EXPERTS:2 * N_EXPERTS].set(wr_lo))
    br = jnp.full((1, LANES), NEG_BIG, _F32).at[0, :N_EXPERTS].set(b_router[l])
    ii = jnp.arange(ROW_TILE)
    tri = (ii[None, :] < ii[:, None]).astype(_BF16)

    x_mid, h2, idx_o, gate_o, rank_o, cnt_o = _post_call(
        x2, ao, su, sv, mod3, sgu_ln_g[l].reshape(1, -1), sgu_ln_b[l].reshape(1, -1), ws2, bsf,
        ga, g_sgu_out[l].reshape(1, -1), woa, wos, b_out[l].reshape(1, -1),
        g_post_mix[l].reshape(1, -1), g_pre_ffn[l].reshape(1, -1), wr, br, tri)

    counts = cnt_o[0, :N_EXPERTS]
    padded = (counts + EXPERT_BLOCK - 1) // EXPERT_BLOCK * EXPERT_BLOCK
    pad_end = jnp.cumsum(padded)
    pad_start = pad_end - padded
    dest = (pad_start[idx_o] + rank_o).astype(jnp.int32).reshape(-1)
    block_start = jnp.arange(N_EXPERT_BLOCKS, dtype=jnp.int32) * EXPERT_BLOCK
    block_expert = jnp.minimum(jnp.sum(pad_end[None, :] <= block_start[:, None], axis=1),
                               N_EXPERTS - 1).astype(jnp.int32)
    n_used = (pad_end[-1:] // EXPERT_BLOCK).astype(jnp.int32)

    xs = _dispatch_call((pad_start + counts).astype(jnp.int32), (padded - counts).astype(jnp.int32),
                        n_used, dest, h2)

    used = padded > 0
    e_ids = jnp.arange(N_EXPERTS, dtype=jnp.int32)
    cand = jnp.where(used, e_ids, N_EXPERTS)
    later_min = lax.cummin(cand[::-1])[::-1]
    next_used = jnp.concatenate([later_min[1:], jnp.full((1,), N_EXPERTS, jnp.int32)])
    ordinal = jnp.cumsum(used.astype(jnp.int32)) - 1
    block_slot = (ordinal[block_expert] & 1).astype(jnp.int32)
    block_first = (block_start == pad_start[block_expert]).astype(jnp.int32)
    block_next = next_used[block_expert].astype(jnp.int32)
    y_rows = _expert_call(block_expert, block_slot, block_first, block_next, n_used, xs,
                          w_gate_up[l], b_gate_up[l].reshape(N_EXPERTS, 1, -1),
                          w_down[l], b_down[l].reshape(N_EXPERTS, 1, -1))
    out = _combine_call(dest, gate_o, x_mid, mod3, g_post_ffn[l].reshape(1, -1), y_rows)
    return out.reshape(BATCH, SEQ, D_MODEL)
```

```python
import functools

import jax
import jax.numpy as jnp
from jax import lax
from jax.experimental import pallas as pl
from jax.experimental.pallas import tpu as pltpu

D_MODEL = 1024
BATCH = 8
SEQ = 4096
TOKENS = BATCH * SEQ
GRID_W = 64
CTX_LEN = 256
N_HEADS = 8
N_KV_HEADS = 2
HEAD_DIM = 64
ATTN_WIDTH = N_HEADS * HEAD_DIM
KV_WIDTH = N_KV_HEADS * HEAD_DIM
WINDOW = 128
ATTN_BLOCK = 128
SGU_HEADS = 8
SGU_HEAD_DIM = 64
SGU_WIDTH = SGU_HEADS * SGU_HEAD_DIM
SGU_CHUNK = 128
KV_START = ATTN_WIDTH
KV_END = ATTN_WIDTH + 2 * KV_WIDTH
IN_WIDTH = KV_END + 2 * SGU_WIDTH
N_EXPERTS = 32
TOP_K = 4
D_FF_EXPERT = 1024
SWIGLU_LIMIT = 7.0
SWIGLU_ALPHA = 1.702
ROPE_THETA = 10000.0
EPS = 1e-6
N_MOD = 6

LANES = 128
NEG_BIG = -1e30
VMEM_LIMIT = 56 * 1024 * 1024

ROW_TILE = 512
MOVE_TILE = 512
ATTN_STEP_BLOCKS = 4
EXPERT_BLOCK = 512
N_EXPERT_BLOCKS = TOKENS * TOP_K // EXPERT_BLOCK + N_EXPERTS
SORTED_ROWS = N_EXPERT_BLOCKS * EXPERT_BLOCK
HEAD_PERM = (0, 4, 1, 5, 2, 6, 3, 7)

_F32 = jnp.float32
_BF16 = jnp.bfloat16


def _rms(x, g):
    ms = jnp.mean(x * x, axis=-1, keepdims=True)
    return x * lax.rsqrt(ms + EPS) * g


ROW_SUB = D_MODEL // LANES


def _store_row_tiles(ref, x):
    n = x.shape[0]
    for cc in range(ROW_SUB):
        ref[pl.ds(cc, n, stride=ROW_SUB), :] = x[:, cc * LANES:(cc + 1) * LANES]


def _load_row_tiles(ref, n):
    return jnp.concatenate([ref[pl.ds(cc, n, stride=ROW_SUB), :] for cc in range(ROW_SUB)], axis=1)


def _gelu_tanh(x):
    c = 0.7978845608028654
    return x * (0.5 * (1.0 + jnp.tanh(c * (x + 0.044715 * (x * x * x)))))


def _ada_kernel(c_ref, w_ref, b_ref, o_ref):
    c = c_ref[...]
    a = c / (1.0 + jnp.exp(-c))
    o_ref[...] = jnp.dot(a, w_ref[...], preferred_element_type=_F32,
                         precision=lax.Precision.HIGHEST) + b_ref[...]


def _ada_call(cc, w_ada, b_ada):
    n = N_MOD * D_MODEL
    tn = 512
    return pl.pallas_call(
        _ada_kernel,
        out_shape=jax.ShapeDtypeStruct((16, n), _F32),
        grid=(n // tn,),
        in_specs=[pl.BlockSpec((16, D_MODEL), lambda j: (0, 0)),
                  pl.BlockSpec((D_MODEL, tn), lambda j: (0, j)),
                  pl.BlockSpec((1, tn), lambda j: (0, j))],
        out_specs=pl.BlockSpec((16, tn), lambda j: (0, j)),
        compiler_params=pltpu.CompilerParams(dimension_semantics=("arbitrary",),
                                             vmem_limit_bytes=VMEM_LIMIT),
        name="ada",
    )(cc, w_ada, b_ada)


def _rope(x, cos, sin_signed, first_half):
    nxt = pltpu.roll(x, LANES - 16, 1)
    prv = pltpu.roll(x, 16, 1)
    return x * cos + jnp.where(first_half, nxt, prv) * sin_signed


def _inproj_kernel(x_ref, mod_ref, g_ref, w_ref, b_ref, cos_ref, sin_ref,
                   q_ref, k_ref, v_ref, su_ref, sv_ref):
    m = mod_ref[0]
    h = _rms(x_ref[...], g_ref[...]) * (1.0 + m[1:2]) + m[0:1]
    hb = h.astype(_BF16)
    cos = cos_ref[...]
    sin = sin_ref[...]
    lane = lax.broadcasted_iota(jnp.int32, cos.shape, 1)
    first_half = (lane & 31) < 16

    q = jnp.dot(hb, w_ref[:, 0:ATTN_WIDTH], preferred_element_type=_F32) + b_ref[:, 0:ATTN_WIDTH]
    for j in range(ATTN_WIDTH // LANES):
        qj = _rope(q[:, j * LANES:(j + 1) * LANES], cos, sin, first_half)
        q_ref[:, j * LANES:(j + 1) * LANES] = (qj * (HEAD_DIM ** -0.5)).astype(_BF16)
    kv = jnp.dot(hb, w_ref[:, KV_START:KV_END], preferred_element_type=_F32) + b_ref[:, KV_START:KV_END]
    k_ref[...] = _rope(kv[:, 0:KV_WIDTH], cos, sin, first_half).astype(_BF16)
    v_ref[...] = kv[:, KV_WIDTH:].astype(_BF16)
    u0 = KV_END
    u1 = KV_END + SGU_WIDTH
    su_ref[...] = jnp.dot(hb, w_ref[:, u0:u1], preferred_element_type=_F32) + b_ref[:, u0:u1]
    sv_ref[...] = jnp.dot(hb, w_ref[:, u1:IN_WIDTH], preferred_element_type=_F32) + b_ref[:, u1:IN_WIDTH]


def _inproj_call(x2, mod3, g_pre, w_in_b, b_in2, cos_t, sin_t):
    tm = ROW_TILE
    tiles_per_batch = SEQ // tm
    row = lambda i: (i, 0)
    const = lambda i: (0, 0)
    return pl.pallas_call(
        _inproj_kernel,
        out_shape=(jax.ShapeDtypeStruct((TOKENS, ATTN_WIDTH), _BF16),
                   jax.ShapeDtypeStruct((TOKENS, KV_WIDTH), _BF16),
                   jax.ShapeDtypeStruct((TOKENS, KV_WIDTH), _BF16),
                   jax.ShapeDtypeStruct((TOKENS, SGU_WIDTH), _F32),
                   jax.ShapeDtypeStruct((TOKENS, SGU_WIDTH), _F32)),
        grid=(TOKENS // tm,),
        in_specs=[pl.BlockSpec((tm, D_MODEL), row),
                  pl.BlockSpec((1, N_MOD, D_MODEL), lambda i: (i // tiles_per_batch, 0, 0)),
                  pl.BlockSpec((1, D_MODEL), const),
                  pl.BlockSpec((D_MODEL, IN_WIDTH), const),
                  pl.BlockSpec((1, IN_WIDTH), const),
                  pl.BlockSpec((tm, LANES), lambda i: (i % tiles_per_batch, 0)),
                  pl.BlockSpec((tm, LANES), lambda i: (i % tiles_per_batch, 0))],
        out_specs=(pl.BlockSpec((tm, ATTN_WIDTH), row),
                   pl.BlockSpec((tm, KV_WIDTH), row),
                   pl.BlockSpec((tm, KV_WIDTH), row),
                   pl.BlockSpec((tm, SGU_WIDTH), row),
                   pl.BlockSpec((tm, SGU_WIDTH), row)),
        compiler_params=pltpu.CompilerParams(dimension_semantics=("arbitrary",),
                                             vmem_limit_bytes=VMEM_LIMIT),
        name="inproj",
    )(x2, mod3, g_pre, w_in_b, b_in2, cos_t, sin_t)


def _ctxproj_kernel(x_ref, mod_ref, g_ref, w_ref, b_ref, k_ref, v_ref):
    m = mod_ref[0]
    h = _rms(x_ref[...], g_ref[...]) * (1.0 + m[1:2]) + m[0:1]
    kv = jnp.dot(h.astype(_BF16), w_ref[...], preferred_element_type=_F32) + b_ref[...]
    k_ref[...] = kv[:, 0:KV_WIDTH].astype(_BF16)
    v_ref[...] = kv[:, KV_WIDTH:].astype(_BF16)


def _ctxproj_call(ctx2, mod3, g_pre, w_kv_b, b_kv2):
    tm = ROW_TILE
    rows = ctx2.shape[0]
    row = lambda i: (i, 0)
    const = lambda i: (0, 0)
    return pl.pallas_call(
        _ctxproj_kernel,
        out_shape=(jax.ShapeDtypeStruct((rows, KV_WIDTH), _BF16),
                   jax.ShapeDtypeStruct((rows, KV_WIDTH), _BF16)),
        grid=(rows // tm,),
        in_specs=[pl.BlockSpec((tm, D_MODEL), row),
                  pl.BlockSpec((1, N_MOD, D_MODEL), lambda i: (BATCH, 0, 0)),
                  pl.BlockSpec((1, D_MODEL), const),
                  pl.BlockSpec((D_MODEL, 2 * KV_WIDTH), const),
                  pl.BlockSpec((1, 2 * KV_WIDTH), const)],
        out_specs=(pl.BlockSpec((tm, KV_WIDTH), row),
                   pl.BlockSpec((tm, KV_WIDTH), row)),
        compiler_params=pltpu.CompilerParams(dimension_semantics=("arbitrary",),
                                             vmem_limit_bytes=VMEM_LIMIT),
        name="ctxproj",
    )(ctx2, mod3, g_pre, w_kv_b, b_kv2)


def _attn_kernel(sink_ref, q_ref, kp_ref, km_ref, kn_ref, vp_ref, vm_ref, vn_ref,
                 kc_ref, vc_ref, o_ref):
    n = pl.program_id(1)
    nstep = pl.num_programs(1)
    blk = ATTN_BLOCK
    nq = ATTN_STEP_BLOCKS
    rows2 = 2 * blk
    r = lax.broadcasted_iota(jnp.int32, (rows2, blk), 0) & (blk - 1)
    c = lax.broadcasted_iota(jnp.int32, (rows2, blk), 1)
    tri_prev = c >= r
    tri_next = c <= r
    row1 = lax.broadcasted_iota(jnp.int32, (rows2, 1), 0)
    lane_q = lax.broadcasted_iota(jnp.int32, (blk, LANES), 1)
    n_grp = ATTN_WIDTH // LANES
    k_blocks = [kp_ref[...]] + [km_ref[t * blk:(t + 1) * blk, :] for t in range(nq)] + [kn_ref[...]]
    v_blocks = [vp_ref[...]] + [vm_ref[t * blk:(t + 1) * blk, :] for t in range(nq)] + [vn_ref[...]]

    def scores(t):
        keys = jnp.concatenate(k_blocks[t:t + 3] + [kc_ref[...]], axis=0)
        q_rows = []
        for j in range(n_grp):
            qg = q_ref[t * blk:(t + 1) * blk, j * LANES:(j + 1) * LANES]
            zero = jnp.zeros_like(qg)
            q_rows += [jnp.where(lane_q < HEAD_DIM, qg, zero), jnp.where(lane_q >= HEAD_DIM, qg, zero)]
        return lax.dot_general(jnp.concatenate(q_rows, axis=0), keys, (((1,), (1,)), ((), ())),
                               preferred_element_type=_F32)

    def softmax(t, s_all):
        mask_a = (tri_prev & (n > 0)) if t == 0 else tri_prev
        mask_c = (tri_next & (n < nstep - 1)) if t == nq - 1 else tri_next
        p_rows, denoms = [], []
        for j in range(n_grp):
            s = s_all[j * rows2:(j + 1) * rows2]
            s = jnp.concatenate([jnp.where(mask_a, s[:, 0:blk], NEG_BIG), s[:, blk:2 * blk],
                                 jnp.where(mask_c, s[:, 2 * blk:3 * blk], NEG_BIG), s[:, 3 * blk:]],
                                axis=1)
            sk = jnp.where(row1 < blk, sink_ref[2 * j], sink_ref[2 * j + 1])
            m = jnp.maximum(jnp.max(s, axis=-1, keepdims=True), sk)
            p = jnp.exp(s - m)
            denoms.append(jnp.sum(p, axis=-1, keepdims=True) + jnp.exp(sk - m))
            p_rows.append(p.astype(_BF16))
        return jnp.concatenate(p_rows, axis=0), denoms

    def weighted_values(t, p_all, denoms):
        vals = jnp.concatenate(v_blocks[t:t + 3] + [vc_ref[...]], axis=0)
        o_all = jnp.dot(p_all, vals, preferred_element_type=_F32)
        for j in range(n_grp):
            o2 = o_all[j * rows2:(j + 1) * rows2] / denoms[j]
            og = jnp.where(lane_q < HEAD_DIM, o2[0:blk], o2[blk:])
            o_ref[t * blk:(t + 1) * blk, j * LANES:(j + 1) * LANES] = og.astype(_BF16)

    s_next = scores(0)
    probs = None
    for t in range(nq + 1):
        s_cur, s_next = s_next, (scores(t + 1) if t + 1 < nq else None)
        if probs is not None:
            weighted_values(t - 1, *probs)
        probs = softmax(t, s_cur) if t < nq else None


def _attn_call(sink_p, q, k, v, kc, vc):
    nblk = SEQ // ATTN_BLOCK
    nq = ATTN_STEP_BLOCKS
    nstep = nblk // nq
    own = lambda b, n: (b * nstep + n, 0)
    prev = lambda b, n: (b * nblk + jnp.maximum(nq * n - 1, 0), 0)
    nxt = lambda b, n: (b * nblk + jnp.minimum(nq * n + nq, nblk - 1), 0)
    ctx = lambda b, n: (b, 0)
    kv1 = (ATTN_BLOCK, KV_WIDTH)
    kvm = (nq * ATTN_BLOCK, KV_WIDTH)
    return pl.pallas_call(
        _attn_kernel,
        out_shape=jax.ShapeDtypeStruct((TOKENS, ATTN_WIDTH), _BF16),
        grid=(BATCH, nstep),
        in_specs=[pl.BlockSpec(memory_space=pltpu.SMEM),
                  pl.BlockSpec((nq * ATTN_BLOCK, ATTN_WIDTH), own),
                  pl.BlockSpec(kv1, prev), pl.BlockSpec(kvm, own), pl.BlockSpec(kv1, nxt),
                  pl.BlockSpec(kv1, prev), pl.BlockSpec(kvm, own), pl.BlockSpec(kv1, nxt),
                  pl.BlockSpec((CTX_LEN, KV_WIDTH), ctx),
                  pl.BlockSpec((CTX_LEN, KV_WIDTH), ctx)],
        out_specs=pl.BlockSpec((nq * ATTN_BLOCK, ATTN_WIDTH), own),
        compiler_params=pltpu.CompilerParams(dimension_semantics=("arbitrary", "arbitrary"),
                                             vmem_limit_bytes=VMEM_LIMIT),
        name="attn",
    )(sink_p, q, k, k, k, v, v, v, kc, vc)


def _post_kernel(x_ref, ao_ref, su_ref, sv_ref, mod_ref, lng_ref, lnb_ref, ws_ref, bs_ref,
                 ga_ref, gs_ref, woa_ref, wos_ref, bo_ref, gpost_ref, gpre_ref, wr_ref, br_ref,
                 tri_ref,
                 xmid_ref, h2_ref, idx_ref, gate_ref, rank_ref, cnt_ref,
                 mixed_ref, carry_ref):
    tm = x_ref.shape[0]
    m = mod_ref[0]

    @pl.when(pl.program_id(0) == 0)
    def _():
        carry_ref[...] = jnp.zeros_like(carry_ref)

    gv = _gelu_tanh(sv_ref[...])
    mu = jnp.mean(gv, axis=-1, keepdims=True)
    gc = gv - mu
    var = jnp.mean(gc * gc, axis=-1, keepdims=True)
    vb = (gc * lax.rsqrt(var + EPS) * lng_ref[...] + lnb_ref[...]).astype(_BF16)
    lane = lax.broadcasted_iota(jnp.int32, (SGU_CHUNK, LANES), 1)
    for c in range(tm // SGU_CHUNK):
        r0 = c * SGU_CHUNK
        for p in range(SGU_WIDTH // LANES):
            l0 = p * LANES
            r = jnp.dot(ws_ref[p], vb[r0:r0 + SGU_CHUNK, l0:l0 + LANES], preferred_element_type=_F32)
            mixed = jnp.where(lane < SGU_HEAD_DIM, r[0:SGU_CHUNK], r[SGU_CHUNK:])
            mixed_ref[r0:r0 + SGU_CHUNK, l0:l0 + LANES] = mixed + bs_ref[:, l0:l0 + LANES]
    sgu_o = _gelu_tanh(su_ref[...]) * mixed_ref[...]

    oa = _rms(ao_ref[...].astype(_F32), ga_ref[...]).astype(_BF16)
    os_ = _rms(sgu_o, gs_ref[...]).astype(_BF16)
    mix = (jnp.dot(oa, woa_ref[...], preferred_element_type=_F32)
           + jnp.dot(os_, wos_ref[...], preferred_element_type=_F32) + bo_ref[...])
    x_mid = x_ref[...] + m[2:3] * _rms(mix, gpost_ref[...])
    xmid_ref[...] = x_mid
    h2 = _rms(x_mid, gpre_ref[...]) * (1.0 + m[4:5]) + m[3:4]
    _store_row_tiles(h2_ref, h2)

    h_hi = h2.astype(_BF16)
    h_lo = (h2 - h_hi.astype(_F32)).astype(_BF16)
    r = (jnp.dot(h_hi, wr_ref[...], preferred_element_type=_F32)
         + jnp.dot(h_lo, wr_ref[...], preferred_element_type=_F32))
    lg = r + pltpu.roll(r, LANES - N_EXPERTS, 1) + br_ref[...]
    lane_r = lax.broadcasted_iota(jnp.int32, lg.shape, 1)
    lane_f = lane_r.astype(_F32)
    tops, hots = [], []
    for _k in range(TOP_K):
        mx = jnp.max(lg, axis=-1, keepdims=True)
        pick = jnp.min(jnp.where(lg == mx, lane_f, float(LANES)), axis=-1, keepdims=True)
        hot = lane_f == pick
        tops.append((mx, pick))
        hots.append(hot)
        lg = jnp.where(hot, 2.0 * NEG_BIG, lg)
    es = [jnp.exp(t[0] - tops[0][0]) for t in tops]
    esum = es[0] + es[1] + es[2] + es[3]
    multi = jnp.zeros(lg.shape, _F32)
    for hot in hots:
        multi = multi + jnp.where(hot, 1.0, 0.0)
    cum = jnp.dot(tri_ref[...], multi.astype(_BF16), preferred_element_type=_F32) + carry_ref[...]
    tok_per_row = LANES // TOP_K
    row_r = lax.broadcasted_iota(jnp.int32, lg.shape, 0)
    lane_base = (row_r & (tok_per_row - 1)) * TOP_K
    gate_o = jnp.zeros(lg.shape, _F32)
    idx_e = jnp.zeros(lg.shape, _F32)
    rank_hi_e = jnp.zeros(lg.shape, _F32)
    rank_lo_e = jnp.zeros(lg.shape, _F32)
    for kk in range(TOP_K):
        rk = jnp.sum(jnp.where(hots[kk], cum, 0.0), axis=-1, keepdims=True)
        rk_hi = jnp.floor(rk * (1.0 / 256.0))
        here = lane_r == lane_base + kk
        gate_o = jnp.where(lane_r == kk, es[kk] / esum, gate_o)
        idx_e = jnp.where(here, tops[kk][1], idx_e)
        rank_hi_e = jnp.where(here, rk_hi, rank_hi_e)
        rank_lo_e = jnp.where(here, rk - 256.0 * rk_hi, rank_lo_e)
    fr = lax.broadcasted_iota(jnp.int32, (tm // tok_per_row, tm), 0)
    fc = lax.broadcasted_iota(jnp.int32, (tm // tok_per_row, tm), 1)
    fold = jnp.where(lax.shift_right_logical(fc, tok_per_row.bit_length() - 1) == fr, 1.0, 0.0).astype(_BF16)
    idx_ref[...] = jnp.dot(fold, idx_e.astype(_BF16), preferred_element_type=_F32).astype(jnp.int32)
    rank_ref[...] = (256.0 * jnp.dot(fold, rank_hi_e.astype(_BF16), preferred_element_type=_F32)
                     + jnp.dot(fold, rank_lo_e.astype(_BF16), preferred_element_type=_F32)).astype(jnp.int32)
    gate_ref[...] = gate_o
    carry_ref[...] += jnp.sum(multi, axis=0, keepdims=True)
    cnt_ref[...] = carry_ref[...].astype(jnp.int32)


def _post_call(x2, ao, su, sv, mod3, lng, lnb, ws2, bsf, ga, gs, woa, wos, bo, gpost, gpre,
               wr, br, tri):
    tm = ROW_TILE
    tiles_per_batch = SEQ // tm
    row = lambda i: (i, 0)
    const = lambda i: (0, 0)
    const3 = lambda i: (0, 0, 0)
    return pl.pallas_call(
        _post_kernel,
        out_shape=(jax.ShapeDtypeStruct((TOKENS, D_MODEL), _F32),
                   jax.ShapeDtypeStruct((TOKENS * ROW_SUB, LANES), _F32),
                   jax.ShapeDtypeStruct((TOKENS * TOP_K // LANES, LANES), jnp.int32),
                   jax.ShapeDtypeStruct((TOKENS, LANES), _F32),
                   jax.ShapeDtypeStruct((TOKENS * TOP_K // LANES, LANES), jnp.int32),
                   jax.ShapeDtypeStruct((1, LANES), jnp.int32)),
        grid=(TOKENS // tm,),
        in_specs=[pl.BlockSpec((tm, D_MODEL), row),
                  pl.BlockSpec((tm, ATTN_WIDTH), row),
                  pl.BlockSpec((tm, SGU_WIDTH), row),
                  pl.BlockSpec((tm, SGU_WIDTH), row),
                  pl.BlockSpec((1, N_MOD, D_MODEL), lambda i: (i // tiles_per_batch, 0, 0)),
                  pl.BlockSpec((1, SGU_WIDTH), const),
                  pl.BlockSpec((1, SGU_WIDTH), const),
                  pl.BlockSpec((SGU_WIDTH // LANES, 2 * SGU_CHUNK, SGU_CHUNK), const3),
                  pl.BlockSpec((SGU_CHUNK, SGU_WIDTH), const),
                  pl.BlockSpec((1, ATTN_WIDTH), const),
                  pl.BlockSpec((1, SGU_WIDTH), const),
                  pl.BlockSpec((ATTN_WIDTH, D_MODEL), const),
                  pl.BlockSpec((SGU_WIDTH, D_MODEL), const),
                  pl.BlockSpec((1, D_MODEL), const),
                  pl.BlockSpec((1, D_MODEL), const),
                  pl.BlockSpec((1, D_MODEL), const),
                  pl.BlockSpec((D_MODEL, LANES), const),
                  pl.BlockSpec((1, LANES), const),
                  pl.BlockSpec((tm, tm), const)],
        out_specs=(pl.BlockSpec((tm, D_MODEL), row),
                   pl.BlockSpec((tm * ROW_SUB, LANES), row),
                   pl.BlockSpec((tm * TOP_K // LANES, LANES), row),
                   pl.BlockSpec((tm, LANES), row),
                   pl.BlockSpec((tm * TOP_K // LANES, LANES), row),
                   pl.BlockSpec((1, LANES), const)),
        scratch_shapes=[pltpu.VMEM((tm, SGU_WIDTH), _F32),
                        pltpu.VMEM((1, LANES), _F32)],
        compiler_params=pltpu.CompilerParams(dimension_semantics=("arbitrary",),
                                             vmem_limit_bytes=VMEM_LIMIT),
        name="post",
    )(x2, ao, su, sv, mod3, lng, lnb, ws2, bsf, ga, gs, woa, wos, bo, gpost, gpre, wr, br, tri)


def _dispatch_kernel(fs_ref, fl_ref, nu_ref, dest_ref, h2_ref, xs_ref, zero_ref, sem, zsem):
    tm = h2_ref.shape[0] // ROW_SUB
    n_token_steps = TOKENS // tm
    i = pl.program_id(0)

    @pl.when(i < n_token_steps)
    def _():
        def issue(r, carry):
            for kk in range(TOP_K):
                d = pl.multiple_of(dest_ref[r * TOP_K + kk] * ROW_SUB, ROW_SUB)
                pltpu.make_async_copy(h2_ref.at[pl.ds(pl.multiple_of(r * ROW_SUB, ROW_SUB), ROW_SUB)],
                                      xs_ref.at[pl.ds(d, ROW_SUB)], sem).start(priority=kk % 2)
            return carry

        lax.fori_loop(0, tm, issue, 0, unroll=8)
        for kk in range(TOP_K):
            pltpu.make_async_copy(h2_ref, xs_ref.at[pl.ds(0, tm * ROW_SUB)], sem).wait()

    @pl.when(i == n_token_steps)
    def _():
        zero_ref[...] = jnp.zeros_like(zero_ref)
        block_rows = EXPERT_BLOCK * ROW_SUB

        def pad_run(e, wait):
            pos = fs_ref[e]
            length = fl_ref[e]
            for bit in reversed(range(EXPERT_BLOCK.bit_length() - 1)):
                size = 1 << bit
                take = length & size

                @pl.when(take != 0)
                def _():
                    cp = pltpu.make_async_copy(
                        zero_ref.at[pl.ds(0, size * ROW_SUB)],
                        xs_ref.at[pl.ds(pl.multiple_of(pos * ROW_SUB, ROW_SUB), size * ROW_SUB)], zsem)
                    if wait:
                        cp.wait()
                    else:
                        cp.start()

                pos = pos + take

        def tail_block(blk, wait):
            cp = pltpu.make_async_copy(
                zero_ref, xs_ref.at[pl.ds(pl.multiple_of(blk * block_rows, block_rows), block_rows)], zsem)
            if wait:
                cp.wait()
            else:
                cp.start()

        for wait in (False, True):
            lax.fori_loop(0, N_EXPERTS, lambda e, c, w=wait: (pad_run(e, w), c)[1], 0)
            lax.fori_loop(nu_ref[0], N_EXPERT_BLOCKS, lambda blk, c, w=wait: (tail_block(blk, w), c)[1], 0)


def _dispatch_call(fill_start, fill_len, n_used, dest_flat, h2):
    tm = MOVE_TILE
    n_token_steps = TOKENS // tm
    grid_spec = pltpu.PrefetchScalarGridSpec(
        num_scalar_prefetch=3,
        grid=(n_token_steps + 1,),
        in_specs=[pl.BlockSpec((tm * TOP_K,), lambda i, fs, fl, nu: (jnp.minimum(i, n_token_steps - 1),),
                               memory_space=pltpu.SMEM),
                  pl.BlockSpec((tm * ROW_SUB, LANES),
                               lambda i, fs, fl, nu: (jnp.minimum(i, n_token_steps - 1), 0))],
        out_specs=pl.BlockSpec(memory_space=pl.ANY),
        scratch_shapes=[pltpu.VMEM((EXPERT_BLOCK * ROW_SUB, LANES), _F32),
                        pltpu.SemaphoreType.DMA,
                        pltpu.SemaphoreType.DMA])
    return pl.pallas_call(
        _dispatch_kernel,
        out_shape=jax.ShapeDtypeStruct((SORTED_ROWS * ROW_SUB, LANES), _F32),
        grid_spec=grid_spec,
        compiler_params=pltpu.CompilerParams(dimension_semantics=("arbitrary",),
                                             vmem_limit_bytes=VMEM_LIMIT),
        name="dispatch",
    )(fill_start, fill_len, n_used, dest_flat, h2)


def _expert_kernel(be_ref, slot_ref, first_ref, nxt_ref, nu_ref,
                   xs_ref, wgu_hbm, bgu_ref, wd_hbm, bd_ref, y_ref, wgu_buf, wd_buf, wgu_bf, wd_bf, sem):
    b = pl.program_id(0)

    def weight_copies(e, slot):
        return (pltpu.make_async_copy(wgu_hbm.at[e], wgu_buf.at[slot], sem.at[0, slot]),
                pltpu.make_async_copy(wd_hbm.at[e], wd_buf.at[slot], sem.at[1, slot]))

    @pl.when(b < nu_ref[0])
    def _():
        e = be_ref[b]
        slot = slot_ref[b]

        @pl.when(b == 0)
        def _():
            for cp in weight_copies(e, slot):
                cp.start()

        @pl.when(first_ref[b] == 1)
        def _():
            for cp in weight_copies(e, slot):
                cp.wait()

            @pl.when(nxt_ref[b] < N_EXPERTS)
            def _():
                for cp in weight_copies(nxt_ref[b], 1 - slot):
                    cp.start()

            wgu_bf[...] = wgu_buf[slot].astype(_BF16)
            wd_bf[...] = wd_buf[slot].astype(_BF16)

        x = _load_row_tiles(xs_ref, EXPERT_BLOCK).astype(_BF16)
        gu = jnp.dot(x, wgu_bf[...], preferred_element_type=_F32) + bgu_ref[0]
        gate = jnp.minimum(gu[:, 0:D_FF_EXPERT], SWIGLU_LIMIT)
        up = jnp.clip(gu[:, D_FF_EXPERT:], -SWIGLU_LIMIT, SWIGLU_LIMIT)
        act = (up + 1.0) * gate * (1.0 / (1.0 + jnp.exp(-SWIGLU_ALPHA * gate)))
        y = jnp.dot(act.astype(_BF16), wd_bf[...], preferred_element_type=_F32) + bd_ref[0]
        _store_row_tiles(y_ref, y)

    @pl.when(b >= nu_ref[0])
    def _():
        y_ref[...] = jnp.zeros_like(y_ref)


def _expert_call(block_expert, block_slot, block_first, block_next, n_used, xs, wgu, bgu3, wd, bd3):
    tb = EXPERT_BLOCK
    live = lambda b, be, sl, fi, nx, nu: (jnp.minimum(b, nu[0] - 1), 0)
    bsel = lambda b, be, sl, fi, nx, nu: (be[b], 0, 0)
    grid_spec = pltpu.PrefetchScalarGridSpec(
        num_scalar_prefetch=5,
        grid=(N_EXPERT_BLOCKS,),
        in_specs=[pl.BlockSpec((tb * ROW_SUB, LANES), live),
                  pl.BlockSpec(memory_space=pl.ANY),
                  pl.BlockSpec((1, 1, 2 * D_FF_EXPERT), bsel),
                  pl.BlockSpec(memory_space=pl.ANY),
                  pl.BlockSpec((1, 1, D_MODEL), bsel)],
        out_specs=pl.BlockSpec((tb * ROW_SUB, LANES), lambda b, be, sl, fi, nx, nu: (b, 0)),
        scratch_shapes=[pltpu.VMEM((2, D_MODEL, 2 * D_FF_EXPERT), _F32),
                        pltpu.VMEM((2, D_FF_EXPERT, D_MODEL), _F32),
                        pltpu.VMEM((D_MODEL, 2 * D_FF_EXPERT), _BF16),
                        pltpu.VMEM((D_FF_EXPERT, D_MODEL), _BF16),
                        pltpu.SemaphoreType.DMA((2, 2))])
    return pl.pallas_call(
        _expert_kernel,
        out_shape=jax.ShapeDtypeStruct((SORTED_ROWS * ROW_SUB, LANES), _F32),
        grid_spec=grid_spec,
        compiler_params=pltpu.CompilerParams(dimension_semantics=("arbitrary",),
                                             vmem_limit_bytes=VMEM_LIMIT),
        name="expert",
    )(block_expert, block_slot, block_first, block_next, n_used, xs, wgu, bgu3, wd, bd3)


def _combine_kernel(dest_ref, dest_next_ref, gate_ref, xmid_ref, mod_ref, g_ref, y_ref, o_ref,
                    buf_ref, sem):
    tm = xmid_ref.shape[0]
    m = mod_ref[0]
    i = pl.program_id(0)
    n = pl.num_programs(0)

    def gather_rows(idx_ref, slot):
        def issue(r, carry):
            for kk in range(TOP_K):
                d = pl.multiple_of(idx_ref[r * TOP_K + kk] * ROW_SUB, ROW_SUB)
                pltpu.make_async_copy(
                    y_ref.at[pl.ds(d, ROW_SUB)],
                    buf_ref.at[slot, kk, pl.ds(pl.multiple_of(r * ROW_SUB, ROW_SUB), ROW_SUB)],
                    sem.at[slot]).start(priority=kk % 2)
            return carry

        lax.fori_loop(0, tm, issue, 0, unroll=8)

    @pl.when(i == 0)
    def _():
        gather_rows(dest_ref, 0)

    @pl.when(i + 1 < n)
    def _():
        gather_rows(dest_next_ref, (i + 1) % 2)

    slot = i % 2
    for kk in range(TOP_K):
        pltpu.make_async_copy(y_ref.at[pl.ds(0, tm * ROW_SUB)], buf_ref.at[slot, kk],
                              sem.at[slot]).wait()
    g = gate_ref[...]
    pieces = []
    for cc in range(ROW_SUB):
        piece = buf_ref[slot, 0, pl.ds(cc, tm, stride=ROW_SUB), :] * g[:, 0:1]
        for kk in range(1, TOP_K):
            piece = piece + buf_ref[slot, kk, pl.ds(cc, tm, stride=ROW_SUB), :] * g[:, kk:kk + 1]
        pieces.append(piece)
    ffn = jnp.concatenate(pieces, axis=1)
    o_ref[...] = xmid_ref[...] + m[5:6] * _rms(ffn, g_ref[...])


def _combine_call(dest_flat, gates, x_mid, mod3, gpost_ffn, y_rows):
    tm = MOVE_TILE
    tiles_per_batch = SEQ // tm
    n_steps = TOKENS // tm
    return pl.pallas_call(
        _combine_kernel,
        out_shape=jax.ShapeDtypeStruct((TOKENS, D_MODEL), _F32),
        grid=(n_steps,),
        in_specs=[pl.BlockSpec((tm * TOP_K,), lambda i: (i,), memory_space=pltpu.SMEM),
                  pl.BlockSpec((tm * TOP_K,), lambda i: (jnp.minimum(i + 1, n_steps - 1),),
                               memory_space=pltpu.SMEM),
                  pl.BlockSpec((tm, LANES), lambda i: (i, 0)),
                  pl.BlockSpec((tm, D_MODEL), lambda i: (i, 0)),
                  pl.BlockSpec((1, N_MOD, D_MODEL), lambda i: (i // tiles_per_batch, 0, 0)),
                  pl.BlockSpec((1, D_MODEL), lambda i: (0, 0)),
                  pl.BlockSpec(memory_space=pl.ANY)],
        out_specs=pl.BlockSpec((tm, D_MODEL), lambda i: (i, 0)),
        scratch_shapes=[pltpu.VMEM((2, TOP_K, tm * ROW_SUB, LANES), _F32),
                        pltpu.SemaphoreType.DMA((2,))],
        compiler_params=pltpu.CompilerParams(dimension_semantics=("arbitrary",),
                                             vmem_limit_bytes=VMEM_LIMIT),
        name="combine",
    )(dest_flat, dest_flat, gates, x_mid, mod3, gpost_ffn, y_rows)


def _rope_tables():
    pos = jnp.arange(SEQ, dtype=_F32)
    pos_row = jnp.floor(pos / GRID_W)
    pos_col = pos - pos_row * GRID_W
    n_freq = HEAD_DIM // 4
    inv_freq = ROPE_THETA ** (-jnp.arange(n_freq, dtype=_F32) / n_freq)
    d = jnp.arange(LANES) % HEAD_DIM
    f = inv_freq[d % n_freq]
    ang = jnp.where((d < HEAD_DIM // 2)[None, :], pos_row[:, None] * f[None, :], pos_col[:, None] * f[None, :])
    sign = jnp.where((d % (HEAD_DIM // 2)) < n_freq, -1.0, 1.0).astype(_F32)
    return jnp.cos(ang), jnp.sin(ang) * sign[None, :]


def _perm_heads(a, axis):
    shape = a.shape
    a = a.reshape(shape[:axis] + (N_HEADS, HEAD_DIM) + shape[axis + 1:])
    a = jnp.take(a, jnp.array(HEAD_PERM), axis=axis)
    return a.reshape(shape)


def kernel(x, c, ctx, c_ctx, w_ada, b_ada, g_pre_mix, g_post_mix, g_pre_ffn, g_post_ffn, w_in, b_in, attn_sink, sgu_ln_g, sgu_ln_b, sgu_w, sgu_b, g_attn_out, g_sgu_out, w_out, b_out, w_router, b_router, w_gate_up, b_gate_up, w_down, b_down):
    l = 0
    x2 = x.reshape(TOKENS, D_MODEL)
    ctx2 = ctx.reshape(BATCH * CTX_LEN, D_MODEL)

    cc = jnp.zeros((16, D_MODEL), _F32).at[:BATCH].set(c).at[BATCH].set(c_ctx)
    mod = _ada_call(cc, w_ada[l], b_ada[l].reshape(1, -1))
    mod3 = mod.reshape(16, N_MOD, D_MODEL)

    w_in_l = w_in[l]
    b_in_l = b_in[l]
    w_in_p = jnp.concatenate([_perm_heads(w_in_l[:, :ATTN_WIDTH], 1), w_in_l[:, ATTN_WIDTH:]], axis=1)
    b_in_p = jnp.concatenate([_perm_heads(b_in_l[:ATTN_WIDTH], 0), b_in_l[ATTN_WIDTH:]], axis=0)
    cos_t, sin_t = _rope_tables()
    g_pre = g_pre_mix[l].reshape(1, -1)

    q, k, v, su, sv = _inproj_call(x2, mod3, g_pre, w_in_p.astype(_BF16), b_in_p.reshape(1, -1), cos_t, sin_t)
    kc, vc = _ctxproj_call(ctx2, mod3, g_pre, w_in_l[:, KV_START:KV_END].astype(_BF16),
                           b_in_l[KV_START:KV_END].reshape(1, -1))

    sink_p = jnp.take(attn_sink[l], jnp.array(HEAD_PERM))
    ao = _attn_call(sink_p, q, k, v, kc, vc)

    ws2 = sgu_w[l].reshape(SGU_WIDTH // LANES, 2 * SGU_CHUNK, SGU_CHUNK).astype(_BF16)
    bsf = jnp.repeat(sgu_b[l].T, SGU_HEAD_DIM, axis=1)
    w_out_l = w_out[l]
    woa = _perm_heads(w_out_l[:ATTN_WIDTH], 0).astype(_BF16)
    wos = w_out_l[ATTN_WIDTH:].astype(_BF16)
    ga = _perm_heads(g_attn_out[l], 0).reshape(1, -1)
    wr_hi = w_router[l].astype(_BF16)
    wr_lo = (w_router[l] - wr_hi.astype(_F32)).astype(_BF16)
    wr = (jnp.zeros((D_MODEL, LANES), _BF16).at[:, :N_EXPERTS].set(wr_hi)
          .at[:, N_EXPERTS:2 * N_EXPERTS].set(wr_lo))
    br = jnp.full((1, LANES), NEG_BIG, _F32).at[0, :N_EXPERTS].set(b_router[l])
    ii = jnp.arange(ROW_TILE)
    tri = (ii[None, :] < ii[:, None]).astype(_BF16)

    x_mid, h2, idx_o, gate_o, rank_o, cnt_o = _post_call(
        x2, ao, su, sv, mod3, sgu_ln_g[l].reshape(1, -1), sgu_ln_b[l].reshape(1, -1), ws2, bsf,
        ga, g_sgu_out[l].reshape(1, -1), woa, wos, b_out[l].reshape(1, -1),
        g_post_mix[l].reshape(1, -1), g_pre_ffn[l].reshape(1, -1), wr, br, tri)

    counts = cnt_o[0, :N_EXPERTS]
    padded = (counts + EXPERT_BLOCK - 1) // EXPERT_BLOCK * EXPERT_BLOCK
    pad_end = jnp.cumsum(padded)
    pad_start = pad_end - padded
    dest = (pad_start[idx_o] + rank_o).astype(jnp.int32).reshape(-1)
    block_start = jnp.arange(N_EXPERT_BLOCKS, dtype=jnp.int32) * EXPERT_BLOCK
    block_expert = jnp.minimum(jnp.sum(pad_end[None, :] <= block_start[:, None], axis=1),
                               N_EXPERTS - 1).astype(jnp.int32)
    n_used = (pad_end[-1:] // EXPERT_BLOCK).astype(jnp.int32)

    xs = _dispatch_call((pad_start + counts).astype(jnp.int32), (padded - counts).astype(jnp.int32),
                        n_used, dest, h2)

    used = padded > 0
    e_ids = jnp.arange(N_EXPERTS, dtype=jnp.int32)
    cand = jnp.where(used, e_ids, N_EXPERTS)
    later_min = lax.cummin(cand[::-1])[::-1]
    next_used = jnp.concatenate([later_min[1:], jnp.full((1,), N_EXPERTS, jnp.int32)])
    ordinal = jnp.cumsum(used.astype(jnp.int32)) - 1
    block_slot = (ordinal[block_expert] & 1).astype(jnp.int32)
    block_first = (block_start == pad_start[block_expert]).astype(jnp.int32)
    block_next = next_used[block_expert].astype(jnp.int32)
    y_rows = _expert_call(block_expert, block_slot, block_first, block_next, n_used, xs,
                          w_gate_up[l], b_gate_up[l].reshape(N_EXPERTS, 1, -1),
                          w_down[l], b_down[l].reshape(N_EXPERTS, 1, -1))
    out = _combine_call(dest, gate_o, x_mid, mod3, g_post_ffn[l].reshape(1, -1), y_rows)
    return out.reshape(BATCH, SEQ, D_MODEL)
```

```python
import functools

import jax
import jax.numpy as jnp
from jax import lax
from jax.experimental import pallas as pl
from jax.experimental.pallas import tpu as pltpu

D_MODEL = 1024
BATCH = 8
SEQ = 4096
TOKENS = BATCH * SEQ
GRID_W = 64
CTX_LEN = 256
N_HEADS = 8
N_KV_HEADS = 2
HEAD_DIM = 64
ATTN_WIDTH = N_HEADS * HEAD_DIM
KV_WIDTH = N_KV_HEADS * HEAD_DIM
WINDOW = 128
ATTN_BLOCK = 128
SGU_HEADS = 8
SGU_HEAD_DIM = 64
SGU_WIDTH = SGU_HEADS * SGU_HEAD_DIM
SGU_CHUNK = 128
KV_START = ATTN_WIDTH
KV_END = ATTN_WIDTH + 2 * KV_WIDTH
IN_WIDTH = KV_END + 2 * SGU_WIDTH
N_EXPERTS = 32
TOP_K = 4
D_FF_EXPERT = 1024
SWIGLU_LIMIT = 7.0
SWIGLU_ALPHA = 1.702
ROPE_THETA = 10000.0
EPS = 1e-6
N_MOD = 6

LANES = 128
NEG_BIG = -1e30
VMEM_LIMIT = 56 * 1024 * 1024

ROW_TILE = 512
MOVE_TILE = 512
ATTN_STEP_BLOCKS = 4
EXPERT_BLOCK = 512
POST_TILE = 1024
POST_SPLIT = 4
N_EXPERT_BLOCKS = TOKENS * TOP_K // EXPERT_BLOCK + N_EXPERTS
SORTED_ROWS = N_EXPERT_BLOCKS * EXPERT_BLOCK
HEAD_PERM = (0, 4, 1, 5, 2, 6, 3, 7)

_F32 = jnp.float32
_BF16 = jnp.bfloat16


def _rms(x, g):
    ms = jnp.mean(x * x, axis=-1, keepdims=True)
    return x * lax.rsqrt(ms + EPS) * g


ROW_SUB = D_MODEL // LANES


def _store_row_tiles(ref, x):
    n = x.shape[0]
    for cc in range(ROW_SUB):
        ref[pl.ds(cc, n, stride=ROW_SUB), :] = x[:, cc * LANES:(cc + 1) * LANES]


def _load_row_tiles(ref, n):
    return jnp.concatenate([ref[pl.ds(cc, n, stride=ROW_SUB), :] for cc in range(ROW_SUB)], axis=1)


def _gelu_tanh(x):
    c = 0.7978845608028654
    return x * (0.5 * (1.0 + jnp.tanh(c * (x + 0.044715 * (x * x * x)))))


def _ada_kernel(c_ref, w_ref, b_ref, o_ref):
    c = c_ref[...]
    a = c / (1.0 + jnp.exp(-c))
    o_ref[...] = jnp.dot(a, w_ref[...], preferred_element_type=_F32,
                         precision=lax.Precision.HIGHEST) + b_ref[...]


def _ada_call(cc, w_ada, b_ada):
    n = N_MOD * D_MODEL
    tn = 512
    return pl.pallas_call(
        _ada_kernel,
        out_shape=jax.ShapeDtypeStruct((16, n), _F32),
        grid=(n // tn,),
        in_specs=[pl.BlockSpec((16, D_MODEL), lambda j: (0, 0)),
                  pl.BlockSpec((D_MODEL, tn), lambda j: (0, j)),
                  pl.BlockSpec((1, tn), lambda j: (0, j))],
        out_specs=pl.BlockSpec((16, tn), lambda j: (0, j)),
        compiler_params=pltpu.CompilerParams(dimension_semantics=("arbitrary",),
                                             vmem_limit_bytes=VMEM_LIMIT),
        name="ada",
    )(cc, w_ada, b_ada)


def _rope(x, cos, sin_signed, first_half):
    nxt = pltpu.roll(x, LANES - 16, 1)
    prv = pltpu.roll(x, 16, 1)
    return x * cos + jnp.where(first_half, nxt, prv) * sin_signed


def _inproj_kernel(x_ref, mod_ref, g_ref, w_ref, b_ref, cos_ref, sin_ref,
                   q_ref, k_ref, v_ref, su_ref, sv_ref):
    m = mod_ref[0]
    h = _rms(x_ref[...], g_ref[...]) * (1.0 + m[1:2]) + m[0:1]
    hb = h.astype(_BF16)
    cos = cos_ref[...]
    sin = sin_ref[...]
    lane = lax.broadcasted_iota(jnp.int32, cos.shape, 1)
    first_half = (lane & 31) < 16

    q = jnp.dot(hb, w_ref[:, 0:ATTN_WIDTH], preferred_element_type=_F32) + b_ref[:, 0:ATTN_WIDTH]
    for j in range(ATTN_WIDTH // LANES):
        qj = _rope(q[:, j * LANES:(j + 1) * LANES], cos, sin, first_half)
        q_ref[:, j * LANES:(j + 1) * LANES] = (qj * (HEAD_DIM ** -0.5)).astype(_BF16)
    kv = jnp.dot(hb, w_ref[:, KV_START:KV_END], preferred_element_type=_F32) + b_ref[:, KV_START:KV_END]
    k_ref[...] = _rope(kv[:, 0:KV_WIDTH], cos, sin, first_half).astype(_BF16)
    v_ref[...] = kv[:, KV_WIDTH:].astype(_BF16)
    u0 = KV_END
    u1 = KV_END + SGU_WIDTH
    su_ref[...] = jnp.dot(hb, w_ref[:, u0:u1], preferred_element_type=_F32) + b_ref[:, u0:u1]
    sv_ref[...] = jnp.dot(hb, w_ref[:, u1:IN_WIDTH], preferred_element_type=_F32) + b_ref[:, u1:IN_WIDTH]


def _inproj_call(x2, mod3, g_pre, w_in_b, b_in2, cos_t, sin_t):
    tm = ROW_TILE
    tiles_per_batch = SEQ // tm
    row = lambda i: (i, 0)
    const = lambda i: (0, 0)
    return pl.pallas_call(
        _inproj_kernel,
        out_shape=(jax.ShapeDtypeStruct((TOKENS, ATTN_WIDTH), _BF16),
                   jax.ShapeDtypeStruct((TOKENS, KV_WIDTH), _BF16),
                   jax.ShapeDtypeStruct((TOKENS, KV_WIDTH), _BF16),
                   jax.ShapeDtypeStruct((TOKENS, SGU_WIDTH), _F32),
                   jax.ShapeDtypeStruct((TOKENS, SGU_WIDTH), _F32)),
        grid=(TOKENS // tm,),
        in_specs=[pl.BlockSpec((tm, D_MODEL), row),
                  pl.BlockSpec((1, N_MOD, D_MODEL), lambda i: (i // tiles_per_batch, 0, 0)),
                  pl.BlockSpec((1, D_MODEL), const),
                  pl.BlockSpec((D_MODEL, IN_WIDTH), const),
                  pl.BlockSpec((1, IN_WIDTH), const),
                  pl.BlockSpec((tm, LANES), lambda i: (i % tiles_per_batch, 0)),
                  pl.BlockSpec((tm, LANES), lambda i: (i % tiles_per_batch, 0))],
        out_specs=(pl.BlockSpec((tm, ATTN_WIDTH), row),
                   pl.BlockSpec((tm, KV_WIDTH), row),
                   pl.BlockSpec((tm, KV_WIDTH), row),
                   pl.BlockSpec((tm, SGU_WIDTH), row),
                   pl.BlockSpec((tm, SGU_WIDTH), row)),
        compiler_params=pltpu.CompilerParams(dimension_semantics=("arbitrary",),
                                             vmem_limit_bytes=VMEM_LIMIT),
        name="inproj",
    )(x2, mod3, g_pre, w_in_b, b_in2, cos_t, sin_t)


def _ctxproj_kernel(x_ref, mod_ref, g_ref, w_ref, b_ref, k_ref, v_ref):
    m = mod_ref[0]
    h = _rms(x_ref[...], g_ref[...]) * (1.0 + m[1:2]) + m[0:1]
    kv = jnp.dot(h.astype(_BF16), w_ref[...], preferred_element_type=_F32) + b_ref[...]
    k_ref[...] = kv[:, 0:KV_WIDTH].astype(_BF16)
    v_ref[...] = kv[:, KV_WIDTH:].astype(_BF16)


def _ctxproj_call(ctx2, mod3, g_pre, w_kv_b, b_kv2):
    tm = ROW_TILE
    rows = ctx2.shape[0]
    row = lambda i: (i, 0)
    const = lambda i: (0, 0)
    return pl.pallas_call(
        _ctxproj_kernel,
        out_shape=(jax.ShapeDtypeStruct((rows, KV_WIDTH), _BF16),
                   jax.ShapeDtypeStruct((rows, KV_WIDTH), _BF16)),
        grid=(rows // tm,),
        in_specs=[pl.BlockSpec((tm, D_MODEL), row),
                  pl.BlockSpec((1, N_MOD, D_MODEL), lambda i: (BATCH, 0, 0)),
                  pl.BlockSpec((1, D_MODEL), const),
                  pl.BlockSpec((D_MODEL, 2 * KV_WIDTH), const),
                  pl.BlockSpec((1, 2 * KV_WIDTH), const)],
        out_specs=(pl.BlockSpec((tm, KV_WIDTH), row),
                   pl.BlockSpec((tm, KV_WIDTH), row)),
        compiler_params=pltpu.CompilerParams(dimension_semantics=("arbitrary",),
                                             vmem_limit_bytes=VMEM_LIMIT),
        name="ctxproj",
    )(ctx2, mod3, g_pre, w_kv_b, b_kv2)


def _attn_kernel(sink_ref, q_ref, kp_ref, km_ref, kn_ref, vp_ref, vm_ref, vn_ref,
                 kc_ref, vc_ref, o_ref):
    n = pl.program_id(1)
    nstep = pl.num_programs(1)
    blk = ATTN_BLOCK
    nq = ATTN_STEP_BLOCKS
    rows2 = 2 * blk
    r = lax.broadcasted_iota(jnp.int32, (rows2, blk), 0) & (blk - 1)
    c = lax.broadcasted_iota(jnp.int32, (rows2, blk), 1)
    tri_prev = c >= r
    tri_next = c <= r
    row1 = lax.broadcasted_iota(jnp.int32, (rows2, 1), 0)
    lane_q = lax.broadcasted_iota(jnp.int32, (blk, LANES), 1)
    n_grp = ATTN_WIDTH // LANES
    k_blocks = [kp_ref[...]] + [km_ref[t * blk:(t + 1) * blk, :] for t in range(nq)] + [kn_ref[...]]
    v_blocks = [vp_ref[...]] + [vm_ref[t * blk:(t + 1) * blk, :] for t in range(nq)] + [vn_ref[...]]

    def scores(t):
        keys = jnp.concatenate(k_blocks[t:t + 3] + [kc_ref[...]], axis=0)
        q_rows = []
        for j in range(n_grp):
            qg = q_ref[t * blk:(t + 1) * blk, j * LANES:(j + 1) * LANES]
            zero = jnp.zeros_like(qg)
            q_rows += [jnp.where(lane_q < HEAD_DIM, qg, zero), jnp.where(lane_q >= HEAD_DIM, qg, zero)]
        return lax.dot_general(jnp.concatenate(q_rows, axis=0), keys, (((1,), (1,)), ((), ())),
                               preferred_element_type=_F32)

    def softmax(t, s_all):
        mask_a = (tri_prev & (n > 0)) if t == 0 else tri_prev
        mask_c = (tri_next & (n < nstep - 1)) if t == nq - 1 else tri_next
        p_rows, denoms = [], []
        for j in range(n_grp):
            s = s_all[j * rows2:(j + 1) * rows2]
            s = jnp.concatenate([jnp.where(mask_a, s[:, 0:blk], NEG_BIG), s[:, blk:2 * blk],
                                 jnp.where(mask_c, s[:, 2 * blk:3 * blk], NEG_BIG), s[:, 3 * blk:]],
                                axis=1)
            sk = jnp.where(row1 < blk, sink_ref[2 * j], sink_ref[2 * j + 1])
            m = jnp.maximum(jnp.max(s, axis=-1, keepdims=True), sk)
            p = jnp.exp(s - m)
            denoms.append(jnp.sum(p, axis=-1, keepdims=True) + jnp.exp(sk - m))
            p_rows.append(p.astype(_BF16))
        return jnp.concatenate(p_rows, axis=0), denoms

    def weighted_values(t, p_all, denoms):
        vals = jnp.concatenate(v_blocks[t:t + 3] + [vc_ref[...]], axis=0)
        o_all = jnp.dot(p_all, vals, preferred_element_type=_F32)
        for j in range(n_grp):
            o2 = o_all[j * rows2:(j + 1) * rows2] / denoms[j]
            og = jnp.where(lane_q < HEAD_DIM, o2[0:blk], o2[blk:])
            o_ref[t * blk:(t + 1) * blk, j * LANES:(j + 1) * LANES] = og.astype(_BF16)

    s_next = scores(0)
    probs = None
    for t in range(nq + 1):
        s_cur, s_next = s_next, (scores(t + 1) if t + 1 < nq else None)
        if probs is not None:
            weighted_values(t - 1, *probs)
        probs = softmax(t, s_cur) if t < nq else None


def _attn_call(sink_p, q, k, v, kc, vc):
    nblk = SEQ // ATTN_BLOCK
    nq = ATTN_STEP_BLOCKS
    nstep = nblk // nq
    own = lambda b, n: (b * nstep + n, 0)
    prev = lambda b, n: (b * nblk + jnp.maximum(nq * n - 1, 0), 0)
    nxt = lambda b, n: (b * nblk + jnp.minimum(nq * n + nq, nblk - 1), 0)
    ctx = lambda b, n: (b, 0)
    kv1 = (ATTN_BLOCK, KV_WIDTH)
    kvm = (nq * ATTN_BLOCK, KV_WIDTH)
    return pl.pallas_call(
        _attn_kernel,
        out_shape=jax.ShapeDtypeStruct((TOKENS, ATTN_WIDTH), _BF16),
        grid=(BATCH, nstep),
        in_specs=[pl.BlockSpec(memory_space=pltpu.SMEM),
                  pl.BlockSpec((nq * ATTN_BLOCK, ATTN_WIDTH), own),
                  pl.BlockSpec(kv1, prev), pl.BlockSpec(kvm, own), pl.BlockSpec(kv1, nxt),
                  pl.BlockSpec(kv1, prev), pl.BlockSpec(kvm, own), pl.BlockSpec(kv1, nxt),
                  pl.BlockSpec((CTX_LEN, KV_WIDTH), ctx),
                  pl.BlockSpec((CTX_LEN, KV_WIDTH), ctx)],
        out_specs=pl.BlockSpec((nq * ATTN_BLOCK, ATTN_WIDTH), own),
        compiler_params=pltpu.CompilerParams(dimension_semantics=("arbitrary", "arbitrary"),
                                             vmem_limit_bytes=VMEM_LIMIT),
        name="attn",
    )(sink_p, q, k, k, k, v, v, v, kc, vc)


def _post_kernel(x_ref, ao_ref, su_ref, sv_ref, mod_ref, lng_ref, lnb_ref, ws_ref, bs_ref,
                 ga_ref, gs_ref, woa_ref, wos_ref, bo_ref, gpost_ref, gpre_ref, wr_ref, br_ref,
                 tri_ref,
                 xmid_ref, h2_ref, idx_ref, gate_ref, rank_ref, cnt_ref,
                 mixed_ref, carry_ref):
    tm = x_ref.shape[0]
    m = mod_ref[0]

    @pl.when(pl.program_id(0) == 0)
    def _():
        carry_ref[...] = jnp.zeros_like(carry_ref)

    n_sub = POST_SPLIT
    ts = tm // n_sub
    tok_per_row = LANES // TOP_K
    lane = lax.broadcasted_iota(jnp.int32, (SGU_CHUNK, LANES), 1)
    st = [dict() for _ in range(n_sub)]
    carry = [carry_ref[...]]

    def rows(h):
        return slice(h * ts, (h + 1) * ts)

    def s1(h):
        gv = _gelu_tanh(sv_ref[rows(h), :])
        mu = jnp.mean(gv, axis=-1, keepdims=True)
        gc = gv - mu
        var = jnp.mean(gc * gc, axis=-1, keepdims=True)
        st[h]["vb"] = (gc * lax.rsqrt(var + EPS) * lng_ref[...] + lnb_ref[...]).astype(_BF16)

    def s2(h):
        vb = st[h].pop("vb")
        for c in range(ts // SGU_CHUNK):
            r0 = c * SGU_CHUNK
            for p in range(SGU_WIDTH // LANES):
                l0 = p * LANES
                r = jnp.dot(ws_ref[p], vb[r0:r0 + SGU_CHUNK, l0:l0 + LANES], preferred_element_type=_F32)
                mixed = jnp.where(lane < SGU_HEAD_DIM, r[0:SGU_CHUNK], r[SGU_CHUNK:])
                mixed_ref[h * ts + r0:h * ts + r0 + SGU_CHUNK, l0:l0 + LANES] = mixed + bs_ref[:, l0:l0 + LANES]
        sgu_o = _gelu_tanh(su_ref[rows(h), :]) * mixed_ref[rows(h), :]
        st[h]["oa"] = _rms(ao_ref[rows(h), :].astype(_F32), ga_ref[...]).astype(_BF16)
        st[h]["os"] = _rms(sgu_o, gs_ref[...]).astype(_BF16)

    def s3(h):
        st[h]["mix"] = (jnp.dot(st[h].pop("oa"), woa_ref[...], preferred_element_type=_F32)
                        + jnp.dot(st[h].pop("os"), wos_ref[...], preferred_element_type=_F32) + bo_ref[...])

    def s4(h):
        x_mid = x_ref[rows(h), :] + m[2:3] * _rms(st[h].pop("mix"), gpost_ref[...])
        xmid_ref[rows(h), :] = x_mid
        h2 = _rms(x_mid, gpre_ref[...]) * (1.0 + m[4:5]) + m[3:4]
        for cc in range(ROW_SUB):
            h2_ref[pl.ds(h * ts * ROW_SUB + cc, ts, stride=ROW_SUB), :] = h2[:, cc * LANES:(cc + 1) * LANES]
        h_hi = h2.astype(_BF16)
        st[h]["h_hi"] = h_hi
        st[h]["h_lo"] = (h2 - h_hi.astype(_F32)).astype(_BF16)

    def s5(h):
        r = (jnp.dot(st[h].pop("h_hi"), wr_ref[...], preferred_element_type=_F32)
             + jnp.dot(st[h].pop("h_lo"), wr_ref[...], preferred_element_type=_F32))
        st[h]["lg"] = r + pltpu.roll(r, LANES - N_EXPERTS, 1) + br_ref[...]

    def s6(h):
        lg = st[h].pop("lg")
        lane_r = lax.broadcasted_iota(jnp.int32, lg.shape, 1)
        lane_f = lane_r.astype(_F32)
        tops, hots = [], []
        for _k in range(TOP_K):
            mx = jnp.max(lg, axis=-1, keepdims=True)
            pick = jnp.min(jnp.where(lg == mx, lane_f, float(LANES)), axis=-1, keepdims=True)
            hot = lane_f == pick
            tops.append((mx, pick))
            hots.append(hot)
            lg = jnp.where(hot, 2.0 * NEG_BIG, lg)
        es = [jnp.exp(t[0] - tops[0][0]) for t in tops]
        esum = es[0] + es[1] + es[2] + es[3]
        multi = jnp.zeros(lg.shape, _F32)
        for hot in hots:
            multi = multi + jnp.where(hot, 1.0, 0.0)
        cum = jnp.dot(tri_ref[...], multi.astype(_BF16), preferred_element_type=_F32) + carry[0]
        row_r = lax.broadcasted_iota(jnp.int32, lg.shape, 0)
        lane_base = (row_r & (tok_per_row - 1)) * TOP_K
        gate_o = jnp.zeros(lg.shape, _F32)
        idx_e = jnp.zeros(lg.shape, _F32)
        rank_hi_e = jnp.zeros(lg.shape, _F32)
        rank_lo_e = jnp.zeros(lg.shape, _F32)
        for kk in range(TOP_K):
            rk = jnp.sum(jnp.where(hots[kk], cum, 0.0), axis=-1, keepdims=True)
            rk_hi = jnp.floor(rk * (1.0 / 256.0))
            here = lane_r == lane_base + kk
            gate_o = jnp.where(lane_r == kk, es[kk] / esum, gate_o)
            idx_e = jnp.where(here, tops[kk][1], idx_e)
            rank_hi_e = jnp.where(here, rk_hi, rank_hi_e)
            rank_lo_e = jnp.where(here, rk - 256.0 * rk_hi, rank_lo_e)
        fr = lax.broadcasted_iota(jnp.int32, (ts // tok_per_row, ts), 0)
        fc = lax.broadcasted_iota(jnp.int32, (ts // tok_per_row, ts), 1)
        fold = jnp.where(lax.shift_right_logical(fc, tok_per_row.bit_length() - 1) == fr, 1.0, 0.0).astype(_BF16)
        fs = slice(h * ts // tok_per_row, (h + 1) * ts // tok_per_row)
        idx_ref[fs, :] = jnp.dot(fold, idx_e.astype(_BF16), preferred_element_type=_F32).astype(jnp.int32)
        rank_ref[fs, :] = (256.0 * jnp.dot(fold, rank_hi_e.astype(_BF16), preferred_element_type=_F32)
                           + jnp.dot(fold, rank_lo_e.astype(_BF16), preferred_element_type=_F32)).astype(jnp.int32)
        gate_ref[rows(h), :] = gate_o
        carry[0] = carry[0] + jnp.sum(multi, axis=0, keepdims=True)

    order = sorted(((k + 2.5 * h, h, k) for h in range(n_sub) for k in range(6)))
    stages = (s1, s2, s3, s4, s5, s6)
    for _, h, k in order:
        stages[k](h)
    carry_ref[...] = carry[0]
    cnt_ref[...] = carry[0].astype(jnp.int32)


def _post_call(x2, ao, su, sv, mod3, lng, lnb, ws2, bsf, ga, gs, woa, wos, bo, gpost, gpre,
               wr, br, tri):
    tm = POST_TILE
    tiles_per_batch = SEQ // tm
    row = lambda i: (i, 0)
    const = lambda i: (0, 0)
    const3 = lambda i: (0, 0, 0)
    return pl.pallas_call(
        _post_kernel,
        out_shape=(jax.ShapeDtypeStruct((TOKENS, D_MODEL), _F32),
                   jax.ShapeDtypeStruct((TOKENS * ROW_SUB, LANES), _F32),
                   jax.ShapeDtypeStruct((TOKENS * TOP_K // LANES, LANES), jnp.int32),
                   jax.ShapeDtypeStruct((TOKENS, LANES), _F32),
                   jax.ShapeDtypeStruct((TOKENS * TOP_K // LANES, LANES), jnp.int32),
                   jax.ShapeDtypeStruct((1, LANES), jnp.int32)),
        grid=(TOKENS // tm,),
        in_specs=[pl.BlockSpec((tm, D_MODEL), row),
                  pl.BlockSpec((tm, ATTN_WIDTH), row),
                  pl.BlockSpec((tm, SGU_WIDTH), row),
                  pl.BlockSpec((tm, SGU_WIDTH), row),
                  pl.BlockSpec((1, N_MOD, D_MODEL), lambda i: (i // tiles_per_batch, 0, 0)),
                  pl.BlockSpec((1, SGU_WIDTH), const),
                  pl.BlockSpec((1, SGU_WIDTH), const),
                  pl.BlockSpec((SGU_WIDTH // LANES, 2 * SGU_CHUNK, SGU_CHUNK), const3),
                  pl.BlockSpec((SGU_CHUNK, SGU_WIDTH), const),
                  pl.BlockSpec((1, ATTN_WIDTH), const),
                  pl.BlockSpec((1, SGU_WIDTH), const),
                  pl.BlockSpec((ATTN_WIDTH, D_MODEL), const),
                  pl.BlockSpec((SGU_WIDTH, D_MODEL), const),
                  pl.BlockSpec((1, D_MODEL), const),
                  pl.BlockSpec((1, D_MODEL), const),
                  pl.BlockSpec((1, D_MODEL), const),
                  pl.BlockSpec((D_MODEL, LANES), const),
                  pl.BlockSpec((1, LANES), const),
                  pl.BlockSpec((tm // POST_SPLIT, tm // POST_SPLIT), const)],
        out_specs=(pl.BlockSpec((tm, D_MODEL), row),
                   pl.BlockSpec((tm * ROW_SUB, LANES), row),
                   pl.BlockSpec((tm * TOP_K // LANES, LANES), row),
                   pl.BlockSpec((tm, LANES), row),
                   pl.BlockSpec((tm * TOP_K // LANES, LANES), row),
                   pl.BlockSpec((1, LANES), const)),
        scratch_shapes=[pltpu.VMEM((tm, SGU_WIDTH), _F32),
                        pltpu.VMEM((1, LANES), _F32)],
        compiler_params=pltpu.CompilerParams(dimension_semantics=("arbitrary",),
                                             vmem_limit_bytes=VMEM_LIMIT),
        name="post",
    )(x2, ao, su, sv, mod3, lng, lnb, ws2, bsf, ga, gs, woa, wos, bo, gpost, gpre, wr, br, tri)


def _dispatch_kernel(fs_ref, fl_ref, nu_ref, dest_ref, h2_ref, xs_ref, zero_ref, sem, zsem):
    tm = h2_ref.shape[0] // ROW_SUB
    n_token_steps = TOKENS // tm
    i = pl.program_id(0)

    @pl.when(i < n_token_steps)
    def _():
        def issue(r, carry):
            for kk in range(TOP_K):
                d = pl.multiple_of(dest_ref[r * TOP_K + kk] * ROW_SUB, ROW_SUB)
                pltpu.make_async_copy(h2_ref.at[pl.ds(pl.multiple_of(r * ROW_SUB, ROW_SUB), ROW_SUB)],
                                      xs_ref.at[pl.ds(d, ROW_SUB)], sem).start(priority=kk % 2)
            return carry

        lax.fori_loop(0, tm, issue, 0, unroll=8)
        for kk in range(TOP_K):
            pltpu.make_async_copy(h2_ref, xs_ref.at[pl.ds(0, tm * ROW_SUB)], sem).wait()

    @pl.when(i == n_token_steps)
    def _():
        zero_ref[...] = jnp.zeros_like(zero_ref)
        block_rows = EXPERT_BLOCK * ROW_SUB

        def pad_run(e, wait):
            pos = fs_ref[e]
            length = fl_ref[e]
            for bit in reversed(range(EXPERT_BLOCK.bit_length() - 1)):
                size = 1 << bit
                take = length & size

                @pl.when(take != 0)
                def _():
                    cp = pltpu.make_async_copy(
                        zero_ref.at[pl.ds(0, size * ROW_SUB)],
                        xs_ref.at[pl.ds(pl.multiple_of(pos * ROW_SUB, ROW_SUB), size * ROW_SUB)], zsem)
                    if wait:
                        cp.wait()
                    else:
                        cp.start()

                pos = pos + take

        def tail_block(blk, wait):
            cp = pltpu.make_async_copy(
                zero_ref, xs_ref.at[pl.ds(pl.multiple_of(blk * block_rows, block_rows), block_rows)], zsem)
            if wait:
                cp.wait()
            else:
                cp.start()

        for wait in (False, True):
            lax.fori_loop(0, N_EXPERTS, lambda e, c, w=wait: (pad_run(e, w), c)[1], 0)
            lax.fori_loop(nu_ref[0], N_EXPERT_BLOCKS, lambda blk, c, w=wait: (tail_block(blk, w), c)[1], 0)


def _dispatch_call(fill_start, fill_len, n_used, dest_flat, h2):
    tm = MOVE_TILE
    n_token_steps = TOKENS // tm
    grid_spec = pltpu.PrefetchScalarGridSpec(
        num_scalar_prefetch=3,
        grid=(n_token_steps + 1,),
        in_specs=[pl.BlockSpec((tm * TOP_K,), lambda i, fs, fl, nu: (jnp.minimum(i, n_token_steps - 1),),
                               memory_space=pltpu.SMEM),
                  pl.BlockSpec((tm * ROW_SUB, LANES),
                               lambda i, fs, fl, nu: (jnp.minimum(i, n_token_steps - 1), 0))],
        out_specs=pl.BlockSpec(memory_space=pl.ANY),
        scratch_shapes=[pltpu.VMEM((EXPERT_BLOCK * ROW_SUB, LANES), _F32),
                        pltpu.SemaphoreType.DMA,
                        pltpu.SemaphoreType.DMA])
    return pl.pallas_call(
        _dispatch_kernel,
        out_shape=jax.ShapeDtypeStruct((SORTED_ROWS * ROW_SUB, LANES), _F32),
        grid_spec=grid_spec,
        compiler_params=pltpu.CompilerParams(dimension_semantics=("arbitrary",),
                                             vmem_limit_bytes=VMEM_LIMIT),
        name="dispatch",
    )(fill_start, fill_len, n_used, dest_flat, h2)


def _expert_kernel(be_ref, slot_ref, first_ref, nxt_ref, nu_ref,
                   xs_ref, wgu_hbm, bgu_ref, wd_hbm, bd_ref, y_ref, wgu_buf, wd_buf, wgu_bf, wd_bf, sem):
    b = pl.program_id(0)

    def weight_copies(e, slot):
        return (pltpu.make_async_copy(wgu_hbm.at[e], wgu_buf.at[slot], sem.at[0, slot]),
                pltpu.make_async_copy(wd_hbm.at[e], wd_buf.at[slot], sem.at[1, slot]))

    @pl.when(b < nu_ref[0])
    def _():
        e = be_ref[b]
        slot = slot_ref[b]

        @pl.when(b == 0)
        def _():
            for cp in weight_copies(e, slot):
                cp.start()

        @pl.when(first_ref[b] == 1)
        def _():
            for cp in weight_copies(e, slot):
                cp.wait()

            @pl.when(nxt_ref[b] < N_EXPERTS)
            def _():
                for cp in weight_copies(nxt_ref[b], 1 - slot):
                    cp.start()

            wgu_bf[...] = wgu_buf[slot].astype(_BF16)
            wd_bf[...] = wd_buf[slot].astype(_BF16)

        x = _load_row_tiles(xs_ref, EXPERT_BLOCK).astype(_BF16)
        gu = jnp.dot(x, wgu_bf[...], preferred_element_type=_F32) + bgu_ref[0]
        gate = jnp.minimum(gu[:, 0:D_FF_EXPERT], SWIGLU_LIMIT)
        up = jnp.clip(gu[:, D_FF_EXPERT:], -SWIGLU_LIMIT, SWIGLU_LIMIT)
        act = (up + 1.0) * gate * (1.0 / (1.0 + jnp.exp(-SWIGLU_ALPHA * gate)))
        y = jnp.dot(act.astype(_BF16), wd_bf[...], preferred_element_type=_F32) + bd_ref[0]
        _store_row_tiles(y_ref, y)

    @pl.when(b >= nu_ref[0])
    def _():
        y_ref[...] = jnp.zeros_like(y_ref)


def _expert_call(block_expert, block_slot, block_first, block_next, n_used, xs, wgu, bgu3, wd, bd3):
    tb = EXPERT_BLOCK
    live = lambda b, be, sl, fi, nx, nu: (jnp.minimum(b, nu[0] - 1), 0)
    bsel = lambda b, be, sl, fi, nx, nu: (be[b], 0, 0)
    grid_spec = pltpu.PrefetchScalarGridSpec(
        num_scalar_prefetch=5,
        grid=(N_EXPERT_BLOCKS,),
        in_specs=[pl.BlockSpec((tb * ROW_SUB, LANES), live),
                  pl.BlockSpec(memory_space=pl.ANY),
                  pl.BlockSpec((1, 1, 2 * D_FF_EXPERT), bsel),
                  pl.BlockSpec(memory_space=pl.ANY),
                  pl.BlockSpec((1, 1, D_MODEL), bsel)],
        out_specs=pl.BlockSpec((tb * ROW_SUB, LANES), lambda b, be, sl, fi, nx, nu: (b, 0)),
        scratch_shapes=[pltpu.VMEM((2, D_MODEL, 2 * D_FF_EXPERT), _F32),
                        pltpu.VMEM((2, D_FF_EXPERT, D_MODEL), _F32),
                        pltpu.VMEM((D_MODEL, 2 * D_FF_EXPERT), _BF16),
                        pltpu.VMEM((D_FF_EXPERT, D_MODEL), _BF16),
                        pltpu.SemaphoreType.DMA((2, 2))])
    return pl.pallas_call(
        _expert_kernel,
        out_shape=jax.ShapeDtypeStruct((SORTED_ROWS * ROW_SUB, LANES), _F32),
        grid_spec=grid_spec,
        compiler_params=pltpu.CompilerParams(dimension_semantics=("arbitrary",),
                                             vmem_limit_bytes=VMEM_LIMIT),
        name="expert",
    )(block_expert, block_slot, block_first, block_next, n_used, xs, wgu, bgu3, wd, bd3)


def _combine_kernel(dest_ref, dest_next_ref, gate_ref, xmid_ref, mod_ref, g_ref, y_ref, o_ref,
                    buf_ref, sem):
    tm = xmid_ref.shape[0]
    m = mod_ref[0]
    i = pl.program_id(0)
    n = pl.num_programs(0)

    def gather_rows(idx_ref, slot):
        def issue(r, carry):
            for kk in range(TOP_K):
                d = pl.multiple_of(idx_ref[r * TOP_K + kk] * ROW_SUB, ROW_SUB)
                pltpu.make_async_copy(
                    y_ref.at[pl.ds(d, ROW_SUB)],
                    buf_ref.at[slot, kk, pl.ds(pl.multiple_of(r * ROW_SUB, ROW_SUB), ROW_SUB)],
                    sem.at[slot]).start(priority=kk % 2)
            return carry

        lax.fori_loop(0, tm, issue, 0, unroll=8)

    @pl.when(i == 0)
    def _():
        gather_rows(dest_ref, 0)

    @pl.when(i + 1 < n)
    def _():
        gather_rows(dest_next_ref, (i + 1) % 2)

    slot = i % 2
    for kk in range(TOP_K):
        pltpu.make_async_copy(y_ref.at[pl.ds(0, tm * ROW_SUB)], buf_ref.at[slot, kk],
                              sem.at[slot]).wait()
    g = gate_ref[...]
    pieces = []
    for cc in range(ROW_SUB):
        piece = buf_ref[slot, 0, pl.ds(cc, tm, stride=ROW_SUB), :] * g[:, 0:1]
        for kk in range(1, TOP_K):
            piece = piece + buf_ref[slot, kk, pl.ds(cc, tm, stride=ROW_SUB), :] * g[:, kk:kk + 1]
        pieces.append(piece)
    ffn = jnp.concatenate(pieces, axis=1)
    o_ref[...] = xmid_ref[...] + m[5:6] * _rms(ffn, g_ref[...])


def _combine_call(dest_flat, gates, x_mid, mod3, gpost_ffn, y_rows):
    tm = MOVE_TILE
    tiles_per_batch = SEQ // tm
    n_steps = TOKENS // tm
    return pl.pallas_call(
        _combine_kernel,
        out_shape=jax.ShapeDtypeStruct((TOKENS, D_MODEL), _F32),
        grid=(n_steps,),
        in_specs=[pl.BlockSpec((tm * TOP_K,), lambda i: (i,), memory_space=pltpu.SMEM),
                  pl.BlockSpec((tm * TOP_K,), lambda i: (jnp.minimum(i + 1, n_steps - 1),),
                               memory_space=pltpu.SMEM),
                  pl.BlockSpec((tm, LANES), lambda i: (i, 0)),
                  pl.BlockSpec((tm, D_MODEL), lambda i: (i, 0)),
                  pl.BlockSpec((1, N_MOD, D_MODEL), lambda i: (i // tiles_per_batch, 0, 0)),
                  pl.BlockSpec((1, D_MODEL), lambda i: (0, 0)),
                  pl.BlockSpec(memory_space=pl.ANY)],
        out_specs=pl.BlockSpec((tm, D_MODEL), lambda i: (i, 0)),
        scratch_shapes=[pltpu.VMEM((2, TOP_K, tm * ROW_SUB, LANES), _F32),
                        pltpu.SemaphoreType.DMA((2,))],
        compiler_params=pltpu.CompilerParams(dimension_semantics=("arbitrary",),
                                             vmem_limit_bytes=VMEM_LIMIT),
        name="combine",
    )(dest_flat, dest_flat, gates, x_mid, mod3, gpost_ffn, y_rows)


def _rope_tables():
    pos = jnp.arange(SEQ, dtype=_F32)
    pos_row = jnp.floor(pos / GRID_W)
    pos_col = pos - pos_row * GRID_W
    n_freq = HEAD_DIM // 4
    inv_freq = ROPE_THETA ** (-jnp.arange(n_freq, dtype=_F32) / n_freq)
    d = jnp.arange(LANES) % HEAD_DIM
    f = inv_freq[d % n_freq]
    ang = jnp.where((d < HEAD_DIM // 2)[None, :], pos_row[:, None] * f[None, :], pos_col[:, None] * f[None, :])
    sign = jnp.where((d % (HEAD_DIM // 2)) < n_freq, -1.0, 1.0).astype(_F32)
    return jnp.cos(ang), jnp.sin(ang) * sign[None, :]


def _perm_heads(a, axis):
    shape = a.shape
    a = a.reshape(shape[:axis] + (N_HEADS, HEAD_DIM) + shape[axis + 1:])
    a = jnp.take(a, jnp.array(HEAD_PERM), axis=axis)
    return a.reshape(shape)


def kernel(x, c, ctx, c_ctx, w_ada, b_ada, g_pre_mix, g_post_mix, g_pre_ffn, g_post_ffn, w_in, b_in, attn_sink, sgu_ln_g, sgu_ln_b, sgu_w, sgu_b, g_attn_out, g_sgu_out, w_out, b_out, w_router, b_router, w_gate_up, b_gate_up, w_down, b_down):
    l = 0
    x2 = x.reshape(TOKENS, D_MODEL)
    ctx2 = ctx.reshape(BATCH * CTX_LEN, D_MODEL)

    cc = jnp.zeros((16, D_MODEL), _F32).at[:BATCH].set(c).at[BATCH].set(c_ctx)
    mod = _ada_call(cc, w_ada[l], b_ada[l].reshape(1, -1))
    mod3 = mod.reshape(16, N_MOD, D_MODEL)

    w_in_l = w_in[l]
    b_in_l = b_in[l]
    w_in_p = jnp.concatenate([_perm_heads(w_in_l[:, :ATTN_WIDTH], 1), w_in_l[:, ATTN_WIDTH:]], axis=1)
    b_in_p = jnp.concatenate([_perm_heads(b_in_l[:ATTN_WIDTH], 0), b_in_l[ATTN_WIDTH:]], axis=0)
    cos_t, sin_t = _rope_tables()
    g_pre = g_pre_mix[l].reshape(1, -1)

    q, k, v, su, sv = _inproj_call(x2, mod3, g_pre, w_in_p.astype(_BF16), b_in_p.reshape(1, -1), cos_t, sin_t)
    kc, vc = _ctxproj_call(ctx2, mod3, g_pre, w_in_l[:, KV_START:KV_END].astype(_BF16),
                           b_in_l[KV_START:KV_END].reshape(1, -1))

    sink_p = jnp.take(attn_sink[l], jnp.array(HEAD_PERM))
    ao = _attn_call(sink_p, q, k, v, kc, vc)

    ws2 = sgu_w[l].reshape(SGU_WIDTH // LANES, 2 * SGU_CHUNK, SGU_CHUNK).astype(_BF16)
    bsf = jnp.repeat(sgu_b[l].T, SGU_HEAD_DIM, axis=1)
    w_out_l = w_out[l]
    woa = _perm_heads(w_out_l[:ATTN_WIDTH], 0).astype(_BF16)
    wos = w_out_l[ATTN_WIDTH:].astype(_BF16)
    ga = _perm_heads(g_attn_out[l], 0).reshape(1, -1)
    wr_hi = w_router[l].astype(_BF16)
    wr_lo = (w_router[l] - wr_hi.astype(_F32)).astype(_BF16)
    wr = (jnp.zeros((D_MODEL, LANES), _BF16).at[:, :N_EXPERTS].set(wr_hi)
          .at[:, N_EXPERTS:2 * N_EXPERTS].set(wr_lo))
    br = jnp.full((1, LANES), NEG_BIG, _F32).at[0, :N_EXPERTS].set(b_router[l])
    ii = jnp.arange(POST_TILE // POST_SPLIT)
    tri = (ii[None, :] < ii[:, None]).astype(_BF16)

    x_mid, h2, idx_o, gate_o, rank_o, cnt_o = _post_call(
        x2, ao, su, sv, mod3, sgu_ln_g[l].reshape(1, -1), sgu_ln_b[l].reshape(1, -1), ws2, bsf,
        ga, g_sgu_out[l].reshape(1, -1), woa, wos, b_out[l].reshape(1, -1),
        g_post_mix[l].reshape(1, -1), g_pre_ffn[l].reshape(1, -1), wr, br, tri)

    counts = cnt_o[0, :N_EXPERTS]
    padded = (counts + EXPERT_BLOCK - 1) // EXPERT_BLOCK * EXPERT_BLOCK
    pad_end = jnp.cumsum(padded)
    pad_start = pad_end - padded
    dest = (pad_start[idx_o] + rank_o).astype(jnp.int32).reshape(-1)
    block_start = jnp.arange(N_EXPERT_BLOCKS, dtype=jnp.int32) * EXPERT_BLOCK
    block_expert = jnp.minimum(jnp.sum(pad_end[None, :] <= block_start[:, None], axis=1),
                               N_EXPERTS - 1).astype(jnp.int32)
    n_used = (pad_end[-1:] // EXPERT_BLOCK).astype(jnp.int32)

    xs = _dispatch_call((pad_start + counts).astype(jnp.int32), (padded - counts).astype(jnp.int32),
                        n_used, dest, h2)

    used = padded > 0
    e_ids = jnp.arange(N_EXPERTS, dtype=jnp.int32)
    cand = jnp.where(used, e_ids, N_EXPERTS)
    later_min = lax.cummin(cand[::-1])[::-1]
    next_used = jnp.concatenate([later_min[1:], jnp.full((1,), N_EXPERTS, jnp.int32)])
    ordinal = jnp.cumsum(used.astype(jnp.int32)) - 1
    block_slot = (ordinal[block_expert] & 1).astype(jnp.int32)
    block_first = (block_start == pad_start[block_expert]).astype(jnp.int32)
    block_next = next_used[block_expert].astype(jnp.int32)
    y_rows = _expert_call(block_expert, block_slot, block_first, block_next, n_used, xs,
                          w_gate_up[l], b_gate_up[l].reshape(N_EXPERTS, 1, -1),
                          w_down[l], b_down[l].reshape(N_EXPERTS, 1, -1))
    out = _combine_call(dest, gate_o, x_mid, mod3, g_post_ffn[l].reshape(1, -1), y_rows)
    return out.reshape(BATCH, SEQ, D_MODEL)
```

```python
import jax
import jax.numpy as jnp
import numpy as np
from jax import lax
from jax.experimental import pallas as pl
from jax.experimental.pallas import tpu as pltpu

D_MODEL = 1024
BATCH = 8
SEQ = 4096
TOKENS = BATCH * SEQ
GRID_W = 64
CTX_LEN = 256
N_HEADS = 8
N_KV_HEADS = 2
HEAD_DIM = 64
ATTN_WIDTH = N_HEADS * HEAD_DIM
KV_WIDTH = N_KV_HEADS * HEAD_DIM
WINDOW = 128
ATTN_BLOCK = 128
SGU_HEADS = 8
SGU_HEAD_DIM = 64
SGU_WIDTH = SGU_HEADS * SGU_HEAD_DIM
SGU_CHUNK = 128
KV_START = ATTN_WIDTH
KV_END = ATTN_WIDTH + 2 * KV_WIDTH
IN_WIDTH = KV_END + 2 * SGU_WIDTH
N_EXPERTS = 32
TOP_K = 4
D_FF_EXPERT = 1024
SWIGLU_LIMIT = 7.0
SWIGLU_ALPHA = 1.702
ROPE_THETA = 10000.0
EPS = 1e-6
N_MOD = 6

LANES = 128
NEG_BIG = -1e30
VMEM_LIMIT = 56 * 1024 * 1024

ROW_TILE = 512
MOVE_TILE = 512
ATTN_STEP_BLOCKS = 8
EXPERT_BLOCK = 512
POST_TILE = 1024
POST_SPLIT = 4
N_EXPERT_BLOCKS = TOKENS * TOP_K // EXPERT_BLOCK + N_EXPERTS
SORTED_ROWS = N_EXPERT_BLOCKS * EXPERT_BLOCK
HEAD_PERM = (0, 4, 1, 5, 2, 6, 3, 7)

_F32 = jnp.float32
_BF16 = jnp.bfloat16


def _rms(x, g):
    ms = jnp.mean(x * x, axis=-1, keepdims=True)
    return x * lax.rsqrt(ms + EPS) * g


ROW_SUB = D_MODEL // LANES


def _store_row_tiles(ref, x):
    n = x.shape[0]
    for cc in range(ROW_SUB):
        ref[pl.ds(cc, n, stride=ROW_SUB), :] = x[:, cc * LANES:(cc + 1) * LANES]


def _load_row_tiles(ref, n):
    return jnp.concatenate([ref[pl.ds(cc, n, stride=ROW_SUB), :] for cc in range(ROW_SUB)], axis=1)


def _gelu_tanh(x):
    c = 0.7978845608028654
    return x * (0.5 * (1.0 + jnp.tanh(c * (x + 0.044715 * (x * x * x)))))


def _ada_kernel(c_ref, w_ref, b_ref, o_ref):
    c = c_ref[...]
    a = c / (1.0 + jnp.exp(-c))
    o_ref[...] = jnp.dot(a, w_ref[...], preferred_element_type=_F32,
                         precision=lax.Precision.HIGHEST) + b_ref[...]


def _ada_call(cc, w_ada, b_ada):
    n = N_MOD * D_MODEL
    tn = 512
    return pl.pallas_call(
        _ada_kernel,
        out_shape=jax.ShapeDtypeStruct((16, n), _F32),
        grid=(n // tn,),
        in_specs=[pl.BlockSpec((16, D_MODEL), lambda j: (0, 0)),
                  pl.BlockSpec((D_MODEL, tn), lambda j: (0, j)),
                  pl.BlockSpec((1, tn), lambda j: (0, j))],
        out_specs=pl.BlockSpec((16, tn), lambda j: (0, j)),
        compiler_params=pltpu.CompilerParams(dimension_semantics=("arbitrary",),
                                             vmem_limit_bytes=VMEM_LIMIT),
        name="ada",
    )(cc, w_ada, b_ada)


def _rope(x, cos, sin_signed, first_half):
    nxt = pltpu.roll(x, LANES - 16, 1)
    prv = pltpu.roll(x, 16, 1)
    return x * cos + jnp.where(first_half, nxt, prv) * sin_signed


def _inproj_kernel(x_ref, mod_ref, g_ref, w_ref, b_ref, cos_ref, sin_ref,
                   q_ref, k_ref, v_ref, su_ref, sv_ref):
    m = mod_ref[0]
    h = _rms(x_ref[...], g_ref[...]) * (1.0 + m[1:2]) + m[0:1]
    hb = h.astype(_BF16)
    cos = cos_ref[...]
    sin = sin_ref[...]
    lane = lax.broadcasted_iota(jnp.int32, cos.shape, 1)
    first_half = (lane & 31) < 16

    q = jnp.dot(hb, w_ref[:, 0:ATTN_WIDTH], preferred_element_type=_F32) + b_ref[:, 0:ATTN_WIDTH]
    for j in range(ATTN_WIDTH // LANES):
        qj = _rope(q[:, j * LANES:(j + 1) * LANES], cos, sin, first_half)
        q_ref[:, j * LANES:(j + 1) * LANES] = (qj * (HEAD_DIM ** -0.5)).astype(_BF16)
    kv = jnp.dot(hb, w_ref[:, KV_START:KV_END], preferred_element_type=_F32) + b_ref[:, KV_START:KV_END]
    k_ref[...] = _rope(kv[:, 0:KV_WIDTH], cos, sin, first_half).astype(_BF16)
    v_ref[...] = kv[:, KV_WIDTH:].astype(_BF16)
    u0 = KV_END
    u1 = KV_END + SGU_WIDTH
    su_ref[...] = jnp.dot(hb, w_ref[:, u0:u1], preferred_element_type=_F32) + b_ref[:, u0:u1]
    sv_ref[...] = jnp.dot(hb, w_ref[:, u1:IN_WIDTH], preferred_element_type=_F32) + b_ref[:, u1:IN_WIDTH]


def _inproj_call(x2, mod3, g_pre, w_in_b, b_in2, cos_t, sin_t):
    tm = ROW_TILE
    tiles_per_batch = SEQ // tm
    row = lambda i: (i, 0)
    const = lambda i: (0, 0)
    return pl.pallas_call(
        _inproj_kernel,
        out_shape=(jax.ShapeDtypeStruct((TOKENS, ATTN_WIDTH), _BF16),
                   jax.ShapeDtypeStruct((TOKENS, KV_WIDTH), _BF16),
                   jax.ShapeDtypeStruct((TOKENS, KV_WIDTH), _BF16),
                   jax.ShapeDtypeStruct((TOKENS, SGU_WIDTH), _F32),
                   jax.ShapeDtypeStruct((TOKENS, SGU_WIDTH), _F32)),
        grid=(TOKENS // tm,),
        in_specs=[pl.BlockSpec((tm, D_MODEL), row),
                  pl.BlockSpec((1, N_MOD, D_MODEL), lambda i: (i // tiles_per_batch, 0, 0)),
                  pl.BlockSpec((1, D_MODEL), const),
                  pl.BlockSpec((D_MODEL, IN_WIDTH), const),
                  pl.BlockSpec((1, IN_WIDTH), const),
                  pl.BlockSpec((tm, LANES), lambda i: (i % tiles_per_batch, 0)),
                  pl.BlockSpec((tm, LANES), lambda i: (i % tiles_per_batch, 0))],
        out_specs=(pl.BlockSpec((tm, ATTN_WIDTH), row),
                   pl.BlockSpec((tm, KV_WIDTH), row),
                   pl.BlockSpec((tm, KV_WIDTH), row),
                   pl.BlockSpec((tm, SGU_WIDTH), row),
                   pl.BlockSpec((tm, SGU_WIDTH), row)),
        compiler_params=pltpu.CompilerParams(dimension_semantics=("arbitrary",),
                                             vmem_limit_bytes=VMEM_LIMIT),
        name="inproj",
    )(x2, mod3, g_pre, w_in_b, b_in2, cos_t, sin_t)


def _ctxproj_kernel(x_ref, mod_ref, g_ref, w_ref, b_ref, k_ref, v_ref):
    m = mod_ref[0]
    h = _rms(x_ref[...], g_ref[...]) * (1.0 + m[1:2]) + m[0:1]
    kv = jnp.dot(h.astype(_BF16), w_ref[...], preferred_element_type=_F32) + b_ref[...]
    k_ref[...] = kv[:, 0:KV_WIDTH].astype(_BF16)
    v_ref[...] = kv[:, KV_WIDTH:].astype(_BF16)


def _ctxproj_call(ctx2, mod3, g_pre, w_kv_b, b_kv2):
    tm = ROW_TILE
    rows = ctx2.shape[0]
    row = lambda i: (i, 0)
    const = lambda i: (0, 0)
    return pl.pallas_call(
        _ctxproj_kernel,
        out_shape=(jax.ShapeDtypeStruct((rows, KV_WIDTH), _BF16),
                   jax.ShapeDtypeStruct((rows, KV_WIDTH), _BF16)),
        grid=(rows // tm,),
        in_specs=[pl.BlockSpec((tm, D_MODEL), row),
                  pl.BlockSpec((1, N_MOD, D_MODEL), lambda i: (BATCH, 0, 0)),
                  pl.BlockSpec((1, D_MODEL), const),
                  pl.BlockSpec((D_MODEL, 2 * KV_WIDTH), const),
                  pl.BlockSpec((1, 2 * KV_WIDTH), const)],
        out_specs=(pl.BlockSpec((tm, KV_WIDTH), row),
                   pl.BlockSpec((tm, KV_WIDTH), row)),
        compiler_params=pltpu.CompilerParams(dimension_semantics=("arbitrary",),
                                             vmem_limit_bytes=VMEM_LIMIT),
        name="ctxproj",
    )(ctx2, mod3, g_pre, w_kv_b, b_kv2)


def _attn_kernel(sink_ref, q_ref, kp_ref, km_ref, kn_ref, vp_ref, vm_ref, vn_ref,
                 kc_ref, vc_ref, o_ref):
    n = pl.program_id(1)
    nstep = pl.num_programs(1)
    blk = ATTN_BLOCK
    nq = ATTN_STEP_BLOCKS
    rows2 = 2 * blk
    r = lax.broadcasted_iota(jnp.int32, (rows2, blk), 0) & (blk - 1)
    c = lax.broadcasted_iota(jnp.int32, (rows2, blk), 1)
    tri_prev = c >= r
    tri_next = c <= r
    row1 = lax.broadcasted_iota(jnp.int32, (rows2, 1), 0)
    lane_q = lax.broadcasted_iota(jnp.int32, (blk, LANES), 1)
    n_grp = ATTN_WIDTH // LANES
    k_blocks = [kp_ref[...]] + [km_ref[t * blk:(t + 1) * blk, :] for t in range(nq)] + [kn_ref[...]]
    v_blocks = [vp_ref[...]] + [vm_ref[t * blk:(t + 1) * blk, :] for t in range(nq)] + [vn_ref[...]]

    def scores(t):
        keys = jnp.concatenate(k_blocks[t:t + 3] + [kc_ref[...]], axis=0)
        q_rows = []
        for j in range(n_grp):
            qg = q_ref[t * blk:(t + 1) * blk, j * LANES:(j + 1) * LANES]
            zero = jnp.zeros_like(qg)
            q_rows += [jnp.where(lane_q < HEAD_DIM, qg, zero), jnp.where(lane_q >= HEAD_DIM, qg, zero)]
        return lax.dot_general(jnp.concatenate(q_rows, axis=0), keys, (((1,), (1,)), ((), ())),
                               preferred_element_type=_F32)

    def softmax(t, s_all):
        mask_a = (tri_prev & (n > 0)) if t == 0 else tri_prev
        mask_c = (tri_next & (n < nstep - 1)) if t == nq - 1 else tri_next
        p_rows, denoms = [], []
        for j in range(n_grp):
            s = s_all[j * rows2:(j + 1) * rows2]
            s = jnp.concatenate([jnp.where(mask_a, s[:, 0:blk], NEG_BIG), s[:, blk:2 * blk],
                                 jnp.where(mask_c, s[:, 2 * blk:3 * blk], NEG_BIG), s[:, 3 * blk:]],
                                axis=1)
            sk = jnp.where(row1 < blk, sink_ref[2 * j], sink_ref[2 * j + 1])
            m = jnp.maximum(jnp.max(s, axis=-1, keepdims=True), sk)
            p = jnp.exp(s - m)
            denoms.append(jnp.sum(p, axis=-1, keepdims=True) + jnp.exp(sk - m))
            p_rows.append(p.astype(_BF16))
        return jnp.concatenate(p_rows, axis=0), denoms

    def weighted_values(t, p_all, denoms):
        vals = jnp.concatenate(v_blocks[t:t + 3] + [vc_ref[...]], axis=0)
        o_all = jnp.dot(p_all, vals, preferred_element_type=_F32)
        for j in range(n_grp):
            o2 = o_all[j * rows2:(j + 1) * rows2] / denoms[j]
            og = jnp.where(lane_q < HEAD_DIM, o2[0:blk], o2[blk:])
            o_ref[t * blk:(t + 1) * blk, j * LANES:(j + 1) * LANES] = og.astype(_BF16)

    s_next = scores(0)
    probs = None
    for t in range(nq + 1):
        s_cur, s_next = s_next, (scores(t + 1) if t + 1 < nq else None)
        if probs is not None:
            weighted_values(t - 1, *probs)
        probs = softmax(t, s_cur) if t < nq else None


def _attn_call(sink_p, q, k, v, kc, vc):
    nblk = SEQ // ATTN_BLOCK
    nq = ATTN_STEP_BLOCKS
    nstep = nblk // nq
    own = lambda b, n: (b * nstep + n, 0)
    prev = lambda b, n: (b * nblk + jnp.maximum(nq * n - 1, 0), 0)
    nxt = lambda b, n: (b * nblk + jnp.minimum(nq * n + nq, nblk - 1), 0)
    ctx = lambda b, n: (b, 0)
    kv1 = (ATTN_BLOCK, KV_WIDTH)
    kvm = (nq * ATTN_BLOCK, KV_WIDTH)
    return pl.pallas_call(
        _attn_kernel,
        out_shape=jax.ShapeDtypeStruct((TOKENS, ATTN_WIDTH), _BF16),
        grid=(BATCH, nstep),
        in_specs=[pl.BlockSpec(memory_space=pltpu.SMEM),
                  pl.BlockSpec((nq * ATTN_BLOCK, ATTN_WIDTH), own),
                  pl.BlockSpec(kv1, prev), pl.BlockSpec(kvm, own), pl.BlockSpec(kv1, nxt),
                  pl.BlockSpec(kv1, prev), pl.BlockSpec(kvm, own), pl.BlockSpec(kv1, nxt),
                  pl.BlockSpec((CTX_LEN, KV_WIDTH), ctx),
                  pl.BlockSpec((CTX_LEN, KV_WIDTH), ctx)],
        out_specs=pl.BlockSpec((nq * ATTN_BLOCK, ATTN_WIDTH), own),
        compiler_params=pltpu.CompilerParams(dimension_semantics=("arbitrary", "arbitrary"),
                                             vmem_limit_bytes=VMEM_LIMIT),
        name="attn",
    )(sink_p, q, k, k, k, v, v, v, kc, vc)


def _post_kernel(x_ref, ao_ref, su_ref, sv_ref, mod_ref, lng_ref, lnb_ref, ws_ref, bs_ref,
                 ga_ref, gs_ref, woa_ref, wos_ref, bo_ref, gpost_ref, gpre_ref, wr_ref, br_ref,
                 tri_ref,
                 xmid_ref, h2_ref, idx_ref, gate_ref, rank_ref, cnt_ref,
                 mixed_ref, carry_ref):
    tm = x_ref.shape[0]
    m = mod_ref[0]

    @pl.when(pl.program_id(0) == 0)
    def _():
        carry_ref[...] = jnp.zeros_like(carry_ref)

    n_sub = POST_SPLIT
    ts = tm // n_sub
    tok_per_row = LANES // TOP_K
    lane = lax.broadcasted_iota(jnp.int32, (SGU_CHUNK, LANES), 1)
    st = [dict() for _ in range(n_sub)]
    carry = [carry_ref[...]]

    def rows(h):
        return slice(h * ts, (h + 1) * ts)

    def s1(h):
        gv = _gelu_tanh(sv_ref[rows(h), :])
        mu = jnp.mean(gv, axis=-1, keepdims=True)
        gc = gv - mu
        var = jnp.mean(gc * gc, axis=-1, keepdims=True)
        st[h]["vb"] = (gc * lax.rsqrt(var + EPS) * lng_ref[...] + lnb_ref[...]).astype(_BF16)

    def s2(h):
        vb = st[h].pop("vb")
        for c in range(ts // SGU_CHUNK):
            r0 = c * SGU_CHUNK
            for p in range(SGU_WIDTH // LANES):
                l0 = p * LANES
                r = jnp.dot(ws_ref[p], vb[r0:r0 + SGU_CHUNK, l0:l0 + LANES], preferred_element_type=_F32)
                mixed = jnp.where(lane < SGU_HEAD_DIM, r[0:SGU_CHUNK], r[SGU_CHUNK:])
                mixed_ref[h * ts + r0:h * ts + r0 + SGU_CHUNK, l0:l0 + LANES] = mixed + bs_ref[:, l0:l0 + LANES]
        sgu_o = _gelu_tanh(su_ref[rows(h), :]) * mixed_ref[rows(h), :]
        st[h]["oa"] = _rms(ao_ref[rows(h), :].astype(_F32), ga_ref[...]).astype(_BF16)
        st[h]["os"] = _rms(sgu_o, gs_ref[...]).astype(_BF16)

    def s3(h):
        st[h]["mix"] = (jnp.dot(st[h].pop("oa"), woa_ref[...], preferred_element_type=_F32)
                        + jnp.dot(st[h].pop("os"), wos_ref[...], preferred_element_type=_F32) + bo_ref[...])

    def s4(h):
        x_mid = x_ref[rows(h), :] + m[2:3] * _rms(st[h].pop("mix"), gpost_ref[...])
        xmid_ref[rows(h), :] = x_mid
        h2 = _rms(x_mid, gpre_ref[...]) * (1.0 + m[4:5]) + m[3:4]
        for cc in range(ROW_SUB):
            h2_ref[pl.ds(h * ts * ROW_SUB + cc, ts, stride=ROW_SUB), :] = h2[:, cc * LANES:(cc + 1) * LANES]
        h_hi = h2.astype(_BF16)
        st[h]["h_hi"] = h_hi
        st[h]["h_lo"] = (h2 - h_hi.astype(_F32)).astype(_BF16)

    def s5(h):
        r = (jnp.dot(st[h].pop("h_hi"), wr_ref[...], preferred_element_type=_F32)
             + jnp.dot(st[h].pop("h_lo"), wr_ref[...], preferred_element_type=_F32))
        st[h]["lg"] = r + pltpu.roll(r, LANES - N_EXPERTS, 1) + br_ref[...]

    def s6(h):
        lg = st[h].pop("lg")
        lane_r = lax.broadcasted_iota(jnp.int32, lg.shape, 1)
        lane_f = lane_r.astype(_F32)
        tops, hots = [], []
        for _k in range(TOP_K):
            mx = jnp.max(lg, axis=-1, keepdims=True)
            pick = jnp.min(jnp.where(lg == mx, lane_f, float(LANES)), axis=-1, keepdims=True)
            hot = lane_f == pick
            tops.append((mx, pick))
            hots.append(hot)
            lg = jnp.where(hot, 2.0 * NEG_BIG, lg)
        es = [jnp.exp(t[0] - tops[0][0]) for t in tops]
        esum = es[0] + es[1] + es[2] + es[3]
        multi = jnp.zeros(lg.shape, _F32)
        for hot in hots:
            multi = multi + jnp.where(hot, 1.0, 0.0)
        cum = jnp.dot(tri_ref[...], multi.astype(_BF16), preferred_element_type=_F32) + carry[0]
        row_r = lax.broadcasted_iota(jnp.int32, lg.shape, 0)
        lane_base = (row_r & (tok_per_row - 1)) * TOP_K
        gate_o = jnp.zeros(lg.shape, _F32)
        idx_e = jnp.zeros(lg.shape, _F32)
        rank_hi_e = jnp.zeros(lg.shape, _F32)
        rank_lo_e = jnp.zeros(lg.shape, _F32)
        for kk in range(TOP_K):
            rk = jnp.sum(jnp.where(hots[kk], cum, 0.0), axis=-1, keepdims=True)
            rk_hi = jnp.floor(rk * (1.0 / 256.0))
            here = lane_r == lane_base + kk
            gate_o = jnp.where(lane_r == kk, es[kk] / esum, gate_o)
            idx_e = jnp.where(here, tops[kk][1], idx_e)
            rank_hi_e = jnp.where(here, rk_hi, rank_hi_e)
            rank_lo_e = jnp.where(here, rk - 256.0 * rk_hi, rank_lo_e)
        fr = lax.broadcasted_iota(jnp.int32, (ts // tok_per_row, ts), 0)
        fc = lax.broadcasted_iota(jnp.int32, (ts // tok_per_row, ts), 1)
        fold = jnp.where(lax.shift_right_logical(fc, tok_per_row.bit_length() - 1) == fr, 1.0, 0.0).astype(_BF16)
        fs = slice(h * ts // tok_per_row, (h + 1) * ts // tok_per_row)
        idx_ref[fs, :] = jnp.dot(fold, idx_e.astype(_BF16), preferred_element_type=_F32).astype(jnp.int32)
        rank_ref[fs, :] = (256.0 * jnp.dot(fold, rank_hi_e.astype(_BF16), preferred_element_type=_F32)
                           + jnp.dot(fold, rank_lo_e.astype(_BF16), preferred_element_type=_F32)).astype(jnp.int32)
        gate_ref[rows(h), :] = gate_o
        carry[0] = carry[0] + jnp.sum(multi, axis=0, keepdims=True)

    order = sorted(((k + 2.5 * h, h, k) for h in range(n_sub) for k in range(6)))
    stages = (s1, s2, s3, s4, s5, s6)
    for _, h, k in order:
        stages[k](h)
    carry_ref[...] = carry[0]
    cnt_ref[...] = carry[0].astype(jnp.int32)


def _post_call(x2, ao, su, sv, mod3, lng, lnb, ws2, bsf, ga, gs, woa, wos, bo, gpost, gpre,
               wr, br, tri):
    tm = POST_TILE
    tiles_per_batch = SEQ // tm
    row = lambda i: (i, 0)
    const = lambda i: (0, 0)
    const3 = lambda i: (0, 0, 0)
    return pl.pallas_call(
        _post_kernel,
        out_shape=(jax.ShapeDtypeStruct((TOKENS, D_MODEL), _F32),
                   jax.ShapeDtypeStruct((TOKENS * ROW_SUB, LANES), _F32),
                   jax.ShapeDtypeStruct((TOKENS * TOP_K // LANES, LANES), jnp.int32),
                   jax.ShapeDtypeStruct((TOKENS, LANES), _F32),
                   jax.ShapeDtypeStruct((TOKENS * TOP_K // LANES, LANES), jnp.int32),
                   jax.ShapeDtypeStruct((1, LANES), jnp.int32)),
        grid=(TOKENS // tm,),
        in_specs=[pl.BlockSpec((tm, D_MODEL), row),
                  pl.BlockSpec((tm, ATTN_WIDTH), row),
                  pl.BlockSpec((tm, SGU_WIDTH), row),
                  pl.BlockSpec((tm, SGU_WIDTH), row),
                  pl.BlockSpec((1, N_MOD, D_MODEL), lambda i: (i // tiles_per_batch, 0, 0)),
                  pl.BlockSpec((1, SGU_WIDTH), const),
                  pl.BlockSpec((1, SGU_WIDTH), const),
                  pl.BlockSpec((SGU_WIDTH // LANES, 2 * SGU_CHUNK, SGU_CHUNK), const3),
                  pl.BlockSpec((SGU_CHUNK, SGU_WIDTH), const),
                  pl.BlockSpec((1, ATTN_WIDTH), const),
                  pl.BlockSpec((1, SGU_WIDTH), const),
                  pl.BlockSpec((ATTN_WIDTH, D_MODEL), const),
                  pl.BlockSpec((SGU_WIDTH, D_MODEL), const),
                  pl.BlockSpec((1, D_MODEL), const),
                  pl.BlockSpec((1, D_MODEL), const),
                  pl.BlockSpec((1, D_MODEL), const),
                  pl.BlockSpec((D_MODEL, LANES), const),
                  pl.BlockSpec((1, LANES), const),
                  pl.BlockSpec((tm // POST_SPLIT, tm // POST_SPLIT), const)],
        out_specs=(pl.BlockSpec((tm, D_MODEL), row),
                   pl.BlockSpec((tm * ROW_SUB, LANES), row),
                   pl.BlockSpec((tm * TOP_K // LANES, LANES), row),
                   pl.BlockSpec((tm, LANES), row),
                   pl.BlockSpec((tm * TOP_K // LANES, LANES), row),
                   pl.BlockSpec((1, LANES), const)),
        scratch_shapes=[pltpu.VMEM((tm, SGU_WIDTH), _F32),
                        pltpu.VMEM((1, LANES), _F32)],
        compiler_params=pltpu.CompilerParams(dimension_semantics=("arbitrary",),
                                             vmem_limit_bytes=VMEM_LIMIT),
        name="post",
    )(x2, ao, su, sv, mod3, lng, lnb, ws2, bsf, ga, gs, woa, wos, bo, gpost, gpre, wr, br, tri)


def _dispatch_kernel(fs_ref, fl_ref, nu_ref, dest_ref, h2_ref, xs_ref, zero_ref, sem, zsem):
    tm = h2_ref.shape[0] // ROW_SUB
    n_token_steps = TOKENS // tm
    i = pl.program_id(0)

    @pl.when(i < n_token_steps)
    def _():
        def issue(r, carry):
            for kk in range(TOP_K):
                d = pl.multiple_of(dest_ref[r * TOP_K + kk] * ROW_SUB, ROW_SUB)
                pltpu.make_async_copy(h2_ref.at[pl.ds(pl.multiple_of(r * ROW_SUB, ROW_SUB), ROW_SUB)],
                                      xs_ref.at[pl.ds(d, ROW_SUB)], sem).start(priority=kk % 2)
            return carry

        lax.fori_loop(0, tm, issue, 0, unroll=8)
        for kk in range(TOP_K):
            pltpu.make_async_copy(h2_ref, xs_ref.at[pl.ds(0, tm * ROW_SUB)], sem).wait()

    @pl.when(i == n_token_steps)
    def _():
        zero_ref[...] = jnp.zeros_like(zero_ref)
        block_rows = EXPERT_BLOCK * ROW_SUB

        def pad_run(e, wait):
            pos = fs_ref[e]
            length = fl_ref[e]
            for bit in reversed(range(EXPERT_BLOCK.bit_length() - 1)):
                size = 1 << bit
                take = length & size

                @pl.when(take != 0)
                def _():
                    cp = pltpu.make_async_copy(
                        zero_ref.at[pl.ds(0, size * ROW_SUB)],
                        xs_ref.at[pl.ds(pl.multiple_of(pos * ROW_SUB, ROW_SUB), size * ROW_SUB)], zsem)
                    if wait:
                        cp.wait()
                    else:
                        cp.start()

                pos = pos + take

        def tail_block(blk, wait):
            cp = pltpu.make_async_copy(
                zero_ref, xs_ref.at[pl.ds(pl.multiple_of(blk * block_rows, block_rows), block_rows)], zsem)
            if wait:
                cp.wait()
            else:
                cp.start()

        for wait in (False, True):
            lax.fori_loop(0, N_EXPERTS, lambda e, c, w=wait: (pad_run(e, w), c)[1], 0)
            lax.fori_loop(nu_ref[0], N_EXPERT_BLOCKS, lambda blk, c, w=wait: (tail_block(blk, w), c)[1], 0)


def _dispatch_call(fill_start, fill_len, n_used, dest_flat, h2):
    tm = MOVE_TILE
    n_token_steps = TOKENS // tm
    grid_spec = pltpu.PrefetchScalarGridSpec(
        num_scalar_prefetch=3,
        grid=(n_token_steps + 1,),
        in_specs=[pl.BlockSpec((tm * TOP_K,), lambda i, fs, fl, nu: (jnp.minimum(i, n_token_steps - 1),),
                               memory_space=pltpu.SMEM),
                  pl.BlockSpec((tm * ROW_SUB, LANES),
                               lambda i, fs, fl, nu: (jnp.minimum(i, n_token_steps - 1), 0))],
        out_specs=pl.BlockSpec(memory_space=pl.ANY),
        scratch_shapes=[pltpu.VMEM((EXPERT_BLOCK * ROW_SUB, LANES), _F32),
                        pltpu.SemaphoreType.DMA,
                        pltpu.SemaphoreType.DMA])
    return pl.pallas_call(
        _dispatch_kernel,
        out_shape=jax.ShapeDtypeStruct((SORTED_ROWS * ROW_SUB, LANES), _F32),
        grid_spec=grid_spec,
        compiler_params=pltpu.CompilerParams(dimension_semantics=("arbitrary",),
                                             vmem_limit_bytes=VMEM_LIMIT),
        name="dispatch",
    )(fill_start, fill_len, n_used, dest_flat, h2)


def _expert_kernel(be_ref, slot_ref, first_ref, nxt_ref, nu_ref,
                   xs_ref, wgu_hbm, bgu0_ref, bgu1_ref, wd_hbm, bd0_ref, bd1_ref, y_ref,
                   wgu_stage, wd_stage, wgu_bf, wd_bf, sem):
    p = pl.program_id(0)
    half = EXPERT_BLOCK * ROW_SUB
    n_pairs_used = (nu_ref[0] + 1) // 2

    def stage_copies(e):
        return (pltpu.make_async_copy(wgu_hbm.at[e], wgu_stage, sem.at[0]),
                pltpu.make_async_copy(wd_hbm.at[e], wd_stage, sem.at[1]))

    @pl.when(p == 0)
    def _():
        for cp in stage_copies(be_ref[0]):
            cp.start()

    for b in (2 * p, 2 * p + 1):
        @pl.when((first_ref[b] == 1) & (b < nu_ref[0]))
        def _():
            for cp in stage_copies(be_ref[b]):
                cp.wait()
            wgu_bf[slot_ref[b]] = wgu_stage[...].astype(_BF16)
            wd_bf[slot_ref[b]] = wd_stage[...].astype(_BF16)

            @pl.when(nxt_ref[b] < N_EXPERTS)
            def _():
                for cp in stage_copies(nxt_ref[b]):
                    cp.start()

    def ffn(x, slot, bgu_ref, bd_ref):
        gu = jnp.dot(x, wgu_bf[slot], preferred_element_type=_F32) + bgu_ref[0]
        gate = jnp.minimum(gu[:, 0:D_FF_EXPERT], SWIGLU_LIMIT)
        up = jnp.clip(gu[:, D_FF_EXPERT:], -SWIGLU_LIMIT, SWIGLU_LIMIT)
        act = (up + 1.0) * gate * (1.0 / (1.0 + jnp.exp(-SWIGLU_ALPHA * gate)))
        return jnp.dot(act.astype(_BF16), wd_bf[slot], preferred_element_type=_F32) + bd_ref[0]

    def load_rows(which):
        return jnp.concatenate([xs_ref[pl.ds(which * half + cc, EXPERT_BLOCK, stride=ROW_SUB), :]
                                for cc in range(ROW_SUB)], axis=1).astype(_BF16)

    def store_rows(which, y):
        for cc in range(ROW_SUB):
            y_ref[pl.ds(which * half + cc, EXPERT_BLOCK, stride=ROW_SUB), :] = y[:, cc * LANES:(cc + 1) * LANES]

    @pl.when(p < n_pairs_used)
    def _():
        y0 = ffn(load_rows(0), slot_ref[2 * p], bgu0_ref, bd0_ref)
        x1 = load_rows(1)
        store_rows(0, y0)
        store_rows(1, ffn(x1, slot_ref[2 * p + 1], bgu1_ref, bd1_ref))

    @pl.when(p >= n_pairs_used)
    def _():
        y_ref[...] = jnp.zeros_like(y_ref)


def _expert_call(block_expert, block_slot, block_first, block_next, n_used, xs, wgu, bgu3, wd, bd3):
    tb = 2 * EXPERT_BLOCK
    live = lambda p, be, sl, fi, nx, nu: (jnp.minimum(p, (nu[0] - 1) // 2), 0)
    bsel0 = lambda p, be, sl, fi, nx, nu: (be[2 * p], 0, 0)
    bsel1 = lambda p, be, sl, fi, nx, nu: (be[2 * p + 1], 0, 0)
    grid_spec = pltpu.PrefetchScalarGridSpec(
        num_scalar_prefetch=5,
        grid=(N_EXPERT_BLOCKS // 2,),
        in_specs=[pl.BlockSpec((tb * ROW_SUB, LANES), live),
                  pl.BlockSpec(memory_space=pl.ANY),
                  pl.BlockSpec((1, 1, 2 * D_FF_EXPERT), bsel0),
                  pl.BlockSpec((1, 1, 2 * D_FF_EXPERT), bsel1),
                  pl.BlockSpec(memory_space=pl.ANY),
                  pl.BlockSpec((1, 1, D_MODEL), bsel0),
                  pl.BlockSpec((1, 1, D_MODEL), bsel1)],
        out_specs=pl.BlockSpec((tb * ROW_SUB, LANES), lambda p, be, sl, fi, nx, nu: (p, 0)),
        scratch_shapes=[pltpu.VMEM((D_MODEL, 2 * D_FF_EXPERT), _F32),
                        pltpu.VMEM((D_FF_EXPERT, D_MODEL), _F32),
                        pltpu.VMEM((2, D_MODEL, 2 * D_FF_EXPERT), _BF16),
                        pltpu.VMEM((2, D_FF_EXPERT, D_MODEL), _BF16),
                        pltpu.SemaphoreType.DMA((2,))])
    return pl.pallas_call(
        _expert_kernel,
        out_shape=jax.ShapeDtypeStruct((SORTED_ROWS * ROW_SUB, LANES), _F32),
        grid_spec=grid_spec,
        compiler_params=pltpu.CompilerParams(dimension_semantics=("arbitrary",),
                                             vmem_limit_bytes=VMEM_LIMIT),
        name="expert",
    )(block_expert, block_slot, block_first, block_next, n_used, xs, wgu, bgu3, bgu3, wd, bd3, bd3)


def _combine_kernel(dest_ref, dest_next_ref, gate_ref, xmid_ref, mod_ref, g_ref, y_ref, o_ref,
                    buf_ref, sem):
    tm = xmid_ref.shape[0]
    m = mod_ref[0]
    i = pl.program_id(0)
    n = pl.num_programs(0)

    def gather_rows(idx_ref, slot):
        def issue(r, carry):
            for kk in range(TOP_K):
                d = pl.multiple_of(idx_ref[r * TOP_K + kk] * ROW_SUB, ROW_SUB)
                pltpu.make_async_copy(
                    y_ref.at[pl.ds(d, ROW_SUB)],
                    buf_ref.at[slot, kk, pl.ds(pl.multiple_of(r * ROW_SUB, ROW_SUB), ROW_SUB)],
                    sem.at[slot]).start(priority=kk % 2)
            return carry

        lax.fori_loop(0, tm, issue, 0, unroll=8)

    @pl.when(i == 0)
    def _():
        gather_rows(dest_ref, 0)

    @pl.when(i + 1 < n)
    def _():
        gather_rows(dest_next_ref, (i + 1) % 2)

    slot = i % 2
    for kk in range(TOP_K):
        pltpu.make_async_copy(y_ref.at[pl.ds(0, tm * ROW_SUB)], buf_ref.at[slot, kk],
                              sem.at[slot]).wait()
    g = gate_ref[...]
    pieces = []
    for cc in range(ROW_SUB):
        piece = buf_ref[slot, 0, pl.ds(cc, tm, stride=ROW_SUB), :] * g[:, 0:1]
        for kk in range(1, TOP_K):
            piece = piece + buf_ref[slot, kk, pl.ds(cc, tm, stride=ROW_SUB), :] * g[:, kk:kk + 1]
        pieces.append(piece)
    ffn = jnp.concatenate(pieces, axis=1)
    o_ref[...] = xmid_ref[...] + m[5:6] * _rms(ffn, g_ref[...])


def _combine_call(dest_flat, gates, x_mid, mod3, gpost_ffn, y_rows):
    tm = MOVE_TILE
    tiles_per_batch = SEQ // tm
    n_steps = TOKENS // tm
    return pl.pallas_call(
        _combine_kernel,
        out_shape=jax.ShapeDtypeStruct((TOKENS, D_MODEL), _F32),
        grid=(n_steps,),
        in_specs=[pl.BlockSpec((tm * TOP_K,), lambda i: (i,), memory_space=pltpu.SMEM),
                  pl.BlockSpec((tm * TOP_K,), lambda i: (jnp.minimum(i + 1, n_steps - 1),),
                               memory_space=pltpu.SMEM),
                  pl.BlockSpec((tm, LANES), lambda i: (i, 0)),
                  pl.BlockSpec((tm, D_MODEL), lambda i: (i, 0)),
                  pl.BlockSpec((1, N_MOD, D_MODEL), lambda i: (i // tiles_per_batch, 0, 0)),
                  pl.BlockSpec((1, D_MODEL), lambda i: (0, 0)),
                  pl.BlockSpec(memory_space=pl.ANY)],
        out_specs=pl.BlockSpec((tm, D_MODEL), lambda i: (i, 0)),
        scratch_shapes=[pltpu.VMEM((2, TOP_K, tm * ROW_SUB, LANES), _F32),
                        pltpu.SemaphoreType.DMA((2,))],
        compiler_params=pltpu.CompilerParams(dimension_semantics=("arbitrary",),
                                             vmem_limit_bytes=VMEM_LIMIT),
        name="combine",
    )(dest_flat, dest_flat, gates, x_mid, mod3, gpost_ffn, y_rows)


def _rope_tables():
    pos = np.arange(SEQ)
    pos_row = (pos // GRID_W).astype(np.float64)
    pos_col = (pos % GRID_W).astype(np.float64)
    n_freq = HEAD_DIM // 4
    inv_freq = ROPE_THETA ** (-np.arange(n_freq, dtype=np.float64) / n_freq)
    d = np.arange(LANES) % HEAD_DIM
    f = inv_freq[d % n_freq]
    ang = np.where((d < HEAD_DIM // 2)[None, :], pos_row[:, None] * f[None, :], pos_col[:, None] * f[None, :])
    sign = np.where((d % (HEAD_DIM // 2)) < n_freq, -1.0, 1.0)
    return (jnp.asarray(np.cos(ang), dtype=_F32), jnp.asarray(np.sin(ang) * sign[None, :], dtype=_F32))


def _perm_heads(a, axis):
    shape = a.shape
    a = a.reshape(shape[:axis] + (N_HEADS, HEAD_DIM) + shape[axis + 1:])
    a = jnp.take(a, jnp.array(HEAD_PERM), axis=axis)
    return a.reshape(shape)


def kernel(x, c, ctx, c_ctx, w_ada, b_ada, g_pre_mix, g_post_mix, g_pre_ffn, g_post_ffn, w_in, b_in, attn_sink, sgu_ln_g, sgu_ln_b, sgu_w, sgu_b, g_attn_out, g_sgu_out, w_out, b_out, w_router, b_router, w_gate_up, b_gate_up, w_down, b_down):
    l = 0
    x2 = x.reshape(TOKENS, D_MODEL)
    ctx2 = ctx.reshape(BATCH * CTX_LEN, D_MODEL)

    cc = jnp.zeros((16, D_MODEL), _F32).at[:BATCH].set(c).at[BATCH].set(c_ctx)
    mod = _ada_call(cc, w_ada[l], b_ada[l].reshape(1, -1))
    mod3 = mod.reshape(16, N_MOD, D_MODEL)

    w_in_l = w_in[l]
    b_in_l = b_in[l]
    w_in_p = jnp.concatenate([_perm_heads(w_in_l[:, :ATTN_WIDTH], 1), w_in_l[:, ATTN_WIDTH:]], axis=1)
    b_in_p = jnp.concatenate([_perm_heads(b_in_l[:ATTN_WIDTH], 0), b_in_l[ATTN_WIDTH:]], axis=0)
    cos_t, sin_t = _rope_tables()
    g_pre = g_pre_mix[l].reshape(1, -1)

    q, k, v, su, sv = _inproj_call(x2, mod3, g_pre, w_in_p.astype(_BF16), b_in_p.reshape(1, -1), cos_t, sin_t)
    kc, vc = _ctxproj_call(ctx2, mod3, g_pre, w_in_l[:, KV_START:KV_END].astype(_BF16),
                           b_in_l[KV_START:KV_END].reshape(1, -1))

    sink_p = jnp.take(attn_sink[l], jnp.array(HEAD_PERM))
    ao = _attn_call(sink_p, q, k, v, kc, vc)

    ws2 = sgu_w[l].reshape(SGU_WIDTH // LANES, 2 * SGU_CHUNK, SGU_CHUNK).astype(_BF16)
    bsf = jnp.repeat(sgu_b[l].T, SGU_HEAD_DIM, axis=1)
    w_out_l = w_out[l]
    woa = _perm_heads(w_out_l[:ATTN_WIDTH], 0).astype(_BF16)
    wos = w_out_l[ATTN_WIDTH:].astype(_BF16)
    ga = _perm_heads(g_attn_out[l], 0).reshape(1, -1)
    wr_hi = w_router[l].astype(_BF16)
    wr_lo = (w_router[l] - wr_hi.astype(_F32)).astype(_BF16)
    wr = (jnp.zeros((D_MODEL, LANES), _BF16).at[:, :N_EXPERTS].set(wr_hi)
          .at[:, N_EXPERTS:2 * N_EXPERTS].set(wr_lo))
    br = jnp.full((1, LANES), NEG_BIG, _F32).at[0, :N_EXPERTS].set(b_router[l])
    ii = np.arange(POST_TILE // POST_SPLIT)
    tri = jnp.asarray(ii[None, :] < ii[:, None], dtype=_BF16)

    x_mid, h2, idx_o, gate_o, rank_o, cnt_o = _post_call(
        x2, ao, su, sv, mod3, sgu_ln_g[l].reshape(1, -1), sgu_ln_b[l].reshape(1, -1), ws2, bsf,
        ga, g_sgu_out[l].reshape(1, -1), woa, wos, b_out[l].reshape(1, -1),
        g_post_mix[l].reshape(1, -1), g_pre_ffn[l].reshape(1, -1), wr, br, tri)

    counts = cnt_o[0, :N_EXPERTS]
    padded = (counts + EXPERT_BLOCK - 1) // EXPERT_BLOCK * EXPERT_BLOCK
    pad_end = jnp.cumsum(padded)
    pad_start = pad_end - padded
    dest = (pad_start[idx_o] + rank_o).astype(jnp.int32).reshape(-1)
    block_start = jnp.arange(N_EXPERT_BLOCKS, dtype=jnp.int32) * EXPERT_BLOCK
    block_expert = jnp.minimum(jnp.sum(pad_end[None, :] <= block_start[:, None], axis=1),
                               N_EXPERTS - 1).astype(jnp.int32)
    n_used = (pad_end[-1:] // EXPERT_BLOCK).astype(jnp.int32)

    xs = _dispatch_call((pad_start + counts).astype(jnp.int32), (padded - counts).astype(jnp.int32),
                        n_used, dest, h2)

    used = padded > 0
    e_ids = jnp.arange(N_EXPERTS, dtype=jnp.int32)
    cand = jnp.where(used, e_ids, N_EXPERTS)
    later_min = lax.cummin(cand[::-1])[::-1]
    next_used = jnp.concatenate([later_min[1:], jnp.full((1,), N_EXPERTS, jnp.int32)])
    ordinal = jnp.cumsum(used.astype(jnp.int32)) - 1
    block_id = jnp.arange(N_EXPERT_BLOCKS, dtype=jnp.int32)
    sched_expert = block_expert[jnp.minimum(block_id, n_used[0] - 1)]
    block_slot = (ordinal[sched_expert] & 1).astype(jnp.int32)
    block_first = ((block_start == pad_start[sched_expert]) & (block_id < n_used[0])).astype(jnp.int32)
    block_next = next_used[sched_expert].astype(jnp.int32)
    y_rows = _expert_call(sched_expert, block_slot, block_first, block_next, n_used, xs,
                          w_gate_up[l], b_gate_up[l].reshape(N_EXPERTS, 1, -1),
                          w_down[l], b_down[l].reshape(N_EXPERTS, 1, -1))
    out = _combine_call(dest, gate_o, x_mid, mod3, g_post_ffn[l].reshape(1, -1), y_rows)
    return out.reshape(BATCH, SEQ, D_MODEL)
```

```python
import jax
import jax.numpy as jnp
import numpy as np
from jax import lax
from jax.experimental import pallas as pl
from jax.experimental.pallas import tpu as pltpu

D_MODEL = 1024
BATCH = 8
SEQ = 4096
TOKENS = BATCH * SEQ
GRID_W = 64
CTX_LEN = 256
N_HEADS = 8
N_KV_HEADS = 2
HEAD_DIM = 64
ATTN_WIDTH = N_HEADS * HEAD_DIM
KV_WIDTH = N_KV_HEADS * HEAD_DIM
WINDOW = 128
ATTN_BLOCK = 128
SGU_HEADS = 8
SGU_HEAD_DIM = 64
SGU_WIDTH = SGU_HEADS * SGU_HEAD_DIM
SGU_CHUNK = 128
KV_START = ATTN_WIDTH
KV_END = ATTN_WIDTH + 2 * KV_WIDTH
IN_WIDTH = KV_END + 2 * SGU_WIDTH
N_EXPERTS = 32
TOP_K = 4
D_FF_EXPERT = 1024
SWIGLU_LIMIT = 7.0
SWIGLU_ALPHA = 1.702
ROPE_THETA = 10000.0
EPS = 1e-6
N_MOD = 6

LANES = 128
NEG_BIG = -1e30
VMEM_LIMIT = 56 * 1024 * 1024

ROW_TILE = 512
MOVE_TILE = 512
ATTN_STEP_BLOCKS = 8
EXPERT_BLOCK = 512
POST_TILE = 1024
POST_SPLIT = 4
N_EXPERT_BLOCKS = TOKENS * TOP_K // EXPERT_BLOCK + N_EXPERTS
SORTED_ROWS = N_EXPERT_BLOCKS * EXPERT_BLOCK
HEAD_PERM = (0, 4, 1, 5, 2, 6, 3, 7)

_F32 = jnp.float32
_BF16 = jnp.bfloat16


def _rms(x, g):
    ms = jnp.mean(x * x, axis=-1, keepdims=True)
    return x * lax.rsqrt(ms + EPS) * g


ROW_SUB = D_MODEL // LANES


def _store_row_tiles(ref, x):
    n = x.shape[0]
    for cc in range(ROW_SUB):
        ref[pl.ds(cc, n, stride=ROW_SUB), :] = x[:, cc * LANES:(cc + 1) * LANES]


def _load_row_tiles(ref, n):
    return jnp.concatenate([ref[pl.ds(cc, n, stride=ROW_SUB), :] for cc in range(ROW_SUB)], axis=1)


def _gelu_tanh(x):
    c = 0.7978845608028654
    return x * (0.5 * (1.0 + jnp.tanh(c * (x + 0.044715 * (x * x * x)))))


def _ada_kernel(c_ref, w_ref, b_ref, o_ref):
    c = c_ref[...]
    a = c / (1.0 + jnp.exp(-c))
    o_ref[...] = jnp.dot(a, w_ref[...], preferred_element_type=_F32,
                         precision=lax.Precision.HIGHEST) + b_ref[...]


def _ada_call(cc, w_ada, b_ada):
    n = N_MOD * D_MODEL
    tn = 512
    return pl.pallas_call(
        _ada_kernel,
        out_shape=jax.ShapeDtypeStruct((16, n), _F32),
        grid=(n // tn,),
        in_specs=[pl.BlockSpec((16, D_MODEL), lambda j: (0, 0)),
                  pl.BlockSpec((D_MODEL, tn), lambda j: (0, j)),
                  pl.BlockSpec((1, tn), lambda j: (0, j))],
        out_specs=pl.BlockSpec((16, tn), lambda j: (0, j)),
        compiler_params=pltpu.CompilerParams(dimension_semantics=("arbitrary",),
                                             vmem_limit_bytes=VMEM_LIMIT),
        name="ada",
    )(cc, w_ada, b_ada)


def _rope(x, cos, sin_signed, first_half):
    nxt = pltpu.roll(x, LANES - 16, 1)
    prv = pltpu.roll(x, 16, 1)
    return x * cos + jnp.where(first_half, nxt, prv) * sin_signed


def _inproj_kernel(x_ref, mod_ref, g_ref, w_ref, b_ref, cos_ref, sin_ref,
                   q_ref, k_ref, v_ref, su_ref, sv_ref):
    m = mod_ref[0]
    h = _rms(x_ref[...], g_ref[...]) * (1.0 + m[1:2]) + m[0:1]
    hb = h.astype(_BF16)
    cos = cos_ref[...]
    sin = sin_ref[...]
    lane = lax.broadcasted_iota(jnp.int32, cos.shape, 1)
    first_half = (lane & 31) < 16

    q = jnp.dot(hb, w_ref[:, 0:ATTN_WIDTH], preferred_element_type=_F32) + b_ref[:, 0:ATTN_WIDTH]
    for j in range(ATTN_WIDTH // LANES):
        qj = _rope(q[:, j * LANES:(j + 1) * LANES], cos, sin, first_half)
        q_ref[:, j * LANES:(j + 1) * LANES] = (qj * (HEAD_DIM ** -0.5)).astype(_BF16)
    kv = jnp.dot(hb, w_ref[:, KV_START:KV_END], preferred_element_type=_F32) + b_ref[:, KV_START:KV_END]
    k_ref[...] = _rope(kv[:, 0:KV_WIDTH], cos, sin, first_half).astype(_BF16)
    v_ref[...] = kv[:, KV_WIDTH:].astype(_BF16)
    u0 = KV_END
    u1 = KV_END + SGU_WIDTH
    su_ref[...] = jnp.dot(hb, w_ref[:, u0:u1], preferred_element_type=_F32) + b_ref[:, u0:u1]
    sv_ref[...] = jnp.dot(hb, w_ref[:, u1:IN_WIDTH], preferred_element_type=_F32) + b_ref[:, u1:IN_WIDTH]


def _inproj_call(x2, mod3, g_pre, w_in_b, b_in2, cos_t, sin_t):
    tm = ROW_TILE
    tiles_per_batch = SEQ // tm
    row = lambda i: (i, 0)
    const = lambda i: (0, 0)
    return pl.pallas_call(
        _inproj_kernel,
        out_shape=(jax.ShapeDtypeStruct((TOKENS, ATTN_WIDTH), _BF16),
                   jax.ShapeDtypeStruct((TOKENS, KV_WIDTH), _BF16),
                   jax.ShapeDtypeStruct((TOKENS, KV_WIDTH), _BF16),
                   jax.ShapeDtypeStruct((TOKENS, SGU_WIDTH), _F32),
                   jax.ShapeDtypeStruct((TOKENS, SGU_WIDTH), _F32)),
        grid=(TOKENS // tm,),
        in_specs=[pl.BlockSpec((tm, D_MODEL), row),
                  pl.BlockSpec((1, N_MOD, D_MODEL), lambda i: (i // tiles_per_batch, 0, 0)),
                  pl.BlockSpec((1, D_MODEL), const),
                  pl.BlockSpec((D_MODEL, IN_WIDTH), const),
                  pl.BlockSpec((1, IN_WIDTH), const),
                  pl.BlockSpec((tm, LANES), lambda i: (i % tiles_per_batch, 0)),
                  pl.BlockSpec((tm, LANES), lambda i: (i % tiles_per_batch, 0))],
        out_specs=(pl.BlockSpec((tm, ATTN_WIDTH), row),
                   pl.BlockSpec((tm, KV_WIDTH), row),
                   pl.BlockSpec((tm, KV_WIDTH), row),
                   pl.BlockSpec((tm, SGU_WIDTH), row),
                   pl.BlockSpec((tm, SGU_WIDTH), row)),
        compiler_params=pltpu.CompilerParams(dimension_semantics=("arbitrary",),
                                             vmem_limit_bytes=VMEM_LIMIT),
        name="inproj",
    )(x2, mod3, g_pre, w_in_b, b_in2, cos_t, sin_t)


def _ctxproj_kernel(x_ref, mod_ref, g_ref, w_ref, b_ref, k_ref, v_ref):
    m = mod_ref[0]
    h = _rms(x_ref[...], g_ref[...]) * (1.0 + m[1:2]) + m[0:1]
    kv = jnp.dot(h.astype(_BF16), w_ref[...], preferred_element_type=_F32) + b_ref[...]
    k_ref[...] = kv[:, 0:KV_WIDTH].astype(_BF16)
    v_ref[...] = kv[:, KV_WIDTH:].astype(_BF16)


def _ctxproj_call(ctx2, mod3, g_pre, w_kv_b, b_kv2):
    tm = ROW_TILE
    rows = ctx2.shape[0]
    row = lambda i: (i, 0)
    const = lambda i: (0, 0)
    return pl.pallas_call(
        _ctxproj_kernel,
        out_shape=(jax.ShapeDtypeStruct((rows, KV_WIDTH), _BF16),
                   jax.ShapeDtypeStruct((rows, KV_WIDTH), _BF16)),
        grid=(rows // tm,),
        in_specs=[pl.BlockSpec((tm, D_MODEL), row),
                  pl.BlockSpec((1, N_MOD, D_MODEL), lambda i: (BATCH, 0, 0)),
                  pl.BlockSpec((1, D_MODEL), const),
                  pl.BlockSpec((D_MODEL, 2 * KV_WIDTH), const),
                  pl.BlockSpec((1, 2 * KV_WIDTH), const)],
        out_specs=(pl.BlockSpec((tm, KV_WIDTH), row),
                   pl.BlockSpec((tm, KV_WIDTH), row)),
        compiler_params=pltpu.CompilerParams(dimension_semantics=("arbitrary",),
                                             vmem_limit_bytes=VMEM_LIMIT),
        name="ctxproj",
    )(ctx2, mod3, g_pre, w_kv_b, b_kv2)


def _attn_kernel(sink_ref, q_ref, kp_ref, km_ref, kn_ref, vp_ref, vm_ref, vn_ref,
                 kc_ref, vc_ref, o_ref):
    n = pl.program_id(1)
    nstep = pl.num_programs(1)
    blk = ATTN_BLOCK
    nq = ATTN_STEP_BLOCKS
    rows2 = 2 * blk
    r = lax.broadcasted_iota(jnp.int32, (rows2, blk), 0) & (blk - 1)
    c = lax.broadcasted_iota(jnp.int32, (rows2, blk), 1)
    tri_prev = c >= r
    tri_next = c <= r
    row1 = lax.broadcasted_iota(jnp.int32, (rows2, 1), 0)
    lane_q = lax.broadcasted_iota(jnp.int32, (blk, LANES), 1)
    n_grp = ATTN_WIDTH // LANES
    k_blocks = [kp_ref[...]] + [km_ref[t * blk:(t + 1) * blk, :] for t in range(nq)] + [kn_ref[...]]
    v_blocks = [vp_ref[...]] + [vm_ref[t * blk:(t + 1) * blk, :] for t in range(nq)] + [vn_ref[...]]

    def scores(t):
        keys = jnp.concatenate(k_blocks[t:t + 3] + [kc_ref[...]], axis=0)
        q_rows = []
        for j in range(n_grp):
            qg = q_ref[t * blk:(t + 1) * blk, j * LANES:(j + 1) * LANES]
            zero = jnp.zeros_like(qg)
            q_rows += [jnp.where(lane_q < HEAD_DIM, qg, zero), jnp.where(lane_q >= HEAD_DIM, qg, zero)]
        return lax.dot_general(jnp.concatenate(q_rows, axis=0), keys, (((1,), (1,)), ((), ())),
                               preferred_element_type=_F32)

    def softmax(t, s_all):
        mask_a = (tri_prev & (n > 0)) if t == 0 else tri_prev
        mask_c = (tri_next & (n < nstep - 1)) if t == nq - 1 else tri_next
        p_rows, denoms = [], []
        for j in range(n_grp):
            s = s_all[j * rows2:(j + 1) * rows2]
            s = jnp.concatenate([jnp.where(mask_a, s[:, 0:blk], NEG_BIG), s[:, blk:2 * blk],
                                 jnp.where(mask_c, s[:, 2 * blk:3 * blk], NEG_BIG), s[:, 3 * blk:]],
                                axis=1)
            sk = jnp.where(row1 < blk, sink_ref[2 * j], sink_ref[2 * j + 1])
            m = jnp.maximum(jnp.max(s, axis=-1, keepdims=True), sk)
            p = jnp.exp(s - m)
            denoms.append(jnp.sum(p, axis=-1, keepdims=True) + jnp.exp(sk - m))
            p_rows.append(p.astype(_BF16))
        return jnp.concatenate(p_rows, axis=0), denoms

    def weighted_values(t, p_all, denoms):
        vals = jnp.concatenate(v_blocks[t:t + 3] + [vc_ref[...]], axis=0)
        o_all = jnp.dot(p_all, vals, preferred_element_type=_F32)
        for j in range(n_grp):
            o2 = o_all[j * rows2:(j + 1) * rows2] / denoms[j]
            og = jnp.where(lane_q < HEAD_DIM, o2[0:blk], o2[blk:])
            o_ref[t * blk:(t + 1) * blk, j * LANES:(j + 1) * LANES] = og.astype(_BF16)

    s_next = scores(0)
    probs = None
    for t in range(nq + 1):
        s_cur, s_next = s_next, (scores(t + 1) if t + 1 < nq else None)
        if probs is not None:
            weighted_values(t - 1, *probs)
        probs = softmax(t, s_cur) if t < nq else None


def _attn_call(sink_p, q, k, v, kc, vc):
    nblk = SEQ // ATTN_BLOCK
    nq = ATTN_STEP_BLOCKS
    nstep = nblk // nq
    own = lambda b, n: (b * nstep + n, 0)
    prev = lambda b, n: (b * nblk + jnp.maximum(nq * n - 1, 0), 0)
    nxt = lambda b, n: (b * nblk + jnp.minimum(nq * n + nq, nblk - 1), 0)
    ctx = lambda b, n: (b, 0)
    kv1 = (ATTN_BLOCK, KV_WIDTH)
    kvm = (nq * ATTN_BLOCK, KV_WIDTH)
    return pl.pallas_call(
        _attn_kernel,
        out_shape=jax.ShapeDtypeStruct((TOKENS, ATTN_WIDTH), _BF16),
        grid=(BATCH, nstep),
        in_specs=[pl.BlockSpec(memory_space=pltpu.SMEM),
                  pl.BlockSpec((nq * ATTN_BLOCK, ATTN_WIDTH), own),
                  pl.BlockSpec(kv1, prev), pl.BlockSpec(kvm, own), pl.BlockSpec(kv1, nxt),
                  pl.BlockSpec(kv1, prev), pl.BlockSpec(kvm, own), pl.BlockSpec(kv1, nxt),
                  pl.BlockSpec((CTX_LEN, KV_WIDTH), ctx),
                  pl.BlockSpec((CTX_LEN, KV_WIDTH), ctx)],
        out_specs=pl.BlockSpec((nq * ATTN_BLOCK, ATTN_WIDTH), own),
        compiler_params=pltpu.CompilerParams(dimension_semantics=("arbitrary", "arbitrary"),
                                             vmem_limit_bytes=VMEM_LIMIT),
        name="attn",
    )(sink_p, q, k, k, k, v, v, v, kc, vc)


def _post_kernel(x_ref, ao_ref, su_ref, sv_ref, mod_ref, lng_ref, lnb_ref, ws_ref, bs_ref,
                 ga_ref, gs_ref, woa_ref, wos_ref, bo_ref, gpost_ref, gpre_ref, wr_ref, br_ref,
                 tri_ref,
                 xmid_ref, h2_ref, idx_ref, gate_ref, rank_ref, cnt_ref,
                 mixed_ref, carry_ref):
    tm = x_ref.shape[0]
    m = mod_ref[0]

    @pl.when(pl.program_id(0) == 0)
    def _():
        carry_ref[...] = jnp.zeros_like(carry_ref)

    n_sub = POST_SPLIT
    ts = tm // n_sub
    tok_per_row = LANES // TOP_K
    lane = lax.broadcasted_iota(jnp.int32, (SGU_CHUNK, LANES), 1)
    st = [dict() for _ in range(n_sub)]
    carry = [carry_ref[...]]

    def rows(h):
        return slice(h * ts, (h + 1) * ts)

    def s1(h):
        gv = _gelu_tanh(sv_ref[rows(h), :])
        mu = jnp.mean(gv, axis=-1, keepdims=True)
        gc = gv - mu
        var = jnp.mean(gc * gc, axis=-1, keepdims=True)
        st[h]["vb"] = (gc * lax.rsqrt(var + EPS) * lng_ref[...] + lnb_ref[...]).astype(_BF16)

    def s2(h):
        vb = st[h].pop("vb")
        for c in range(ts // SGU_CHUNK):
            r0 = c * SGU_CHUNK
            for p in range(SGU_WIDTH // LANES):
                l0 = p * LANES
                r = jnp.dot(ws_ref[p], vb[r0:r0 + SGU_CHUNK, l0:l0 + LANES], preferred_element_type=_F32)
                mixed = jnp.where(lane < SGU_HEAD_DIM, r[0:SGU_CHUNK], r[SGU_CHUNK:])
                mixed_ref[h * ts + r0:h * ts + r0 + SGU_CHUNK, l0:l0 + LANES] = mixed + bs_ref[:, l0:l0 + LANES]
        sgu_o = _gelu_tanh(su_ref[rows(h), :]) * mixed_ref[rows(h), :]
        st[h]["oa"] = _rms(ao_ref[rows(h), :].astype(_F32), ga_ref[...]).astype(_BF16)
        st[h]["os"] = _rms(sgu_o, gs_ref[...]).astype(_BF16)

    def s3(h):
        st[h]["mix"] = (jnp.dot(st[h].pop("oa"), woa_ref[...], preferred_element_type=_F32)
                        + jnp.dot(st[h].pop("os"), wos_ref[...], preferred_element_type=_F32) + bo_ref[...])

    def s4(h):
        x_mid = x_ref[rows(h), :] + m[2:3] * _rms(st[h].pop("mix"), gpost_ref[...])
        xmid_ref[rows(h), :] = x_mid
        h2 = _rms(x_mid, gpre_ref[...]) * (1.0 + m[4:5]) + m[3:4]
        for cc in range(ROW_SUB):
            h2_ref[pl.ds(h * ts * ROW_SUB + cc, ts, stride=ROW_SUB), :] = h2[:, cc * LANES:(cc + 1) * LANES]
        h_hi = h2.astype(_BF16)
        st[h]["h_hi"] = h_hi
        st[h]["h_lo"] = (h2 - h_hi.astype(_F32)).astype(_BF16)

    def s5(h):
        r = (jnp.dot(st[h].pop("h_hi"), wr_ref[...], preferred_element_type=_F32)
             + jnp.dot(st[h].pop("h_lo"), wr_ref[...], preferred_element_type=_F32))
        st[h]["lg"] = r + pltpu.roll(r, LANES - N_EXPERTS, 1) + br_ref[...]

    def s6(h):
        lg = st[h].pop("lg")
        lane_r = lax.broadcasted_iota(jnp.int32, lg.shape, 1)
        lane_f = lane_r.astype(_F32)
        tops, hots = [], []
        for _k in range(TOP_K):
            mx = jnp.max(lg, axis=-1, keepdims=True)
            pick = jnp.min(jnp.where(lg == mx, lane_f, float(LANES)), axis=-1, keepdims=True)
            hot = lane_f == pick
            tops.append((mx, pick))
            hots.append(hot)
            lg = jnp.where(hot, 2.0 * NEG_BIG, lg)
        es = [jnp.exp(t[0] - tops[0][0]) for t in tops]
        esum = es[0] + es[1] + es[2] + es[3]
        multi = jnp.zeros(lg.shape, _F32)
        for hot in hots:
            multi = multi + jnp.where(hot, 1.0, 0.0)
        cum = jnp.dot(tri_ref[...], multi.astype(_BF16), preferred_element_type=_F32) + carry[0]
        row_r = lax.broadcasted_iota(jnp.int32, lg.shape, 0)
        lane_base = (row_r & (tok_per_row - 1)) * TOP_K
        gate_o = jnp.zeros(lg.shape, _F32)
        idx_e = jnp.zeros(lg.shape, _F32)
        rank_hi_e = jnp.zeros(lg.shape, _F32)
        rank_lo_e = jnp.zeros(lg.shape, _F32)
        for kk in range(TOP_K):
            rk = jnp.sum(jnp.where(hots[kk], cum, 0.0), axis=-1, keepdims=True)
            rk_hi = jnp.floor(rk * (1.0 / 256.0))
            here = lane_r == lane_base + kk
            gate_o = jnp.where(lane_r == kk, es[kk] / esum, gate_o)
            idx_e = jnp.where(here, tops[kk][1], idx_e)
            rank_hi_e = jnp.where(here, rk_hi, rank_hi_e)
            rank_lo_e = jnp.where(here, rk - 256.0 * rk_hi, rank_lo_e)
        fr = lax.broadcasted_iota(jnp.int32, (ts // tok_per_row, ts), 0)
        fc = lax.broadcasted_iota(jnp.int32, (ts // tok_per_row, ts), 1)
        fold = jnp.where(lax.shift_right_logical(fc, tok_per_row.bit_length() - 1) == fr, 1.0, 0.0).astype(_BF16)
        fs = slice(h * ts // tok_per_row, (h + 1) * ts // tok_per_row)
        idx_ref[fs, :] = jnp.dot(fold, idx_e.astype(_BF16), preferred_element_type=_F32).astype(jnp.int32)
        rank_ref[fs, :] = (256.0 * jnp.dot(fold, rank_hi_e.astype(_BF16), preferred_element_type=_F32)
                           + jnp.dot(fold, rank_lo_e.astype(_BF16), preferred_element_type=_F32)).astype(jnp.int32)
        gate_ref[rows(h), :] = gate_o
        carry[0] = carry[0] + jnp.sum(multi, axis=0, keepdims=True)

    order = sorted(((k + 2.5 * h, h, k) for h in range(n_sub) for k in range(6)))
    stages = (s1, s2, s3, s4, s5, s6)
    for _, h, k in order:
        stages[k](h)
    carry_ref[...] = carry[0]
    cnt_ref[...] = carry[0].astype(jnp.int32)


def _post_call(x2, ao, su, sv, mod3, lng, lnb, ws2, bsf, ga, gs, woa, wos, bo, gpost, gpre,
               wr, br, tri):
    tm = POST_TILE
    tiles_per_batch = SEQ // tm
    row = lambda i: (i, 0)
    const = lambda i: (0, 0)
    const3 = lambda i: (0, 0, 0)
    return pl.pallas_call(
        _post_kernel,
        out_shape=(jax.ShapeDtypeStruct((TOKENS, D_MODEL), _F32),
                   jax.ShapeDtypeStruct((TOKENS * ROW_SUB, LANES), _F32),
                   jax.ShapeDtypeStruct((TOKENS * TOP_K // LANES, LANES), jnp.int32),
                   jax.ShapeDtypeStruct((TOKENS, LANES), _F32),
                   jax.ShapeDtypeStruct((TOKENS * TOP_K // LANES, LANES), jnp.int32),
                   jax.ShapeDtypeStruct((1, LANES), jnp.int32)),
        grid=(TOKENS // tm,),
        in_specs=[pl.BlockSpec((tm, D_MODEL), row),
                  pl.BlockSpec((tm, ATTN_WIDTH), row),
                  pl.BlockSpec((tm, SGU_WIDTH), row),
                  pl.BlockSpec((tm, SGU_WIDTH), row),
                  pl.BlockSpec((1, N_MOD, D_MODEL), lambda i: (i // tiles_per_batch, 0, 0)),
                  pl.BlockSpec((1, SGU_WIDTH), const),
                  pl.BlockSpec((1, SGU_WIDTH), const),
                  pl.BlockSpec((SGU_WIDTH // LANES, 2 * SGU_CHUNK, SGU_CHUNK), const3),
                  pl.BlockSpec((SGU_CHUNK, SGU_WIDTH), const),
                  pl.BlockSpec((1, ATTN_WIDTH), const),
                  pl.BlockSpec((1, SGU_WIDTH), const),
                  pl.BlockSpec((ATTN_WIDTH, D_MODEL), const),
                  pl.BlockSpec((SGU_WIDTH, D_MODEL), const),
                  pl.BlockSpec((1, D_MODEL), const),
                  pl.BlockSpec((1, D_MODEL), const),
                  pl.BlockSpec((1, D_MODEL), const),
                  pl.BlockSpec((D_MODEL, LANES), const),
                  pl.BlockSpec((1, LANES), const),
                  pl.BlockSpec((tm // POST_SPLIT, tm // POST_SPLIT), const)],
        out_specs=(pl.BlockSpec((tm, D_MODEL), row),
                   pl.BlockSpec((tm * ROW_SUB, LANES), row),
                   pl.BlockSpec((tm * TOP_K // LANES, LANES), row),
                   pl.BlockSpec((tm, LANES), row),
                   pl.BlockSpec((tm * TOP_K // LANES, LANES), row),
                   pl.BlockSpec((1, LANES), const)),
        scratch_shapes=[pltpu.VMEM((tm, SGU_WIDTH), _F32),
                        pltpu.VMEM((1, LANES), _F32)],
        compiler_params=pltpu.CompilerParams(dimension_semantics=("arbitrary",),
                                             vmem_limit_bytes=VMEM_LIMIT),
        name="post",
    )(x2, ao, su, sv, mod3, lng, lnb, ws2, bsf, ga, gs, woa, wos, bo, gpost, gpre, wr, br, tri)


def _dispatch_kernel(fs_ref, fl_ref, nu_ref, dest_ref, h2_ref, xs_ref, zero_ref, sem, zsem):
    tm = h2_ref.shape[0] // ROW_SUB
    n_token_steps = TOKENS // tm
    i = pl.program_id(0)

    @pl.when(i < n_token_steps)
    def _():
        def issue(r, carry):
            for kk in range(TOP_K):
                d = pl.multiple_of(dest_ref[r * TOP_K + kk] * ROW_SUB, ROW_SUB)
                pltpu.make_async_copy(h2_ref.at[pl.ds(pl.multiple_of(r * ROW_SUB, ROW_SUB), ROW_SUB)],
                                      xs_ref.at[pl.ds(d, ROW_SUB)], sem).start(priority=kk % 2)
            return carry

        lax.fori_loop(0, tm, issue, 0, unroll=8)
        for kk in range(TOP_K):
            pltpu.make_async_copy(h2_ref, xs_ref.at[pl.ds(0, tm * ROW_SUB)], sem).wait()

    @pl.when(i == n_token_steps)
    def _():
        zero_ref[...] = jnp.zeros_like(zero_ref)
        block_rows = EXPERT_BLOCK * ROW_SUB

        def pad_run(e, wait):
            pos = fs_ref[e]
            length = fl_ref[e]
            for bit in reversed(range(EXPERT_BLOCK.bit_length() - 1)):
                size = 1 << bit
                take = length & size

                @pl.when(take != 0)
                def _():
                    cp = pltpu.make_async_copy(
                        zero_ref.at[pl.ds(0, size * ROW_SUB)],
                        xs_ref.at[pl.ds(pl.multiple_of(pos * ROW_SUB, ROW_SUB), size * ROW_SUB)], zsem)
                    if wait:
                        cp.wait()
                    else:
                        cp.start()

                pos = pos + take

        def tail_block(blk, wait):
            cp = pltpu.make_async_copy(
                zero_ref, xs_ref.at[pl.ds(pl.multiple_of(blk * block_rows, block_rows), block_rows)], zsem)
            if wait:
                cp.wait()
            else:
                cp.start()

        for wait in (False, True):
            lax.fori_loop(0, N_EXPERTS, lambda e, c, w=wait: (pad_run(e, w), c)[1], 0)
            lax.fori_loop(nu_ref[0], N_EXPERT_BLOCKS, lambda blk, c, w=wait: (tail_block(blk, w), c)[1], 0)


def _dispatch_call(fill_start, fill_len, n_used, dest_flat, h2):
    tm = MOVE_TILE
    n_token_steps = TOKENS // tm
    grid_spec = pltpu.PrefetchScalarGridSpec(
        num_scalar_prefetch=3,
        grid=(n_token_steps + 1,),
        in_specs=[pl.BlockSpec((tm * TOP_K,), lambda i, fs, fl, nu: (jnp.minimum(i, n_token_steps - 1),),
                               memory_space=pltpu.SMEM),
                  pl.BlockSpec((tm * ROW_SUB, LANES),
                               lambda i, fs, fl, nu: (jnp.minimum(i, n_token_steps - 1), 0))],
        out_specs=pl.BlockSpec(memory_space=pl.ANY),
        scratch_shapes=[pltpu.VMEM((EXPERT_BLOCK * ROW_SUB, LANES), _F32),
                        pltpu.SemaphoreType.DMA,
                        pltpu.SemaphoreType.DMA])
    return pl.pallas_call(
        _dispatch_kernel,
        out_shape=jax.ShapeDtypeStruct((SORTED_ROWS * ROW_SUB, LANES), _F32),
        grid_spec=grid_spec,
        compiler_params=pltpu.CompilerParams(dimension_semantics=("arbitrary",),
                                             vmem_limit_bytes=VMEM_LIMIT),
        name="dispatch",
    )(fill_start, fill_len, n_used, dest_flat, h2)


def _expert_kernel(be_ref, slot_ref, first_ref, nxt_ref, nu_ref,
                   xs_ref, wgu_hbm, bgu0_ref, bgu1_ref, wd_hbm, bd0_ref, bd1_ref, y_ref,
                   wgu_stage, wd_stage, wgu_bf, wd_bf, sem):
    p = pl.program_id(0)
    half = EXPERT_BLOCK * ROW_SUB
    n_pairs_used = (nu_ref[0] + 1) // 2

    def stage_copies(e):
        return (pltpu.make_async_copy(wgu_hbm.at[e], wgu_stage, sem.at[0]),
                pltpu.make_async_copy(wd_hbm.at[e], wd_stage, sem.at[1]))

    @pl.when(p == 0)
    def _():
        for cp in stage_copies(be_ref[0]):
            cp.start()

    for b in (2 * p, 2 * p + 1):
        @pl.when((first_ref[b] == 1) & (b < nu_ref[0]))
        def _():
            for cp in stage_copies(be_ref[b]):
                cp.wait()
            wgu_bf[slot_ref[b]] = wgu_stage[...].astype(_BF16)
            wd_bf[slot_ref[b]] = wd_stage[...].astype(_BF16)

            @pl.when(nxt_ref[b] < N_EXPERTS)
            def _():
                for cp in stage_copies(nxt_ref[b]):
                    cp.start()

    def ffn(x, slot, bgu_ref, bd_ref):
        gu = jnp.dot(x, wgu_bf[slot], preferred_element_type=_F32) + bgu_ref[0]
        gate = jnp.minimum(gu[:, 0:D_FF_EXPERT], SWIGLU_LIMIT)
        up = jnp.clip(gu[:, D_FF_EXPERT:], -SWIGLU_LIMIT, SWIGLU_LIMIT)
        act = (up + 1.0) * gate * (1.0 / (1.0 + jnp.exp(-SWIGLU_ALPHA * gate)))
        return jnp.dot(act.astype(_BF16), wd_bf[slot], preferred_element_type=_F32) + bd_ref[0]

    def load_rows(which):
        return jnp.concatenate([xs_ref[pl.ds(which * half + cc, EXPERT_BLOCK, stride=ROW_SUB), :]
                                for cc in range(ROW_SUB)], axis=1).astype(_BF16)

    def store_rows(which, y):
        for cc in range(ROW_SUB):
            y_ref[pl.ds(which * half + cc, EXPERT_BLOCK, stride=ROW_SUB), :] = y[:, cc * LANES:(cc + 1) * LANES]

    @pl.when(p < n_pairs_used)
    def _():
        y0 = ffn(load_rows(0), slot_ref[2 * p], bgu0_ref, bd0_ref)
        x1 = load_rows(1)
        store_rows(0, y0)
        store_rows(1, ffn(x1, slot_ref[2 * p + 1], bgu1_ref, bd1_ref))

    @pl.when(p >= n_pairs_used)
    def _():
        y_ref[...] = jnp.zeros_like(y_ref)


def _expert_call(block_expert, block_slot, block_first, block_next, n_used, xs, wgu, bgu3, wd, bd3):
    tb = 2 * EXPERT_BLOCK
    live = lambda p, be, sl, fi, nx, nu: (jnp.minimum(p, (nu[0] - 1) // 2), 0)
    bsel0 = lambda p, be, sl, fi, nx, nu: (be[2 * p], 0, 0)
    bsel1 = lambda p, be, sl, fi, nx, nu: (be[2 * p + 1], 0, 0)
    grid_spec = pltpu.PrefetchScalarGridSpec(
        num_scalar_prefetch=5,
        grid=(N_EXPERT_BLOCKS // 2,),
        in_specs=[pl.BlockSpec((tb * ROW_SUB, LANES), live),
                  pl.BlockSpec(memory_space=pl.ANY),
                  pl.BlockSpec((1, 1, 2 * D_FF_EXPERT), bsel0),
                  pl.BlockSpec((1, 1, 2 * D_FF_EXPERT), bsel1),
                  pl.BlockSpec(memory_space=pl.ANY),
                  pl.BlockSpec((1, 1, D_MODEL), bsel0),
                  pl.BlockSpec((1, 1, D_MODEL), bsel1)],
        out_specs=pl.BlockSpec((tb * ROW_SUB, LANES), lambda p, be, sl, fi, nx, nu: (p, 0)),
        scratch_shapes=[pltpu.VMEM((D_MODEL, 2 * D_FF_EXPERT), _F32),
                        pltpu.VMEM((D_FF_EXPERT, D_MODEL), _F32),
                        pltpu.VMEM((2, D_MODEL, 2 * D_FF_EXPERT), _BF16),
                        pltpu.VMEM((2, D_FF_EXPERT, D_MODEL), _BF16),
                        pltpu.SemaphoreType.DMA((2,))])
    return pl.pallas_call(
        _expert_kernel,
        out_shape=jax.ShapeDtypeStruct((SORTED_ROWS * ROW_SUB, LANES), _F32),
        grid_spec=grid_spec,
        compiler_params=pltpu.CompilerParams(dimension_semantics=("arbitrary",),
                                             vmem_limit_bytes=VMEM_LIMIT),
        name="expert",
    )(block_expert, block_slot, block_first, block_next, n_used, xs, wgu, bgu3, bgu3, wd, bd3, bd3)


def _combine_kernel(dest_ref, dest_next_ref, gate_ref, xmid_ref, mod_ref, g_ref, y_ref, o_ref,
                    buf_ref, sem):
    tm = xmid_ref.shape[0]
    m = mod_ref[0]
    i = pl.program_id(0)
    n = pl.num_programs(0)

    def gather_rows(idx_ref, slot):
        def issue(r, carry):
            for kk in range(TOP_K):
                d = pl.multiple_of(idx_ref[r * TOP_K + kk] * ROW_SUB, ROW_SUB)
                pltpu.make_async_copy(
                    y_ref.at[pl.ds(d, ROW_SUB)],
                    buf_ref.at[slot, kk, pl.ds(pl.multiple_of(r * ROW_SUB, ROW_SUB), ROW_SUB)],
                    sem.at[slot]).start(priority=kk % 2)
            return carry

        lax.fori_loop(0, tm, issue, 0, unroll=8)

    @pl.when(i == 0)
    def _():
        gather_rows(dest_ref, 0)

    @pl.when(i + 1 < n)
    def _():
        gather_rows(dest_next_ref, (i + 1) % 2)

    slot = i % 2
    for kk in range(TOP_K):
        pltpu.make_async_copy(y_ref.at[pl.ds(0, tm * ROW_SUB)], buf_ref.at[slot, kk],
                              sem.at[slot]).wait()
    g = gate_ref[...]
    pieces = []
    for cc in range(ROW_SUB):
        piece = buf_ref[slot, 0, pl.ds(cc, tm, stride=ROW_SUB), :] * g[:, 0:1]
        for kk in range(1, TOP_K):
            piece = piece + buf_ref[slot, kk, pl.ds(cc, tm, stride=ROW_SUB), :] * g[:, kk:kk + 1]
        pieces.append(piece)
    ffn = jnp.concatenate(pieces, axis=1)
    o_ref[...] = xmid_ref[...] + m[5:6] * _rms(ffn, g_ref[...])


def _combine_call(dest_flat, gates, x_mid, mod3, gpost_ffn, y_rows):
    tm = MOVE_TILE
    tiles_per_batch = SEQ // tm
    n_steps = TOKENS // tm
    return pl.pallas_call(
        _combine_kernel,
        out_shape=jax.ShapeDtypeStruct((TOKENS, D_MODEL), _F32),
        grid=(n_steps,),
        in_specs=[pl.BlockSpec((tm * TOP_K,), lambda i: (i,), memory_space=pltpu.SMEM),
                  pl.BlockSpec((tm * TOP_K,), lambda i: (jnp.minimum(i + 1, n_steps - 1),),
                               memory_space=pltpu.SMEM),
                  pl.BlockSpec((tm, LANES), lambda i: (i, 0)),
                  pl.BlockSpec((tm, D_MODEL), lambda i: (i, 0)),
                  pl.BlockSpec((1, N_MOD, D_MODEL), lambda i: (i // tiles_per_batch, 0, 0)),
                  pl.BlockSpec((1, D_MODEL), lambda i: (0, 0)),
                  pl.BlockSpec(memory_space=pl.ANY)],
        out_specs=pl.BlockSpec((tm, D_MODEL), lambda i: (i, 0)),
        scratch_shapes=[pltpu.VMEM((2, TOP_K, tm * ROW_SUB, LANES), _F32),
                        pltpu.SemaphoreType.DMA((2,))],
        compiler_params=pltpu.CompilerParams(dimension_semantics=("arbitrary",),
                                             vmem_limit_bytes=VMEM_LIMIT),
        name="combine",
    )(dest_flat, dest_flat, gates, x_mid, mod3, gpost_ffn, y_rows)


def _rope_tables():
    pos = np.arange(SEQ)
    pos_row = (pos // GRID_W).astype(np.float64)
    pos_col = (pos % GRID_W).astype(np.float64)
    n_freq = HEAD_DIM // 4
    inv_freq = ROPE_THETA ** (-np.arange(n_freq, dtype=np.float64) / n_freq)
    d = np.arange(LANES) % HEAD_DIM
    f = inv_freq[d % n_freq]
    ang = np.where((d < HEAD_DIM // 2)[None, :], pos_row[:, None] * f[None, :], pos_col[:, None] * f[None, :])
    sign = np.where((d % (HEAD_DIM // 2)) < n_freq, -1.0, 1.0)
    return (jnp.asarray(np.cos(ang), dtype=_F32), jnp.asarray(np.sin(ang) * sign[None, :], dtype=_F32))


def _perm_heads(a, axis):
    shape = a.shape
    a = a.reshape(shape[:axis] + (N_HEADS, HEAD_DIM) + shape[axis + 1:])
    a = jnp.take(a, jnp.array(HEAD_PERM), axis=axis)
    return a.reshape(shape)


def kernel(x, c, ctx, c_ctx, w_ada, b_ada, g_pre_mix, g_post_mix, g_pre_ffn, g_post_ffn, w_in, b_in, attn_sink, sgu_ln_g, sgu_ln_b, sgu_w, sgu_b, g_attn_out, g_sgu_out, w_out, b_out, w_router, b_router, w_gate_up, b_gate_up, w_down, b_down):
    l = 0
    x2 = x.reshape(TOKENS, D_MODEL)
    ctx2 = ctx.reshape(BATCH * CTX_LEN, D_MODEL)

    cc = jnp.zeros((16, D_MODEL), _F32).at[:BATCH].set(c).at[BATCH].set(c_ctx)
    mod = _ada_call(cc, w_ada[l], b_ada[l].reshape(1, -1))
    mod3 = mod.reshape(16, N_MOD, D_MODEL)

    w_in_l = w_in[l]
    b_in_l = b_in[l]
    w_in_p = jnp.concatenate([_perm_heads(w_in_l[:, :ATTN_WIDTH], 1), w_in_l[:, ATTN_WIDTH:]], axis=1)
    b_in_p = jnp.concatenate([_perm_heads(b_in_l[:ATTN_WIDTH], 0), b_in_l[ATTN_WIDTH:]], axis=0)
    cos_t, sin_t = _rope_tables()
    g_pre = g_pre_mix[l].reshape(1, -1)

    q, k, v, su, sv = _inproj_call(x2, mod3, g_pre, w_in_p.astype(_BF16), b_in_p.reshape(1, -1), cos_t, sin_t)
    kc, vc = _ctxproj_call(ctx2, mod3, g_pre, w_in_l[:, KV_START:KV_END].astype(_BF16),
                           b_in_l[KV_START:KV_END].reshape(1, -1))

    sink_p = jnp.take(attn_sink[l], jnp.array(HEAD_PERM))
    ao = _attn_call(sink_p, q, k, v, kc, vc)

    ws2 = sgu_w[l].reshape(SGU_WIDTH // LANES, 2 * SGU_CHUNK, SGU_CHUNK).astype(_BF16)
    bsf = jnp.repeat(sgu_b[l].T, SGU_HEAD_DIM, axis=1)
    w_out_l = w_out[l]
    woa = _perm_heads(w_out_l[:ATTN_WIDTH], 0).astype(_BF16)
    wos = w_out_l[ATTN_WIDTH:].astype(_BF16)
    ga = _perm_heads(g_attn_out[l], 0).reshape(1, -1)
    wr_hi = w_router[l].astype(_BF16)
    wr_lo = (w_router[l] - wr_hi.astype(_F32)).astype(_BF16)
    wr = (jnp.zeros((D_MODEL, LANES), _BF16).at[:, :N_EXPERTS].set(wr_hi)
          .at[:, N_EXPERTS:2 * N_EXPERTS].set(wr_lo))
    br = jnp.full((1, LANES), NEG_BIG, _F32).at[0, :N_EXPERTS].set(b_router[l])
    ii = np.arange(POST_TILE // POST_SPLIT)
    tri = jnp.asarray(ii[None, :] < ii[:, None], dtype=_BF16)

    x_mid, h2, idx_o, gate_o, rank_o, cnt_o = _post_call(
        x2, ao, su, sv, mod3, sgu_ln_g[l].reshape(1, -1), sgu_ln_b[l].reshape(1, -1), ws2, bsf,
        ga, g_sgu_out[l].reshape(1, -1), woa, wos, b_out[l].reshape(1, -1),
        g_post_mix[l].reshape(1, -1), g_pre_ffn[l].reshape(1, -1), wr, br, tri)

    counts = cnt_o[0, :N_EXPERTS]
    padded = (counts + EXPERT_BLOCK - 1) // EXPERT_BLOCK * EXPERT_BLOCK
    pad_end = jnp.cumsum(padded)
    pad_start = pad_end - padded
    dest = rank_o
    for e in range(N_EXPERTS):
        dest = dest + jnp.where(idx_o == e, pad_start[e], 0)
    dest = dest.astype(jnp.int32).reshape(-1)
    n_used = (pad_end[-1:] // EXPERT_BLOCK).astype(jnp.int32)

    xs = _dispatch_call((pad_start + counts).astype(jnp.int32), (padded - counts).astype(jnp.int32),
                        n_used, dest, h2)

    e_ids = jnp.arange(N_EXPERTS, dtype=jnp.int32)
    used = padded > 0
    cand = jnp.where(used, e_ids, N_EXPERTS)
    next_used = jnp.min(jnp.where(e_ids[None, :] > e_ids[:, None], cand[None, :], N_EXPERTS), axis=1)
    ordinal = jnp.cumsum(used.astype(jnp.int32)) - 1
    block_id = jnp.arange(N_EXPERT_BLOCKS, dtype=jnp.int32)
    block_start = block_id * EXPERT_BLOCK
    sched_start = jnp.minimum(block_start, (n_used[0] - 1) * EXPERT_BLOCK)
    sched_expert = jnp.minimum(jnp.sum(pad_end[None, :] <= sched_start[:, None], axis=1),
                               N_EXPERTS - 1).astype(jnp.int32)
    onehot = e_ids[None, :] == sched_expert[:, None]
    pick = lambda table: jnp.sum(jnp.where(onehot, table[None, :], 0), axis=1)
    block_slot = (pick(ordinal) & 1).astype(jnp.int32)
    block_first = ((block_start == pick(pad_start)) & (block_id < n_used[0])).astype(jnp.int32)
    block_next = pick(next_used).astype(jnp.int32)
    y_rows = _expert_call(sched_expert, block_slot, block_first, block_next, n_used, xs,
                          w_gate_up[l], b_gate_up[l].reshape(N_EXPERTS, 1, -1),
                          w_down[l], b_down[l].reshape(N_EXPERTS, 1, -1))
    out = _combine_call(dest, gate_o, x_mid, mod3, g_post_ffn[l].reshape(1, -1), y_rows)
    return out.reshape(BATCH, SEQ, D_MODEL)
```

```python
import jax
import jax.numpy as jnp
import numpy as np
from jax import lax
from jax.experimental import pallas as pl
from jax.experimental.pallas import tpu as pltpu

D_MODEL = 1024
BATCH = 8
SEQ = 4096
TOKENS = BATCH * SEQ
GRID_W = 64
CTX_LEN = 256
N_HEADS = 8
N_KV_HEADS = 2
HEAD_DIM = 64
ATTN_WIDTH = N_HEADS * HEAD_DIM
KV_WIDTH = N_KV_HEADS * HEAD_DIM
WINDOW = 128
ATTN_BLOCK = 128
SGU_HEADS = 8
SGU_HEAD_DIM = 64
SGU_WIDTH = SGU_HEADS * SGU_HEAD_DIM
SGU_CHUNK = 128
KV_START = ATTN_WIDTH
KV_END = ATTN_WIDTH + 2 * KV_WIDTH
IN_WIDTH = KV_END + 2 * SGU_WIDTH
N_EXPERTS = 32
TOP_K = 4
D_FF_EXPERT = 1024
SWIGLU_LIMIT = 7.0
SWIGLU_ALPHA = 1.702
ROPE_THETA = 10000.0
EPS = 1e-6
N_MOD = 6

LANES = 128
NEG_BIG = -1e30
VMEM_LIMIT = 56 * 1024 * 1024

ROW_TILE = 512
MOVE_TILE = 512
ATTN_STEP_BLOCKS = 8
EXPERT_BLOCK = 512
POST_TILE = 1024
POST_SPLIT = 4
N_EXPERT_BLOCKS = TOKENS * TOP_K // EXPERT_BLOCK + N_EXPERTS
SORTED_ROWS = N_EXPERT_BLOCKS * EXPERT_BLOCK
HEAD_PERM = (0, 4, 1, 5, 2, 6, 3, 7)

LOG2_E = 1.4426950408889634
QK_SCALE = LOG2_E * HEAD_DIM ** -0.5

_F32 = jnp.float32
_BF16 = jnp.bfloat16


def _rms(x, g):
    ms = jnp.mean(x * x, axis=-1, keepdims=True)
    return x * lax.rsqrt(ms + EPS) * g


ROW_SUB = D_MODEL // LANES


def _store_row_tiles(ref, x):
    n = x.shape[0]
    for cc in range(ROW_SUB):
        ref[pl.ds(cc, n, stride=ROW_SUB), :] = x[:, cc * LANES:(cc + 1) * LANES]


def _load_row_tiles(ref, n):
    return jnp.concatenate([ref[pl.ds(cc, n, stride=ROW_SUB), :] for cc in range(ROW_SUB)], axis=1)


def _gelu_tanh(x):
    c = 0.7978845608028654
    return x * (0.5 * (1.0 + jnp.tanh(c * (x + 0.044715 * (x * x * x)))))


def _ada_kernel(c_ref, w_ref, b_ref, o_ref):
    c = c_ref[...]
    a = c / (1.0 + jnp.exp(-c))
    o_ref[...] = jnp.dot(a, w_ref[...], preferred_element_type=_F32,
                         precision=lax.Precision.HIGHEST) + b_ref[...]


def _ada_call(cc, w_ada, b_ada):
    n = N_MOD * D_MODEL
    tn = 512
    return pl.pallas_call(
        _ada_kernel,
        out_shape=jax.ShapeDtypeStruct((16, n), _F32),
        grid=(n // tn,),
        in_specs=[pl.BlockSpec((16, D_MODEL), lambda j: (0, 0)),
                  pl.BlockSpec((D_MODEL, tn), lambda j: (0, j)),
                  pl.BlockSpec((1, tn), lambda j: (0, j))],
        out_specs=pl.BlockSpec((16, tn), lambda j: (0, j)),
        compiler_params=pltpu.CompilerParams(dimension_semantics=("arbitrary",),
                                             vmem_limit_bytes=VMEM_LIMIT),
        name="ada",
    )(cc, w_ada, b_ada)


def _rope(x, cos, sin_signed, first_half):
    nxt = pltpu.roll(x, LANES - 16, 1)
    prv = pltpu.roll(x, 16, 1)
    return x * cos + jnp.where(first_half, nxt, prv) * sin_signed


def _inproj_kernel(x_ref, mod_ref, g_ref, w_ref, b_ref, cos_ref, sin_ref,
                   q_ref, k_ref, v_ref, su_ref, sv_ref):
    m = mod_ref[0]
    h = _rms(x_ref[...], g_ref[...]) * (1.0 + m[1:2]) + m[0:1]
    hb = h.astype(_BF16)
    cos = cos_ref[...]
    sin = sin_ref[...]
    lane = lax.broadcasted_iota(jnp.int32, cos.shape, 1)
    first_half = (lane & 31) < 16

    q = jnp.dot(hb, w_ref[:, 0:ATTN_WIDTH], preferred_element_type=_F32) + b_ref[:, 0:ATTN_WIDTH]
    for j in range(ATTN_WIDTH // LANES):
        qj = _rope(q[:, j * LANES:(j + 1) * LANES], cos, sin, first_half)
        q_ref[:, j * LANES:(j + 1) * LANES] = (qj * QK_SCALE).astype(_BF16)
    kv = jnp.dot(hb, w_ref[:, KV_START:KV_END], preferred_element_type=_F32) + b_ref[:, KV_START:KV_END]
    k_ref[...] = _rope(kv[:, 0:KV_WIDTH], cos, sin, first_half).astype(_BF16)
    v_ref[...] = kv[:, KV_WIDTH:].astype(_BF16)
    u0 = KV_END
    u1 = KV_END + SGU_WIDTH
    su_ref[...] = jnp.dot(hb, w_ref[:, u0:u1], preferred_element_type=_F32) + b_ref[:, u0:u1]
    sv_ref[...] = jnp.dot(hb, w_ref[:, u1:IN_WIDTH], preferred_element_type=_F32) + b_ref[:, u1:IN_WIDTH]


def _inproj_call(x2, mod3, g_pre, w_in_b, b_in2, cos_t, sin_t):
    tm = ROW_TILE
    tiles_per_batch = SEQ // tm
    row = lambda i: (i, 0)
    const = lambda i: (0, 0)
    return pl.pallas_call(
        _inproj_kernel,
        out_shape=(jax.ShapeDtypeStruct((TOKENS, ATTN_WIDTH), _BF16),
                   jax.ShapeDtypeStruct((TOKENS, KV_WIDTH), _BF16),
                   jax.ShapeDtypeStruct((TOKENS, KV_WIDTH), _BF16),
                   jax.ShapeDtypeStruct((TOKENS, SGU_WIDTH), _F32),
                   jax.ShapeDtypeStruct((TOKENS, SGU_WIDTH), _F32)),
        grid=(TOKENS // tm,),
        in_specs=[pl.BlockSpec((tm, D_MODEL), row),
                  pl.BlockSpec((1, N_MOD, D_MODEL), lambda i: (i // tiles_per_batch, 0, 0)),
                  pl.BlockSpec((1, D_MODEL), const),
                  pl.BlockSpec((D_MODEL, IN_WIDTH), const),
                  pl.BlockSpec((1, IN_WIDTH), const),
                  pl.BlockSpec((tm, LANES), lambda i: (i % tiles_per_batch, 0)),
                  pl.BlockSpec((tm, LANES), lambda i: (i % tiles_per_batch, 0))],
        out_specs=(pl.BlockSpec((tm, ATTN_WIDTH), row),
                   pl.BlockSpec((tm, KV_WIDTH), row),
                   pl.BlockSpec((tm, KV_WIDTH), row),
                   pl.BlockSpec((tm, SGU_WIDTH), row),
                   pl.BlockSpec((tm, SGU_WIDTH), row)),
        compiler_params=pltpu.CompilerParams(dimension_semantics=("arbitrary",),
                                             vmem_limit_bytes=VMEM_LIMIT),
        name="inproj",
    )(x2, mod3, g_pre, w_in_b, b_in2, cos_t, sin_t)


def _ctxproj_kernel(x_ref, mod_ref, g_ref, w_ref, b_ref, k_ref, v_ref):
    m = mod_ref[0]
    h = _rms(x_ref[...], g_ref[...]) * (1.0 + m[1:2]) + m[0:1]
    kv = jnp.dot(h.astype(_BF16), w_ref[...], preferred_element_type=_F32) + b_ref[...]
    k_ref[...] = kv[:, 0:KV_WIDTH].astype(_BF16)
    v_ref[...] = kv[:, KV_WIDTH:].astype(_BF16)


def _ctxproj_call(ctx2, mod3, g_pre, w_kv_b, b_kv2):
    tm = ROW_TILE
    rows = ctx2.shape[0]
    row = lambda i: (i, 0)
    const = lambda i: (0, 0)
    return pl.pallas_call(
        _ctxproj_kernel,
        out_shape=(jax.ShapeDtypeStruct((rows, KV_WIDTH), _BF16),
                   jax.ShapeDtypeStruct((rows, KV_WIDTH), _BF16)),
        grid=(rows // tm,),
        in_specs=[pl.BlockSpec((tm, D_MODEL), row),
                  pl.BlockSpec((1, N_MOD, D_MODEL), lambda i: (BATCH, 0, 0)),
                  pl.BlockSpec((1, D_MODEL), const),
                  pl.BlockSpec((D_MODEL, 2 * KV_WIDTH), const),
                  pl.BlockSpec((1, 2 * KV_WIDTH), const)],
        out_specs=(pl.BlockSpec((tm, KV_WIDTH), row),
                   pl.BlockSpec((tm, KV_WIDTH), row)),
        compiler_params=pltpu.CompilerParams(dimension_semantics=("arbitrary",),
                                             vmem_limit_bytes=VMEM_LIMIT),
        name="ctxproj",
    )(ctx2, mod3, g_pre, w_kv_b, b_kv2)


def _attn_kernel(sink_ref, q_ref, kp_ref, km_ref, kn_ref, vp_ref, vm_ref, vn_ref,
                 kc_ref, vc_ref, o_ref):
    n = pl.program_id(1)
    nstep = pl.num_programs(1)
    blk = ATTN_BLOCK
    nq = ATTN_STEP_BLOCKS
    rows2 = 2 * blk
    r = lax.broadcasted_iota(jnp.int32, (rows2, blk), 0) & (blk - 1)
    c = lax.broadcasted_iota(jnp.int32, (rows2, blk), 1)
    tri_prev = c >= r
    tri_next = c <= r
    row1 = lax.broadcasted_iota(jnp.int32, (rows2, 1), 0)
    lane_q = lax.broadcasted_iota(jnp.int32, (blk, LANES), 1)
    n_grp = ATTN_WIDTH // LANES
    k_blocks = [kp_ref[...]] + [km_ref[t * blk:(t + 1) * blk, :] for t in range(nq)] + [kn_ref[...]]
    v_blocks = [vp_ref[...]] + [vm_ref[t * blk:(t + 1) * blk, :] for t in range(nq)] + [vn_ref[...]]

    def scores(t):
        keys = jnp.concatenate(k_blocks[t:t + 3] + [kc_ref[...]], axis=0)
        q_rows = []
        for j in range(n_grp):
            qg = q_ref[t * blk:(t + 1) * blk, j * LANES:(j + 1) * LANES]
            zero = jnp.zeros_like(qg)
            q_rows += [jnp.where(lane_q < HEAD_DIM, qg, zero), jnp.where(lane_q >= HEAD_DIM, qg, zero)]
        return lax.dot_general(jnp.concatenate(q_rows, axis=0), keys, (((1,), (1,)), ((), ())),
                               preferred_element_type=_F32)

    def softmax(t, s_all):
        mask_a = (tri_prev & (n > 0)) if t == 0 else tri_prev
        mask_c = (tri_next & (n < nstep - 1)) if t == nq - 1 else tri_next
        p_rows, denoms = [], []
        for j in range(n_grp):
            s = s_all[j * rows2:(j + 1) * rows2]
            s = jnp.concatenate([jnp.where(mask_a, s[:, 0:blk], NEG_BIG), s[:, blk:2 * blk],
                                 jnp.where(mask_c, s[:, 2 * blk:3 * blk], NEG_BIG), s[:, 3 * blk:]],
                                axis=1)
            sk = jnp.where(row1 < blk, sink_ref[2 * j], sink_ref[2 * j + 1])
            m = jnp.maximum(jnp.max(s, axis=-1, keepdims=True), sk)
            p = jnp.exp2(s - m)
            denoms.append(jnp.sum(p, axis=-1, keepdims=True) + jnp.exp2(sk - m))
            p_rows.append(p.astype(_BF16))
        return jnp.concatenate(p_rows, axis=0), denoms

    def weighted_values(t, p_all, denoms):
        vals = jnp.concatenate(v_blocks[t:t + 3] + [vc_ref[...]], axis=0)
        o_all = jnp.dot(p_all, vals, preferred_element_type=_F32)
        for j in range(n_grp):
            o2 = o_all[j * rows2:(j + 1) * rows2] / denoms[j]
            og = jnp.where(lane_q < HEAD_DIM, o2[0:blk], o2[blk:])
            o_ref[t * blk:(t + 1) * blk, j * LANES:(j + 1) * LANES] = og.astype(_BF16)

    s_next = scores(0)
    probs = None
    for t in range(nq + 1):
        s_cur, s_next = s_next, (scores(t + 1) if t + 1 < nq else None)
        if probs is not None:
            weighted_values(t - 1, *probs)
        probs = softmax(t, s_cur) if t < nq else None


def _attn_call(sink_p, q, k, v, kc, vc):
    nblk = SEQ // ATTN_BLOCK
    nq = ATTN_STEP_BLOCKS
    nstep = nblk // nq
    own = lambda b, n: (b * nstep + n, 0)
    prev = lambda b, n: (b * nblk + jnp.maximum(nq * n - 1, 0), 0)
    nxt = lambda b, n: (b * nblk + jnp.minimum(nq * n + nq, nblk - 1), 0)
    ctx = lambda b, n: (b, 0)
    kv1 = (ATTN_BLOCK, KV_WIDTH)
    kvm = (nq * ATTN_BLOCK, KV_WIDTH)
    return pl.pallas_call(
        _attn_kernel,
        out_shape=jax.ShapeDtypeStruct((TOKENS, ATTN_WIDTH), _BF16),
        grid=(BATCH, nstep),
        in_specs=[pl.BlockSpec(memory_space=pltpu.SMEM),
                  pl.BlockSpec((nq * ATTN_BLOCK, ATTN_WIDTH), own),
                  pl.BlockSpec(kv1, prev), pl.BlockSpec(kvm, own), pl.BlockSpec(kv1, nxt),
                  pl.BlockSpec(kv1, prev), pl.BlockSpec(kvm, own), pl.BlockSpec(kv1, nxt),
                  pl.BlockSpec((CTX_LEN, KV_WIDTH), ctx),
                  pl.BlockSpec((CTX_LEN, KV_WIDTH), ctx)],
        out_specs=pl.BlockSpec((nq * ATTN_BLOCK, ATTN_WIDTH), own),
        compiler_params=pltpu.CompilerParams(dimension_semantics=("arbitrary", "arbitrary"),
                                             vmem_limit_bytes=VMEM_LIMIT),
        name="attn",
    )(sink_p, q, k, k, k, v, v, v, kc, vc)


def _post_kernel(x_ref, ao_ref, su_ref, sv_ref, mod_ref, lng_ref, lnb_ref, ws_ref, bs_ref,
                 ga_ref, gs_ref, woa_ref, wos_ref, bo_ref, gpost_ref, gpre_ref, wr_ref, br_ref,
                 tri_ref,
                 xmid_ref, h2_ref, idx_ref, gate_ref, rank_ref, cnt_ref,
                 mixed_ref, carry_ref):
    tm = x_ref.shape[0]
    m = mod_ref[0]

    @pl.when(pl.program_id(0) == 0)
    def _():
        carry_ref[...] = jnp.zeros_like(carry_ref)

    n_sub = POST_SPLIT
    ts = tm // n_sub
    tok_per_row = LANES // TOP_K
    lane = lax.broadcasted_iota(jnp.int32, (SGU_CHUNK, LANES), 1)
    st = [dict() for _ in range(n_sub)]
    carry = [carry_ref[...]]

    def rows(h):
        return slice(h * ts, (h + 1) * ts)

    def s1(h):
        gv = _gelu_tanh(sv_ref[rows(h), :])
        mu = jnp.mean(gv, axis=-1, keepdims=True)
        gc = gv - mu
        var = jnp.mean(gc * gc, axis=-1, keepdims=True)
        st[h]["vb"] = (gc * lax.rsqrt(var + EPS) * lng_ref[...] + lnb_ref[...]).astype(_BF16)

    def s2(h):
        vb = st[h].pop("vb")
        for c in range(ts // SGU_CHUNK):
            r0 = c * SGU_CHUNK
            for p in range(SGU_WIDTH // LANES):
                l0 = p * LANES
                r = jnp.dot(ws_ref[p], vb[r0:r0 + SGU_CHUNK, l0:l0 + LANES], preferred_element_type=_F32)
                mixed = jnp.where(lane < SGU_HEAD_DIM, r[0:SGU_CHUNK], r[SGU_CHUNK:])
                mixed_ref[h * ts + r0:h * ts + r0 + SGU_CHUNK, l0:l0 + LANES] = mixed + bs_ref[:, l0:l0 + LANES]
        sgu_o = _gelu_tanh(su_ref[rows(h), :]) * mixed_ref[rows(h), :]
        st[h]["oa"] = _rms(ao_ref[rows(h), :].astype(_F32), ga_ref[...]).astype(_BF16)
        st[h]["os"] = _rms(sgu_o, gs_ref[...]).astype(_BF16)

    def s3(h):
        st[h]["mix"] = (jnp.dot(st[h].pop("oa"), woa_ref[...], preferred_element_type=_F32)
                        + jnp.dot(st[h].pop("os"), wos_ref[...], preferred_element_type=_F32) + bo_ref[...])

    def s4(h):
        x_mid = x_ref[rows(h), :] + m[2:3] * _rms(st[h].pop("mix"), gpost_ref[...])
        xmid_ref[rows(h), :] = x_mid
        h2 = _rms(x_mid, gpre_ref[...]) * (1.0 + m[4:5]) + m[3:4]
        for cc in range(ROW_SUB):
            h2_ref[pl.ds(h * ts * ROW_SUB + cc, ts, stride=ROW_SUB), :] = h2[:, cc * LANES:(cc + 1) * LANES]
        h_hi = h2.astype(_BF16)
        st[h]["h_hi"] = h_hi
        st[h]["h_lo"] = (h2 - h_hi.astype(_F32)).astype(_BF16)

    def s5(h):
        r = (jnp.dot(st[h].pop("h_hi"), wr_ref[...], preferred_element_type=_F32)
             + jnp.dot(st[h].pop("h_lo"), wr_ref[...], preferred_element_type=_F32))
        st[h]["lg"] = r + pltpu.roll(r, LANES - N_EXPERTS, 1) + br_ref[...]

    def s6(h):
        lg = st[h].pop("lg")
        lane_r = lax.broadcasted_iota(jnp.int32, lg.shape, 1)
        lane_f = lane_r.astype(_F32)
        tops, hots = [], []
        for _k in range(TOP_K):
            mx = jnp.max(lg, axis=-1, keepdims=True)
            pick = jnp.min(jnp.where(lg == mx, lane_f, float(LANES)), axis=-1, keepdims=True)
            hot = lane_f == pick
            tops.append((mx, pick))
            hots.append(hot)
            lg = jnp.where(hot, 2.0 * NEG_BIG, lg)
        es = [jnp.exp(t[0] - tops[0][0]) for t in tops]
        esum = es[0] + es[1] + es[2] + es[3]
        multi = jnp.zeros(lg.shape, _F32)
        for hot in hots:
            multi = multi + jnp.where(hot, 1.0, 0.0)
        cum = jnp.dot(tri_ref[...], multi.astype(_BF16), preferred_element_type=_F32) + carry[0]
        row_r = lax.broadcasted_iota(jnp.int32, lg.shape, 0)
        lane_base = (row_r & (tok_per_row - 1)) * TOP_K
        gate_o = jnp.zeros(lg.shape, _F32)
        idx_e = jnp.zeros(lg.shape, _F32)
        rank_hi_e = jnp.zeros(lg.shape, _F32)
        rank_lo_e = jnp.zeros(lg.shape, _F32)
        for kk in range(TOP_K):
            rk = jnp.sum(jnp.where(hots[kk], cum, 0.0), axis=-1, keepdims=True)
            rk_hi = jnp.floor(rk * (1.0 / 256.0))
            here = lane_r == lane_base + kk
            gate_o = jnp.where(lane_r == kk, es[kk] / esum, gate_o)
            idx_e = jnp.where(here, tops[kk][1], idx_e)
            rank_hi_e = jnp.where(here, rk_hi, rank_hi_e)
            rank_lo_e = jnp.where(here, rk - 256.0 * rk_hi, rank_lo_e)
        fr = lax.broadcasted_iota(jnp.int32, (ts // tok_per_row, ts), 0)
        fc = lax.broadcasted_iota(jnp.int32, (ts // tok_per_row, ts), 1)
        fold = jnp.where(lax.shift_right_logical(fc, tok_per_row.bit_length() - 1) == fr, 1.0, 0.0).astype(_BF16)
        fs = slice(h * ts // tok_per_row, (h + 1) * ts // tok_per_row)
        idx_ref[fs, :] = jnp.dot(fold, idx_e.astype(_BF16), preferred_element_type=_F32).astype(jnp.int32)
        rank_ref[fs, :] = (256.0 * jnp.dot(fold, rank_hi_e.astype(_BF16), preferred_element_type=_F32)
                           + jnp.dot(fold, rank_lo_e.astype(_BF16), preferred_element_type=_F32)).astype(jnp.int32)
        gate_ref[rows(h), :] = gate_o
        carry[0] = carry[0] + jnp.sum(multi, axis=0, keepdims=True)

    order = sorted(((k + 2.5 * h, h, k) for h in range(n_sub) for k in range(6)))
    stages = (s1, s2, s3, s4, s5, s6)
    for _, h, k in order:
        stages[k](h)
    carry_ref[...] = carry[0]
    cnt_ref[...] = carry[0].astype(jnp.int32)


def _post_call(x2, ao, su, sv, mod3, lng, lnb, ws2, bsf, ga, gs, woa, wos, bo, gpost, gpre,
               wr, br, tri):
    tm = POST_TILE
    tiles_per_batch = SEQ // tm
    row = lambda i: (i, 0)
    const = lambda i: (0, 0)
    const3 = lambda i: (0, 0, 0)
    return pl.pallas_call(
        _post_kernel,
        out_shape=(jax.ShapeDtypeStruct((TOKENS, D_MODEL), _F32),
                   jax.ShapeDtypeStruct((TOKENS * ROW_SUB, LANES), _F32),
                   jax.ShapeDtypeStruct((TOKENS * TOP_K // LANES, LANES), jnp.int32),
                   jax.ShapeDtypeStruct((TOKENS, LANES), _F32),
                   jax.ShapeDtypeStruct((TOKENS * TOP_K // LANES, LANES), jnp.int32),
                   jax.ShapeDtypeStruct((1, LANES), jnp.int32)),
        grid=(TOKENS // tm,),
        in_specs=[pl.BlockSpec((tm, D_MODEL), row),
                  pl.BlockSpec((tm, ATTN_WIDTH), row),
                  pl.BlockSpec((tm, SGU_WIDTH), row),
                  pl.BlockSpec((tm, SGU_WIDTH), row),
                  pl.BlockSpec((1, N_MOD, D_MODEL), lambda i: (i // tiles_per_batch, 0, 0)),
                  pl.BlockSpec((1, SGU_WIDTH), const),
                  pl.BlockSpec((1, SGU_WIDTH), const),
                  pl.BlockSpec((SGU_WIDTH // LANES, 2 * SGU_CHUNK, SGU_CHUNK), const3),
                  pl.BlockSpec((SGU_CHUNK, SGU_WIDTH), const),
                  pl.BlockSpec((1, ATTN_WIDTH), const),
                  pl.BlockSpec((1, SGU_WIDTH), const),
                  pl.BlockSpec((ATTN_WIDTH, D_MODEL), const),
                  pl.BlockSpec((SGU_WIDTH, D_MODEL), const),
                  pl.BlockSpec((1, D_MODEL), const),
                  pl.BlockSpec((1, D_MODEL), const),
                  pl.BlockSpec((1, D_MODEL), const),
                  pl.BlockSpec((D_MODEL, LANES), const),
                  pl.BlockSpec((1, LANES), const),
                  pl.BlockSpec((tm // POST_SPLIT, tm // POST_SPLIT), const)],
        out_specs=(pl.BlockSpec((tm, D_MODEL), row),
                   pl.BlockSpec((tm * ROW_SUB, LANES), row),
                   pl.BlockSpec((tm * TOP_K // LANES, LANES), row),
                   pl.BlockSpec((tm, LANES), row),
                   pl.BlockSpec((tm * TOP_K // LANES, LANES), row),
                   pl.BlockSpec((1, LANES), const)),
        scratch_shapes=[pltpu.VMEM((tm, SGU_WIDTH), _F32),
                        pltpu.VMEM((1, LANES), _F32)],
        compiler_params=pltpu.CompilerParams(dimension_semantics=("arbitrary",),
                                             vmem_limit_bytes=VMEM_LIMIT),
        name="post",
    )(x2, ao, su, sv, mod3, lng, lnb, ws2, bsf, ga, gs, woa, wos, bo, gpost, gpre, wr, br, tri)


def _dispatch_kernel(fs_ref, fl_ref, nu_ref, dest_ref, h2_ref, xs_ref, zero_ref, sem, zsem):
    tm = h2_ref.shape[0] // ROW_SUB
    n_token_steps = TOKENS // tm
    i = pl.program_id(0)

    @pl.when(i < n_token_steps)
    def _():
        def issue(r, carry):
            for kk in range(TOP_K):
                d = pl.multiple_of(dest_ref[r * TOP_K + kk] * ROW_SUB, ROW_SUB)
                pltpu.make_async_copy(h2_ref.at[pl.ds(pl.multiple_of(r * ROW_SUB, ROW_SUB), ROW_SUB)],
                                      xs_ref.at[pl.ds(d, ROW_SUB)], sem).start(priority=kk % 2)
            return carry

        lax.fori_loop(0, tm, issue, 0, unroll=8)
        for kk in range(TOP_K):
            pltpu.make_async_copy(h2_ref, xs_ref.at[pl.ds(0, tm * ROW_SUB)], sem).wait()

    @pl.when(i == n_token_steps)
    def _():
        zero_ref[...] = jnp.zeros_like(zero_ref)
        block_rows = EXPERT_BLOCK * ROW_SUB

        def pad_run(e, wait):
            pos = fs_ref[e]
            length = fl_ref[e]
            for bit in reversed(range(EXPERT_BLOCK.bit_length() - 1)):
                size = 1 << bit
                take = length & size

                @pl.when(take != 0)
                def _():
                    cp = pltpu.make_async_copy(
                        zero_ref.at[pl.ds(0, size * ROW_SUB)],
                        xs_ref.at[pl.ds(pl.multiple_of(pos * ROW_SUB, ROW_SUB), size * ROW_SUB)], zsem)
                    if wait:
                        cp.wait()
                    else:
                        cp.start()

                pos = pos + take

        def tail_block(blk, wait):
            cp = pltpu.make_async_copy(
                zero_ref, xs_ref.at[pl.ds(pl.multiple_of(blk * block_rows, block_rows), block_rows)], zsem)
            if wait:
                cp.wait()
            else:
                cp.start()

        for wait in (False, True):
            lax.fori_loop(0, N_EXPERTS, lambda e, c, w=wait: (pad_run(e, w), c)[1], 0)
            lax.fori_loop(nu_ref[0], N_EXPERT_BLOCKS, lambda blk, c, w=wait: (tail_block(blk, w), c)[1], 0)


def _dispatch_call(fill_start, fill_len, n_used, dest_flat, h2):
    tm = MOVE_TILE
    n_token_steps = TOKENS // tm
    grid_spec = pltpu.PrefetchScalarGridSpec(
        num_scalar_prefetch=3,
        grid=(n_token_steps + 1,),
        in_specs=[pl.BlockSpec((tm * TOP_K,), lambda i, fs, fl, nu: (jnp.minimum(i, n_token_steps - 1),),
                               memory_space=pltpu.SMEM),
                  pl.BlockSpec((tm * ROW_SUB, LANES),
                               lambda i, fs, fl, nu: (jnp.minimum(i, n_token_steps - 1), 0))],
        out_specs=pl.BlockSpec(memory_space=pl.ANY),
        scratch_shapes=[pltpu.VMEM((EXPERT_BLOCK * ROW_SUB, LANES), _F32),
                        pltpu.SemaphoreType.DMA,
                        pltpu.SemaphoreType.DMA])
    return pl.pallas_call(
        _dispatch_kernel,
        out_shape=jax.ShapeDtypeStruct((SORTED_ROWS * ROW_SUB, LANES), _F32),
        grid_spec=grid_spec,
        compiler_params=pltpu.CompilerParams(dimension_semantics=("arbitrary",),
                                             vmem_limit_bytes=VMEM_LIMIT),
        name="dispatch",
    )(fill_start, fill_len, n_used, dest_flat, h2)


def _expert_kernel(be_ref, slot_ref, first_ref, nxt_ref, nu_ref,
                   xs_ref, wgu_hbm, bgu0_ref, bgu1_ref, wd_hbm, bd0_ref, bd1_ref, y_ref,
                   wgu_stage, wd_stage, wgu_bf, wd_bf, sem):
    p = pl.program_id(0)
    half = EXPERT_BLOCK * ROW_SUB
    n_pairs_used = (nu_ref[0] + 1) // 2

    def stage_copies(e):
        return (pltpu.make_async_copy(wgu_hbm.at[e], wgu_stage, sem.at[0]),
                pltpu.make_async_copy(wd_hbm.at[e], wd_stage, sem.at[1]))

    @pl.when(p == 0)
    def _():
        for cp in stage_copies(be_ref[0]):
            cp.start()

    for b in (2 * p, 2 * p + 1):
        @pl.when((first_ref[b] == 1) & (b < nu_ref[0]))
        def _():
            for cp in stage_copies(be_ref[b]):
                cp.wait()
            wgu_bf[slot_ref[b]] = wgu_stage[...].astype(_BF16)
            wd_bf[slot_ref[b]] = wd_stage[...].astype(_BF16)

            @pl.when(nxt_ref[b] < N_EXPERTS)
            def _():
                for cp in stage_copies(nxt_ref[b]):
                    cp.start()

    def ffn(x, slot, bgu_ref, bd_ref):
        gu = jnp.dot(x, wgu_bf[slot], preferred_element_type=_F32) + bgu_ref[0]
        gate = jnp.minimum(gu[:, 0:D_FF_EXPERT], SWIGLU_LIMIT)
        up = jnp.clip(gu[:, D_FF_EXPERT:], -SWIGLU_LIMIT, SWIGLU_LIMIT)
        act = (up + 1.0) * gate * (1.0 / (1.0 + jnp.exp(-SWIGLU_ALPHA * gate)))
        return jnp.dot(act.astype(_BF16), wd_bf[slot], preferred_element_type=_F32) + bd_ref[0]

    def load_rows(which):
        return jnp.concatenate([xs_ref[pl.ds(which * half + cc, EXPERT_BLOCK, stride=ROW_SUB), :]
                                for cc in range(ROW_SUB)], axis=1).astype(_BF16)

    def store_rows(which, y):
        for cc in range(ROW_SUB):
            y_ref[pl.ds(which * half + cc, EXPERT_BLOCK, stride=ROW_SUB), :] = y[:, cc * LANES:(cc + 1) * LANES]

    @pl.when(p < n_pairs_used)
    def _():
        y0 = ffn(load_rows(0), slot_ref[2 * p], bgu0_ref, bd0_ref)
        x1 = load_rows(1)
        store_rows(0, y0)
        store_rows(1, ffn(x1, slot_ref[2 * p + 1], bgu1_ref, bd1_ref))

    @pl.when(p >= n_pairs_used)
    def _():
        y_ref[...] = jnp.zeros_like(y_ref)


def _expert_call(block_expert, block_slot, block_first, block_next, n_used, xs, wgu, bgu3, wd, bd3):
    tb = 2 * EXPERT_BLOCK
    live = lambda p, be, sl, fi, nx, nu: (jnp.minimum(p, (nu[0] - 1) // 2), 0)
    bsel0 = lambda p, be, sl, fi, nx, nu: (be[2 * p], 0, 0)
    bsel1 = lambda p, be, sl, fi, nx, nu: (be[2 * p + 1], 0, 0)
    grid_spec = pltpu.PrefetchScalarGridSpec(
        num_scalar_prefetch=5,
        grid=(N_EXPERT_BLOCKS // 2,),
        in_specs=[pl.BlockSpec((tb * ROW_SUB, LANES), live),
                  pl.BlockSpec(memory_space=pl.ANY),
                  pl.BlockSpec((1, 1, 2 * D_FF_EXPERT), bsel0),
                  pl.BlockSpec((1, 1, 2 * D_FF_EXPERT), bsel1),
                  pl.BlockSpec(memory_space=pl.ANY),
                  pl.BlockSpec((1, 1, D_MODEL), bsel0),
                  pl.BlockSpec((1, 1, D_MODEL), bsel1)],
        out_specs=pl.BlockSpec((tb * ROW_SUB, LANES), lambda p, be, sl, fi, nx, nu: (p, 0)),
        scratch_shapes=[pltpu.VMEM((D_MODEL, 2 * D_FF_EXPERT), _F32),
                        pltpu.VMEM((D_FF_EXPERT, D_MODEL), _F32),
                        pltpu.VMEM((2, D_MODEL, 2 * D_FF_EXPERT), _BF16),
                        pltpu.VMEM((2, D_FF_EXPERT, D_MODEL), _BF16),
                        pltpu.SemaphoreType.DMA((2,))])
    return pl.pallas_call(
        _expert_kernel,
        out_shape=jax.ShapeDtypeStruct((SORTED_ROWS * ROW_SUB, LANES), _F32),
        grid_spec=grid_spec,
        compiler_params=pltpu.CompilerParams(dimension_semantics=("arbitrary",),
                                             vmem_limit_bytes=VMEM_LIMIT),
        name="expert",
    )(block_expert, block_slot, block_first, block_next, n_used, xs, wgu, bgu3, bgu3, wd, bd3, bd3)


def _combine_kernel(dest_ref, dest_next_ref, gate_ref, xmid_ref, mod_ref, g_ref, y_ref, o_ref,
                    buf_ref, sem):
    tm = xmid_ref.shape[0]
    m = mod_ref[0]
    i = pl.program_id(0)
    n = pl.num_programs(0)

    def gather_rows(idx_ref, slot):
        def issue(r, carry):
            for kk in range(TOP_K):
                d = pl.multiple_of(idx_ref[r * TOP_K + kk] * ROW_SUB, ROW_SUB)
                pltpu.make_async_copy(
                    y_ref.at[pl.ds(d, ROW_SUB)],
                    buf_ref.at[slot, kk, pl.ds(pl.multiple_of(r * ROW_SUB, ROW_SUB), ROW_SUB)],
                    sem.at[slot]).start(priority=kk % 2)
            return carry

        lax.fori_loop(0, tm, issue, 0, unroll=8)

    @pl.when(i == 0)
    def _():
        gather_rows(dest_ref, 0)

    @pl.when(i + 1 < n)
    def _():
        gather_rows(dest_next_ref, (i + 1) % 2)

    slot = i % 2
    for kk in range(TOP_K):
        pltpu.make_async_copy(y_ref.at[pl.ds(0, tm * ROW_SUB)], buf_ref.at[slot, kk],
                              sem.at[slot]).wait()
    g = gate_ref[...]
    pieces = []
    for cc in range(ROW_SUB):
        piece = buf_ref[slot, 0, pl.ds(cc, tm, stride=ROW_SUB), :] * g[:, 0:1]
        for kk in range(1, TOP_K):
            piece = piece + buf_ref[slot, kk, pl.ds(cc, tm, stride=ROW_SUB), :] * g[:, kk:kk + 1]
        pieces.append(piece)
    ffn = jnp.concatenate(pieces, axis=1)
    o_ref[...] = xmid_ref[...] + m[5:6] * _rms(ffn, g_ref[...])


def _combine_call(dest_flat, gates, x_mid, mod3, gpost_ffn, y_rows):
    tm = MOVE_TILE
    tiles_per_batch = SEQ // tm
    n_steps = TOKENS // tm
    return pl.pallas_call(
        _combine_kernel,
        out_shape=jax.ShapeDtypeStruct((TOKENS, D_MODEL), _F32),
        grid=(n_steps,),
        in_specs=[pl.BlockSpec((tm * TOP_K,), lambda i: (i,), memory_space=pltpu.SMEM),
                  pl.BlockSpec((tm * TOP_K,), lambda i: (jnp.minimum(i + 1, n_steps - 1),),
                               memory_space=pltpu.SMEM),
                  pl.BlockSpec((tm, LANES), lambda i: (i, 0)),
                  pl.BlockSpec((tm, D_MODEL), lambda i: (i, 0)),
                  pl.BlockSpec((1, N_MOD, D_MODEL), lambda i: (i // tiles_per_batch, 0, 0)),
                  pl.BlockSpec((1, D_MODEL), lambda i: (0, 0)),
                  pl.BlockSpec(memory_space=pl.ANY)],
        out_specs=pl.BlockSpec((tm, D_MODEL), lambda i: (i, 0)),
        scratch_shapes=[pltpu.VMEM((2, TOP_K, tm * ROW_SUB, LANES), _F32),
                        pltpu.SemaphoreType.DMA((2,))],
        compiler_params=pltpu.CompilerParams(dimension_semantics=("arbitrary",),
                                             vmem_limit_bytes=VMEM_LIMIT),
        name="combine",
    )(dest_flat, dest_flat, gates, x_mid, mod3, gpost_ffn, y_rows)


def _rope_tables():
    pos = np.arange(SEQ)
    pos_row = (pos // GRID_W).astype(np.float64)
    pos_col = (pos % GRID_W).astype(np.float64)
    n_freq = HEAD_DIM // 4
    inv_freq = ROPE_THETA ** (-np.arange(n_freq, dtype=np.float64) / n_freq)
    d = np.arange(LANES) % HEAD_DIM
    f = inv_freq[d % n_freq]
    ang = np.where((d < HEAD_DIM // 2)[None, :], pos_row[:, None] * f[None, :], pos_col[:, None] * f[None, :])
    sign = np.where((d % (HEAD_DIM // 2)) < n_freq, -1.0, 1.0)
    return (jnp.asarray(np.cos(ang), dtype=_F32), jnp.asarray(np.sin(ang) * sign[None, :], dtype=_F32))


def _perm_heads(a, axis):
    shape = a.shape
    a = a.reshape(shape[:axis] + (N_HEADS, HEAD_DIM) + shape[axis + 1:])
    a = jnp.take(a, jnp.array(HEAD_PERM), axis=axis)
    return a.reshape(shape)


def kernel(x, c, ctx, c_ctx, w_ada, b_ada, g_pre_mix, g_post_mix, g_pre_ffn, g_post_ffn, w_in, b_in, attn_sink, sgu_ln_g, sgu_ln_b, sgu_w, sgu_b, g_attn_out, g_sgu_out, w_out, b_out, w_router, b_router, w_gate_up, b_gate_up, w_down, b_down):
    l = 0
    x2 = x.reshape(TOKENS, D_MODEL)
    ctx2 = ctx.reshape(BATCH * CTX_LEN, D_MODEL)

    cc = jnp.zeros((16, D_MODEL), _F32).at[:BATCH].set(c).at[BATCH].set(c_ctx)
    mod = _ada_call(cc, w_ada[l], b_ada[l].reshape(1, -1))
    mod3 = mod.reshape(16, N_MOD, D_MODEL)

    w_in_l = w_in[l]
    b_in_l = b_in[l]
    w_in_p = jnp.concatenate([_perm_heads(w_in_l[:, :ATTN_WIDTH], 1), w_in_l[:, ATTN_WIDTH:]], axis=1)
    b_in_p = jnp.concatenate([_perm_heads(b_in_l[:ATTN_WIDTH], 0), b_in_l[ATTN_WIDTH:]], axis=0)
    cos_t, sin_t = _rope_tables()
    g_pre = g_pre_mix[l].reshape(1, -1)

    q, k, v, su, sv = _inproj_call(x2, mod3, g_pre, w_in_p.astype(_BF16), b_in_p.reshape(1, -1), cos_t, sin_t)
    kc, vc = _ctxproj_call(ctx2, mod3, g_pre, w_in_l[:, KV_START:KV_END].astype(_BF16),
                           b_in_l[KV_START:KV_END].reshape(1, -1))

    sink_p = jnp.take(attn_sink[l], jnp.array(HEAD_PERM)) * LOG2_E
    ao = _attn_call(sink_p, q, k, v, kc, vc)

    ws2 = sgu_w[l].reshape(SGU_WIDTH // LANES, 2 * SGU_CHUNK, SGU_CHUNK).astype(_BF16)
    bsf = jnp.repeat(sgu_b[l].T, SGU_HEAD_DIM, axis=1)
    w_out_l = w_out[l]
    woa = _perm_heads(w_out_l[:ATTN_WIDTH], 0).astype(_BF16)
    wos = w_out_l[ATTN_WIDTH:].astype(_BF16)
    ga = _perm_heads(g_attn_out[l], 0).reshape(1, -1)
    wr_hi = w_router[l].astype(_BF16)
    wr_lo = (w_router[l] - wr_hi.astype(_F32)).astype(_BF16)
    wr = (jnp.zeros((D_MODEL, LANES), _BF16).at[:, :N_EXPERTS].set(wr_hi)
          .at[:, N_EXPERTS:2 * N_EXPERTS].set(wr_lo))
    br = jnp.full((1, LANES), NEG_BIG, _F32).at[0, :N_EXPERTS].set(b_router[l])
    ii = np.arange(POST_TILE // POST_SPLIT)
    tri = jnp.asarray(ii[None, :] < ii[:, None], dtype=_BF16)

    x_mid, h2, idx_o, gate_o, rank_o, cnt_o = _post_call(
        x2, ao, su, sv, mod3, sgu_ln_g[l].reshape(1, -1), sgu_ln_b[l].reshape(1, -1), ws2, bsf,
        ga, g_sgu_out[l].reshape(1, -1), woa, wos, b_out[l].reshape(1, -1),
        g_post_mix[l].reshape(1, -1), g_pre_ffn[l].reshape(1, -1), wr, br, tri)

    counts = cnt_o[0, :N_EXPERTS]
    padded = (counts + EXPERT_BLOCK - 1) // EXPERT_BLOCK * EXPERT_BLOCK
    pad_end = jnp.cumsum(padded)
    pad_start = pad_end - padded
    dest = rank_o
    for e in range(N_EXPERTS):
        dest = dest + jnp.where(idx_o == e, pad_start[e], 0)
    dest = dest.astype(jnp.int32).reshape(-1)
    n_used = (pad_end[-1:] // EXPERT_BLOCK).astype(jnp.int32)

    xs = _dispatch_call((pad_start + counts).astype(jnp.int32), (padded - counts).astype(jnp.int32),
                        n_used, dest, h2)

    e_ids = jnp.arange(N_EXPERTS, dtype=jnp.int32)
    used = padded > 0
    cand = jnp.where(used, e_ids, N_EXPERTS)
    next_used = jnp.min(jnp.where(e_ids[None, :] > e_ids[:, None], cand[None, :], N_EXPERTS), axis=1)
    ordinal = jnp.cumsum(used.astype(jnp.int32)) - 1
    block_id = jnp.arange(N_EXPERT_BLOCKS, dtype=jnp.int32)
    block_start = block_id * EXPERT_BLOCK
    sched_start = jnp.minimum(block_start, (n_used[0] - 1) * EXPERT_BLOCK)
    sched_expert = jnp.minimum(jnp.sum(pad_end[None, :] <= sched_start[:, None], axis=1),
                               N_EXPERTS - 1).astype(jnp.int32)
    onehot = e_ids[None, :] == sched_expert[:, None]
    pick = lambda table: jnp.sum(jnp.where(onehot, table[None, :], 0), axis=1)
    block_slot = (pick(ordinal) & 1).astype(jnp.int32)
    block_first = ((block_start == pick(pad_start)) & (block_id < n_used[0])).astype(jnp.int32)
    block_next = pick(next_used).astype(jnp.int32)
    y_rows = _expert_call(sched_expert, block_slot, block_first, block_next, n_used, xs,
                          w_gate_up[l], b_gate_up[l].reshape(N_EXPERTS, 1, -1),
                          w_down[l], b_down[l].reshape(N_EXPERTS, 1, -1))
    out = _combine_call(dest, gate_o, x_mid, mod3, g_post_ffn[l].reshape(1, -1), y_rows)
    return out.reshape(BATCH, SEQ, D_MODEL)
```

```python
import jax
import jax.numpy as jnp
import numpy as np
from jax import lax
from jax.experimental import pallas as pl
from jax.experimental.pallas import tpu as pltpu

D_MODEL = 1024
BATCH = 8
SEQ = 4096
TOKENS = BATCH * SEQ
GRID_W = 64
CTX_LEN = 256
N_HEADS = 8
N_KV_HEADS = 2
HEAD_DIM = 64
ATTN_WIDTH = N_HEADS * HEAD_DIM
KV_WIDTH = N_KV_HEADS * HEAD_DIM
WINDOW = 128
ATTN_BLOCK = 128
SGU_HEADS = 8
SGU_HEAD_DIM = 64
SGU_WIDTH = SGU_HEADS * SGU_HEAD_DIM
SGU_CHUNK = 128
KV_START = ATTN_WIDTH
KV_END = ATTN_WIDTH + 2 * KV_WIDTH
IN_WIDTH = KV_END + 2 * SGU_WIDTH
N_EXPERTS = 32
TOP_K = 4
D_FF_EXPERT = 1024
SWIGLU_LIMIT = 7.0
SWIGLU_ALPHA = 1.702
ROPE_THETA = 10000.0
EPS = 1e-6
N_MOD = 6

LANES = 128
NEG_BIG = -1e30
VMEM_LIMIT = 56 * 1024 * 1024

ROW_TILE = 1024
SCATTER_TILE = 1024
MOVE_TILE = 512
ATTN_STEP_BLOCKS = 8
EXPERT_BLOCK = 512
POST_TILE = 1024
POST_SPLIT = 4
N_EXPERT_BLOCKS = TOKENS * TOP_K // EXPERT_BLOCK + N_EXPERTS
SORTED_ROWS = N_EXPERT_BLOCKS * EXPERT_BLOCK
HEAD_PERM = (0, 4, 1, 5, 2, 6, 3, 7)

LOG2_E = 1.4426950408889634
QK_SCALE = LOG2_E * HEAD_DIM ** -0.5

_F32 = jnp.float32
_BF16 = jnp.bfloat16


def _rms(x, g):
    ms = jnp.mean(x * x, axis=-1, keepdims=True)
    return x * lax.rsqrt(ms + EPS) * g


ROW_SUB = D_MODEL // LANES


def _store_row_tiles(ref, x):
    n = x.shape[0]
    for cc in range(ROW_SUB):
        ref[pl.ds(cc, n, stride=ROW_SUB), :] = x[:, cc * LANES:(cc + 1) * LANES]


def _load_row_tiles(ref, n):
    return jnp.concatenate([ref[pl.ds(cc, n, stride=ROW_SUB), :] for cc in range(ROW_SUB)], axis=1)


def _gelu_tanh(x):
    c = 0.7978845608028654
    return x * (0.5 * (1.0 + jnp.tanh(c * (x + 0.044715 * (x * x * x)))))


def _ada_kernel(c_ref, w_ref, b_ref, o_ref):
    c = c_ref[...]
    a = c / (1.0 + jnp.exp(-c))
    o_ref[...] = jnp.dot(a, w_ref[...], preferred_element_type=_F32,
                         precision=lax.Precision.HIGHEST) + b_ref[...]


def _ada_call(cc, w_ada, b_ada):
    n = N_MOD * D_MODEL
    tn = 512
    return pl.pallas_call(
        _ada_kernel,
        out_shape=jax.ShapeDtypeStruct((16, n), _F32),
        grid=(n // tn,),
        in_specs=[pl.BlockSpec((16, D_MODEL), lambda j: (0, 0)),
                  pl.BlockSpec((D_MODEL, tn), lambda j: (0, j)),
                  pl.BlockSpec((1, tn), lambda j: (0, j))],
        out_specs=pl.BlockSpec((16, tn), lambda j: (0, j)),
        compiler_params=pltpu.CompilerParams(dimension_semantics=("arbitrary",),
                                             vmem_limit_bytes=VMEM_LIMIT),
        name="ada",
    )(cc, w_ada, b_ada)


def _rope(x, cos, sin_signed, first_half):
    nxt = pltpu.roll(x, LANES - 16, 1)
    prv = pltpu.roll(x, 16, 1)
    return x * cos + jnp.where(first_half, nxt, prv) * sin_signed


def _inproj_kernel(x_ref, mod_ref, g_ref, w_ref, b_ref, cos_ref, sin_ref,
                   q_ref, k_ref, v_ref, su_ref, sv_ref):
    m = mod_ref[0]
    h = _rms(x_ref[...], g_ref[...]) * (1.0 + m[1:2]) + m[0:1]
    hb = h.astype(_BF16)
    cos = cos_ref[...]
    sin = sin_ref[...]
    lane = lax.broadcasted_iota(jnp.int32, cos.shape, 1)
    first_half = (lane & 31) < 16

    q = jnp.dot(hb, w_ref[:, 0:ATTN_WIDTH], preferred_element_type=_F32) + b_ref[:, 0:ATTN_WIDTH]
    for j in range(ATTN_WIDTH // LANES):
        qj = _rope(q[:, j * LANES:(j + 1) * LANES], cos, sin, first_half)
        q_ref[:, j * LANES:(j + 1) * LANES] = (qj * QK_SCALE).astype(_BF16)
    kv = jnp.dot(hb, w_ref[:, KV_START:KV_END], preferred_element_type=_F32) + b_ref[:, KV_START:KV_END]
    k_ref[...] = _rope(kv[:, 0:KV_WIDTH], cos, sin, first_half).astype(_BF16)
    v_ref[...] = kv[:, KV_WIDTH:].astype(_BF16)
    u0 = KV_END
    u1 = KV_END + SGU_WIDTH
    su_ref[...] = jnp.dot(hb, w_ref[:, u0:u1], preferred_element_type=_F32) + b_ref[:, u0:u1]
    sv_ref[...] = jnp.dot(hb, w_ref[:, u1:IN_WIDTH], preferred_element_type=_F32) + b_ref[:, u1:IN_WIDTH]


def _inproj_call(x2, mod3, g_pre, w_in_b, b_in2, cos_t, sin_t):
    tm = ROW_TILE
    tiles_per_batch = SEQ // tm
    row = lambda i: (i, 0)
    const = lambda i: (0, 0)
    return pl.pallas_call(
        _inproj_kernel,
        out_shape=(jax.ShapeDtypeStruct((TOKENS, ATTN_WIDTH), _BF16),
                   jax.ShapeDtypeStruct((TOKENS, KV_WIDTH), _BF16),
                   jax.ShapeDtypeStruct((TOKENS, KV_WIDTH), _BF16),
                   jax.ShapeDtypeStruct((TOKENS, SGU_WIDTH), _F32),
                   jax.ShapeDtypeStruct((TOKENS, SGU_WIDTH), _F32)),
        grid=(TOKENS // tm,),
        in_specs=[pl.BlockSpec((tm, D_MODEL), row),
                  pl.BlockSpec((1, N_MOD, D_MODEL), lambda i: (i // tiles_per_batch, 0, 0)),
                  pl.BlockSpec((1, D_MODEL), const),
                  pl.BlockSpec((D_MODEL, IN_WIDTH), const),
                  pl.BlockSpec((1, IN_WIDTH), const),
                  pl.BlockSpec((tm, LANES), lambda i: (i % tiles_per_batch, 0)),
                  pl.BlockSpec((tm, LANES), lambda i: (i % tiles_per_batch, 0))],
        out_specs=(pl.BlockSpec((tm, ATTN_WIDTH), row),
                   pl.BlockSpec((tm, KV_WIDTH), row),
                   pl.BlockSpec((tm, KV_WIDTH), row),
                   pl.BlockSpec((tm, SGU_WIDTH), row),
                   pl.BlockSpec((tm, SGU_WIDTH), row)),
        compiler_params=pltpu.CompilerParams(dimension_semantics=("arbitrary",),
                                             vmem_limit_bytes=VMEM_LIMIT),
        name="inproj",
    )(x2, mod3, g_pre, w_in_b, b_in2, cos_t, sin_t)


def _ctxproj_kernel(x_ref, mod_ref, g_ref, w_ref, b_ref, k_ref, v_ref):
    m = mod_ref[0]
    h = _rms(x_ref[...], g_ref[...]) * (1.0 + m[1:2]) + m[0:1]
    kv = jnp.dot(h.astype(_BF16), w_ref[...], preferred_element_type=_F32) + b_ref[...]
    k_ref[...] = kv[:, 0:KV_WIDTH].astype(_BF16)
    v_ref[...] = kv[:, KV_WIDTH:].astype(_BF16)


def _ctxproj_call(ctx2, mod3, g_pre, w_kv_b, b_kv2):
    tm = ROW_TILE
    rows = ctx2.shape[0]
    row = lambda i: (i, 0)
    const = lambda i: (0, 0)
    return pl.pallas_call(
        _ctxproj_kernel,
        out_shape=(jax.ShapeDtypeStruct((rows, KV_WIDTH), _BF16),
                   jax.ShapeDtypeStruct((rows, KV_WIDTH), _BF16)),
        grid=(rows // tm,),
        in_specs=[pl.BlockSpec((tm, D_MODEL), row),
                  pl.BlockSpec((1, N_MOD, D_MODEL), lambda i: (BATCH, 0, 0)),
                  pl.BlockSpec((1, D_MODEL), const),
                  pl.BlockSpec((D_MODEL, 2 * KV_WIDTH), const),
                  pl.BlockSpec((1, 2 * KV_WIDTH), const)],
        out_specs=(pl.BlockSpec((tm, KV_WIDTH), row),
                   pl.BlockSpec((tm, KV_WIDTH), row)),
        compiler_params=pltpu.CompilerParams(dimension_semantics=("arbitrary",),
                                             vmem_limit_bytes=VMEM_LIMIT),
        name="ctxproj",
    )(ctx2, mod3, g_pre, w_kv_b, b_kv2)


def _attn_kernel(sink_ref, q_ref, kp_ref, km_ref, kn_ref, vp_ref, vm_ref, vn_ref,
                 kc_ref, vc_ref, o_ref):
    n = pl.program_id(1)
    nstep = pl.num_programs(1)
    blk = ATTN_BLOCK
    nq = ATTN_STEP_BLOCKS
    rows2 = 2 * blk
    r = lax.broadcasted_iota(jnp.int32, (rows2, blk), 0) & (blk - 1)
    c = lax.broadcasted_iota(jnp.int32, (rows2, blk), 1)
    tri_prev = c >= r
    tri_next = c <= r
    row1 = lax.broadcasted_iota(jnp.int32, (rows2, 1), 0)
    lane_q = lax.broadcasted_iota(jnp.int32, (blk, LANES), 1)
    n_grp = ATTN_WIDTH // LANES
    k_blocks = [kp_ref[...]] + [km_ref[t * blk:(t + 1) * blk, :] for t in range(nq)] + [kn_ref[...]]
    v_blocks = [vp_ref[...]] + [vm_ref[t * blk:(t + 1) * blk, :] for t in range(nq)] + [vn_ref[...]]

    def scores(t):
        keys = jnp.concatenate(k_blocks[t:t + 3] + [kc_ref[...]], axis=0)
        q_rows = []
        for j in range(n_grp):
            qg = q_ref[t * blk:(t + 1) * blk, j * LANES:(j + 1) * LANES]
            zero = jnp.zeros_like(qg)
            q_rows += [jnp.where(lane_q < HEAD_DIM, qg, zero), jnp.where(lane_q >= HEAD_DIM, qg, zero)]
        return lax.dot_general(jnp.concatenate(q_rows, axis=0), keys, (((1,), (1,)), ((), ())),
                               preferred_element_type=_F32)

    def softmax(t, s_all):
        mask_a = (tri_prev & (n > 0)) if t == 0 else tri_prev
        mask_c = (tri_next & (n < nstep - 1)) if t == nq - 1 else tri_next
        p_rows, denoms = [], []
        for j in range(n_grp):
            s = s_all[j * rows2:(j + 1) * rows2]
            s = jnp.concatenate([jnp.where(mask_a, s[:, 0:blk], NEG_BIG), s[:, blk:2 * blk],
                                 jnp.where(mask_c, s[:, 2 * blk:3 * blk], NEG_BIG), s[:, 3 * blk:]],
                                axis=1)
            sk = jnp.where(row1 < blk, sink_ref[2 * j], sink_ref[2 * j + 1])
            m = jnp.maximum(jnp.max(s, axis=-1, keepdims=True), sk)
            p = jnp.exp2(s - m)
            denoms.append(jnp.sum(p, axis=-1, keepdims=True) + jnp.exp2(sk - m))
            p_rows.append(p.astype(_BF16))
        return jnp.concatenate(p_rows, axis=0), denoms

    def weighted_values(t, p_all, denoms):
        vals = jnp.concatenate(v_blocks[t:t + 3] + [vc_ref[...]], axis=0)
        o_all = jnp.dot(p_all, vals, preferred_element_type=_F32)
        for j in range(n_grp):
            o2 = o_all[j * rows2:(j + 1) * rows2] / denoms[j]
            og = jnp.where(lane_q < HEAD_DIM, o2[0:blk], o2[blk:])
            o_ref[t * blk:(t + 1) * blk, j * LANES:(j + 1) * LANES] = og.astype(_BF16)

    s_next = scores(0)
    probs = None
    for t in range(nq + 1):
        s_cur, s_next = s_next, (scores(t + 1) if t + 1 < nq else None)
        if probs is not None:
            weighted_values(t - 1, *probs)
        probs = softmax(t, s_cur) if t < nq else None


def _attn_call(sink_p, q, k, v, kc, vc):
    nblk = SEQ // ATTN_BLOCK
    nq = ATTN_STEP_BLOCKS
    nstep = nblk // nq
    own = lambda b, n: (b * nstep + n, 0)
    prev = lambda b, n: (b * nblk + jnp.maximum(nq * n - 1, 0), 0)
    nxt = lambda b, n: (b * nblk + jnp.minimum(nq * n + nq, nblk - 1), 0)
    ctx = lambda b, n: (b, 0)
    kv1 = (ATTN_BLOCK, KV_WIDTH)
    kvm = (nq * ATTN_BLOCK, KV_WIDTH)
    return pl.pallas_call(
        _attn_kernel,
        out_shape=jax.ShapeDtypeStruct((TOKENS, ATTN_WIDTH), _BF16),
        grid=(BATCH, nstep),
        in_specs=[pl.BlockSpec(memory_space=pltpu.SMEM),
                  pl.BlockSpec((nq * ATTN_BLOCK, ATTN_WIDTH), own),
                  pl.BlockSpec(kv1, prev), pl.BlockSpec(kvm, own), pl.BlockSpec(kv1, nxt),
                  pl.BlockSpec(kv1, prev), pl.BlockSpec(kvm, own), pl.BlockSpec(kv1, nxt),
                  pl.BlockSpec((CTX_LEN, KV_WIDTH), ctx),
                  pl.BlockSpec((CTX_LEN, KV_WIDTH), ctx)],
        out_specs=pl.BlockSpec((nq * ATTN_BLOCK, ATTN_WIDTH), own),
        compiler_params=pltpu.CompilerParams(dimension_semantics=("arbitrary", "arbitrary"),
                                             vmem_limit_bytes=VMEM_LIMIT),
        name="attn",
    )(sink_p, q, k, k, k, v, v, v, kc, vc)


def _post_kernel(x_ref, ao_ref, su_ref, sv_ref, mod_ref, lng_ref, lnb_ref, ws_ref, bs_ref,
                 ga_ref, gs_ref, woa_ref, wos_ref, bo_ref, gpost_ref, gpre_ref, wr_ref, br_ref,
                 tri_ref,
                 xmid_ref, h2_ref, idx_ref, gate_ref, rank_ref, cnt_ref,
                 mixed_ref, carry_ref):
    tm = x_ref.shape[0]
    m = mod_ref[0]

    @pl.when(pl.program_id(0) == 0)
    def _():
        carry_ref[...] = jnp.zeros_like(carry_ref)

    n_sub = POST_SPLIT
    ts = tm // n_sub
    tok_per_row = LANES // TOP_K
    lane = lax.broadcasted_iota(jnp.int32, (SGU_CHUNK, LANES), 1)
    st = [dict() for _ in range(n_sub)]
    carry = [carry_ref[...]]

    def rows(h):
        return slice(h * ts, (h + 1) * ts)

    def s1(h):
        gv = _gelu_tanh(sv_ref[rows(h), :])
        mu = jnp.mean(gv, axis=-1, keepdims=True)
        gc = gv - mu
        var = jnp.mean(gc * gc, axis=-1, keepdims=True)
        st[h]["vb"] = (gc * lax.rsqrt(var + EPS) * lng_ref[...] + lnb_ref[...]).astype(_BF16)

    def s2(h):
        vb = st[h].pop("vb")
        for c in range(ts // SGU_CHUNK):
            r0 = c * SGU_CHUNK
            for p in range(SGU_WIDTH // LANES):
                l0 = p * LANES
                r = jnp.dot(ws_ref[p], vb[r0:r0 + SGU_CHUNK, l0:l0 + LANES], preferred_element_type=_F32)
                mixed = jnp.where(lane < SGU_HEAD_DIM, r[0:SGU_CHUNK], r[SGU_CHUNK:])
                mixed_ref[h * ts + r0:h * ts + r0 + SGU_CHUNK, l0:l0 + LANES] = mixed + bs_ref[:, l0:l0 + LANES]
        sgu_o = _gelu_tanh(su_ref[rows(h), :]) * mixed_ref[rows(h), :]
        st[h]["oa"] = _rms(ao_ref[rows(h), :].astype(_F32), ga_ref[...]).astype(_BF16)
        st[h]["os"] = _rms(sgu_o, gs_ref[...]).astype(_BF16)

    def s3(h):
        st[h]["mix"] = (jnp.dot(st[h].pop("oa"), woa_ref[...], preferred_element_type=_F32)
                        + jnp.dot(st[h].pop("os"), wos_ref[...], preferred_element_type=_F32) + bo_ref[...])

    def s4(h):
        x_mid = x_ref[rows(h), :] + m[2:3] * _rms(st[h].pop("mix"), gpost_ref[...])
        xmid_ref[rows(h), :] = x_mid
        h2 = _rms(x_mid, gpre_ref[...]) * (1.0 + m[4:5]) + m[3:4]
        for cc in range(ROW_SUB):
            h2_ref[pl.ds(h * ts * ROW_SUB + cc, ts, stride=ROW_SUB), :] = h2[:, cc * LANES:(cc + 1) * LANES]
        h_hi = h2.astype(_BF16)
        st[h]["h_hi"] = h_hi
        st[h]["h_lo"] = (h2 - h_hi.astype(_F32)).astype(_BF16)

    def s5(h):
        r = (jnp.dot(st[h].pop("h_hi"), wr_ref[...], preferred_element_type=_F32)
             + jnp.dot(st[h].pop("h_lo"), wr_ref[...], preferred_element_type=_F32))
        st[h]["lg"] = r + pltpu.roll(r, LANES - N_EXPERTS, 1) + br_ref[...]

    def s6(h):
        lg = st[h].pop("lg")
        lane_r = lax.broadcasted_iota(jnp.int32, lg.shape, 1)
        lane_f = lane_r.astype(_F32)
        tops, hots = [], []
        for _k in range(TOP_K):
            mx = jnp.max(lg, axis=-1, keepdims=True)
            pick = jnp.min(jnp.where(lg == mx, lane_f, float(LANES)), axis=-1, keepdims=True)
            hot = lane_f == pick
            tops.append((mx, pick))
            hots.append(hot)
            lg = jnp.where(hot, 2.0 * NEG_BIG, lg)
        es = [jnp.exp(t[0] - tops[0][0]) for t in tops]
        esum = es[0] + es[1] + es[2] + es[3]
        multi = jnp.zeros(lg.shape, _F32)
        for hot in hots:
            multi = multi + jnp.where(hot, 1.0, 0.0)
        cum = jnp.dot(tri_ref[...], multi.astype(_BF16), preferred_element_type=_F32) + carry[0]
        row_r = lax.broadcasted_iota(jnp.int32, lg.shape, 0)
        lane_base = (row_r & (tok_per_row - 1)) * TOP_K
        gate_o = jnp.zeros(lg.shape, _F32)
        idx_e = jnp.zeros(lg.shape, _F32)
        rank_hi_e = jnp.zeros(lg.shape, _F32)
        rank_lo_e = jnp.zeros(lg.shape, _F32)
        for kk in range(TOP_K):
            rk = jnp.sum(jnp.where(hots[kk], cum, 0.0), axis=-1, keepdims=True)
            rk_hi = jnp.floor(rk * (1.0 / 256.0))
            here = lane_r == lane_base + kk
            gate_o = jnp.where(lane_r == kk, es[kk] / esum, gate_o)
            idx_e = jnp.where(here, tops[kk][1], idx_e)
            rank_hi_e = jnp.where(here, rk_hi, rank_hi_e)
            rank_lo_e = jnp.where(here, rk - 256.0 * rk_hi, rank_lo_e)
        fr = lax.broadcasted_iota(jnp.int32, (ts // tok_per_row, ts), 0)
        fc = lax.broadcasted_iota(jnp.int32, (ts // tok_per_row, ts), 1)
        fold = jnp.where(lax.shift_right_logical(fc, tok_per_row.bit_length() - 1) == fr, 1.0, 0.0).astype(_BF16)
        fs = slice(h * ts // tok_per_row, (h + 1) * ts // tok_per_row)
        idx_ref[fs, :] = jnp.dot(fold, idx_e.astype(_BF16), preferred_element_type=_F32).astype(jnp.int32)
        rank_ref[fs, :] = (256.0 * jnp.dot(fold, rank_hi_e.astype(_BF16), preferred_element_type=_F32)
                           + jnp.dot(fold, rank_lo_e.astype(_BF16), preferred_element_type=_F32)).astype(jnp.int32)
        gate_ref[rows(h), :] = gate_o
        carry[0] = carry[0] + jnp.sum(multi, axis=0, keepdims=True)

    order = sorted(((k + 2.5 * h, h, k) for h in range(n_sub) for k in range(6)))
    stages = (s1, s2, s3, s4, s5, s6)
    for _, h, k in order:
        stages[k](h)
    carry_ref[...] = carry[0]
    cnt_ref[...] = carry[0].astype(jnp.int32)


def _post_call(x2, ao, su, sv, mod3, lng, lnb, ws2, bsf, ga, gs, woa, wos, bo, gpost, gpre,
               wr, br, tri):
    tm = POST_TILE
    tiles_per_batch = SEQ // tm
    row = lambda i: (i, 0)
    const = lambda i: (0, 0)
    const3 = lambda i: (0, 0, 0)
    return pl.pallas_call(
        _post_kernel,
        out_shape=(jax.ShapeDtypeStruct((TOKENS, D_MODEL), _F32),
                   jax.ShapeDtypeStruct((TOKENS * ROW_SUB, LANES), _F32),
                   jax.ShapeDtypeStruct((TOKENS * TOP_K // LANES, LANES), jnp.int32),
                   jax.ShapeDtypeStruct((TOKENS, LANES), _F32),
                   jax.ShapeDtypeStruct((TOKENS * TOP_K // LANES, LANES), jnp.int32),
                   jax.ShapeDtypeStruct((1, LANES), jnp.int32)),
        grid=(TOKENS // tm,),
        in_specs=[pl.BlockSpec((tm, D_MODEL), row),
                  pl.BlockSpec((tm, ATTN_WIDTH), row),
                  pl.BlockSpec((tm, SGU_WIDTH), row),
                  pl.BlockSpec((tm, SGU_WIDTH), row),
                  pl.BlockSpec((1, N_MOD, D_MODEL), lambda i: (i // tiles_per_batch, 0, 0)),
                  pl.BlockSpec((1, SGU_WIDTH), const),
                  pl.BlockSpec((1, SGU_WIDTH), const),
                  pl.BlockSpec((SGU_WIDTH // LANES, 2 * SGU_CHUNK, SGU_CHUNK), const3),
                  pl.BlockSpec((SGU_CHUNK, SGU_WIDTH), const),
                  pl.BlockSpec((1, ATTN_WIDTH), const),
                  pl.BlockSpec((1, SGU_WIDTH), const),
                  pl.BlockSpec((ATTN_WIDTH, D_MODEL), const),
                  pl.BlockSpec((SGU_WIDTH, D_MODEL), const),
                  pl.BlockSpec((1, D_MODEL), const),
                  pl.BlockSpec((1, D_MODEL), const),
                  pl.BlockSpec((1, D_MODEL), const),
                  pl.BlockSpec((D_MODEL, LANES), const),
                  pl.BlockSpec((1, LANES), const),
                  pl.BlockSpec((tm // POST_SPLIT, tm // POST_SPLIT), const)],
        out_specs=(pl.BlockSpec((tm, D_MODEL), row),
                   pl.BlockSpec((tm * ROW_SUB, LANES), row),
                   pl.BlockSpec((tm * TOP_K // LANES, LANES), row),
                   pl.BlockSpec((tm, LANES), row),
                   pl.BlockSpec((tm * TOP_K // LANES, LANES), row),
                   pl.BlockSpec((1, LANES), const)),
        scratch_shapes=[pltpu.VMEM((tm, SGU_WIDTH), _F32),
                        pltpu.VMEM((1, LANES), _F32)],
        compiler_params=pltpu.CompilerParams(dimension_semantics=("arbitrary",),
                                             vmem_limit_bytes=VMEM_LIMIT),
        name="post",
    )(x2, ao, su, sv, mod3, lng, lnb, ws2, bsf, ga, gs, woa, wos, bo, gpost, gpre, wr, br, tri)


def _dispatch_kernel(fs_ref, fl_ref, nu_ref, dest_ref, h2_ref, xs_ref, zero_ref, sem, zsem):
    tm = h2_ref.shape[0] // ROW_SUB
    n_token_steps = TOKENS // tm
    i = pl.program_id(0)

    @pl.when(i < n_token_steps)
    def _():
        def issue(r, carry):
            for kk in range(TOP_K):
                d = pl.multiple_of(dest_ref[r * TOP_K + kk] * ROW_SUB, ROW_SUB)
                pltpu.make_async_copy(h2_ref.at[pl.ds(pl.multiple_of(r * ROW_SUB, ROW_SUB), ROW_SUB)],
                                      xs_ref.at[pl.ds(d, ROW_SUB)], sem).start(priority=kk % 2)
            return carry

        lax.fori_loop(0, tm, issue, 0, unroll=8)
        for kk in range(TOP_K):
            pltpu.make_async_copy(h2_ref, xs_ref.at[pl.ds(0, tm * ROW_SUB)], sem).wait()

    @pl.when(i == n_token_steps)
    def _():
        zero_ref[...] = jnp.zeros_like(zero_ref)
        block_rows = EXPERT_BLOCK * ROW_SUB

        def pad_run(e, wait):
            pos = fs_ref[e]
            length = fl_ref[e]
            for bit in reversed(range(EXPERT_BLOCK.bit_length() - 1)):
                size = 1 << bit
                take = length & size

                @pl.when(take != 0)
                def _():
                    cp = pltpu.make_async_copy(
                        zero_ref.at[pl.ds(0, size * ROW_SUB)],
                        xs_ref.at[pl.ds(pl.multiple_of(pos * ROW_SUB, ROW_SUB), size * ROW_SUB)], zsem)
                    if wait:
                        cp.wait()
                    else:
                        cp.start()

                pos = pos + take

        def tail_block(blk, wait):
            cp = pltpu.make_async_copy(
                zero_ref, xs_ref.at[pl.ds(pl.multiple_of(blk * block_rows, block_rows), block_rows)], zsem)
            if wait:
                cp.wait()
            else:
                cp.start()

        for wait in (False, True):
            lax.fori_loop(0, N_EXPERTS, lambda e, c, w=wait: (pad_run(e, w), c)[1], 0)
            lax.fori_loop(nu_ref[0], N_EXPERT_BLOCKS, lambda blk, c, w=wait: (tail_block(blk, w), c)[1], 0)


def _dispatch_call(fill_start, fill_len, n_used, dest_flat, h2):
    tm = SCATTER_TILE
    n_token_steps = TOKENS // tm
    grid_spec = pltpu.PrefetchScalarGridSpec(
        num_scalar_prefetch=3,
        grid=(n_token_steps + 1,),
        in_specs=[pl.BlockSpec((tm * TOP_K,), lambda i, fs, fl, nu: (jnp.minimum(i, n_token_steps - 1),),
                               memory_space=pltpu.SMEM),
                  pl.BlockSpec((tm * ROW_SUB, LANES),
                               lambda i, fs, fl, nu: (jnp.minimum(i, n_token_steps - 1), 0))],
        out_specs=pl.BlockSpec(memory_space=pl.ANY),
        scratch_shapes=[pltpu.VMEM((EXPERT_BLOCK * ROW_SUB, LANES), _F32),
                        pltpu.SemaphoreType.DMA,
                        pltpu.SemaphoreType.DMA])
    return pl.pallas_call(
        _dispatch_kernel,
        out_shape=jax.ShapeDtypeStruct((SORTED_ROWS * ROW_SUB, LANES), _F32),
        grid_spec=grid_spec,
        compiler_params=pltpu.CompilerParams(dimension_semantics=("arbitrary",),
                                             vmem_limit_bytes=VMEM_LIMIT),
        name="dispatch",
    )(fill_start, fill_len, n_used, dest_flat, h2)


def _expert_kernel(be_ref, slot_ref, first_ref, nxt_ref, nu_ref,
                   xs_ref, wgu_hbm, bgu0_ref, bgu1_ref, wd_hbm, bd0_ref, bd1_ref, y_ref,
                   wgu_stage, wd_stage, wgu_bf, wd_bf, sem):
    p = pl.program_id(0)
    half = EXPERT_BLOCK * ROW_SUB
    n_pairs_used = (nu_ref[0] + 1) // 2

    def stage_copies(e):
        return (pltpu.make_async_copy(wgu_hbm.at[e], wgu_stage, sem.at[0]),
                pltpu.make_async_copy(wd_hbm.at[e], wd_stage, sem.at[1]))

    @pl.when(p == 0)
    def _():
        for cp in stage_copies(be_ref[0]):
            cp.start()

    for b in (2 * p, 2 * p + 1):
        @pl.when((first_ref[b] == 1) & (b < nu_ref[0]))
        def _():
            for cp in stage_copies(be_ref[b]):
                cp.wait()
            wgu_bf[slot_ref[b]] = wgu_stage[...].astype(_BF16)
            wd_bf[slot_ref[b]] = wd_stage[...].astype(_BF16)

            @pl.when(nxt_ref[b] < N_EXPERTS)
            def _():
                for cp in stage_copies(nxt_ref[b]):
                    cp.start()

    def ffn(x, slot, bgu_ref, bd_ref):
        gu = jnp.dot(x, wgu_bf[slot], preferred_element_type=_F32) + bgu_ref[0]
        gate = jnp.minimum(gu[:, 0:D_FF_EXPERT], SWIGLU_LIMIT)
        up = jnp.clip(gu[:, D_FF_EXPERT:], -SWIGLU_LIMIT, SWIGLU_LIMIT)
        act = (up + 1.0) * gate * (1.0 / (1.0 + jnp.exp(-SWIGLU_ALPHA * gate)))
        return jnp.dot(act.astype(_BF16), wd_bf[slot], preferred_element_type=_F32) + bd_ref[0]

    def load_rows(which):
        return jnp.concatenate([xs_ref[pl.ds(which * half + cc, EXPERT_BLOCK, stride=ROW_SUB), :]
                                for cc in range(ROW_SUB)], axis=1).astype(_BF16)

    def store_rows(which, y):
        for cc in range(ROW_SUB):
            y_ref[pl.ds(which * half + cc, EXPERT_BLOCK, stride=ROW_SUB), :] = y[:, cc * LANES:(cc + 1) * LANES]

    @pl.when(p < n_pairs_used)
    def _():
        y0 = ffn(load_rows(0), slot_ref[2 * p], bgu0_ref, bd0_ref)
        x1 = load_rows(1)
        store_rows(0, y0)
        store_rows(1, ffn(x1, slot_ref[2 * p + 1], bgu1_ref, bd1_ref))

    @pl.when(p >= n_pairs_used)
    def _():
        y_ref[...] = jnp.zeros_like(y_ref)


def _expert_call(block_expert, block_slot, block_first, block_next, n_used, xs, wgu, bgu3, wd, bd3):
    tb = 2 * EXPERT_BLOCK
    live = lambda p, be, sl, fi, nx, nu: (jnp.minimum(p, (nu[0] - 1) // 2), 0)
    bsel0 = lambda p, be, sl, fi, nx, nu: (be[2 * p], 0, 0)
    bsel1 = lambda p, be, sl, fi, nx, nu: (be[2 * p + 1], 0, 0)
    grid_spec = pltpu.PrefetchScalarGridSpec(
        num_scalar_prefetch=5,
        grid=(N_EXPERT_BLOCKS // 2,),
        in_specs=[pl.BlockSpec((tb * ROW_SUB, LANES), live),
                  pl.BlockSpec(memory_space=pl.ANY),
                  pl.BlockSpec((1, 1, 2 * D_FF_EXPERT), bsel0),
                  pl.BlockSpec((1, 1, 2 * D_FF_EXPERT), bsel1),
                  pl.BlockSpec(memory_space=pl.ANY),
                  pl.BlockSpec((1, 1, D_MODEL), bsel0),
                  pl.BlockSpec((1, 1, D_MODEL), bsel1)],
        out_specs=pl.BlockSpec((tb * ROW_SUB, LANES), lambda p, be, sl, fi, nx, nu: (p, 0)),
        scratch_shapes=[pltpu.VMEM((D_MODEL, 2 * D_FF_EXPERT), _F32),
                        pltpu.VMEM((D_FF_EXPERT, D_MODEL), _F32),
                        pltpu.VMEM((2, D_MODEL, 2 * D_FF_EXPERT), _BF16),
                        pltpu.VMEM((2, D_FF_EXPERT, D_MODEL), _BF16),
                        pltpu.SemaphoreType.DMA((2,))])
    return pl.pallas_call(
        _expert_kernel,
        out_shape=jax.ShapeDtypeStruct((SORTED_ROWS * ROW_SUB, LANES), _F32),
        grid_spec=grid_spec,
        compiler_params=pltpu.CompilerParams(dimension_semantics=("arbitrary",),
                                             vmem_limit_bytes=VMEM_LIMIT),
        name="expert",
    )(block_expert, block_slot, block_first, block_next, n_used, xs, wgu, bgu3, bgu3, wd, bd3, bd3)


def _combine_kernel(dest_ref, dest_next_ref, gate_ref, xmid_ref, mod_ref, g_ref, y_ref, o_ref,
                    buf_ref, sem):
    tm = xmid_ref.shape[0]
    m = mod_ref[0]
    i = pl.program_id(0)
    n = pl.num_programs(0)

    def gather_rows(idx_ref, slot):
        def issue(r, carry):
            for kk in range(TOP_K):
                d = pl.multiple_of(idx_ref[r * TOP_K + kk] * ROW_SUB, ROW_SUB)
                pltpu.make_async_copy(
                    y_ref.at[pl.ds(d, ROW_SUB)],
                    buf_ref.at[slot, kk, pl.ds(pl.multiple_of(r * ROW_SUB, ROW_SUB), ROW_SUB)],
                    sem.at[slot]).start(priority=kk % 2)
            return carry

        lax.fori_loop(0, tm, issue, 0, unroll=8)

    @pl.when(i == 0)
    def _():
        gather_rows(dest_ref, 0)

    @pl.when(i + 1 < n)
    def _():
        gather_rows(dest_next_ref, (i + 1) % 2)

    slot = i % 2
    for kk in range(TOP_K):
        pltpu.make_async_copy(y_ref.at[pl.ds(0, tm * ROW_SUB)], buf_ref.at[slot, kk],
                              sem.at[slot]).wait()
    g = gate_ref[...]
    pieces = []
    for cc in range(ROW_SUB):
        piece = buf_ref[slot, 0, pl.ds(cc, tm, stride=ROW_SUB), :] * g[:, 0:1]
        for kk in range(1, TOP_K):
            piece = piece + buf_ref[slot, kk, pl.ds(cc, tm, stride=ROW_SUB), :] * g[:, kk:kk + 1]
        pieces.append(piece)
    ffn = jnp.concatenate(pieces, axis=1)
    o_ref[...] = xmid_ref[...] + m[5:6] * _rms(ffn, g_ref[...])


def _combine_call(dest_flat, gates, x_mid, mod3, gpost_ffn, y_rows):
    tm = MOVE_TILE
    tiles_per_batch = SEQ // tm
    n_steps = TOKENS // tm
    return pl.pallas_call(
        _combine_kernel,
        out_shape=jax.ShapeDtypeStruct((TOKENS, D_MODEL), _F32),
        grid=(n_steps,),
        in_specs=[pl.BlockSpec((tm * TOP_K,), lambda i: (i,), memory_space=pltpu.SMEM),
                  pl.BlockSpec((tm * TOP_K,), lambda i: (jnp.minimum(i + 1, n_steps - 1),),
                               memory_space=pltpu.SMEM),
                  pl.BlockSpec((tm, LANES), lambda i: (i, 0)),
                  pl.BlockSpec((tm, D_MODEL), lambda i: (i, 0)),
                  pl.BlockSpec((1, N_MOD, D_MODEL), lambda i: (i // tiles_per_batch, 0, 0)),
                  pl.BlockSpec((1, D_MODEL), lambda i: (0, 0)),
                  pl.BlockSpec(memory_space=pl.ANY)],
        out_specs=pl.BlockSpec((tm, D_MODEL), lambda i: (i, 0)),
        scratch_shapes=[pltpu.VMEM((2, TOP_K, tm * ROW_SUB, LANES), _F32),
                        pltpu.SemaphoreType.DMA((2,))],
        compiler_params=pltpu.CompilerParams(dimension_semantics=("arbitrary",),
                                             vmem_limit_bytes=VMEM_LIMIT),
        name="combine",
    )(dest_flat, dest_flat, gates, x_mid, mod3, gpost_ffn, y_rows)


def _rope_tables():
    pos = np.arange(SEQ)
    pos_row = (pos // GRID_W).astype(np.float64)
    pos_col = (pos % GRID_W).astype(np.float64)
    n_freq = HEAD_DIM // 4
    inv_freq = ROPE_THETA ** (-np.arange(n_freq, dtype=np.float64) / n_freq)
    d = np.arange(LANES) % HEAD_DIM
    f = inv_freq[d % n_freq]
    ang = np.where((d < HEAD_DIM // 2)[None, :], pos_row[:, None] * f[None, :], pos_col[:, None] * f[None, :])
    sign = np.where((d % (HEAD_DIM // 2)) < n_freq, -1.0, 1.0)
    return (jnp.asarray(np.cos(ang), dtype=_F32), jnp.asarray(np.sin(ang) * sign[None, :], dtype=_F32))


def _perm_heads(a, axis):
    shape = a.shape
    a = a.reshape(shape[:axis] + (N_HEADS, HEAD_DIM) + shape[axis + 1:])
    a = jnp.take(a, jnp.array(HEAD_PERM), axis=axis)
    return a.reshape(shape)


def kernel(x, c, ctx, c_ctx, w_ada, b_ada, g_pre_mix, g_post_mix, g_pre_ffn, g_post_ffn, w_in, b_in, attn_sink, sgu_ln_g, sgu_ln_b, sgu_w, sgu_b, g_attn_out, g_sgu_out, w_out, b_out, w_router, b_router, w_gate_up, b_gate_up, w_down, b_down):
    l = 0
    x2 = x.reshape(TOKENS, D_MODEL)
    ctx2 = ctx.reshape(BATCH * CTX_LEN, D_MODEL)

    cc = jnp.zeros((16, D_MODEL), _F32).at[:BATCH].set(c).at[BATCH].set(c_ctx)
    mod = _ada_call(cc, w_ada[l], b_ada[l].reshape(1, -1))
    mod3 = mod.reshape(16, N_MOD, D_MODEL)

    w_in_l = w_in[l]
    b_in_l = b_in[l]
    w_in_p = jnp.concatenate([_perm_heads(w_in_l[:, :ATTN_WIDTH], 1), w_in_l[:, ATTN_WIDTH:]], axis=1)
    b_in_p = jnp.concatenate([_perm_heads(b_in_l[:ATTN_WIDTH], 0), b_in_l[ATTN_WIDTH:]], axis=0)
    cos_t, sin_t = _rope_tables()
    g_pre = g_pre_mix[l].reshape(1, -1)

    q, k, v, su, sv = _inproj_call(x2, mod3, g_pre, w_in_p.astype(_BF16), b_in_p.reshape(1, -1), cos_t, sin_t)
    kc, vc = _ctxproj_call(ctx2, mod3, g_pre, w_in_l[:, KV_START:KV_END].astype(_BF16),
                           b_in_l[KV_START:KV_END].reshape(1, -1))

    sink_p = jnp.take(attn_sink[l], jnp.array(HEAD_PERM)) * LOG2_E
    ao = _attn_call(sink_p, q, k, v, kc, vc)

    ws2 = sgu_w[l].reshape(SGU_WIDTH // LANES, 2 * SGU_CHUNK, SGU_CHUNK).astype(_BF16)
    bsf = jnp.repeat(sgu_b[l].T, SGU_HEAD_DIM, axis=1)
    w_out_l = w_out[l]
    woa = _perm_heads(w_out_l[:ATTN_WIDTH], 0).astype(_BF16)
    wos = w_out_l[ATTN_WIDTH:].astype(_BF16)
    ga = _perm_heads(g_attn_out[l], 0).reshape(1, -1)
    wr_hi = w_router[l].astype(_BF16)
    wr_lo = (w_router[l] - wr_hi.astype(_F32)).astype(_BF16)
    wr = (jnp.zeros((D_MODEL, LANES), _BF16).at[:, :N_EXPERTS].set(wr_hi)
          .at[:, N_EXPERTS:2 * N_EXPERTS].set(wr_lo))
    br = jnp.full((1, LANES), NEG_BIG, _F32).at[0, :N_EXPERTS].set(b_router[l])
    ii = np.arange(POST_TILE // POST_SPLIT)
    tri = jnp.asarray(ii[None, :] < ii[:, None], dtype=_BF16)

    x_mid, h2, idx_o, gate_o, rank_o, cnt_o = _post_call(
        x2, ao, su, sv, mod3, sgu_ln_g[l].reshape(1, -1), sgu_ln_b[l].reshape(1, -1), ws2, bsf,
        ga, g_sgu_out[l].reshape(1, -1), woa, wos, b_out[l].reshape(1, -1),
        g_post_mix[l].reshape(1, -1), g_pre_ffn[l].reshape(1, -1), wr, br, tri)

    counts = cnt_o[0, :N_EXPERTS]
    padded = (counts + EXPERT_BLOCK - 1) // EXPERT_BLOCK * EXPERT_BLOCK
    pad_end = jnp.cumsum(padded)
    pad_start = pad_end - padded
    dest = rank_o
    for e in range(N_EXPERTS):
        dest = dest + jnp.where(idx_o == e, pad_start[e], 0)
    dest = dest.astype(jnp.int32).reshape(-1)
    n_used = (pad_end[-1:] // EXPERT_BLOCK).astype(jnp.int32)

    xs = _dispatch_call((pad_start + counts).astype(jnp.int32), (padded - counts).astype(jnp.int32),
                        n_used, dest, h2)

    e_ids = jnp.arange(N_EXPERTS, dtype=jnp.int32)
    used = padded > 0
    cand = jnp.where(used, e_ids, N_EXPERTS)
    next_used = jnp.min(jnp.where(e_ids[None, :] > e_ids[:, None], cand[None, :], N_EXPERTS), axis=1)
    ordinal = jnp.cumsum(used.astype(jnp.int32)) - 1
    block_id = jnp.arange(N_EXPERT_BLOCKS, dtype=jnp.int32)
    block_start = block_id * EXPERT_BLOCK
    sched_start = jnp.minimum(block_start, (n_used[0] - 1) * EXPERT_BLOCK)
    sched_expert = jnp.minimum(jnp.sum(pad_end[None, :] <= sched_start[:, None], axis=1),
                               N_EXPERTS - 1).astype(jnp.int32)
    onehot = e_ids[None, :] == sched_expert[:, None]
    pick = lambda table: jnp.sum(jnp.where(onehot, table[None, :], 0), axis=1)
    block_slot = (pick(ordinal) & 1).astype(jnp.int32)
    block_first = ((block_start == pick(pad_start)) & (block_id < n_used[0])).astype(jnp.int32)
    block_next = pick(next_used).astype(jnp.int32)
    y_rows = _expert_call(sched_expert, block_slot, block_first, block_next, n_used, xs,
                          w_gate_up[l], b_gate_up[l].reshape(N_EXPERTS, 1, -1),
                          w_down[l], b_down[l].reshape(N_EXPERTS, 1, -1))
    out = _combine_call(dest, gate_o, x_mid, mod3, g_post_ffn[l].reshape(1, -1), y_rows)
    return out.reshape(BATCH, SEQ, D_MODEL)
```

```python
import jax
import jax.numpy as jnp
import numpy as np
from jax import lax
from jax.experimental import pallas as pl
from jax.experimental.pallas import tpu as pltpu

D_MODEL = 1024
BATCH = 8
SEQ = 4096
TOKENS = BATCH * SEQ
GRID_W = 64
CTX_LEN = 256
N_HEADS = 8
N_KV_HEADS = 2
HEAD_DIM = 64
ATTN_WIDTH = N_HEADS * HEAD_DIM
KV_WIDTH = N_KV_HEADS * HEAD_DIM
WINDOW = 128
ATTN_BLOCK = 128
SGU_HEADS = 8
SGU_HEAD_DIM = 64
SGU_WIDTH = SGU_HEADS * SGU_HEAD_DIM
SGU_CHUNK = 128
KV_START = ATTN_WIDTH
KV_END = ATTN_WIDTH + 2 * KV_WIDTH
IN_WIDTH = KV_END + 2 * SGU_WIDTH
N_EXPERTS = 32
TOP_K = 4
D_FF_EXPERT = 1024
SWIGLU_LIMIT = 7.0
SWIGLU_ALPHA = 1.702
ROPE_THETA = 10000.0
EPS = 1e-6
N_MOD = 6

LANES = 128
NEG_BIG = -1e30
VMEM_LIMIT = 56 * 1024 * 1024

ROW_TILE = 1024
SCATTER_TILE = 2048
MOVE_TILE = 512
ATTN_STEP_BLOCKS = 8
EXPERT_BLOCK = 512
POST_TILE = 1024
POST_SPLIT = 4
N_EXPERT_BLOCKS = TOKENS * TOP_K // EXPERT_BLOCK + N_EXPERTS
SORTED_ROWS = N_EXPERT_BLOCKS * EXPERT_BLOCK
HEAD_PERM = (0, 4, 1, 5, 2, 6, 3, 7)

LOG2_E = 1.4426950408889634
QK_SCALE = LOG2_E * HEAD_DIM ** -0.5

_F32 = jnp.float32
_BF16 = jnp.bfloat16


def _rms(x, g):
    ms = jnp.mean(x * x, axis=-1, keepdims=True)
    return x * lax.rsqrt(ms + EPS) * g


ROW_SUB = D_MODEL // LANES


def _store_row_tiles(ref, x):
    n = x.shape[0]
    for cc in range(ROW_SUB):
        ref[pl.ds(cc, n, stride=ROW_SUB), :] = x[:, cc * LANES:(cc + 1) * LANES]


def _load_row_tiles(ref, n):
    return jnp.concatenate([ref[pl.ds(cc, n, stride=ROW_SUB), :] for cc in range(ROW_SUB)], axis=1)


def _gelu_tanh(x):
    c = 0.7978845608028654
    return x * (0.5 * (1.0 + jnp.tanh(c * (x + 0.044715 * (x * x * x)))))


def _ada_kernel(c_ref, w_ref, b_ref, o_ref):
    c = c_ref[...]
    a = c / (1.0 + jnp.exp(-c))
    o_ref[...] = jnp.dot(a, w_ref[...], preferred_element_type=_F32,
                         precision=lax.Precision.HIGHEST) + b_ref[...]


def _ada_call(cc, w_ada, b_ada):
    n = N_MOD * D_MODEL
    tn = 512
    return pl.pallas_call(
        _ada_kernel,
        out_shape=jax.ShapeDtypeStruct((16, n), _F32),
        grid=(n // tn,),
        in_specs=[pl.BlockSpec((16, D_MODEL), lambda j: (0, 0)),
                  pl.BlockSpec((D_MODEL, tn), lambda j: (0, j)),
                  pl.BlockSpec((1, tn), lambda j: (0, j))],
        out_specs=pl.BlockSpec((16, tn), lambda j: (0, j)),
        compiler_params=pltpu.CompilerParams(dimension_semantics=("arbitrary",),
                                             vmem_limit_bytes=VMEM_LIMIT),
        name="ada",
    )(cc, w_ada, b_ada)


def _rope(x, cos, sin_signed, first_half):
    nxt = pltpu.roll(x, LANES - 16, 1)
    prv = pltpu.roll(x, 16, 1)
    return x * cos + jnp.where(first_half, nxt, prv) * sin_signed


def _inproj_kernel(x_ref, mod_ref, g_ref, w_ref, b_ref, cos_ref, sin_ref,
                   q_ref, k_ref, v_ref, su_ref, sv_ref):
    m = mod_ref[0]
    h = _rms(x_ref[...], g_ref[...]) * (1.0 + m[1:2]) + m[0:1]
    hb = h.astype(_BF16)
    cos = cos_ref[...]
    sin = sin_ref[...]
    lane = lax.broadcasted_iota(jnp.int32, cos.shape, 1)
    first_half = (lane & 31) < 16

    q = jnp.dot(hb, w_ref[:, 0:ATTN_WIDTH], preferred_element_type=_F32) + b_ref[:, 0:ATTN_WIDTH]
    for j in range(ATTN_WIDTH // LANES):
        qj = _rope(q[:, j * LANES:(j + 1) * LANES], cos, sin, first_half)
        q_ref[:, j * LANES:(j + 1) * LANES] = (qj * QK_SCALE).astype(_BF16)
    kv = jnp.dot(hb, w_ref[:, KV_START:KV_END], preferred_element_type=_F32) + b_ref[:, KV_START:KV_END]
    k_ref[...] = _rope(kv[:, 0:KV_WIDTH], cos, sin, first_half).astype(_BF16)
    v_ref[...] = kv[:, KV_WIDTH:].astype(_BF16)
    u0 = KV_END
    u1 = KV_END + SGU_WIDTH
    su_ref[...] = jnp.dot(hb, w_ref[:, u0:u1], preferred_element_type=_F32) + b_ref[:, u0:u1]
    sv_ref[...] = jnp.dot(hb, w_ref[:, u1:IN_WIDTH], preferred_element_type=_F32) + b_ref[:, u1:IN_WIDTH]


def _inproj_call(x2, mod3, g_pre, w_in_b, b_in2, cos_t, sin_t):
    tm = ROW_TILE
    tiles_per_batch = SEQ // tm
    row = lambda i: (i, 0)
    const = lambda i: (0, 0)
    return pl.pallas_call(
        _inproj_kernel,
        out_shape=(jax.ShapeDtypeStruct((TOKENS, ATTN_WIDTH), _BF16),
                   jax.ShapeDtypeStruct((TOKENS, KV_WIDTH), _BF16),
                   jax.ShapeDtypeStruct((TOKENS, KV_WIDTH), _BF16),
                   jax.ShapeDtypeStruct((TOKENS, SGU_WIDTH), _F32),
                   jax.ShapeDtypeStruct((TOKENS, SGU_WIDTH), _F32)),
        grid=(TOKENS // tm,),
        in_specs=[pl.BlockSpec((tm, D_MODEL), row),
                  pl.BlockSpec((1, N_MOD, D_MODEL), lambda i: (i // tiles_per_batch, 0, 0)),
                  pl.BlockSpec((1, D_MODEL), const),
                  pl.BlockSpec((D_MODEL, IN_WIDTH), const),
                  pl.BlockSpec((1, IN_WIDTH), const),
                  pl.BlockSpec((tm, LANES), lambda i: (i % tiles_per_batch, 0)),
                  pl.BlockSpec((tm, LANES), lambda i: (i % tiles_per_batch, 0))],
        out_specs=(pl.BlockSpec((tm, ATTN_WIDTH), row),
                   pl.BlockSpec((tm, KV_WIDTH), row),
                   pl.BlockSpec((tm, KV_WIDTH), row),
                   pl.BlockSpec((tm, SGU_WIDTH), row),
                   pl.BlockSpec((tm, SGU_WIDTH), row)),
        compiler_params=pltpu.CompilerParams(dimension_semantics=("arbitrary",),
                                             vmem_limit_bytes=VMEM_LIMIT),
        name="inproj",
    )(x2, mod3, g_pre, w_in_b, b_in2, cos_t, sin_t)


def _ctxproj_kernel(x_ref, mod_ref, g_ref, w_ref, b_ref, k_ref, v_ref):
    m = mod_ref[0]
    h = _rms(x_ref[...], g_ref[...]) * (1.0 + m[1:2]) + m[0:1]
    kv = jnp.dot(h.astype(_BF16), w_ref[...], preferred_element_type=_F32) + b_ref[...]
    k_ref[...] = kv[:, 0:KV_WIDTH].astype(_BF16)
    v_ref[...] = kv[:, KV_WIDTH:].astype(_BF16)


def _ctxproj_call(ctx2, mod3, g_pre, w_kv_b, b_kv2):
    tm = ROW_TILE
    rows = ctx2.shape[0]
    row = lambda i: (i, 0)
    const = lambda i: (0, 0)
    return pl.pallas_call(
        _ctxproj_kernel,
        out_shape=(jax.ShapeDtypeStruct((rows, KV_WIDTH), _BF16),
                   jax.ShapeDtypeStruct((rows, KV_WIDTH), _BF16)),
        grid=(rows // tm,),
        in_specs=[pl.BlockSpec((tm, D_MODEL), row),
                  pl.BlockSpec((1, N_MOD, D_MODEL), lambda i: (BATCH, 0, 0)),
                  pl.BlockSpec((1, D_MODEL), const),
                  pl.BlockSpec((D_MODEL, 2 * KV_WIDTH), const),
                  pl.BlockSpec((1, 2 * KV_WIDTH), const)],
        out_specs=(pl.BlockSpec((tm, KV_WIDTH), row),
                   pl.BlockSpec((tm, KV_WIDTH), row)),
        compiler_params=pltpu.CompilerParams(dimension_semantics=("arbitrary",),
                                             vmem_limit_bytes=VMEM_LIMIT),
        name="ctxproj",
    )(ctx2, mod3, g_pre, w_kv_b, b_kv2)


def _attn_kernel(sink_ref, q_ref, kp_ref, km_ref, kn_ref, vp_ref, vm_ref, vn_ref,
                 kc_ref, vc_ref, o_ref):
    n = pl.program_id(1)
    nstep = pl.num_programs(1)
    blk = ATTN_BLOCK
    nq = ATTN_STEP_BLOCKS
    rows2 = 2 * blk
    r = lax.broadcasted_iota(jnp.int32, (rows2, blk), 0) & (blk - 1)
    c = lax.broadcasted_iota(jnp.int32, (rows2, blk), 1)
    tri_prev = c >= r
    tri_next = c <= r
    row1 = lax.broadcasted_iota(jnp.int32, (rows2, 1), 0)
    lane_q = lax.broadcasted_iota(jnp.int32, (blk, LANES), 1)
    n_grp = ATTN_WIDTH // LANES
    k_blocks = [kp_ref[...]] + [km_ref[t * blk:(t + 1) * blk, :] for t in range(nq)] + [kn_ref[...]]
    v_blocks = [vp_ref[...]] + [vm_ref[t * blk:(t + 1) * blk, :] for t in range(nq)] + [vn_ref[...]]

    def scores(t):
        keys = jnp.concatenate(k_blocks[t:t + 3] + [kc_ref[...]], axis=0)
        q_rows = []
        for j in range(n_grp):
            qg = q_ref[t * blk:(t + 1) * blk, j * LANES:(j + 1) * LANES]
            zero = jnp.zeros_like(qg)
            q_rows += [jnp.where(lane_q < HEAD_DIM, qg, zero), jnp.where(lane_q >= HEAD_DIM, qg, zero)]
        return lax.dot_general(jnp.concatenate(q_rows, axis=0), keys, (((1,), (1,)), ((), ())),
                               preferred_element_type=_F32)

    def softmax(t, s_all):
        mask_a = (tri_prev & (n > 0)) if t == 0 else tri_prev
        mask_c = (tri_next & (n < nstep - 1)) if t == nq - 1 else tri_next
        p_rows, denoms = [], []
        for j in range(n_grp):
            s = s_all[j * rows2:(j + 1) * rows2]
            s = jnp.concatenate([jnp.where(mask_a, s[:, 0:blk], NEG_BIG), s[:, blk:2 * blk],
                                 jnp.where(mask_c, s[:, 2 * blk:3 * blk], NEG_BIG), s[:, 3 * blk:]],
                                axis=1)
            sk = jnp.where(row1 < blk, sink_ref[2 * j], sink_ref[2 * j + 1])
            m = jnp.maximum(jnp.max(s, axis=-1, keepdims=True), sk)
            p = jnp.exp2(s - m)
            denoms.append(jnp.sum(p, axis=-1, keepdims=True) + jnp.exp2(sk - m))
            p_rows.append(p.astype(_BF16))
        return jnp.concatenate(p_rows, axis=0), denoms

    def weighted_values(t, p_all, denoms):
        vals = jnp.concatenate(v_blocks[t:t + 3] + [vc_ref[...]], axis=0)
        o_all = jnp.dot(p_all, vals, preferred_element_type=_F32)
        for j in range(n_grp):
            o2 = o_all[j * rows2:(j + 1) * rows2] / denoms[j]
            og = jnp.where(lane_q < HEAD_DIM, o2[0:blk], o2[blk:])
            o_ref[t * blk:(t + 1) * blk, j * LANES:(j + 1) * LANES] = og.astype(_BF16)

    s_next = scores(0)
    probs = None
    for t in range(nq + 1):
        s_cur, s_next = s_next, (scores(t + 1) if t + 1 < nq else None)
        if probs is not None:
            weighted_values(t - 1, *probs)
        probs = softmax(t, s_cur) if t < nq else None


def _attn_call(sink_p, q, k, v, kc, vc):
    nblk = SEQ // ATTN_BLOCK
    nq = ATTN_STEP_BLOCKS
    nstep = nblk // nq
    own = lambda b, n: (b * nstep + n, 0)
    prev = lambda b, n: (b * nblk + jnp.maximum(nq * n - 1, 0), 0)
    nxt = lambda b, n: (b * nblk + jnp.minimum(nq * n + nq, nblk - 1), 0)
    ctx = lambda b, n: (b, 0)
    kv1 = (ATTN_BLOCK, KV_WIDTH)
    kvm = (nq * ATTN_BLOCK, KV_WIDTH)
    return pl.pallas_call(
        _attn_kernel,
        out_shape=jax.ShapeDtypeStruct((TOKENS, ATTN_WIDTH), _BF16),
        grid=(BATCH, nstep),
        in_specs=[pl.BlockSpec(memory_space=pltpu.SMEM),
                  pl.BlockSpec((nq * ATTN_BLOCK, ATTN_WIDTH), own),
                  pl.BlockSpec(kv1, prev), pl.BlockSpec(kvm, own), pl.BlockSpec(kv1, nxt),
                  pl.BlockSpec(kv1, prev), pl.BlockSpec(kvm, own), pl.BlockSpec(kv1, nxt),
                  pl.BlockSpec((CTX_LEN, KV_WIDTH), ctx),
                  pl.BlockSpec((CTX_LEN, KV_WIDTH), ctx)],
        out_specs=pl.BlockSpec((nq * ATTN_BLOCK, ATTN_WIDTH), own),
        compiler_params=pltpu.CompilerParams(dimension_semantics=("arbitrary", "arbitrary"),
                                             vmem_limit_bytes=VMEM_LIMIT),
        name="attn",
    )(sink_p, q, k, k, k, v, v, v, kc, vc)


def _post_kernel(x_ref, ao_ref, su_ref, sv_ref, mod_ref, lng_ref, lnb_ref, ws_ref, bs_ref,
                 ga_ref, gs_ref, woa_ref, wos_ref, bo_ref, gpost_ref, gpre_ref, wr_ref, br_ref,
                 tri_ref,
                 xmid_ref, h2_ref, idx_ref, gate_ref, rank_ref, cnt_ref,
                 mixed_ref, carry_ref):
    tm = x_ref.shape[0]
    m = mod_ref[0]

    @pl.when(pl.program_id(0) == 0)
    def _():
        carry_ref[...] = jnp.zeros_like(carry_ref)

    n_sub = POST_SPLIT
    ts = tm // n_sub
    tok_per_row = LANES // TOP_K
    lane = lax.broadcasted_iota(jnp.int32, (SGU_CHUNK, LANES), 1)
    st = [dict() for _ in range(n_sub)]
    carry = [carry_ref[...]]

    def rows(h):
        return slice(h * ts, (h + 1) * ts)

    def s1(h):
        gv = _gelu_tanh(sv_ref[rows(h), :])
        mu = jnp.mean(gv, axis=-1, keepdims=True)
        gc = gv - mu
        var = jnp.mean(gc * gc, axis=-1, keepdims=True)
        st[h]["vb"] = (gc * lax.rsqrt(var + EPS) * lng_ref[...] + lnb_ref[...]).astype(_BF16)

    def s2(h):
        vb = st[h].pop("vb")
        for c in range(ts // SGU_CHUNK):
            r0 = c * SGU_CHUNK
            for p in range(SGU_WIDTH // LANES):
                l0 = p * LANES
                r = jnp.dot(ws_ref[p], vb[r0:r0 + SGU_CHUNK, l0:l0 + LANES], preferred_element_type=_F32)
                mixed = jnp.where(lane < SGU_HEAD_DIM, r[0:SGU_CHUNK], r[SGU_CHUNK:])
                mixed_ref[h * ts + r0:h * ts + r0 + SGU_CHUNK, l0:l0 + LANES] = mixed + bs_ref[:, l0:l0 + LANES]
        sgu_o = _gelu_tanh(su_ref[rows(h), :]) * mixed_ref[rows(h), :]
        st[h]["oa"] = _rms(ao_ref[rows(h), :].astype(_F32), ga_ref[...]).astype(_BF16)
        st[h]["os"] = _rms(sgu_o, gs_ref[...]).astype(_BF16)

    def s3(h):
        st[h]["mix"] = (jnp.dot(st[h].pop("oa"), woa_ref[...], preferred_element_type=_F32)
                        + jnp.dot(st[h].pop("os"), wos_ref[...], preferred_element_type=_F32) + bo_ref[...])

    def s4(h):
        x_mid = x_ref[rows(h), :] + m[2:3] * _rms(st[h].pop("mix"), gpost_ref[...])
        xmid_ref[rows(h), :] = x_mid
        h2 = _rms(x_mid, gpre_ref[...]) * (1.0 + m[4:5]) + m[3:4]
        for cc in range(ROW_SUB):
            h2_ref[pl.ds(h * ts * ROW_SUB + cc, ts, stride=ROW_SUB), :] = h2[:, cc * LANES:(cc + 1) * LANES]
        h_hi = h2.astype(_BF16)
        st[h]["h_hi"] = h_hi
        st[h]["h_lo"] = (h2 - h_hi.astype(_F32)).astype(_BF16)

    def s5(h):
        r = (jnp.dot(st[h].pop("h_hi"), wr_ref[...], preferred_element_type=_F32)
             + jnp.dot(st[h].pop("h_lo"), wr_ref[...], preferred_element_type=_F32))
        st[h]["lg"] = r + pltpu.roll(r, LANES - N_EXPERTS, 1) + br_ref[...]

    def s6(h):
        lg = st[h].pop("lg")
        lane_r = lax.broadcasted_iota(jnp.int32, lg.shape, 1)
        lane_f = lane_r.astype(_F32)
        tops, hots = [], []
        for _k in range(TOP_K):
            mx = jnp.max(lg, axis=-1, keepdims=True)
            pick = jnp.min(jnp.where(lg == mx, lane_f, float(LANES)), axis=-1, keepdims=True)
            hot = lane_f == pick
            tops.append((mx, pick))
            hots.append(hot)
            lg = jnp.where(hot, 2.0 * NEG_BIG, lg)
        es = [jnp.exp(t[0] - tops[0][0]) for t in tops]
        esum = es[0] + es[1] + es[2] + es[3]
        multi = jnp.zeros(lg.shape, _F32)
        for hot in hots:
            multi = multi + jnp.where(hot, 1.0, 0.0)
        cum = jnp.dot(tri_ref[...], multi.astype(_BF16), preferred_element_type=_F32) + carry[0]
        row_r = lax.broadcasted_iota(jnp.int32, lg.shape, 0)
        lane_base = (row_r & (tok_per_row - 1)) * TOP_K
        gate_o = jnp.zeros(lg.shape, _F32)
        idx_e = jnp.zeros(lg.shape, _F32)
        rank_hi_e = jnp.zeros(lg.shape, _F32)
        rank_lo_e = jnp.zeros(lg.shape, _F32)
        for kk in range(TOP_K):
            rk = jnp.sum(jnp.where(hots[kk], cum, 0.0), axis=-1, keepdims=True)
            rk_hi = jnp.floor(rk * (1.0 / 256.0))
            here = lane_r == lane_base + kk
            gate_o = jnp.where(lane_r == kk, es[kk] / esum, gate_o)
            idx_e = jnp.where(here, tops[kk][1], idx_e)
            rank_hi_e = jnp.where(here, rk_hi, rank_hi_e)
            rank_lo_e = jnp.where(here, rk - 256.0 * rk_hi, rank_lo_e)
        fr = lax.broadcasted_iota(jnp.int32, (ts // tok_per_row, ts), 0)
        fc = lax.broadcasted_iota(jnp.int32, (ts // tok_per_row, ts), 1)
        fold = jnp.where(lax.shift_right_logical(fc, tok_per_row.bit_length() - 1) == fr, 1.0, 0.0).astype(_BF16)
        fs = slice(h * ts // tok_per_row, (h + 1) * ts // tok_per_row)
        idx_ref[fs, :] = jnp.dot(fold, idx_e.astype(_BF16), preferred_element_type=_F32).astype(jnp.int32)
        rank_ref[fs, :] = (256.0 * jnp.dot(fold, rank_hi_e.astype(_BF16), preferred_element_type=_F32)
                           + jnp.dot(fold, rank_lo_e.astype(_BF16), preferred_element_type=_F32)).astype(jnp.int32)
        gate_ref[rows(h), :] = gate_o
        carry[0] = carry[0] + jnp.sum(multi, axis=0, keepdims=True)

    order = sorted(((k + 2.5 * h, h, k) for h in range(n_sub) for k in range(6)))
    stages = (s1, s2, s3, s4, s5, s6)
    for _, h, k in order:
        stages[k](h)
    carry_ref[...] = carry[0]
    cnt_ref[...] = carry[0].astype(jnp.int32)


def _post_call(x2, ao, su, sv, mod3, lng, lnb, ws2, bsf, ga, gs, woa, wos, bo, gpost, gpre,
               wr, br, tri):
    tm = POST_TILE
    tiles_per_batch = SEQ // tm
    row = lambda i: (i, 0)
    const = lambda i: (0, 0)
    const3 = lambda i: (0, 0, 0)
    return pl.pallas_call(
        _post_kernel,
        out_shape=(jax.ShapeDtypeStruct((TOKENS, D_MODEL), _F32),
                   jax.ShapeDtypeStruct((TOKENS * ROW_SUB, LANES), _F32),
                   jax.ShapeDtypeStruct((TOKENS * TOP_K // LANES, LANES), jnp.int32),
                   jax.ShapeDtypeStruct((TOKENS, LANES), _F32),
                   jax.ShapeDtypeStruct((TOKENS * TOP_K // LANES, LANES), jnp.int32),
                   jax.ShapeDtypeStruct((1, LANES), jnp.int32)),
        grid=(TOKENS // tm,),
        in_specs=[pl.BlockSpec((tm, D_MODEL), row),
                  pl.BlockSpec((tm, ATTN_WIDTH), row),
                  pl.BlockSpec((tm, SGU_WIDTH), row),
                  pl.BlockSpec((tm, SGU_WIDTH), row),
                  pl.BlockSpec((1, N_MOD, D_MODEL), lambda i: (i // tiles_per_batch, 0, 0)),
                  pl.BlockSpec((1, SGU_WIDTH), const),
                  pl.BlockSpec((1, SGU_WIDTH), const),
                  pl.BlockSpec((SGU_WIDTH // LANES, 2 * SGU_CHUNK, SGU_CHUNK), const3),
                  pl.BlockSpec((SGU_CHUNK, SGU_WIDTH), const),
                  pl.BlockSpec((1, ATTN_WIDTH), const),
                  pl.BlockSpec((1, SGU_WIDTH), const),
                  pl.BlockSpec((ATTN_WIDTH, D_MODEL), const),
                  pl.BlockSpec((SGU_WIDTH, D_MODEL), const),
                  pl.BlockSpec((1, D_MODEL), const),
                  pl.BlockSpec((1, D_MODEL), const),
                  pl.BlockSpec((1, D_MODEL), const),
                  pl.BlockSpec((D_MODEL, LANES), const),
                  pl.BlockSpec((1, LANES), const),
                  pl.BlockSpec((tm // POST_SPLIT, tm // POST_SPLIT), const)],
        out_specs=(pl.BlockSpec((tm, D_MODEL), row),
                   pl.BlockSpec((tm * ROW_SUB, LANES), row),
                   pl.BlockSpec((tm * TOP_K // LANES, LANES), row),
                   pl.BlockSpec((tm, LANES), row),
                   pl.BlockSpec((tm * TOP_K // LANES, LANES), row),
                   pl.BlockSpec((1, LANES), const)),
        scratch_shapes=[pltpu.VMEM((tm, SGU_WIDTH), _F32),
                        pltpu.VMEM((1, LANES), _F32)],
        compiler_params=pltpu.CompilerParams(dimension_semantics=("arbitrary",),
                                             vmem_limit_bytes=VMEM_LIMIT),
        name="post",
    )(x2, ao, su, sv, mod3, lng, lnb, ws2, bsf, ga, gs, woa, wos, bo, gpost, gpre, wr, br, tri)


def _dest_kernel(ps_ref, idx_ref, rank_ref, o_ref):
    idx = idx_ref[...]
    dest = rank_ref[...]
    for e in range(N_EXPERTS):
        dest = dest + jnp.where(idx == e, ps_ref[e], 0)
    o_ref[...] = dest


def _dest_call(pad_start, idx_flat, rank_flat):
    return pl.pallas_call(
        _dest_kernel,
        out_shape=jax.ShapeDtypeStruct(idx_flat.shape, jnp.int32),
        grid=(1,),
        in_specs=[pl.BlockSpec(memory_space=pltpu.SMEM),
                  pl.BlockSpec(idx_flat.shape, lambda i: (0, 0)),
                  pl.BlockSpec(rank_flat.shape, lambda i: (0, 0))],
        out_specs=pl.BlockSpec(idx_flat.shape, lambda i: (0, 0)),
        compiler_params=pltpu.CompilerParams(dimension_semantics=("arbitrary",),
                                             vmem_limit_bytes=VMEM_LIMIT),
        name="dest",
    )(pad_start, idx_flat, rank_flat)


def _dispatch_kernel(fs_ref, fl_ref, nu_ref, dest_ref, h2_ref, xs_ref, zero_ref, sem, zsem):
    tm = h2_ref.shape[0] // ROW_SUB
    n_token_steps = TOKENS // tm
    i = pl.program_id(0)

    @pl.when(i < n_token_steps)
    def _():
        def issue(r, carry):
            for kk in range(TOP_K):
                d = pl.multiple_of(dest_ref[r * TOP_K + kk] * ROW_SUB, ROW_SUB)
                pltpu.make_async_copy(h2_ref.at[pl.ds(pl.multiple_of(r * ROW_SUB, ROW_SUB), ROW_SUB)],
                                      xs_ref.at[pl.ds(d, ROW_SUB)], sem).start(priority=kk % 2)
            return carry

        lax.fori_loop(0, tm, issue, 0, unroll=8)
        for kk in range(TOP_K):
            pltpu.make_async_copy(h2_ref, xs_ref.at[pl.ds(0, tm * ROW_SUB)], sem).wait()

    @pl.when(i == n_token_steps)
    def _():
        zero_ref[...] = jnp.zeros_like(zero_ref)
        block_rows = EXPERT_BLOCK * ROW_SUB

        def pad_run(e, wait):
            pos = fs_ref[e]
            length = fl_ref[e]
            for bit in reversed(range(EXPERT_BLOCK.bit_length() - 1)):
                size = 1 << bit
                take = length & size

                @pl.when(take != 0)
                def _():
                    cp = pltpu.make_async_copy(
                        zero_ref.at[pl.ds(0, size * ROW_SUB)],
                        xs_ref.at[pl.ds(pl.multiple_of(pos * ROW_SUB, ROW_SUB), size * ROW_SUB)], zsem)
                    if wait:
                        cp.wait()
                    else:
                        cp.start()

                pos = pos + take

        def tail_block(blk, wait):
            cp = pltpu.make_async_copy(
                zero_ref, xs_ref.at[pl.ds(pl.multiple_of(blk * block_rows, block_rows), block_rows)], zsem)
            if wait:
                cp.wait()
            else:
                cp.start()

        for wait in (False, True):
            lax.fori_loop(0, N_EXPERTS, lambda e, c, w=wait: (pad_run(e, w), c)[1], 0)
            lax.fori_loop(nu_ref[0], N_EXPERT_BLOCKS, lambda blk, c, w=wait: (tail_block(blk, w), c)[1], 0)


def _dispatch_call(fill_start, fill_len, n_used, dest_flat, h2):
    tm = SCATTER_TILE
    n_token_steps = TOKENS // tm
    grid_spec = pltpu.PrefetchScalarGridSpec(
        num_scalar_prefetch=3,
        grid=(n_token_steps + 1,),
        in_specs=[pl.BlockSpec((tm * TOP_K,), lambda i, fs, fl, nu: (jnp.minimum(i, n_token_steps - 1),),
                               memory_space=pltpu.SMEM),
                  pl.BlockSpec((tm * ROW_SUB, LANES),
                               lambda i, fs, fl, nu: (jnp.minimum(i, n_token_steps - 1), 0))],
        out_specs=pl.BlockSpec(memory_space=pl.ANY),
        scratch_shapes=[pltpu.VMEM((EXPERT_BLOCK * ROW_SUB, LANES), _F32),
                        pltpu.SemaphoreType.DMA,
                        pltpu.SemaphoreType.DMA])
    return pl.pallas_call(
        _dispatch_kernel,
        out_shape=jax.ShapeDtypeStruct((SORTED_ROWS * ROW_SUB, LANES), _F32),
        grid_spec=grid_spec,
        compiler_params=pltpu.CompilerParams(dimension_semantics=("arbitrary",),
                                             vmem_limit_bytes=VMEM_LIMIT),
        name="dispatch",
    )(fill_start, fill_len, n_used, dest_flat, h2)


def _expert_kernel(be_ref, slot_ref, first_ref, nxt_ref, nu_ref,
                   xs_ref, wgu_hbm, bgu0_ref, bgu1_ref, wd_hbm, bd0_ref, bd1_ref, y_ref,
                   wgu_stage, wd_stage, wgu_bf, wd_bf, sem):
    p = pl.program_id(0)
    half = EXPERT_BLOCK * ROW_SUB
    n_pairs_used = (nu_ref[0] + 1) // 2

    def stage_copies(e):
        return (pltpu.make_async_copy(wgu_hbm.at[e], wgu_stage, sem.at[0]),
                pltpu.make_async_copy(wd_hbm.at[e], wd_stage, sem.at[1]))

    @pl.when(p == 0)
    def _():
        for cp in stage_copies(be_ref[0]):
            cp.start()

    for b in (2 * p, 2 * p + 1):
        @pl.when((first_ref[b] == 1) & (b < nu_ref[0]))
        def _():
            for cp in stage_copies(be_ref[b]):
                cp.wait()
            wgu_bf[slot_ref[b]] = wgu_stage[...].astype(_BF16)
            wd_bf[slot_ref[b]] = wd_stage[...].astype(_BF16)

            @pl.when(nxt_ref[b] < N_EXPERTS)
            def _():
                for cp in stage_copies(nxt_ref[b]):
                    cp.start()

    def ffn(x, slot, bgu_ref, bd_ref):
        gu = jnp.dot(x, wgu_bf[slot], preferred_element_type=_F32) + bgu_ref[0]
        gate = jnp.minimum(gu[:, 0:D_FF_EXPERT], SWIGLU_LIMIT)
        up = jnp.clip(gu[:, D_FF_EXPERT:], -SWIGLU_LIMIT, SWIGLU_LIMIT)
        act = (up + 1.0) * gate * (1.0 / (1.0 + jnp.exp(-SWIGLU_ALPHA * gate)))
        return jnp.dot(act.astype(_BF16), wd_bf[slot], preferred_element_type=_F32) + bd_ref[0]

    def load_rows(which):
        return jnp.concatenate([xs_ref[pl.ds(which * half + cc, EXPERT_BLOCK, stride=ROW_SUB), :]
                                for cc in range(ROW_SUB)], axis=1).astype(_BF16)

    def store_rows(which, y):
        for cc in range(ROW_SUB):
            y_ref[pl.ds(which * half + cc, EXPERT_BLOCK, stride=ROW_SUB), :] = y[:, cc * LANES:(cc + 1) * LANES]

    @pl.when(p < n_pairs_used)
    def _():
        y0 = ffn(load_rows(0), slot_ref[2 * p], bgu0_ref, bd0_ref)
        x1 = load_rows(1)
        store_rows(0, y0)
        store_rows(1, ffn(x1, slot_ref[2 * p + 1], bgu1_ref, bd1_ref))

    @pl.when(p >= n_pairs_used)
    def _():
        y_ref[...] = jnp.zeros_like(y_ref)


def _expert_call(block_expert, block_slot, block_first, block_next, n_used, xs, wgu, bgu3, wd, bd3):
    tb = 2 * EXPERT_BLOCK
    live = lambda p, be, sl, fi, nx, nu: (jnp.minimum(p, (nu[0] - 1) // 2), 0)
    bsel0 = lambda p, be, sl, fi, nx, nu: (be[2 * p], 0, 0)
    bsel1 = lambda p, be, sl, fi, nx, nu: (be[2 * p + 1], 0, 0)
    grid_spec = pltpu.PrefetchScalarGridSpec(
        num_scalar_prefetch=5,
        grid=(N_EXPERT_BLOCKS // 2,),
        in_specs=[pl.BlockSpec((tb * ROW_SUB, LANES), live),
                  pl.BlockSpec(memory_space=pl.ANY),
                  pl.BlockSpec((1, 1, 2 * D_FF_EXPERT), bsel0),
                  pl.BlockSpec((1, 1, 2 * D_FF_EXPERT), bsel1),
                  pl.BlockSpec(memory_space=pl.ANY),
                  pl.BlockSpec((1, 1, D_MODEL), bsel0),
                  pl.BlockSpec((1, 1, D_MODEL), bsel1)],
        out_specs=pl.BlockSpec((tb * ROW_SUB, LANES), lambda p, be, sl, fi, nx, nu: (p, 0)),
        scratch_shapes=[pltpu.VMEM((D_MODEL, 2 * D_FF_EXPERT), _F32),
                        pltpu.VMEM((D_FF_EXPERT, D_MODEL), _F32),
                        pltpu.VMEM((2, D_MODEL, 2 * D_FF_EXPERT), _BF16),
                        pltpu.VMEM((2, D_FF_EXPERT, D_MODEL), _BF16),
                        pltpu.SemaphoreType.DMA((2,))])
    return pl.pallas_call(
        _expert_kernel,
        out_shape=jax.ShapeDtypeStruct((SORTED_ROWS * ROW_SUB, LANES), _F32),
        grid_spec=grid_spec,
        compiler_params=pltpu.CompilerParams(dimension_semantics=("arbitrary",),
                                             vmem_limit_bytes=VMEM_LIMIT),
        name="expert",
    )(block_expert, block_slot, block_first, block_next, n_used, xs, wgu, bgu3, bgu3, wd, bd3, bd3)


def _combine_kernel(dest_ref, dest_next_ref, gate_ref, xmid_ref, mod_ref, g_ref, y_ref, o_ref,
                    buf_ref, sem):
    tm = xmid_ref.shape[0]
    m = mod_ref[0]
    i = pl.program_id(0)
    n = pl.num_programs(0)

    def gather_rows(idx_ref, slot):
        def issue(r, carry):
            for kk in range(TOP_K):
                d = pl.multiple_of(idx_ref[r * TOP_K + kk] * ROW_SUB, ROW_SUB)
                pltpu.make_async_copy(
                    y_ref.at[pl.ds(d, ROW_SUB)],
                    buf_ref.at[slot, kk, pl.ds(pl.multiple_of(r * ROW_SUB, ROW_SUB), ROW_SUB)],
                    sem.at[slot]).start(priority=kk % 2)
            return carry

        lax.fori_loop(0, tm, issue, 0, unroll=8)

    @pl.when(i == 0)
    def _():
        gather_rows(dest_ref, 0)

    @pl.when(i + 1 < n)
    def _():
        gather_rows(dest_next_ref, (i + 1) % 2)

    slot = i % 2
    for kk in range(TOP_K):
        pltpu.make_async_copy(y_ref.at[pl.ds(0, tm * ROW_SUB)], buf_ref.at[slot, kk],
                              sem.at[slot]).wait()
    g = gate_ref[...]
    pieces = []
    for cc in range(ROW_SUB):
        piece = buf_ref[slot, 0, pl.ds(cc, tm, stride=ROW_SUB), :] * g[:, 0:1]
        for kk in range(1, TOP_K):
            piece = piece + buf_ref[slot, kk, pl.ds(cc, tm, stride=ROW_SUB), :] * g[:, kk:kk + 1]
        pieces.append(piece)
    ffn = jnp.concatenate(pieces, axis=1)
    o_ref[...] = xmid_ref[...] + m[5:6] * _rms(ffn, g_ref[...])


def _combine_call(dest_flat, gates, x_mid, mod3, gpost_ffn, y_rows):
    tm = MOVE_TILE
    tiles_per_batch = SEQ // tm
    n_steps = TOKENS // tm
    return pl.pallas_call(
        _combine_kernel,
        out_shape=jax.ShapeDtypeStruct((TOKENS, D_MODEL), _F32),
        grid=(n_steps,),
        in_specs=[pl.BlockSpec((tm * TOP_K,), lambda i: (i,), memory_space=pltpu.SMEM),
                  pl.BlockSpec((tm * TOP_K,), lambda i: (jnp.minimum(i + 1, n_steps - 1),),
                               memory_space=pltpu.SMEM),
                  pl.BlockSpec((tm, LANES), lambda i: (i, 0)),
                  pl.BlockSpec((tm, D_MODEL), lambda i: (i, 0)),
                  pl.BlockSpec((1, N_MOD, D_MODEL), lambda i: (i // tiles_per_batch, 0, 0)),
                  pl.BlockSpec((1, D_MODEL), lambda i: (0, 0)),
                  pl.BlockSpec(memory_space=pl.ANY)],
        out_specs=pl.BlockSpec((tm, D_MODEL), lambda i: (i, 0)),
        scratch_shapes=[pltpu.VMEM((2, TOP_K, tm * ROW_SUB, LANES), _F32),
                        pltpu.SemaphoreType.DMA((2,))],
        compiler_params=pltpu.CompilerParams(dimension_semantics=("arbitrary",),
                                             vmem_limit_bytes=VMEM_LIMIT),
        name="combine",
    )(dest_flat, dest_flat, gates, x_mid, mod3, gpost_ffn, y_rows)


def _rope_tables():
    pos = np.arange(SEQ)
    pos_row = (pos // GRID_W).astype(np.float64)
    pos_col = (pos % GRID_W).astype(np.float64)
    n_freq = HEAD_DIM // 4
    inv_freq = ROPE_THETA ** (-np.arange(n_freq, dtype=np.float64) / n_freq)
    d = np.arange(LANES) % HEAD_DIM
    f = inv_freq[d % n_freq]
    ang = np.where((d < HEAD_DIM // 2)[None, :], pos_row[:, None] * f[None, :], pos_col[:, None] * f[None, :])
    sign = np.where((d % (HEAD_DIM // 2)) < n_freq, -1.0, 1.0)
    return (jnp.asarray(np.cos(ang), dtype=_F32), jnp.asarray(np.sin(ang) * sign[None, :], dtype=_F32))


def _perm_heads(a, axis):
    shape = a.shape
    a = a.reshape(shape[:axis] + (N_HEADS, HEAD_DIM) + shape[axis + 1:])
    a = jnp.take(a, jnp.array(HEAD_PERM), axis=axis)
    return a.reshape(shape)


def kernel(x, c, ctx, c_ctx, w_ada, b_ada, g_pre_mix, g_post_mix, g_pre_ffn, g_post_ffn, w_in, b_in, attn_sink, sgu_ln_g, sgu_ln_b, sgu_w, sgu_b, g_attn_out, g_sgu_out, w_out, b_out, w_router, b_router, w_gate_up, b_gate_up, w_down, b_down):
    l = 0
    x2 = x.reshape(TOKENS, D_MODEL)
    ctx2 = ctx.reshape(BATCH * CTX_LEN, D_MODEL)

    cc = jnp.zeros((16, D_MODEL), _F32).at[:BATCH].set(c).at[BATCH].set(c_ctx)
    mod = _ada_call(cc, w_ada[l], b_ada[l].reshape(1, -1))
    mod3 = mod.reshape(16, N_MOD, D_MODEL)

    w_in_l = w_in[l]
    b_in_l = b_in[l]
    w_in_p = jnp.concatenate([_perm_heads(w_in_l[:, :ATTN_WIDTH], 1), w_in_l[:, ATTN_WIDTH:]], axis=1)
    b_in_p = jnp.concatenate([_perm_heads(b_in_l[:ATTN_WIDTH], 0), b_in_l[ATTN_WIDTH:]], axis=0)
    cos_t, sin_t = _rope_tables()
    g_pre = g_pre_mix[l].reshape(1, -1)

    q, k, v, su, sv = _inproj_call(x2, mod3, g_pre, w_in_p.astype(_BF16), b_in_p.reshape(1, -1), cos_t, sin_t)
    kc, vc = _ctxproj_call(ctx2, mod3, g_pre, w_in_l[:, KV_START:KV_END].astype(_BF16),
                           b_in_l[KV_START:KV_END].reshape(1, -1))

    sink_p = jnp.take(attn_sink[l], jnp.array(HEAD_PERM)) * LOG2_E
    ao = _attn_call(sink_p, q, k, v, kc, vc)

    ws2 = sgu_w[l].reshape(SGU_WIDTH // LANES, 2 * SGU_CHUNK, SGU_CHUNK).astype(_BF16)
    bsf = jnp.repeat(sgu_b[l].T, SGU_HEAD_DIM, axis=1)
    w_out_l = w_out[l]
    woa = _perm_heads(w_out_l[:ATTN_WIDTH], 0).astype(_BF16)
    wos = w_out_l[ATTN_WIDTH:].astype(_BF16)
    ga = _perm_heads(g_attn_out[l], 0).reshape(1, -1)
    wr_hi = w_router[l].astype(_BF16)
    wr_lo = (w_router[l] - wr_hi.astype(_F32)).astype(_BF16)
    wr = (jnp.zeros((D_MODEL, LANES), _BF16).at[:, :N_EXPERTS].set(wr_hi)
          .at[:, N_EXPERTS:2 * N_EXPERTS].set(wr_lo))
    br = jnp.full((1, LANES), NEG_BIG, _F32).at[0, :N_EXPERTS].set(b_router[l])
    ii = np.arange(POST_TILE // POST_SPLIT)
    tri = jnp.asarray(ii[None, :] < ii[:, None], dtype=_BF16)

    x_mid, h2, idx_o, gate_o, rank_o, cnt_o = _post_call(
        x2, ao, su, sv, mod3, sgu_ln_g[l].reshape(1, -1), sgu_ln_b[l].reshape(1, -1), ws2, bsf,
        ga, g_sgu_out[l].reshape(1, -1), woa, wos, b_out[l].reshape(1, -1),
        g_post_mix[l].reshape(1, -1), g_pre_ffn[l].reshape(1, -1), wr, br, tri)

    counts = cnt_o[0, :N_EXPERTS]
    padded = (counts + EXPERT_BLOCK - 1) // EXPERT_BLOCK * EXPERT_BLOCK
    pad_end = jnp.cumsum(padded)
    pad_start = pad_end - padded
    dest = _dest_call(pad_start.astype(jnp.int32), idx_o, rank_o).reshape(-1)
    n_used = (pad_end[-1:] // EXPERT_BLOCK).astype(jnp.int32)

    xs = _dispatch_call((pad_start + counts).astype(jnp.int32), (padded - counts).astype(jnp.int32),
                        n_used, dest, h2)

    e_ids = jnp.arange(N_EXPERTS, dtype=jnp.int32)
    used = padded > 0
    cand = jnp.where(used, e_ids, N_EXPERTS)
    next_used = jnp.min(jnp.where(e_ids[None, :] > e_ids[:, None], cand[None, :], N_EXPERTS), axis=1)
    ordinal = jnp.cumsum(used.astype(jnp.int32)) - 1
    block_id = jnp.arange(N_EXPERT_BLOCKS, dtype=jnp.int32)
    block_start = block_id * EXPERT_BLOCK
    sched_start = jnp.minimum(block_start, (n_used[0] - 1) * EXPERT_BLOCK)
    sched_expert = jnp.minimum(jnp.sum(pad_end[None, :] <= sched_start[:, None], axis=1),
                               N_EXPERTS - 1).astype(jnp.int32)
    onehot = e_ids[None, :] == sched_expert[:, None]
    pick = lambda table: jnp.sum(jnp.where(onehot, table[None, :], 0), axis=1)
    block_slot = (pick(ordinal) & 1).astype(jnp.int32)
    block_first = ((block_start == pick(pad_start)) & (block_id < n_used[0])).astype(jnp.int32)
    block_next = pick(next_used).astype(jnp.int32)
    y_rows = _expert_call(sched_expert, block_slot, block_first, block_next, n_used, xs,
                          w_gate_up[l], b_gate_up[l].reshape(N_EXPERTS, 1, -1),
                          w_down[l], b_down[l].reshape(N_EXPERTS, 1, -1))
    out = _combine_call(dest, gate_o, x_mid, mod3, g_post_ffn[l].reshape(1, -1), y_rows)
    return out.reshape(BATCH, SEQ, D_MODEL)
```

```python
import jax
import jax.numpy as jnp
import numpy as np
from jax import lax
from jax.experimental import pallas as pl
from jax.experimental.pallas import tpu as pltpu

D_MODEL = 1024
BATCH = 8
SEQ = 4096
TOKENS = BATCH * SEQ
GRID_W = 64
CTX_LEN = 256
N_HEADS = 8
N_KV_HEADS = 2
HEAD_DIM = 64
ATTN_WIDTH = N_HEADS * HEAD_DIM
KV_WIDTH = N_KV_HEADS * HEAD_DIM
WINDOW = 128
ATTN_BLOCK = 128
SGU_HEADS = 8
SGU_HEAD_DIM = 64
SGU_WIDTH = SGU_HEADS * SGU_HEAD_DIM
SGU_CHUNK = 128
KV_START = ATTN_WIDTH
KV_END = ATTN_WIDTH + 2 * KV_WIDTH
IN_WIDTH = KV_END + 2 * SGU_WIDTH
N_EXPERTS = 32
TOP_K = 4
D_FF_EXPERT = 1024
SWIGLU_LIMIT = 7.0
SWIGLU_ALPHA = 1.702
ROPE_THETA = 10000.0
EPS = 1e-6
N_MOD = 6

LANES = 128
NEG_BIG = -1e30
VMEM_LIMIT = 56 * 1024 * 1024

ROW_TILE = 1024
SCATTER_TILE = 2048
MOVE_TILE = 256
ATTN_STEP_BLOCKS = 8
EXPERT_BLOCK = 512
POST_TILE = 1024
POST_SPLIT = 4
N_EXPERT_BLOCKS = TOKENS * TOP_K // EXPERT_BLOCK + N_EXPERTS
SORTED_ROWS = N_EXPERT_BLOCKS * EXPERT_BLOCK
HEAD_PERM = (0, 4, 1, 5, 2, 6, 3, 7)

LOG2_E = 1.4426950408889634
QK_SCALE = LOG2_E * HEAD_DIM ** -0.5

_F32 = jnp.float32
_BF16 = jnp.bfloat16


def _rms(x, g):
    ms = jnp.mean(x * x, axis=-1, keepdims=True)
    return x * lax.rsqrt(ms + EPS) * g


ROW_SUB = D_MODEL // LANES


def _store_row_tiles(ref, x):
    n = x.shape[0]
    for cc in range(ROW_SUB):
        ref[pl.ds(cc, n, stride=ROW_SUB), :] = x[:, cc * LANES:(cc + 1) * LANES]


def _load_row_tiles(ref, n):
    return jnp.concatenate([ref[pl.ds(cc, n, stride=ROW_SUB), :] for cc in range(ROW_SUB)], axis=1)


def _gelu_tanh(x):
    c = 0.7978845608028654
    return x * (0.5 * (1.0 + jnp.tanh(c * (x + 0.044715 * (x * x * x)))))


def _ada_kernel(c_ref, w_ref, b_ref, o_ref):
    c = c_ref[...]
    a = c / (1.0 + jnp.exp(-c))
    o_ref[...] = jnp.dot(a, w_ref[...], preferred_element_type=_F32,
                         precision=lax.Precision.HIGHEST) + b_ref[...]


def _ada_call(cc, w_ada, b_ada):
    n = N_MOD * D_MODEL
    tn = 512
    return pl.pallas_call(
        _ada_kernel,
        out_shape=jax.ShapeDtypeStruct((16, n), _F32),
        grid=(n // tn,),
        in_specs=[pl.BlockSpec((16, D_MODEL), lambda j: (0, 0)),
                  pl.BlockSpec((D_MODEL, tn), lambda j: (0, j)),
                  pl.BlockSpec((1, tn), lambda j: (0, j))],
        out_specs=pl.BlockSpec((16, tn), lambda j: (0, j)),
        compiler_params=pltpu.CompilerParams(dimension_semantics=("arbitrary",),
                                             vmem_limit_bytes=VMEM_LIMIT),
        name="ada",
    )(cc, w_ada, b_ada)


def _rope(x, cos, sin_signed, first_half):
    nxt = pltpu.roll(x, LANES - 16, 1)
    prv = pltpu.roll(x, 16, 1)
    return x * cos + jnp.where(first_half, nxt, prv) * sin_signed


def _inproj_kernel(x_ref, mod_ref, g_ref, w_ref, b_ref, cos_ref, sin_ref,
                   q_ref, k_ref, v_ref, su_ref, sv_ref):
    m = mod_ref[0]
    h = _rms(x_ref[...], g_ref[...]) * (1.0 + m[1:2]) + m[0:1]
    hb = h.astype(_BF16)
    cos = cos_ref[...]
    sin = sin_ref[...]
    lane = lax.broadcasted_iota(jnp.int32, cos.shape, 1)
    first_half = (lane & 31) < 16

    q = jnp.dot(hb, w_ref[:, 0:ATTN_WIDTH], preferred_element_type=_F32) + b_ref[:, 0:ATTN_WIDTH]
    for j in range(ATTN_WIDTH // LANES):
        qj = _rope(q[:, j * LANES:(j + 1) * LANES], cos, sin, first_half)
        q_ref[:, j * LANES:(j + 1) * LANES] = (qj * QK_SCALE).astype(_BF16)
    kv = jnp.dot(hb, w_ref[:, KV_START:KV_END], preferred_element_type=_F32) + b_ref[:, KV_START:KV_END]
    k_ref[...] = _rope(kv[:, 0:KV_WIDTH], cos, sin, first_half).astype(_BF16)
    v_ref[...] = kv[:, KV_WIDTH:].astype(_BF16)
    u0 = KV_END
    u1 = KV_END + SGU_WIDTH
    su_ref[...] = jnp.dot(hb, w_ref[:, u0:u1], preferred_element_type=_F32) + b_ref[:, u0:u1]
    sv_ref[...] = jnp.dot(hb, w_ref[:, u1:IN_WIDTH], preferred_element_type=_F32) + b_ref[:, u1:IN_WIDTH]


def _inproj_call(x2, mod3, g_pre, w_in_b, b_in2, cos_t, sin_t):
    tm = ROW_TILE
    tiles_per_batch = SEQ // tm
    row = lambda i: (i, 0)
    const = lambda i: (0, 0)
    return pl.pallas_call(
        _inproj_kernel,
        out_shape=(jax.ShapeDtypeStruct((TOKENS, ATTN_WIDTH), _BF16),
                   jax.ShapeDtypeStruct((TOKENS, KV_WIDTH), _BF16),
                   jax.ShapeDtypeStruct((TOKENS, KV_WIDTH), _BF16),
                   jax.ShapeDtypeStruct((TOKENS, SGU_WIDTH), _F32),
                   jax.ShapeDtypeStruct((TOKENS, SGU_WIDTH), _F32)),
        grid=(TOKENS // tm,),
        in_specs=[pl.BlockSpec((tm, D_MODEL), row),
                  pl.BlockSpec((1, N_MOD, D_MODEL), lambda i: (i // tiles_per_batch, 0, 0)),
                  pl.BlockSpec((1, D_MODEL), const),
                  pl.BlockSpec((D_MODEL, IN_WIDTH), const),
                  pl.BlockSpec((1, IN_WIDTH), const),
                  pl.BlockSpec((tm, LANES), lambda i: (i % tiles_per_batch, 0)),
                  pl.BlockSpec((tm, LANES), lambda i: (i % tiles_per_batch, 0))],
        out_specs=(pl.BlockSpec((tm, ATTN_WIDTH), row),
                   pl.BlockSpec((tm, KV_WIDTH), row),
                   pl.BlockSpec((tm, KV_WIDTH), row),
                   pl.BlockSpec((tm, SGU_WIDTH), row),
                   pl.BlockSpec((tm, SGU_WIDTH), row)),
        compiler_params=pltpu.CompilerParams(dimension_semantics=("arbitrary",),
                                             vmem_limit_bytes=VMEM_LIMIT),
        name="inproj",
    )(x2, mod3, g_pre, w_in_b, b_in2, cos_t, sin_t)


def _ctxproj_kernel(x_ref, mod_ref, g_ref, w_ref, b_ref, k_ref, v_ref):
    m = mod_ref[0]
    h = _rms(x_ref[...], g_ref[...]) * (1.0 + m[1:2]) + m[0:1]
    kv = jnp.dot(h.astype(_BF16), w_ref[...], preferred_element_type=_F32) + b_ref[...]
    k_ref[...] = kv[:, 0:KV_WIDTH].astype(_BF16)
    v_ref[...] = kv[:, KV_WIDTH:].astype(_BF16)


def _ctxproj_call(ctx2, mod3, g_pre, w_kv_b, b_kv2):
    tm = ROW_TILE
    rows = ctx2.shape[0]
    row = lambda i: (i, 0)
    const = lambda i: (0, 0)
    return pl.pallas_call(
        _ctxproj_kernel,
        out_shape=(jax.ShapeDtypeStruct((rows, KV_WIDTH), _BF16),
                   jax.ShapeDtypeStruct((rows, KV_WIDTH), _BF16)),
        grid=(rows // tm,),
        in_specs=[pl.BlockSpec((tm, D_MODEL), row),
                  pl.BlockSpec((1, N_MOD, D_MODEL), lambda i: (BATCH, 0, 0)),
                  pl.BlockSpec((1, D_MODEL), const),
                  pl.BlockSpec((D_MODEL, 2 * KV_WIDTH), const),
                  pl.BlockSpec((1, 2 * KV_WIDTH), const)],
        out_specs=(pl.BlockSpec((tm, KV_WIDTH), row),
                   pl.BlockSpec((tm, KV_WIDTH), row)),
        compiler_params=pltpu.CompilerParams(dimension_semantics=("arbitrary",),
                                             vmem_limit_bytes=VMEM_LIMIT),
        name="ctxproj",
    )(ctx2, mod3, g_pre, w_kv_b, b_kv2)


def _attn_kernel(sink_ref, q_ref, kp_ref, km_ref, kn_ref, vp_ref, vm_ref, vn_ref,
                 kc_ref, vc_ref, o_ref):
    n = pl.program_id(1)
    nstep = pl.num_programs(1)
    blk = ATTN_BLOCK
    nq = ATTN_STEP_BLOCKS
    rows2 = 2 * blk
    r = lax.broadcasted_iota(jnp.int32, (rows2, blk), 0) & (blk - 1)
    c = lax.broadcasted_iota(jnp.int32, (rows2, blk), 1)
    tri_prev = c >= r
    tri_next = c <= r
    row1 = lax.broadcasted_iota(jnp.int32, (rows2, 1), 0)
    lane_q = lax.broadcasted_iota(jnp.int32, (blk, LANES), 1)
    n_grp = ATTN_WIDTH // LANES
    k_blocks = [kp_ref[...]] + [km_ref[t * blk:(t + 1) * blk, :] for t in range(nq)] + [kn_ref[...]]
    v_blocks = [vp_ref[...]] + [vm_ref[t * blk:(t + 1) * blk, :] for t in range(nq)] + [vn_ref[...]]

    def scores(t):
        keys = jnp.concatenate(k_blocks[t:t + 3] + [kc_ref[...]], axis=0)
        q_rows = []
        for j in range(n_grp):
            qg = q_ref[t * blk:(t + 1) * blk, j * LANES:(j + 1) * LANES]
            zero = jnp.zeros_like(qg)
            q_rows += [jnp.where(lane_q < HEAD_DIM, qg, zero), jnp.where(lane_q >= HEAD_DIM, qg, zero)]
        return lax.dot_general(jnp.concatenate(q_rows, axis=0), keys, (((1,), (1,)), ((), ())),
                               preferred_element_type=_F32)

    def softmax(t, s_all):
        mask_a = (tri_prev & (n > 0)) if t == 0 else tri_prev
        mask_c = (tri_next & (n < nstep - 1)) if t == nq - 1 else tri_next
        p_rows, denoms = [], []
        for j in range(n_grp):
            s = s_all[j * rows2:(j + 1) * rows2]
            s = jnp.concatenate([jnp.where(mask_a, s[:, 0:blk], NEG_BIG), s[:, blk:2 * blk],
                                 jnp.where(mask_c, s[:, 2 * blk:3 * blk], NEG_BIG), s[:, 3 * blk:]],
                                axis=1)
            sk = jnp.where(row1 < blk, sink_ref[2 * j], sink_ref[2 * j + 1])
            m = jnp.maximum(jnp.max(s, axis=-1, keepdims=True), sk)
            p = jnp.exp2(s - m)
            denoms.append(jnp.sum(p, axis=-1, keepdims=True) + jnp.exp2(sk - m))
            p_rows.append(p.astype(_BF16))
        return jnp.concatenate(p_rows, axis=0), denoms

    def weighted_values(t, p_all, denoms):
        vals = jnp.concatenate(v_blocks[t:t + 3] + [vc_ref[...]], axis=0)
        o_all = jnp.dot(p_all, vals, preferred_element_type=_F32)
        for j in range(n_grp):
            o2 = o_all[j * rows2:(j + 1) * rows2] / denoms[j]
            og = jnp.where(lane_q < HEAD_DIM, o2[0:blk], o2[blk:])
            o_ref[t * blk:(t + 1) * blk, j * LANES:(j + 1) * LANES] = og.astype(_BF16)

    s_next = scores(0)
    probs = None
    for t in range(nq + 1):
        s_cur, s_next = s_next, (scores(t + 1) if t + 1 < nq else None)
        if probs is not None:
            weighted_values(t - 1, *probs)
        probs = softmax(t, s_cur) if t < nq else None


def _attn_call(sink_p, q, k, v, kc, vc):
    nblk = SEQ // ATTN_BLOCK
    nq = ATTN_STEP_BLOCKS
    nstep = nblk // nq
    own = lambda b, n: (b * nstep + n, 0)
    prev = lambda b, n: (b * nblk + jnp.maximum(nq * n - 1, 0), 0)
    nxt = lambda b, n: (b * nblk + jnp.minimum(nq * n + nq, nblk - 1), 0)
    ctx = lambda b, n: (b, 0)
    kv1 = (ATTN_BLOCK, KV_WIDTH)
    kvm = (nq * ATTN_BLOCK, KV_WIDTH)
    return pl.pallas_call(
        _attn_kernel,
        out_shape=jax.ShapeDtypeStruct((TOKENS, ATTN_WIDTH), _BF16),
        grid=(BATCH, nstep),
        in_specs=[pl.BlockSpec(memory_space=pltpu.SMEM),
                  pl.BlockSpec((nq * ATTN_BLOCK, ATTN_WIDTH), own),
                  pl.BlockSpec(kv1, prev), pl.BlockSpec(kvm, own), pl.BlockSpec(kv1, nxt),
                  pl.BlockSpec(kv1, prev), pl.BlockSpec(kvm, own), pl.BlockSpec(kv1, nxt),
                  pl.BlockSpec((CTX_LEN, KV_WIDTH), ctx),
                  pl.BlockSpec((CTX_LEN, KV_WIDTH), ctx)],
        out_specs=pl.BlockSpec((nq * ATTN_BLOCK, ATTN_WIDTH), own),
        compiler_params=pltpu.CompilerParams(dimension_semantics=("arbitrary", "arbitrary"),
                                             vmem_limit_bytes=VMEM_LIMIT),
        name="attn",
    )(sink_p, q, k, k, k, v, v, v, kc, vc)


def _post_kernel(x_ref, ao_ref, su_ref, sv_ref, mod_ref, lng_ref, lnb_ref, ws_ref, bs_ref,
                 ga_ref, gs_ref, woa_ref, wos_ref, bo_ref, gpost_ref, gpre_ref, wr_ref, br_ref,
                 tri_ref,
                 xmid_ref, h2_ref, idx_ref, gate_ref, rank_ref, cnt_ref,
                 mixed_ref, carry_ref):
    tm = x_ref.shape[0]
    m = mod_ref[0]

    @pl.when(pl.program_id(0) == 0)
    def _():
        carry_ref[...] = jnp.zeros_like(carry_ref)

    n_sub = POST_SPLIT
    ts = tm // n_sub
    tok_per_row = LANES // TOP_K
    lane = lax.broadcasted_iota(jnp.int32, (SGU_CHUNK, LANES), 1)
    st = [dict() for _ in range(n_sub)]
    carry = [carry_ref[...]]

    def rows(h):
        return slice(h * ts, (h + 1) * ts)

    def s1(h):
        gv = _gelu_tanh(sv_ref[rows(h), :])
        mu = jnp.mean(gv, axis=-1, keepdims=True)
        gc = gv - mu
        var = jnp.mean(gc * gc, axis=-1, keepdims=True)
        st[h]["vb"] = (gc * lax.rsqrt(var + EPS) * lng_ref[...] + lnb_ref[...]).astype(_BF16)

    def s2(h):
        vb = st[h].pop("vb")
        for c in range(ts // SGU_CHUNK):
            r0 = c * SGU_CHUNK
            for p in range(SGU_WIDTH // LANES):
                l0 = p * LANES
                r = jnp.dot(ws_ref[p], vb[r0:r0 + SGU_CHUNK, l0:l0 + LANES], preferred_element_type=_F32)
                mixed = jnp.where(lane < SGU_HEAD_DIM, r[0:SGU_CHUNK], r[SGU_CHUNK:])
                mixed_ref[h * ts + r0:h * ts + r0 + SGU_CHUNK, l0:l0 + LANES] = mixed + bs_ref[:, l0:l0 + LANES]
        sgu_o = _gelu_tanh(su_ref[rows(h), :]) * mixed_ref[rows(h), :]
        st[h]["oa"] = _rms(ao_ref[rows(h), :].astype(_F32), ga_ref[...]).astype(_BF16)
        st[h]["os"] = _rms(sgu_o, gs_ref[...]).astype(_BF16)

    def s3(h):
        st[h]["mix"] = (jnp.dot(st[h].pop("oa"), woa_ref[...], preferred_element_type=_F32)
                        + jnp.dot(st[h].pop("os"), wos_ref[...], preferred_element_type=_F32) + bo_ref[...])

    def s4(h):
        x_mid = x_ref[rows(h), :] + m[2:3] * _rms(st[h].pop("mix"), gpost_ref[...])
        xmid_ref[rows(h), :] = x_mid
        h2 = _rms(x_mid, gpre_ref[...]) * (1.0 + m[4:5]) + m[3:4]
        for cc in range(ROW_SUB):
            h2_ref[pl.ds(h * ts * ROW_SUB + cc, ts, stride=ROW_SUB), :] = h2[:, cc * LANES:(cc + 1) * LANES]
        h_hi = h2.astype(_BF16)
        st[h]["h_hi"] = h_hi
        st[h]["h_lo"] = (h2 - h_hi.astype(_F32)).astype(_BF16)

    def s5(h):
        r = (jnp.dot(st[h].pop("h_hi"), wr_ref[...], preferred_element_type=_F32)
             + jnp.dot(st[h].pop("h_lo"), wr_ref[...], preferred_element_type=_F32))
        st[h]["lg"] = r + pltpu.roll(r, LANES - N_EXPERTS, 1) + br_ref[...]

    def s6(h):
        lg = st[h].pop("lg")
        lane_r = lax.broadcasted_iota(jnp.int32, lg.shape, 1)
        lane_f = lane_r.astype(_F32)
        tops, hots = [], []
        for _k in range(TOP_K):
            mx = jnp.max(lg, axis=-1, keepdims=True)
            pick = jnp.min(jnp.where(lg == mx, lane_f, float(LANES)), axis=-1, keepdims=True)
            hot = lane_f == pick
            tops.append((mx, pick))
            hots.append(hot)
            lg = jnp.where(hot, 2.0 * NEG_BIG, lg)
        es = [jnp.exp(t[0] - tops[0][0]) for t in tops]
        esum = es[0] + es[1] + es[2] + es[3]
        multi = jnp.zeros(lg.shape, _F32)
        for hot in hots:
            multi = multi + jnp.where(hot, 1.0, 0.0)
        cum = jnp.dot(tri_ref[...], multi.astype(_BF16), preferred_element_type=_F32) + carry[0]
        row_r = lax.broadcasted_iota(jnp.int32, lg.shape, 0)
        lane_base = (row_r & (tok_per_row - 1)) * TOP_K
        gate_o = jnp.zeros(lg.shape, _F32)
        idx_e = jnp.zeros(lg.shape, _F32)
        rank_hi_e = jnp.zeros(lg.shape, _F32)
        rank_lo_e = jnp.zeros(lg.shape, _F32)
        for kk in range(TOP_K):
            rk = jnp.sum(jnp.where(hots[kk], cum, 0.0), axis=-1, keepdims=True)
            rk_hi = jnp.floor(rk * (1.0 / 256.0))
            here = lane_r == lane_base + kk
            gate_o = jnp.where(lane_r == kk, es[kk] / esum, gate_o)
            idx_e = jnp.where(here, tops[kk][1], idx_e)
            rank_hi_e = jnp.where(here, rk_hi, rank_hi_e)
            rank_lo_e = jnp.where(here, rk - 256.0 * rk_hi, rank_lo_e)
        fr = lax.broadcasted_iota(jnp.int32, (ts // tok_per_row, ts), 0)
        fc = lax.broadcasted_iota(jnp.int32, (ts // tok_per_row, ts), 1)
        fold = jnp.where(lax.shift_right_logical(fc, tok_per_row.bit_length() - 1) == fr, 1.0, 0.0).astype(_BF16)
        fs = slice(h * ts // tok_per_row, (h + 1) * ts // tok_per_row)
        idx_ref[fs, :] = jnp.dot(fold, idx_e.astype(_BF16), preferred_element_type=_F32).astype(jnp.int32)
        rank_ref[fs, :] = (256.0 * jnp.dot(fold, rank_hi_e.astype(_BF16), preferred_element_type=_F32)
                           + jnp.dot(fold, rank_lo_e.astype(_BF16), preferred_element_type=_F32)).astype(jnp.int32)
        gate_ref[rows(h), :] = gate_o
        carry[0] = carry[0] + jnp.sum(multi, axis=0, keepdims=True)

    order = sorted(((k + 2.5 * h, h, k) for h in range(n_sub) for k in range(6)))
    stages = (s1, s2, s3, s4, s5, s6)
    for _, h, k in order:
        stages[k](h)
    carry_ref[...] = carry[0]
    cnt_ref[...] = carry[0].astype(jnp.int32)


def _post_call(x2, ao, su, sv, mod3, lng, lnb, ws2, bsf, ga, gs, woa, wos, bo, gpost, gpre,
               wr, br, tri):
    tm = POST_TILE
    tiles_per_batch = SEQ // tm
    row = lambda i: (i, 0)
    const = lambda i: (0, 0)
    const3 = lambda i: (0, 0, 0)
    return pl.pallas_call(
        _post_kernel,
        out_shape=(jax.ShapeDtypeStruct((TOKENS, D_MODEL), _F32),
                   jax.ShapeDtypeStruct((TOKENS * ROW_SUB, LANES), _F32),
                   jax.ShapeDtypeStruct((TOKENS * TOP_K // LANES, LANES), jnp.int32),
                   jax.ShapeDtypeStruct((TOKENS, LANES), _F32),
                   jax.ShapeDtypeStruct((TOKENS * TOP_K // LANES, LANES), jnp.int32),
                   jax.ShapeDtypeStruct((1, LANES), jnp.int32)),
        grid=(TOKENS // tm,),
        in_specs=[pl.BlockSpec((tm, D_MODEL), row),
                  pl.BlockSpec((tm, ATTN_WIDTH), row),
                  pl.BlockSpec((tm, SGU_WIDTH), row),
                  pl.BlockSpec((tm, SGU_WIDTH), row),
                  pl.BlockSpec((1, N_MOD, D_MODEL), lambda i: (i // tiles_per_batch, 0, 0)),
                  pl.BlockSpec((1, SGU_WIDTH), const),
                  pl.BlockSpec((1, SGU_WIDTH), const),
                  pl.BlockSpec((SGU_WIDTH // LANES, 2 * SGU_CHUNK, SGU_CHUNK), const3),
                  pl.BlockSpec((SGU_CHUNK, SGU_WIDTH), const),
                  pl.BlockSpec((1, ATTN_WIDTH), const),
                  pl.BlockSpec((1, SGU_WIDTH), const),
                  pl.BlockSpec((ATTN_WIDTH, D_MODEL), const),
                  pl.BlockSpec((SGU_WIDTH, D_MODEL), const),
                  pl.BlockSpec((1, D_MODEL), const),
                  pl.BlockSpec((1, D_MODEL), const),
                  pl.BlockSpec((1, D_MODEL), const),
                  pl.BlockSpec((D_MODEL, LANES), const),
                  pl.BlockSpec((1, LANES), const),
                  pl.BlockSpec((tm // POST_SPLIT, tm // POST_SPLIT), const)],
        out_specs=(pl.BlockSpec((tm, D_MODEL), row),
                   pl.BlockSpec((tm * ROW_SUB, LANES), row),
                   pl.BlockSpec((tm * TOP_K // LANES, LANES), row),
                   pl.BlockSpec((tm, LANES), row),
                   pl.BlockSpec((tm * TOP_K // LANES, LANES), row),
                   pl.BlockSpec((1, LANES), const)),
        scratch_shapes=[pltpu.VMEM((tm, SGU_WIDTH), _F32),
                        pltpu.VMEM((1, LANES), _F32)],
        compiler_params=pltpu.CompilerParams(dimension_semantics=("arbitrary",),
                                             vmem_limit_bytes=VMEM_LIMIT),
        name="post",
    )(x2, ao, su, sv, mod3, lng, lnb, ws2, bsf, ga, gs, woa, wos, bo, gpost, gpre, wr, br, tri)


def _dest_kernel(ps_ref, idx_ref, rank_ref, o_ref):
    idx = idx_ref[...]
    dest = rank_ref[...]
    for e in range(N_EXPERTS):
        dest = dest + jnp.where(idx == e, ps_ref[e], 0)
    o_ref[...] = dest


def _dest_call(pad_start, idx_flat, rank_flat):
    return pl.pallas_call(
        _dest_kernel,
        out_shape=jax.ShapeDtypeStruct(idx_flat.shape, jnp.int32),
        grid=(1,),
        in_specs=[pl.BlockSpec(memory_space=pltpu.SMEM),
                  pl.BlockSpec(idx_flat.shape, lambda i: (0, 0)),
                  pl.BlockSpec(rank_flat.shape, lambda i: (0, 0))],
        out_specs=pl.BlockSpec(idx_flat.shape, lambda i: (0, 0)),
        compiler_params=pltpu.CompilerParams(dimension_semantics=("arbitrary",),
                                             vmem_limit_bytes=VMEM_LIMIT),
        name="dest",
    )(pad_start, idx_flat, rank_flat)


def _dispatch_kernel(fs_ref, fl_ref, nu_ref, dest_ref, h2_ref, xs_ref, zero_ref, sem, zsem):
    tm = h2_ref.shape[0] // ROW_SUB
    n_token_steps = TOKENS // tm
    i = pl.program_id(0)

    @pl.when(i < n_token_steps)
    def _():
        def issue(r, carry):
            for kk in range(TOP_K):
                d = pl.multiple_of(dest_ref[r * TOP_K + kk] * ROW_SUB, ROW_SUB)
                pltpu.make_async_copy(h2_ref.at[pl.ds(pl.multiple_of(r * ROW_SUB, ROW_SUB), ROW_SUB)],
                                      xs_ref.at[pl.ds(d, ROW_SUB)], sem).start(priority=kk % 2)
            return carry

        lax.fori_loop(0, tm, issue, 0, unroll=8)
        for kk in range(TOP_K):
            pltpu.make_async_copy(h2_ref, xs_ref.at[pl.ds(0, tm * ROW_SUB)], sem).wait()

    @pl.when(i == n_token_steps)
    def _():
        zero_ref[...] = jnp.zeros_like(zero_ref)
        block_rows = EXPERT_BLOCK * ROW_SUB

        def pad_run(e, wait):
            pos = fs_ref[e]
            length = fl_ref[e]
            for bit in reversed(range(EXPERT_BLOCK.bit_length() - 1)):
                size = 1 << bit
                take = length & size

                @pl.when(take != 0)
                def _():
                    cp = pltpu.make_async_copy(
                        zero_ref.at[pl.ds(0, size * ROW_SUB)],
                        xs_ref.at[pl.ds(pl.multiple_of(pos * ROW_SUB, ROW_SUB), size * ROW_SUB)], zsem)
                    if wait:
                        cp.wait()
                    else:
                        cp.start()

                pos = pos + take

        def tail_block(blk, wait):
            cp = pltpu.make_async_copy(
                zero_ref, xs_ref.at[pl.ds(pl.multiple_of(blk * block_rows, block_rows), block_rows)], zsem)
            if wait:
                cp.wait()
            else:
                cp.start()

        for wait in (False, True):
            lax.fori_loop(0, N_EXPERTS, lambda e, c, w=wait: (pad_run(e, w), c)[1], 0)
            lax.fori_loop(nu_ref[0], N_EXPERT_BLOCKS, lambda blk, c, w=wait: (tail_block(blk, w), c)[1], 0)


def _dispatch_call(fill_start, fill_len, n_used, dest_flat, h2):
    tm = SCATTER_TILE
    n_token_steps = TOKENS // tm
    grid_spec = pltpu.PrefetchScalarGridSpec(
        num_scalar_prefetch=3,
        grid=(n_token_steps + 1,),
        in_specs=[pl.BlockSpec((tm * TOP_K,), lambda i, fs, fl, nu: (jnp.minimum(i, n_token_steps - 1),),
                               memory_space=pltpu.SMEM),
                  pl.BlockSpec((tm * ROW_SUB, LANES),
                               lambda i, fs, fl, nu: (jnp.minimum(i, n_token_steps - 1), 0))],
        out_specs=pl.BlockSpec(memory_space=pl.ANY),
        scratch_shapes=[pltpu.VMEM((EXPERT_BLOCK * ROW_SUB, LANES), _F32),
                        pltpu.SemaphoreType.DMA,
                        pltpu.SemaphoreType.DMA])
    return pl.pallas_call(
        _dispatch_kernel,
        out_shape=jax.ShapeDtypeStruct((SORTED_ROWS * ROW_SUB, LANES), _F32),
        grid_spec=grid_spec,
        compiler_params=pltpu.CompilerParams(dimension_semantics=("arbitrary",),
                                             vmem_limit_bytes=VMEM_LIMIT),
        name="dispatch",
    )(fill_start, fill_len, n_used, dest_flat, h2)


def _expert_kernel(be_ref, slot_ref, first_ref, nxt_ref, nu_ref,
                   xs_ref, wgu_hbm, bgu0_ref, bgu1_ref, wd_hbm, bd0_ref, bd1_ref, y_ref,
                   wgu_stage, wd_stage, wgu_bf, wd_bf, sem):
    p = pl.program_id(0)
    half = EXPERT_BLOCK * ROW_SUB
    n_pairs_used = (nu_ref[0] + 1) // 2

    def stage_copies(e):
        return (pltpu.make_async_copy(wgu_hbm.at[e], wgu_stage, sem.at[0]),
                pltpu.make_async_copy(wd_hbm.at[e], wd_stage, sem.at[1]))

    @pl.when(p == 0)
    def _():
        for cp in stage_copies(be_ref[0]):
            cp.start()

    for b in (2 * p, 2 * p + 1):
        @pl.when((first_ref[b] == 1) & (b < nu_ref[0]))
        def _():
            for cp in stage_copies(be_ref[b]):
                cp.wait()
            wgu_bf[slot_ref[b]] = wgu_stage[...].astype(_BF16)
            wd_bf[slot_ref[b]] = wd_stage[...].astype(_BF16)

            @pl.when(nxt_ref[b] < N_EXPERTS)
            def _():
                for cp in stage_copies(nxt_ref[b]):
                    cp.start()

    def ffn(x, slot, bgu_ref, bd_ref):
        gu = jnp.dot(x, wgu_bf[slot], preferred_element_type=_F32) + bgu_ref[0]
        gate = jnp.minimum(gu[:, 0:D_FF_EXPERT], SWIGLU_LIMIT)
        up = jnp.clip(gu[:, D_FF_EXPERT:], -SWIGLU_LIMIT, SWIGLU_LIMIT)
        act = (up + 1.0) * gate * (1.0 / (1.0 + jnp.exp(-SWIGLU_ALPHA * gate)))
        return jnp.dot(act.astype(_BF16), wd_bf[slot], preferred_element_type=_F32) + bd_ref[0]

    def load_rows(which):
        return jnp.concatenate([xs_ref[pl.ds(which * half + cc, EXPERT_BLOCK, stride=ROW_SUB), :]
                                for cc in range(ROW_SUB)], axis=1).astype(_BF16)

    def store_rows(which, y):
        for cc in range(ROW_SUB):
            y_ref[pl.ds(which * half + cc, EXPERT_BLOCK, stride=ROW_SUB), :] = y[:, cc * LANES:(cc + 1) * LANES]

    @pl.when(p < n_pairs_used)
    def _():
        y0 = ffn(load_rows(0), slot_ref[2 * p], bgu0_ref, bd0_ref)
        x1 = load_rows(1)
        store_rows(0, y0)
        store_rows(1, ffn(x1, slot_ref[2 * p + 1], bgu1_ref, bd1_ref))

    @pl.when(p >= n_pairs_used)
    def _():
        y_ref[...] = jnp.zeros_like(y_ref)


def _expert_call(block_expert, block_slot, block_first, block_next, n_used, xs, wgu, bgu3, wd, bd3):
    tb = 2 * EXPERT_BLOCK
    live = lambda p, be, sl, fi, nx, nu: (jnp.minimum(p, (nu[0] - 1) // 2), 0)
    bsel0 = lambda p, be, sl, fi, nx, nu: (be[2 * p], 0, 0)
    bsel1 = lambda p, be, sl, fi, nx, nu: (be[2 * p + 1], 0, 0)
    grid_spec = pltpu.PrefetchScalarGridSpec(
        num_scalar_prefetch=5,
        grid=(N_EXPERT_BLOCKS // 2,),
        in_specs=[pl.BlockSpec((tb * ROW_SUB, LANES), live),
                  pl.BlockSpec(memory_space=pl.ANY),
                  pl.BlockSpec((1, 1, 2 * D_FF_EXPERT), bsel0),
                  pl.BlockSpec((1, 1, 2 * D_FF_EXPERT), bsel1),
                  pl.BlockSpec(memory_space=pl.ANY),
                  pl.BlockSpec((1, 1, D_MODEL), bsel0),
                  pl.BlockSpec((1, 1, D_MODEL), bsel1)],
        out_specs=pl.BlockSpec((tb * ROW_SUB, LANES), lambda p, be, sl, fi, nx, nu: (p, 0)),
        scratch_shapes=[pltpu.VMEM((D_MODEL, 2 * D_FF_EXPERT), _F32),
                        pltpu.VMEM((D_FF_EXPERT, D_MODEL), _F32),
                        pltpu.VMEM((2, D_MODEL, 2 * D_FF_EXPERT), _BF16),
                        pltpu.VMEM((2, D_FF_EXPERT, D_MODEL), _BF16),
                        pltpu.SemaphoreType.DMA((2,))])
    return pl.pallas_call(
        _expert_kernel,
        out_shape=jax.ShapeDtypeStruct((SORTED_ROWS * ROW_SUB, LANES), _F32),
        grid_spec=grid_spec,
        compiler_params=pltpu.CompilerParams(dimension_semantics=("arbitrary",),
                                             vmem_limit_bytes=VMEM_LIMIT),
        name="expert",
    )(block_expert, block_slot, block_first, block_next, n_used, xs, wgu, bgu3, bgu3, wd, bd3, bd3)


def _combine_kernel(dest_ref, dest_next_ref, gate_ref, xmid_ref, mod_ref, g_ref, y_ref, o_ref,
                    buf_ref, sem):
    tm = xmid_ref.shape[0]
    m = mod_ref[0]
    i = pl.program_id(0)
    n = pl.num_programs(0)

    def gather_rows(idx_ref, slot):
        def issue(r, carry):
            for kk in range(TOP_K):
                d = pl.multiple_of(idx_ref[r * TOP_K + kk] * ROW_SUB, ROW_SUB)
                pltpu.make_async_copy(
                    y_ref.at[pl.ds(d, ROW_SUB)],
                    buf_ref.at[slot, kk, pl.ds(pl.multiple_of(r * ROW_SUB, ROW_SUB), ROW_SUB)],
                    sem.at[slot]).start(priority=kk % 2)
            return carry

        lax.fori_loop(0, tm, issue, 0, unroll=8)

    @pl.when(i == 0)
    def _():
        gather_rows(dest_ref, 0)

    @pl.when(i + 1 < n)
    def _():
        gather_rows(dest_next_ref, (i + 1) % 2)

    slot = i % 2
    for kk in range(TOP_K):
        pltpu.make_async_copy(y_ref.at[pl.ds(0, tm * ROW_SUB)], buf_ref.at[slot, kk],
                              sem.at[slot]).wait()
    g = gate_ref[...]
    pieces = []
    for cc in range(ROW_SUB):
        piece = buf_ref[slot, 0, pl.ds(cc, tm, stride=ROW_SUB), :] * g[:, 0:1]
        for kk in range(1, TOP_K):
            piece = piece + buf_ref[slot, kk, pl.ds(cc, tm, stride=ROW_SUB), :] * g[:, kk:kk + 1]
        pieces.append(piece)
    ffn = jnp.concatenate(pieces, axis=1)
    o_ref[...] = xmid_ref[...] + m[5:6] * _rms(ffn, g_ref[...])


def _combine_call(dest_flat, gates, x_mid, mod3, gpost_ffn, y_rows):
    tm = MOVE_TILE
    tiles_per_batch = SEQ // tm
    n_steps = TOKENS // tm
    return pl.pallas_call(
        _combine_kernel,
        out_shape=jax.ShapeDtypeStruct((TOKENS, D_MODEL), _F32),
        grid=(n_steps,),
        in_specs=[pl.BlockSpec((tm * TOP_K,), lambda i: (i,), memory_space=pltpu.SMEM),
                  pl.BlockSpec((tm * TOP_K,), lambda i: (jnp.minimum(i + 1, n_steps - 1),),
                               memory_space=pltpu.SMEM),
                  pl.BlockSpec((tm, LANES), lambda i: (i, 0)),
                  pl.BlockSpec((tm, D_MODEL), lambda i: (i, 0)),
                  pl.BlockSpec((1, N_MOD, D_MODEL), lambda i: (i // tiles_per_batch, 0, 0)),
                  pl.BlockSpec((1, D_MODEL), lambda i: (0, 0)),
                  pl.BlockSpec(memory_space=pl.ANY)],
        out_specs=pl.BlockSpec((tm, D_MODEL), lambda i: (i, 0)),
        scratch_shapes=[pltpu.VMEM((2, TOP_K, tm * ROW_SUB, LANES), _F32),
                        pltpu.SemaphoreType.DMA((2,))],
        compiler_params=pltpu.CompilerParams(dimension_semantics=("arbitrary",),
                                             vmem_limit_bytes=VMEM_LIMIT),
        name="combine",
    )(dest_flat, dest_flat, gates, x_mid, mod3, gpost_ffn, y_rows)


def _rope_tables():
    pos = np.arange(SEQ)
    pos_row = (pos // GRID_W).astype(np.float64)
    pos_col = (pos % GRID_W).astype(np.float64)
    n_freq = HEAD_DIM // 4
    inv_freq = ROPE_THETA ** (-np.arange(n_freq, dtype=np.float64) / n_freq)
    d = np.arange(LANES) % HEAD_DIM
    f = inv_freq[d % n_freq]
    ang = np.where((d < HEAD_DIM // 2)[None, :], pos_row[:, None] * f[None, :], pos_col[:, None] * f[None, :])
    sign = np.where((d % (HEAD_DIM // 2)) < n_freq, -1.0, 1.0)
    return (jnp.asarray(np.cos(ang), dtype=_F32), jnp.asarray(np.sin(ang) * sign[None, :], dtype=_F32))


def _perm_heads(a, axis):
    shape = a.shape
    a = a.reshape(shape[:axis] + (N_HEADS, HEAD_DIM) + shape[axis + 1:])
    a = jnp.take(a, jnp.array(HEAD_PERM), axis=axis)
    return a.reshape(shape)


def kernel(x, c, ctx, c_ctx, w_ada, b_ada, g_pre_mix, g_post_mix, g_pre_ffn, g_post_ffn, w_in, b_in, attn_sink, sgu_ln_g, sgu_ln_b, sgu_w, sgu_b, g_attn_out, g_sgu_out, w_out, b_out, w_router, b_router, w_gate_up, b_gate_up, w_down, b_down):
    l = 0
    x2 = x.reshape(TOKENS, D_MODEL)
    ctx2 = ctx.reshape(BATCH * CTX_LEN, D_MODEL)

    cc = jnp.zeros((16, D_MODEL), _F32).at[:BATCH].set(c).at[BATCH].set(c_ctx)
    mod = _ada_call(cc, w_ada[l], b_ada[l].reshape(1, -1))
    mod3 = mod.reshape(16, N_MOD, D_MODEL)

    w_in_l = w_in[l]
    b_in_l = b_in[l]
    w_in_p = jnp.concatenate([_perm_heads(w_in_l[:, :ATTN_WIDTH], 1), w_in_l[:, ATTN_WIDTH:]], axis=1)
    b_in_p = jnp.concatenate([_perm_heads(b_in_l[:ATTN_WIDTH], 0), b_in_l[ATTN_WIDTH:]], axis=0)
    cos_t, sin_t = _rope_tables()
    g_pre = g_pre_mix[l].reshape(1, -1)

    q, k, v, su, sv = _inproj_call(x2, mod3, g_pre, w_in_p.astype(_BF16), b_in_p.reshape(1, -1), cos_t, sin_t)
    kc, vc = _ctxproj_call(ctx2, mod3, g_pre, w_in_l[:, KV_START:KV_END].astype(_BF16),
                           b_in_l[KV_START:KV_END].reshape(1, -1))

    sink_p = jnp.take(attn_sink[l], jnp.array(HEAD_PERM)) * LOG2_E
    ao = _attn_call(sink_p, q, k, v, kc, vc)

    ws2 = sgu_w[l].reshape(SGU_WIDTH // LANES, 2 * SGU_CHUNK, SGU_CHUNK).astype(_BF16)
    bsf = jnp.repeat(sgu_b[l].T, SGU_HEAD_DIM, axis=1)
    w_out_l = w_out[l]
    woa = _perm_heads(w_out_l[:ATTN_WIDTH], 0).astype(_BF16)
    wos = w_out_l[ATTN_WIDTH:].astype(_BF16)
    ga = _perm_heads(g_attn_out[l], 0).reshape(1, -1)
    wr_hi = w_router[l].astype(_BF16)
    wr_lo = (w_router[l] - wr_hi.astype(_F32)).astype(_BF16)
    wr = (jnp.zeros((D_MODEL, LANES), _BF16).at[:, :N_EXPERTS].set(wr_hi)
          .at[:, N_EXPERTS:2 * N_EXPERTS].set(wr_lo))
    br = jnp.full((1, LANES), NEG_BIG, _F32).at[0, :N_EXPERTS].set(b_router[l])
    ii = np.arange(POST_TILE // POST_SPLIT)
    tri = jnp.asarray(ii[None, :] < ii[:, None], dtype=_BF16)

    x_mid, h2, idx_o, gate_o, rank_o, cnt_o = _post_call(
        x2, ao, su, sv, mod3, sgu_ln_g[l].reshape(1, -1), sgu_ln_b[l].reshape(1, -1), ws2, bsf,
        ga, g_sgu_out[l].reshape(1, -1), woa, wos, b_out[l].reshape(1, -1),
        g_post_mix[l].reshape(1, -1), g_pre_ffn[l].reshape(1, -1), wr, br, tri)

    counts = cnt_o[0, :N_EXPERTS]
    padded = (counts + EXPERT_BLOCK - 1) // EXPERT_BLOCK * EXPERT_BLOCK
    pad_end = jnp.cumsum(padded)
    pad_start = pad_end - padded
    dest = _dest_call(pad_start.astype(jnp.int32), idx_o, rank_o).reshape(-1)
    n_used = (pad_end[-1:] // EXPERT_BLOCK).astype(jnp.int32)

    xs = _dispatch_call((pad_start + counts).astype(jnp.int32), (padded - counts).astype(jnp.int32),
                        n_used, dest, h2)

    e_ids = jnp.arange(N_EXPERTS, dtype=jnp.int32)
    used = padded > 0
    cand = jnp.where(used, e_ids, N_EXPERTS)
    next_used = jnp.min(jnp.where(e_ids[None, :] > e_ids[:, None], cand[None, :], N_EXPERTS), axis=1)
    ordinal = jnp.cumsum(used.astype(jnp.int32)) - 1
    block_id = jnp.arange(N_EXPERT_BLOCKS, dtype=jnp.int32)
    block_start = block_id * EXPERT_BLOCK
    sched_start = jnp.minimum(block_start, (n_used[0] - 1) * EXPERT_BLOCK)
    sched_expert = jnp.minimum(jnp.sum(pad_end[None, :] <= sched_start[:, None], axis=1),
                               N_EXPERTS - 1).astype(jnp.int32)
    onehot = e_ids[None, :] == sched_expert[:, None]
    pick = lambda table: jnp.sum(jnp.where(onehot, table[None, :], 0), axis=1)
    block_slot = (pick(ordinal) & 1).astype(jnp.int32)
    block_first = ((block_start == pick(pad_start)) & (block_id < n_used[0])).astype(jnp.int32)
    block_next = pick(next_used).astype(jnp.int32)
    y_rows = _expert_call(sched_expert, block_slot, block_first, block_next, n_used, xs,
                          w_gate_up[l], b_gate_up[l].reshape(N_EXPERTS, 1, -1),
                          w_down[l], b_down[l].reshape(N_EXPERTS, 1, -1))
    out = _combine_call(dest, gate_o, x_mid, mod3, g_post_ffn[l].reshape(1, -1), y_rows)
    return out.reshape(BATCH, SEQ, D_MODEL)
```

```python
import jax
import jax.numpy as jnp
import numpy as np
from jax import lax
from jax.experimental import pallas as pl
from jax.experimental.pallas import tpu as pltpu

D_MODEL = 1024
BATCH = 8
SEQ = 4096
TOKENS = BATCH * SEQ
GRID_W = 64
CTX_LEN = 256
N_HEADS = 8
N_KV_HEADS = 2
HEAD_DIM = 64
ATTN_WIDTH = N_HEADS * HEAD_DIM
KV_WIDTH = N_KV_HEADS * HEAD_DIM
WINDOW = 128
ATTN_BLOCK = 128
SGU_HEADS = 8
SGU_HEAD_DIM = 64
SGU_WIDTH = SGU_HEADS * SGU_HEAD_DIM
SGU_CHUNK = 128
KV_START = ATTN_WIDTH
KV_END = ATTN_WIDTH + 2 * KV_WIDTH
IN_WIDTH = KV_END + 2 * SGU_WIDTH
N_EXPERTS = 32
TOP_K = 4
D_FF_EXPERT = 1024
SWIGLU_LIMIT = 7.0
SWIGLU_ALPHA = 1.702
ROPE_THETA = 10000.0
EPS = 1e-6
N_MOD = 6

LANES = 128
NEG_BIG = -1e30
VMEM_LIMIT = 56 * 1024 * 1024

ROW_TILE = 1024
SCATTER_TILE = 2048
MOVE_TILE = 256
ATTN_STEP_BLOCKS = 8
EXPERT_BLOCK = 512
POST_TILE = 1024
POST_SPLIT = 4
N_EXPERT_BLOCKS = TOKENS * TOP_K // EXPERT_BLOCK + N_EXPERTS
SORTED_ROWS = N_EXPERT_BLOCKS * EXPERT_BLOCK
HEAD_PERM = (0, 4, 1, 5, 2, 6, 3, 7)

LOG2_E = 1.4426950408889634
QK_SCALE = LOG2_E * HEAD_DIM ** -0.5

_F32 = jnp.float32
_BF16 = jnp.bfloat16


def _rms(x, g):
    ms = jnp.mean(x * x, axis=-1, keepdims=True)
    return x * lax.rsqrt(ms + EPS) * g


ROW_SUB = D_MODEL // LANES
GATHER_PITCH = ROW_SUB + 1


def _store_row_tiles(ref, x):
    n = x.shape[0]
    for cc in range(ROW_SUB):
        ref[pl.ds(cc, n, stride=ROW_SUB), :] = x[:, cc * LANES:(cc + 1) * LANES]


def _load_row_tiles(ref, n):
    return jnp.concatenate([ref[pl.ds(cc, n, stride=ROW_SUB), :] for cc in range(ROW_SUB)], axis=1)


def _gelu_tanh(x):
    c = 0.7978845608028654
    return x * (0.5 * (1.0 + jnp.tanh(c * (x + 0.044715 * (x * x * x)))))


def _ada_kernel(c_ref, w_ref, b_ref, o_ref):
    c = c_ref[...]
    a = c / (1.0 + jnp.exp(-c))
    o_ref[...] = jnp.dot(a, w_ref[...], preferred_element_type=_F32,
                         precision=lax.Precision.HIGHEST) + b_ref[...]


def _ada_call(cc, w_ada, b_ada):
    n = N_MOD * D_MODEL
    tn = 512
    return pl.pallas_call(
        _ada_kernel,
        out_shape=jax.ShapeDtypeStruct((16, n), _F32),
        grid=(n // tn,),
        in_specs=[pl.BlockSpec((16, D_MODEL), lambda j: (0, 0)),
                  pl.BlockSpec((D_MODEL, tn), lambda j: (0, j)),
                  pl.BlockSpec((1, tn), lambda j: (0, j))],
        out_specs=pl.BlockSpec((16, tn), lambda j: (0, j)),
        compiler_params=pltpu.CompilerParams(dimension_semantics=("arbitrary",),
                                             vmem_limit_bytes=VMEM_LIMIT),
        name="ada",
    )(cc, w_ada, b_ada)


def _rope(x, cos, sin_signed, first_half):
    nxt = pltpu.roll(x, LANES - 16, 1)
    prv = pltpu.roll(x, 16, 1)
    return x * cos + jnp.where(first_half, nxt, prv) * sin_signed


def _inproj_kernel(x_ref, mod_ref, g_ref, w_ref, b_ref, cos_ref, sin_ref,
                   q_ref, k_ref, v_ref, su_ref, sv_ref):
    m = mod_ref[0]
    h = _rms(x_ref[...], g_ref[...]) * (1.0 + m[1:2]) + m[0:1]
    hb = h.astype(_BF16)
    cos = cos_ref[...]
    sin = sin_ref[...]
    lane = lax.broadcasted_iota(jnp.int32, cos.shape, 1)
    first_half = (lane & 31) < 16

    q = jnp.dot(hb, w_ref[:, 0:ATTN_WIDTH], preferred_element_type=_F32) + b_ref[:, 0:ATTN_WIDTH]
    for j in range(ATTN_WIDTH // LANES):
        qj = _rope(q[:, j * LANES:(j + 1) * LANES], cos, sin, first_half)
        q_ref[:, j * LANES:(j + 1) * LANES] = (qj * QK_SCALE).astype(_BF16)
    kv = jnp.dot(hb, w_ref[:, KV_START:KV_END], preferred_element_type=_F32) + b_ref[:, KV_START:KV_END]
    k_ref[...] = _rope(kv[:, 0:KV_WIDTH], cos, sin, first_half).astype(_BF16)
    v_ref[...] = kv[:, KV_WIDTH:].astype(_BF16)
    u0 = KV_END
    u1 = KV_END + SGU_WIDTH
    su_ref[...] = jnp.dot(hb, w_ref[:, u0:u1], preferred_element_type=_F32) + b_ref[:, u0:u1]
    sv_ref[...] = jnp.dot(hb, w_ref[:, u1:IN_WIDTH], preferred_element_type=_F32) + b_ref[:, u1:IN_WIDTH]


def _inproj_call(x2, mod3, g_pre, w_in_b, b_in2, cos_t, sin_t):
    tm = ROW_TILE
    tiles_per_batch = SEQ // tm
    row = lambda i: (i, 0)
    const = lambda i: (0, 0)
    return pl.pallas_call(
        _inproj_kernel,
        out_shape=(jax.ShapeDtypeStruct((TOKENS, ATTN_WIDTH), _BF16),
                   jax.ShapeDtypeStruct((TOKENS, KV_WIDTH), _BF16),
                   jax.ShapeDtypeStruct((TOKENS, KV_WIDTH), _BF16),
                   jax.ShapeDtypeStruct((TOKENS, SGU_WIDTH), _F32),
                   jax.ShapeDtypeStruct((TOKENS, SGU_WIDTH), _F32)),
        grid=(TOKENS // tm,),
        in_specs=[pl.BlockSpec((tm, D_MODEL), row),
                  pl.BlockSpec((1, N_MOD, D_MODEL), lambda i: (i // tiles_per_batch, 0, 0)),
                  pl.BlockSpec((1, D_MODEL), const),
                  pl.BlockSpec((D_MODEL, IN_WIDTH), const),
                  pl.BlockSpec((1, IN_WIDTH), const),
                  pl.BlockSpec((tm, LANES), lambda i: (i % tiles_per_batch, 0)),
                  pl.BlockSpec((tm, LANES), lambda i: (i % tiles_per_batch, 0))],
        out_specs=(pl.BlockSpec((tm, ATTN_WIDTH), row),
                   pl.BlockSpec((tm, KV_WIDTH), row),
                   pl.BlockSpec((tm, KV_WIDTH), row),
                   pl.BlockSpec((tm, SGU_WIDTH), row),
                   pl.BlockSpec((tm, SGU_WIDTH), row)),
        compiler_params=pltpu.CompilerParams(dimension_semantics=("arbitrary",),
                                             vmem_limit_bytes=VMEM_LIMIT),
        name="inproj",
    )(x2, mod3, g_pre, w_in_b, b_in2, cos_t, sin_t)


def _ctxproj_kernel(x_ref, mod_ref, g_ref, w_ref, b_ref, k_ref, v_ref):
    m = mod_ref[0]
    h = _rms(x_ref[...], g_ref[...]) * (1.0 + m[1:2]) + m[0:1]
    kv = jnp.dot(h.astype(_BF16), w_ref[...], preferred_element_type=_F32) + b_ref[...]
    k_ref[...] = kv[:, 0:KV_WIDTH].astype(_BF16)
    v_ref[...] = kv[:, KV_WIDTH:].astype(_BF16)


def _ctxproj_call(ctx2, mod3, g_pre, w_kv_b, b_kv2):
    tm = ROW_TILE
    rows = ctx2.shape[0]
    row = lambda i: (i, 0)
    const = lambda i: (0, 0)
    return pl.pallas_call(
        _ctxproj_kernel,
        out_shape=(jax.ShapeDtypeStruct((rows, KV_WIDTH), _BF16),
                   jax.ShapeDtypeStruct((rows, KV_WIDTH), _BF16)),
        grid=(rows // tm,),
        in_specs=[pl.BlockSpec((tm, D_MODEL), row),
                  pl.BlockSpec((1, N_MOD, D_MODEL), lambda i: (BATCH, 0, 0)),
                  pl.BlockSpec((1, D_MODEL), const),
                  pl.BlockSpec((D_MODEL, 2 * KV_WIDTH), const),
                  pl.BlockSpec((1, 2 * KV_WIDTH), const)],
        out_specs=(pl.BlockSpec((tm, KV_WIDTH), row),
                   pl.BlockSpec((tm, KV_WIDTH), row)),
        compiler_params=pltpu.CompilerParams(dimension_semantics=("arbitrary",),
                                             vmem_limit_bytes=VMEM_LIMIT),
        name="ctxproj",
    )(ctx2, mod3, g_pre, w_kv_b, b_kv2)


def _attn_kernel(sink_ref, q_ref, kp_ref, km_ref, kn_ref, vp_ref, vm_ref, vn_ref,
                 kc_ref, vc_ref, o_ref):
    n = pl.program_id(1)
    nstep = pl.num_programs(1)
    blk = ATTN_BLOCK
    nq = ATTN_STEP_BLOCKS
    rows2 = 2 * blk
    r = lax.broadcasted_iota(jnp.int32, (rows2, blk), 0) & (blk - 1)
    c = lax.broadcasted_iota(jnp.int32, (rows2, blk), 1)
    tri_prev = c >= r
    tri_next = c <= r
    row1 = lax.broadcasted_iota(jnp.int32, (rows2, 1), 0)
    lane_q = lax.broadcasted_iota(jnp.int32, (blk, LANES), 1)
    n_grp = ATTN_WIDTH // LANES
    k_blocks = [kp_ref[...]] + [km_ref[t * blk:(t + 1) * blk, :] for t in range(nq)] + [kn_ref[...]]
    v_blocks = [vp_ref[...]] + [vm_ref[t * blk:(t + 1) * blk, :] for t in range(nq)] + [vn_ref[...]]

    def scores(t):
        keys = jnp.concatenate(k_blocks[t:t + 3] + [kc_ref[...]], axis=0)
        q_rows = []
        for j in range(n_grp):
            qg = q_ref[t * blk:(t + 1) * blk, j * LANES:(j + 1) * LANES]
            zero = jnp.zeros_like(qg)
            q_rows += [jnp.where(lane_q < HEAD_DIM, qg, zero), jnp.where(lane_q >= HEAD_DIM, qg, zero)]
        return lax.dot_general(jnp.concatenate(q_rows, axis=0), keys, (((1,), (1,)), ((), ())),
                               preferred_element_type=_F32)

    def softmax(t, s_all):
        mask_a = (tri_prev & (n > 0)) if t == 0 else tri_prev
        mask_c = (tri_next & (n < nstep - 1)) if t == nq - 1 else tri_next
        p_rows, denoms = [], []
        for j in range(n_grp):
            s = s_all[j * rows2:(j + 1) * rows2]
            s = jnp.concatenate([jnp.where(mask_a, s[:, 0:blk], NEG_BIG), s[:, blk:2 * blk],
                                 jnp.where(mask_c, s[:, 2 * blk:3 * blk], NEG_BIG), s[:, 3 * blk:]],
                                axis=1)
            sk = jnp.where(row1 < blk, sink_ref[2 * j], sink_ref[2 * j + 1])
            m = jnp.maximum(jnp.max(s, axis=-1, keepdims=True), sk)
            p = jnp.exp2(s - m)
            denoms.append(jnp.sum(p, axis=-1, keepdims=True) + jnp.exp2(sk - m))
            p_rows.append(p.astype(_BF16))
        return jnp.concatenate(p_rows, axis=0), denoms

    def weighted_values(t, p_all, denoms):
        vals = jnp.concatenate(v_blocks[t:t + 3] + [vc_ref[...]], axis=0)
        o_all = jnp.dot(p_all, vals, preferred_element_type=_F32)
        for j in range(n_grp):
            o2 = o_all[j * rows2:(j + 1) * rows2] / denoms[j]
            og = jnp.where(lane_q < HEAD_DIM, o2[0:blk], o2[blk:])
            o_ref[t * blk:(t + 1) * blk, j * LANES:(j + 1) * LANES] = og.astype(_BF16)

    s_next = scores(0)
    probs = None
    for t in range(nq + 1):
        s_cur, s_next = s_next, (scores(t + 1) if t + 1 < nq else None)
        if probs is not None:
            weighted_values(t - 1, *probs)
        probs = softmax(t, s_cur) if t < nq else None


def _attn_call(sink_p, q, k, v, kc, vc):
    nblk = SEQ // ATTN_BLOCK
    nq = ATTN_STEP_BLOCKS
    nstep = nblk // nq
    own = lambda b, n: (b * nstep + n, 0)
    prev = lambda b, n: (b * nblk + jnp.maximum(nq * n - 1, 0), 0)
    nxt = lambda b, n: (b * nblk + jnp.minimum(nq * n + nq, nblk - 1), 0)
    ctx = lambda b, n: (b, 0)
    kv1 = (ATTN_BLOCK, KV_WIDTH)
    kvm = (nq * ATTN_BLOCK, KV_WIDTH)
    return pl.pallas_call(
        _attn_kernel,
        out_shape=jax.ShapeDtypeStruct((TOKENS, ATTN_WIDTH), _BF16),
        grid=(BATCH, nstep),
        in_specs=[pl.BlockSpec(memory_space=pltpu.SMEM),
                  pl.BlockSpec((nq * ATTN_BLOCK, ATTN_WIDTH), own),
                  pl.BlockSpec(kv1, prev), pl.BlockSpec(kvm, own), pl.BlockSpec(kv1, nxt),
                  pl.BlockSpec(kv1, prev), pl.BlockSpec(kvm, own), pl.BlockSpec(kv1, nxt),
                  pl.BlockSpec((CTX_LEN, KV_WIDTH), ctx),
                  pl.BlockSpec((CTX_LEN, KV_WIDTH), ctx)],
        out_specs=pl.BlockSpec((nq * ATTN_BLOCK, ATTN_WIDTH), own),
        compiler_params=pltpu.CompilerParams(dimension_semantics=("arbitrary", "arbitrary"),
                                             vmem_limit_bytes=VMEM_LIMIT),
        name="attn",
    )(sink_p, q, k, k, k, v, v, v, kc, vc)


def _post_kernel(x_ref, ao_ref, su_ref, sv_ref, mod_ref, lng_ref, lnb_ref, ws_ref, bs_ref,
                 ga_ref, gs_ref, woa_ref, wos_ref, bo_ref, gpost_ref, gpre_ref, wr_ref, br_ref,
                 tri_ref,
                 xmid_ref, h2_ref, idx_ref, gate_ref, rank_ref, cnt_ref,
                 mixed_ref, carry_ref):
    tm = x_ref.shape[0]
    m = mod_ref[0]

    @pl.when(pl.program_id(0) == 0)
    def _():
        carry_ref[...] = jnp.zeros_like(carry_ref)

    n_sub = POST_SPLIT
    ts = tm // n_sub
    tok_per_row = LANES // TOP_K
    lane = lax.broadcasted_iota(jnp.int32, (SGU_CHUNK, LANES), 1)
    st = [dict() for _ in range(n_sub)]
    carry = [carry_ref[...]]

    def rows(h):
        return slice(h * ts, (h + 1) * ts)

    def s1(h):
        gv = _gelu_tanh(sv_ref[rows(h), :])
        mu = jnp.mean(gv, axis=-1, keepdims=True)
        gc = gv - mu
        var = jnp.mean(gc * gc, axis=-1, keepdims=True)
        st[h]["vb"] = (gc * lax.rsqrt(var + EPS) * lng_ref[...] + lnb_ref[...]).astype(_BF16)

    def s2(h):
        vb = st[h].pop("vb")
        for c in range(ts // SGU_CHUNK):
            r0 = c * SGU_CHUNK
            for p in range(SGU_WIDTH // LANES):
                l0 = p * LANES
                r = jnp.dot(ws_ref[p], vb[r0:r0 + SGU_CHUNK, l0:l0 + LANES], preferred_element_type=_F32)
                mixed = jnp.where(lane < SGU_HEAD_DIM, r[0:SGU_CHUNK], r[SGU_CHUNK:])
                mixed_ref[h * ts + r0:h * ts + r0 + SGU_CHUNK, l0:l0 + LANES] = mixed + bs_ref[:, l0:l0 + LANES]
        sgu_o = _gelu_tanh(su_ref[rows(h), :]) * mixed_ref[rows(h), :]
        st[h]["oa"] = _rms(ao_ref[rows(h), :].astype(_F32), ga_ref[...]).astype(_BF16)
        st[h]["os"] = _rms(sgu_o, gs_ref[...]).astype(_BF16)

    def s3(h):
        st[h]["mix"] = (jnp.dot(st[h].pop("oa"), woa_ref[...], preferred_element_type=_F32)
                        + jnp.dot(st[h].pop("os"), wos_ref[...], preferred_element_type=_F32) + bo_ref[...])

    def s4(h):
        x_mid = x_ref[rows(h), :] + m[2:3] * _rms(st[h].pop("mix"), gpost_ref[...])
        xmid_ref[rows(h), :] = x_mid
        h2 = _rms(x_mid, gpre_ref[...]) * (1.0 + m[4:5]) + m[3:4]
        for cc in range(ROW_SUB):
            h2_ref[pl.ds(h * ts * ROW_SUB + cc, ts, stride=ROW_SUB), :] = h2[:, cc * LANES:(cc + 1) * LANES]
        h_hi = h2.astype(_BF16)
        st[h]["h_hi"] = h_hi
        st[h]["h_lo"] = (h2 - h_hi.astype(_F32)).astype(_BF16)

    def s5(h):
        r = (jnp.dot(st[h].pop("h_hi"), wr_ref[...], preferred_element_type=_F32)
             + jnp.dot(st[h].pop("h_lo"), wr_ref[...], preferred_element_type=_F32))
        st[h]["lg"] = r + pltpu.roll(r, LANES - N_EXPERTS, 1) + br_ref[...]

    def s6(h):
        lg = st[h].pop("lg")
        lane_r = lax.broadcasted_iota(jnp.int32, lg.shape, 1)
        lane_f = lane_r.astype(_F32)
        tops, hots = [], []
        for _k in range(TOP_K):
            mx = jnp.max(lg, axis=-1, keepdims=True)
            pick = jnp.min(jnp.where(lg == mx, lane_f, float(LANES)), axis=-1, keepdims=True)
            hot = lane_f == pick
            tops.append((mx, pick))
            hots.append(hot)
            lg = jnp.where(hot, 2.0 * NEG_BIG, lg)
        es = [jnp.exp(t[0] - tops[0][0]) for t in tops]
        esum = es[0] + es[1] + es[2] + es[3]
        multi = jnp.zeros(lg.shape, _F32)
        for hot in hots:
            multi = multi + jnp.where(hot, 1.0, 0.0)
        cum = jnp.dot(tri_ref[...], multi.astype(_BF16), preferred_element_type=_F32) + carry[0]
        row_r = lax.broadcasted_iota(jnp.int32, lg.shape, 0)
        lane_base = (row_r & (tok_per_row - 1)) * TOP_K
        gate_o = jnp.zeros(lg.shape, _F32)
        idx_e = jnp.zeros(lg.shape, _F32)
        rank_hi_e = jnp.zeros(lg.shape, _F32)
        rank_lo_e = jnp.zeros(lg.shape, _F32)
        for kk in range(TOP_K):
            rk = jnp.sum(jnp.where(hots[kk], cum, 0.0), axis=-1, keepdims=True)
            rk_hi = jnp.floor(rk * (1.0 / 256.0))
            here = lane_r == lane_base + kk
            gate_o = jnp.where(lane_r == kk, es[kk] / esum, gate_o)
            idx_e = jnp.where(here, tops[kk][1], idx_e)
            rank_hi_e = jnp.where(here, rk_hi, rank_hi_e)
            rank_lo_e = jnp.where(here, rk - 256.0 * rk_hi, rank_lo_e)
        fr = lax.broadcasted_iota(jnp.int32, (ts // tok_per_row, ts), 0)
        fc = lax.broadcasted_iota(jnp.int32, (ts // tok_per_row, ts), 1)
        fold = jnp.where(lax.shift_right_logical(fc, tok_per_row.bit_length() - 1) == fr, 1.0, 0.0).astype(_BF16)
        fs = slice(h * ts // tok_per_row, (h + 1) * ts // tok_per_row)
        idx_ref[fs, :] = jnp.dot(fold, idx_e.astype(_BF16), preferred_element_type=_F32).astype(jnp.int32)
        rank_ref[fs, :] = (256.0 * jnp.dot(fold, rank_hi_e.astype(_BF16), preferred_element_type=_F32)
                           + jnp.dot(fold, rank_lo_e.astype(_BF16), preferred_element_type=_F32)).astype(jnp.int32)
        gate_ref[rows(h), :] = gate_o
        carry[0] = carry[0] + jnp.sum(multi, axis=0, keepdims=True)

    order = sorted(((k + 2.5 * h, h, k) for h in range(n_sub) for k in range(6)))
    stages = (s1, s2, s3, s4, s5, s6)
    for _, h, k in order:
        stages[k](h)
    carry_ref[...] = carry[0]
    cnt_ref[...] = carry[0].astype(jnp.int32)


def _post_call(x2, ao, su, sv, mod3, lng, lnb, ws2, bsf, ga, gs, woa, wos, bo, gpost, gpre,
               wr, br, tri):
    tm = POST_TILE
    tiles_per_batch = SEQ // tm
    row = lambda i: (i, 0)
    const = lambda i: (0, 0)
    const3 = lambda i: (0, 0, 0)
    return pl.pallas_call(
        _post_kernel,
        out_shape=(jax.ShapeDtypeStruct((TOKENS, D_MODEL), _F32),
                   jax.ShapeDtypeStruct((TOKENS * ROW_SUB, LANES), _F32),
                   jax.ShapeDtypeStruct((TOKENS * TOP_K // LANES, LANES), jnp.int32),
                   jax.ShapeDtypeStruct((TOKENS, LANES), _F32),
                   jax.ShapeDtypeStruct((TOKENS * TOP_K // LANES, LANES), jnp.int32),
                   jax.ShapeDtypeStruct((1, LANES), jnp.int32)),
        grid=(TOKENS // tm,),
        in_specs=[pl.BlockSpec((tm, D_MODEL), row),
                  pl.BlockSpec((tm, ATTN_WIDTH), row),
                  pl.BlockSpec((tm, SGU_WIDTH), row),
                  pl.BlockSpec((tm, SGU_WIDTH), row),
                  pl.BlockSpec((1, N_MOD, D_MODEL), lambda i: (i // tiles_per_batch, 0, 0)),
                  pl.BlockSpec((1, SGU_WIDTH), const),
                  pl.BlockSpec((1, SGU_WIDTH), const),
                  pl.BlockSpec((SGU_WIDTH // LANES, 2 * SGU_CHUNK, SGU_CHUNK), const3),
                  pl.BlockSpec((SGU_CHUNK, SGU_WIDTH), const),
                  pl.BlockSpec((1, ATTN_WIDTH), const),
                  pl.BlockSpec((1, SGU_WIDTH), const),
                  pl.BlockSpec((ATTN_WIDTH, D_MODEL), const),
                  pl.BlockSpec((SGU_WIDTH, D_MODEL), const),
                  pl.BlockSpec((1, D_MODEL), const),
                  pl.BlockSpec((1, D_MODEL), const),
                  pl.BlockSpec((1, D_MODEL), const),
                  pl.BlockSpec((D_MODEL, LANES), const),
                  pl.BlockSpec((1, LANES), const),
                  pl.BlockSpec((tm // POST_SPLIT, tm // POST_SPLIT), const)],
        out_specs=(pl.BlockSpec((tm, D_MODEL), row),
                   pl.BlockSpec((tm * ROW_SUB, LANES), row),
                   pl.BlockSpec((tm * TOP_K // LANES, LANES), row),
                   pl.BlockSpec((tm, LANES), row),
                   pl.BlockSpec((tm * TOP_K // LANES, LANES), row),
                   pl.BlockSpec((1, LANES), const)),
        scratch_shapes=[pltpu.VMEM((tm, SGU_WIDTH), _F32),
                        pltpu.VMEM((1, LANES), _F32)],
        compiler_params=pltpu.CompilerParams(dimension_semantics=("arbitrary",),
                                             vmem_limit_bytes=VMEM_LIMIT),
        name="post",
    )(x2, ao, su, sv, mod3, lng, lnb, ws2, bsf, ga, gs, woa, wos, bo, gpost, gpre, wr, br, tri)


def _dest_kernel(ps_ref, idx_ref, rank_ref, o_ref):
    idx = idx_ref[...]
    dest = rank_ref[...]
    for e in range(N_EXPERTS):
        dest = dest + jnp.where(idx == e, ps_ref[e], 0)
    o_ref[...] = dest


def _dest_call(pad_start, idx_flat, rank_flat):
    return pl.pallas_call(
        _dest_kernel,
        out_shape=jax.ShapeDtypeStruct(idx_flat.shape, jnp.int32),
        grid=(1,),
        in_specs=[pl.BlockSpec(memory_space=pltpu.SMEM),
                  pl.BlockSpec(idx_flat.shape, lambda i: (0, 0)),
                  pl.BlockSpec(rank_flat.shape, lambda i: (0, 0))],
        out_specs=pl.BlockSpec(idx_flat.shape, lambda i: (0, 0)),
        compiler_params=pltpu.CompilerParams(dimension_semantics=("arbitrary",),
                                             vmem_limit_bytes=VMEM_LIMIT),
        name="dest",
    )(pad_start, idx_flat, rank_flat)


def _dispatch_kernel(fs_ref, fl_ref, nu_ref, dest_ref, h2_ref, xs_ref, zero_ref, sem, zsem):
    tm = h2_ref.shape[0] // ROW_SUB
    n_token_steps = TOKENS // tm
    i = pl.program_id(0)

    @pl.when(i < n_token_steps)
    def _():
        def issue(r, carry):
            for kk in range(TOP_K):
                d = pl.multiple_of(dest_ref[r * TOP_K + kk] * ROW_SUB, ROW_SUB)
                pltpu.make_async_copy(h2_ref.at[pl.ds(pl.multiple_of(r * ROW_SUB, ROW_SUB), ROW_SUB)],
                                      xs_ref.at[pl.ds(d, ROW_SUB)], sem).start(priority=kk % 2)
            return carry

        lax.fori_loop(0, tm, issue, 0, unroll=8)
        for kk in range(TOP_K):
            pltpu.make_async_copy(h2_ref, xs_ref.at[pl.ds(0, tm * ROW_SUB)], sem).wait()

    @pl.when(i == n_token_steps)
    def _():
        zero_ref[...] = jnp.zeros_like(zero_ref)
        block_rows = EXPERT_BLOCK * ROW_SUB

        def pad_run(e, wait):
            pos = fs_ref[e]
            length = fl_ref[e]
            for bit in reversed(range(EXPERT_BLOCK.bit_length() - 1)):
                size = 1 << bit
                take = length & size

                @pl.when(take != 0)
                def _():
                    cp = pltpu.make_async_copy(
                        zero_ref.at[pl.ds(0, size * ROW_SUB)],
                        xs_ref.at[pl.ds(pl.multiple_of(pos * ROW_SUB, ROW_SUB), size * ROW_SUB)], zsem)
                    if wait:
                        cp.wait()
                    else:
                        cp.start()

                pos = pos + take

        def tail_block(blk, wait):
            cp = pltpu.make_async_copy(
                zero_ref, xs_ref.at[pl.ds(pl.multiple_of(blk * block_rows, block_rows), block_rows)], zsem)
            if wait:
                cp.wait()
            else:
                cp.start()

        for wait in (False, True):
            lax.fori_loop(0, N_EXPERTS, lambda e, c, w=wait: (pad_run(e, w), c)[1], 0)
            lax.fori_loop(nu_ref[0], N_EXPERT_BLOCKS, lambda blk, c, w=wait: (tail_block(blk, w), c)[1], 0)


def _dispatch_call(fill_start, fill_len, n_used, dest_flat, h2):
    tm = SCATTER_TILE
    n_token_steps = TOKENS // tm
    grid_spec = pltpu.PrefetchScalarGridSpec(
        num_scalar_prefetch=3,
        grid=(n_token_steps + 1,),
        in_specs=[pl.BlockSpec((tm * TOP_K,), lambda i, fs, fl, nu: (jnp.minimum(i, n_token_steps - 1),),
                               memory_space=pltpu.SMEM),
                  pl.BlockSpec((tm * ROW_SUB, LANES),
                               lambda i, fs, fl, nu: (jnp.minimum(i, n_token_steps - 1), 0))],
        out_specs=pl.BlockSpec(memory_space=pl.ANY),
        scratch_shapes=[pltpu.VMEM((EXPERT_BLOCK * ROW_SUB, LANES), _F32),
                        pltpu.SemaphoreType.DMA,
                        pltpu.SemaphoreType.DMA])
    return pl.pallas_call(
        _dispatch_kernel,
        out_shape=jax.ShapeDtypeStruct((SORTED_ROWS * ROW_SUB, LANES), _F32),
        grid_spec=grid_spec,
        compiler_params=pltpu.CompilerParams(dimension_semantics=("arbitrary",),
                                             vmem_limit_bytes=VMEM_LIMIT),
        name="dispatch",
    )(fill_start, fill_len, n_used, dest_flat, h2)


def _expert_kernel(be_ref, slot_ref, first_ref, nxt_ref, nu_ref,
                   xs_ref, wgu_hbm, bgu0_ref, bgu1_ref, wd_hbm, bd0_ref, bd1_ref, y_ref,
                   wgu_stage, wd_stage, wgu_bf, wd_bf, sem):
    p = pl.program_id(0)
    half = EXPERT_BLOCK * ROW_SUB
    n_pairs_used = (nu_ref[0] + 1) // 2

    def stage_copies(e):
        return (pltpu.make_async_copy(wgu_hbm.at[e], wgu_stage, sem.at[0]),
                pltpu.make_async_copy(wd_hbm.at[e], wd_stage, sem.at[1]))

    @pl.when(p == 0)
    def _():
        for cp in stage_copies(be_ref[0]):
            cp.start()

    for b in (2 * p, 2 * p + 1):
        @pl.when((first_ref[b] == 1) & (b < nu_ref[0]))
        def _():
            for cp in stage_copies(be_ref[b]):
                cp.wait()
            wgu_bf[slot_ref[b]] = wgu_stage[...].astype(_BF16)
            wd_bf[slot_ref[b]] = wd_stage[...].astype(_BF16)

            @pl.when(nxt_ref[b] < N_EXPERTS)
            def _():
                for cp in stage_copies(nxt_ref[b]):
                    cp.start()

    def ffn(x, slot, bgu_ref, bd_ref):
        gu = jnp.dot(x, wgu_bf[slot], preferred_element_type=_F32) + bgu_ref[0]
        gate = jnp.minimum(gu[:, 0:D_FF_EXPERT], SWIGLU_LIMIT)
        up = jnp.clip(gu[:, D_FF_EXPERT:], -SWIGLU_LIMIT, SWIGLU_LIMIT)
        act = (up + 1.0) * gate * (1.0 / (1.0 + jnp.exp(-SWIGLU_ALPHA * gate)))
        return jnp.dot(act.astype(_BF16), wd_bf[slot], preferred_element_type=_F32) + bd_ref[0]

    def load_rows(which):
        return jnp.concatenate([xs_ref[pl.ds(which * half + cc, EXPERT_BLOCK, stride=ROW_SUB), :]
                                for cc in range(ROW_SUB)], axis=1).astype(_BF16)

    def store_rows(which, y):
        for cc in range(ROW_SUB):
            y_ref[pl.ds(which * half + cc, EXPERT_BLOCK, stride=ROW_SUB), :] = y[:, cc * LANES:(cc + 1) * LANES]

    @pl.when(p < n_pairs_used)
    def _():
        y0 = ffn(load_rows(0), slot_ref[2 * p], bgu0_ref, bd0_ref)
        x1 = load_rows(1)
        store_rows(0, y0)
        store_rows(1, ffn(x1, slot_ref[2 * p + 1], bgu1_ref, bd1_ref))

    @pl.when(p >= n_pairs_used)
    def _():
        y_ref[...] = jnp.zeros_like(y_ref)


def _expert_call(block_expert, block_slot, block_first, block_next, n_used, xs, wgu, bgu3, wd, bd3):
    tb = 2 * EXPERT_BLOCK
    live = lambda p, be, sl, fi, nx, nu: (jnp.minimum(p, (nu[0] - 1) // 2), 0)
    bsel0 = lambda p, be, sl, fi, nx, nu: (be[2 * p], 0, 0)
    bsel1 = lambda p, be, sl, fi, nx, nu: (be[2 * p + 1], 0, 0)
    grid_spec = pltpu.PrefetchScalarGridSpec(
        num_scalar_prefetch=5,
        grid=(N_EXPERT_BLOCKS // 2,),
        in_specs=[pl.BlockSpec((tb * ROW_SUB, LANES), live),
                  pl.BlockSpec(memory_space=pl.ANY),
                  pl.BlockSpec((1, 1, 2 * D_FF_EXPERT), bsel0),
                  pl.BlockSpec((1, 1, 2 * D_FF_EXPERT), bsel1),
                  pl.BlockSpec(memory_space=pl.ANY),
                  pl.BlockSpec((1, 1, D_MODEL), bsel0),
                  pl.BlockSpec((1, 1, D_MODEL), bsel1)],
        out_specs=pl.BlockSpec((tb * ROW_SUB, LANES), lambda p, be, sl, fi, nx, nu: (p, 0)),
        scratch_shapes=[pltpu.VMEM((D_MODEL, 2 * D_FF_EXPERT), _F32),
                        pltpu.VMEM((D_FF_EXPERT, D_MODEL), _F32),
                        pltpu.VMEM((2, D_MODEL, 2 * D_FF_EXPERT), _BF16),
                        pltpu.VMEM((2, D_FF_EXPERT, D_MODEL), _BF16),
                        pltpu.SemaphoreType.DMA((2,))])
    return pl.pallas_call(
        _expert_kernel,
        out_shape=jax.ShapeDtypeStruct((SORTED_ROWS * ROW_SUB, LANES), _F32),
        grid_spec=grid_spec,
        compiler_params=pltpu.CompilerParams(dimension_semantics=("arbitrary",),
                                             vmem_limit_bytes=VMEM_LIMIT),
        name="expert",
    )(block_expert, block_slot, block_first, block_next, n_used, xs, wgu, bgu3, bgu3, wd, bd3, bd3)


def _combine_kernel(dest_ref, dest_next_ref, gate_ref, xmid_ref, mod_ref, g_ref, y_ref, o_ref,
                    buf_ref, sem):
    tm = xmid_ref.shape[0]
    m = mod_ref[0]
    i = pl.program_id(0)
    n = pl.num_programs(0)

    def gather_rows(idx_ref, slot):
        def issue(r, carry):
            for kk in range(TOP_K):
                d = pl.multiple_of(idx_ref[r * TOP_K + kk] * ROW_SUB, ROW_SUB)
                pltpu.make_async_copy(
                    y_ref.at[pl.ds(d, ROW_SUB)],
                    buf_ref.at[slot, kk, pl.ds(r * GATHER_PITCH, ROW_SUB)],
                    sem.at[slot]).start(priority=kk % 2)
            return carry

        lax.fori_loop(0, tm, issue, 0, unroll=8)

    @pl.when(i == 0)
    def _():
        gather_rows(dest_ref, 0)

    @pl.when(i + 1 < n)
    def _():
        gather_rows(dest_next_ref, (i + 1) % 2)

    slot = i % 2
    for kk in range(TOP_K):
        pltpu.make_async_copy(y_ref.at[pl.ds(0, tm * ROW_SUB)], buf_ref.at[slot, kk, pl.ds(0, tm * ROW_SUB)],
                              sem.at[slot]).wait()
    g = gate_ref[...]
    pieces = []
    for cc in range(ROW_SUB):
        piece = buf_ref[slot, 0, pl.ds(cc, tm, stride=GATHER_PITCH), :] * g[:, 0:1]
        for kk in range(1, TOP_K):
            piece = piece + buf_ref[slot, kk, pl.ds(cc, tm, stride=GATHER_PITCH), :] * g[:, kk:kk + 1]
        pieces.append(piece)
    ffn = jnp.concatenate(pieces, axis=1)
    o_ref[...] = xmid_ref[...] + m[5:6] * _rms(ffn, g_ref[...])


def _combine_call(dest_flat, gates, x_mid, mod3, gpost_ffn, y_rows):
    tm = MOVE_TILE
    tiles_per_batch = SEQ // tm
    n_steps = TOKENS // tm
    return pl.pallas_call(
        _combine_kernel,
        out_shape=jax.ShapeDtypeStruct((TOKENS, D_MODEL), _F32),
        grid=(n_steps,),
        in_specs=[pl.BlockSpec((tm * TOP_K,), lambda i: (i,), memory_space=pltpu.SMEM),
                  pl.BlockSpec((tm * TOP_K,), lambda i: (jnp.minimum(i + 1, n_steps - 1),),
                               memory_space=pltpu.SMEM),
                  pl.BlockSpec((tm, LANES), lambda i: (i, 0)),
                  pl.BlockSpec((tm, D_MODEL), lambda i: (i, 0)),
                  pl.BlockSpec((1, N_MOD, D_MODEL), lambda i: (i // tiles_per_batch, 0, 0)),
                  pl.BlockSpec((1, D_MODEL), lambda i: (0, 0)),
                  pl.BlockSpec(memory_space=pl.ANY)],
        out_specs=pl.BlockSpec((tm, D_MODEL), lambda i: (i, 0)),
        scratch_shapes=[pltpu.VMEM((2, TOP_K, tm * GATHER_PITCH, LANES), _F32),
                        pltpu.SemaphoreType.DMA((2,))],
        compiler_params=pltpu.CompilerParams(dimension_semantics=("arbitrary",),
                                             vmem_limit_bytes=VMEM_LIMIT),
        name="combine",
    )(dest_flat, dest_flat, gates, x_mid, mod3, gpost_ffn, y_rows)


def _rope_tables():
    pos = np.arange(SEQ)
    pos_row = (pos // GRID_W).astype(np.float64)
    pos_col = (pos % GRID_W).astype(np.float64)
    n_freq = HEAD_DIM // 4
    inv_freq = ROPE_THETA ** (-np.arange(n_freq, dtype=np.float64) / n_freq)
    d = np.arange(LANES) % HEAD_DIM
    f = inv_freq[d % n_freq]
    ang = np.where((d < HEAD_DIM // 2)[None, :], pos_row[:, None] * f[None, :], pos_col[:, None] * f[None, :])
    sign = np.where((d % (HEAD_DIM // 2)) < n_freq, -1.0, 1.0)
    return (jnp.asarray(np.cos(ang), dtype=_F32), jnp.asarray(np.sin(ang) * sign[None, :], dtype=_F32))


def _perm_heads(a, axis):
    shape = a.shape
    a = a.reshape(shape[:axis] + (N_HEADS, HEAD_DIM) + shape[axis + 1:])
    a = jnp.take(a, jnp.array(HEAD_PERM), axis=axis)
    return a.reshape(shape)


def kernel(x, c, ctx, c_ctx, w_ada, b_ada, g_pre_mix, g_post_mix, g_pre_ffn, g_post_ffn, w_in, b_in, attn_sink, sgu_ln_g, sgu_ln_b, sgu_w, sgu_b, g_attn_out, g_sgu_out, w_out, b_out, w_router, b_router, w_gate_up, b_gate_up, w_down, b_down):
    l = 0
    x2 = x.reshape(TOKENS, D_MODEL)
    ctx2 = ctx.reshape(BATCH * CTX_LEN, D_MODEL)

    cc = jnp.zeros((16, D_MODEL), _F32).at[:BATCH].set(c).at[BATCH].set(c_ctx)
    mod = _ada_call(cc, w_ada[l], b_ada[l].reshape(1, -1))
    mod3 = mod.reshape(16, N_MOD, D_MODEL)

    w_in_l = w_in[l]
    b_in_l = b_in[l]
    w_in_p = jnp.concatenate([_perm_heads(w_in_l[:, :ATTN_WIDTH], 1), w_in_l[:, ATTN_WIDTH:]], axis=1)
    b_in_p = jnp.concatenate([_perm_heads(b_in_l[:ATTN_WIDTH], 0), b_in_l[ATTN_WIDTH:]], axis=0)
    cos_t, sin_t = _rope_tables()
    g_pre = g_pre_mix[l].reshape(1, -1)

    q, k, v, su, sv = _inproj_call(x2, mod3, g_pre, w_in_p.astype(_BF16), b_in_p.reshape(1, -1), cos_t, sin_t)
    kc, vc = _ctxproj_call(ctx2, mod3, g_pre, w_in_l[:, KV_START:KV_END].astype(_BF16),
                           b_in_l[KV_START:KV_END].reshape(1, -1))

    sink_p = jnp.take(attn_sink[l], jnp.array(HEAD_PERM)) * LOG2_E
    ao = _attn_call(sink_p, q, k, v, kc, vc)

    ws2 = sgu_w[l].reshape(SGU_WIDTH // LANES, 2 * SGU_CHUNK, SGU_CHUNK).astype(_BF16)
    bsf = jnp.repeat(sgu_b[l].T, SGU_HEAD_DIM, axis=1)
    w_out_l = w_out[l]
    woa = _perm_heads(w_out_l[:ATTN_WIDTH], 0).astype(_BF16)
    wos = w_out_l[ATTN_WIDTH:].astype(_BF16)
    ga = _perm_heads(g_attn_out[l], 0).reshape(1, -1)
    wr_hi = w_router[l].astype(_BF16)
    wr_lo = (w_router[l] - wr_hi.astype(_F32)).astype(_BF16)
    wr = (jnp.zeros((D_MODEL, LANES), _BF16).at[:, :N_EXPERTS].set(wr_hi)
          .at[:, N_EXPERTS:2 * N_EXPERTS].set(wr_lo))
    br = jnp.full((1, LANES), NEG_BIG, _F32).at[0, :N_EXPERTS].set(b_router[l])
    ii = np.arange(POST_TILE // POST_SPLIT)
    tri = jnp.asarray(ii[None, :] < ii[:, None], dtype=_BF16)

    x_mid, h2, idx_o, gate_o, rank_o, cnt_o = _post_call(
        x2, ao, su, sv, mod3, sgu_ln_g[l].reshape(1, -1), sgu_ln_b[l].reshape(1, -1), ws2, bsf,
        ga, g_sgu_out[l].reshape(1, -1), woa, wos, b_out[l].reshape(1, -1),
        g_post_mix[l].reshape(1, -1), g_pre_ffn[l].reshape(1, -1), wr, br, tri)

    counts = cnt_o[0, :N_EXPERTS]
    padded = (counts + EXPERT_BLOCK - 1) // EXPERT_BLOCK * EXPERT_BLOCK
    pad_end = jnp.cumsum(padded)
    pad_start = pad_end - padded
    dest = _dest_call(pad_start.astype(jnp.int32), idx_o, rank_o).reshape(-1)
    n_used = (pad_end[-1:] // EXPERT_BLOCK).astype(jnp.int32)

    xs = _dispatch_call((pad_start + counts).astype(jnp.int32), (padded - counts).astype(jnp.int32),
                        n_used, dest, h2)

    e_ids = jnp.arange(N_EXPERTS, dtype=jnp.int32)
    used = padded > 0
    cand = jnp.where(used, e_ids, N_EXPERTS)
    next_used = jnp.min(jnp.where(e_ids[None, :] > e_ids[:, None], cand[None, :], N_EXPERTS), axis=1)
    ordinal = jnp.cumsum(used.astype(jnp.int32)) - 1
    block_id = jnp.arange(N_EXPERT_BLOCKS, dtype=jnp.int32)
    block_start = block_id * EXPERT_BLOCK
    sched_start = jnp.minimum(block_start, (n_used[0] - 1) * EXPERT_BLOCK)
    sched_expert = jnp.minimum(jnp.sum(pad_end[None, :] <= sched_start[:, None], axis=1),
                               N_EXPERTS - 1).astype(jnp.int32)
    onehot = e_ids[None, :] == sched_expert[:, None]
    pick = lambda table: jnp.sum(jnp.where(onehot, table[None, :], 0), axis=1)
    block_slot = (pick(ordinal) & 1).astype(jnp.int32)
    block_first = ((block_start == pick(pad_start)) & (block_id < n_used[0])).astype(jnp.int32)
    block_next = pick(next_used).astype(jnp.int32)
    y_rows = _expert_call(sched_expert, block_slot, block_first, block_next, n_used, xs,
                          w_gate_up[l], b_gate_up[l].reshape(N_EXPERTS, 1, -1),
                          w_down[l], b_down[l].reshape(N_EXPERTS, 1, -1))
    out = _combine_call(dest, gate_o, x_mid, mod3, g_post_ffn[l].reshape(1, -1), y_rows)
    return out.reshape(BATCH, SEQ, D_MODEL)
```

```python
import jax
import jax.numpy as jnp
import numpy as np
from jax import lax
from jax.experimental import pallas as pl
from jax.experimental.pallas import tpu as pltpu

D_MODEL = 1024
BATCH = 8
SEQ = 4096
TOKENS = BATCH * SEQ
GRID_W = 64
CTX_LEN = 256
N_HEADS = 8
N_KV_HEADS = 2
HEAD_DIM = 64
ATTN_WIDTH = N_HEADS * HEAD_DIM
KV_WIDTH = N_KV_HEADS * HEAD_DIM
WINDOW = 128
ATTN_BLOCK = 128
SGU_HEADS = 8
SGU_HEAD_DIM = 64
SGU_WIDTH = SGU_HEADS * SGU_HEAD_DIM
SGU_CHUNK = 128
KV_START = ATTN_WIDTH
KV_END = ATTN_WIDTH + 2 * KV_WIDTH
IN_WIDTH = KV_END + 2 * SGU_WIDTH
N_EXPERTS = 32
TOP_K = 4
D_FF_EXPERT = 1024
SWIGLU_LIMIT = 7.0
SWIGLU_ALPHA = 1.702
ROPE_THETA = 10000.0
EPS = 1e-6
N_MOD = 6

LANES = 128
NEG_BIG = -1e30
VMEM_LIMIT = 56 * 1024 * 1024

ROW_TILE = 1024
SCATTER_TILE = 2048
MOVE_TILE = 256
ATTN_STEP_BLOCKS = 8
EXPERT_BLOCK = 512
POST_TILE = 1024
POST_SPLIT = 4
N_EXPERT_BLOCKS = TOKENS * TOP_K // EXPERT_BLOCK + N_EXPERTS
SORTED_ROWS = N_EXPERT_BLOCKS * EXPERT_BLOCK
HEAD_PERM = (0, 4, 1, 5, 2, 6, 3, 7)

LOG2_E = 1.4426950408889634
QK_SCALE = LOG2_E * HEAD_DIM ** -0.5

_F32 = jnp.float32
_BF16 = jnp.bfloat16


def _rms(x, g):
    ms = jnp.mean(x * x, axis=-1, keepdims=True)
    return x * lax.rsqrt(ms + EPS) * g


ROW_SUB = D_MODEL // LANES
GATHER_PITCH = ROW_SUB + 1


def _store_row_tiles(ref, x):
    n = x.shape[0]
    for cc in range(ROW_SUB):
        ref[pl.ds(cc, n, stride=ROW_SUB), :] = x[:, cc * LANES:(cc + 1) * LANES]


def _load_row_tiles(ref, n):
    return jnp.concatenate([ref[pl.ds(cc, n, stride=ROW_SUB), :] for cc in range(ROW_SUB)], axis=1)


def _gelu_tanh(x):
    c = 0.7978845608028654
    return x * (0.5 * (1.0 + jnp.tanh(c * (x + 0.044715 * (x * x * x)))))


def _ada_kernel(c_ref, w_ref, b_ref, o_ref):
    c = c_ref[...]
    a = c / (1.0 + jnp.exp(-c))
    w = w_ref[...]
    a_hi = a.astype(_BF16)
    a_lo = (a - a_hi.astype(_F32)).astype(_BF16)
    w_hi = w.astype(_BF16)
    w_lo = (w - w_hi.astype(_F32)).astype(_BF16)
    o_ref[...] = (jnp.dot(a_hi, w_hi, preferred_element_type=_F32)
                  + jnp.dot(a_lo, w_hi, preferred_element_type=_F32)
                  + jnp.dot(a_hi, w_lo, preferred_element_type=_F32) + b_ref[...])


def _ada_call(cc, w_ada, b_ada):
    n = N_MOD * D_MODEL
    tn = 512
    return pl.pallas_call(
        _ada_kernel,
        out_shape=jax.ShapeDtypeStruct((16, n), _F32),
        grid=(n // tn,),
        in_specs=[pl.BlockSpec((16, D_MODEL), lambda j: (0, 0)),
                  pl.BlockSpec((D_MODEL, tn), lambda j: (0, j)),
                  pl.BlockSpec((1, tn), lambda j: (0, j))],
        out_specs=pl.BlockSpec((16, tn), lambda j: (0, j)),
        compiler_params=pltpu.CompilerParams(dimension_semantics=("arbitrary",),
                                             vmem_limit_bytes=VMEM_LIMIT),
        name="ada",
    )(cc, w_ada, b_ada)


def _rope(x, cos, sin_signed, first_half):
    nxt = pltpu.roll(x, LANES - 16, 1)
    prv = pltpu.roll(x, 16, 1)
    return x * cos + jnp.where(first_half, nxt, prv) * sin_signed


def _inproj_kernel(x_ref, mod_ref, g_ref, w_ref, b_ref, cos_ref, sin_ref,
                   q_ref, k_ref, v_ref, su_ref, sv_ref):
    m = mod_ref[0]
    h = _rms(x_ref[...], g_ref[...]) * (1.0 + m[1:2]) + m[0:1]
    hb = h.astype(_BF16)
    cos = cos_ref[...]
    sin = sin_ref[...]
    lane = lax.broadcasted_iota(jnp.int32, cos.shape, 1)
    first_half = (lane & 31) < 16

    q = jnp.dot(hb, w_ref[:, 0:ATTN_WIDTH], preferred_element_type=_F32) + b_ref[:, 0:ATTN_WIDTH]
    for j in range(ATTN_WIDTH // LANES):
        qj = _rope(q[:, j * LANES:(j + 1) * LANES], cos, sin, first_half)
        q_ref[:, j * LANES:(j + 1) * LANES] = (qj * QK_SCALE).astype(_BF16)
    kv = jnp.dot(hb, w_ref[:, KV_START:KV_END], preferred_element_type=_F32) + b_ref[:, KV_START:KV_END]
    k_ref[...] = _rope(kv[:, 0:KV_WIDTH], cos, sin, first_half).astype(_BF16)
    v_ref[...] = kv[:, KV_WIDTH:].astype(_BF16)
    u0 = KV_END
    u1 = KV_END + SGU_WIDTH
    su_ref[...] = jnp.dot(hb, w_ref[:, u0:u1], preferred_element_type=_F32) + b_ref[:, u0:u1]
    sv_ref[...] = jnp.dot(hb, w_ref[:, u1:IN_WIDTH], preferred_element_type=_F32) + b_ref[:, u1:IN_WIDTH]


def _inproj_call(x2, mod3, g_pre, w_in_b, b_in2, cos_t, sin_t):
    tm = ROW_TILE
    tiles_per_batch = SEQ // tm
    row = lambda i: (i, 0)
    const = lambda i: (0, 0)
    return pl.pallas_call(
        _inproj_kernel,
        out_shape=(jax.ShapeDtypeStruct((TOKENS, ATTN_WIDTH), _BF16),
                   jax.ShapeDtypeStruct((TOKENS, KV_WIDTH), _BF16),
                   jax.ShapeDtypeStruct((TOKENS, KV_WIDTH), _BF16),
                   jax.ShapeDtypeStruct((TOKENS, SGU_WIDTH), _F32),
                   jax.ShapeDtypeStruct((TOKENS, SGU_WIDTH), _F32)),
        grid=(TOKENS // tm,),
        in_specs=[pl.BlockSpec((tm, D_MODEL), row),
                  pl.BlockSpec((1, N_MOD, D_MODEL), lambda i: (i // tiles_per_batch, 0, 0)),
                  pl.BlockSpec((1, D_MODEL), const),
                  pl.BlockSpec((D_MODEL, IN_WIDTH), const),
                  pl.BlockSpec((1, IN_WIDTH), const),
                  pl.BlockSpec((tm, LANES), lambda i: (i % tiles_per_batch, 0)),
                  pl.BlockSpec((tm, LANES), lambda i: (i % tiles_per_batch, 0))],
        out_specs=(pl.BlockSpec((tm, ATTN_WIDTH), row),
                   pl.BlockSpec((tm, KV_WIDTH), row),
                   pl.BlockSpec((tm, KV_WIDTH), row),
                   pl.BlockSpec((tm, SGU_WIDTH), row),
                   pl.BlockSpec((tm, SGU_WIDTH), row)),
        compiler_params=pltpu.CompilerParams(dimension_semantics=("arbitrary",),
                                             vmem_limit_bytes=VMEM_LIMIT),
        name="inproj",
    )(x2, mod3, g_pre, w_in_b, b_in2, cos_t, sin_t)


def _ctxproj_kernel(x_ref, mod_ref, g_ref, w_ref, b_ref, k_ref, v_ref):
    m = mod_ref[0]
    h = _rms(x_ref[...], g_ref[...]) * (1.0 + m[1:2]) + m[0:1]
    kv = jnp.dot(h.astype(_BF16), w_ref[...], preferred_element_type=_F32) + b_ref[...]
    k_ref[...] = kv[:, 0:KV_WIDTH].astype(_BF16)
    v_ref[...] = kv[:, KV_WIDTH:].astype(_BF16)


def _ctxproj_call(ctx2, mod3, g_pre, w_kv_b, b_kv2):
    tm = ROW_TILE
    rows = ctx2.shape[0]
    row = lambda i: (i, 0)
    const = lambda i: (0, 0)
    return pl.pallas_call(
        _ctxproj_kernel,
        out_shape=(jax.ShapeDtypeStruct((rows, KV_WIDTH), _BF16),
                   jax.ShapeDtypeStruct((rows, KV_WIDTH), _BF16)),
        grid=(rows // tm,),
        in_specs=[pl.BlockSpec((tm, D_MODEL), row),
                  pl.BlockSpec((1, N_MOD, D_MODEL), lambda i: (BATCH, 0, 0)),
                  pl.BlockSpec((1, D_MODEL), const),
                  pl.BlockSpec((D_MODEL, 2 * KV_WIDTH), const),
                  pl.BlockSpec((1, 2 * KV_WIDTH), const)],
        out_specs=(pl.BlockSpec((tm, KV_WIDTH), row),
                   pl.BlockSpec((tm, KV_WIDTH), row)),
        compiler_params=pltpu.CompilerParams(dimension_semantics=("arbitrary",),
                                             vmem_limit_bytes=VMEM_LIMIT),
        name="ctxproj",
    )(ctx2, mod3, g_pre, w_kv_b, b_kv2)


def _attn_kernel(sink_ref, q_ref, kp_ref, km_ref, kn_ref, vp_ref, vm_ref, vn_ref,
                 kc_ref, vc_ref, o_ref):
    n = pl.program_id(1)
    nstep = pl.num_programs(1)
    blk = ATTN_BLOCK
    nq = ATTN_STEP_BLOCKS
    rows2 = 2 * blk
    r = lax.broadcasted_iota(jnp.int32, (rows2, blk), 0) & (blk - 1)
    c = lax.broadcasted_iota(jnp.int32, (rows2, blk), 1)
    tri_prev = c >= r
    tri_next = c <= r
    row1 = lax.broadcasted_iota(jnp.int32, (rows2, 1), 0)
    lane_q = lax.broadcasted_iota(jnp.int32, (blk, LANES), 1)
    n_grp = ATTN_WIDTH // LANES
    k_blocks = [kp_ref[...]] + [km_ref[t * blk:(t + 1) * blk, :] for t in range(nq)] + [kn_ref[...]]
    v_blocks = [vp_ref[...]] + [vm_ref[t * blk:(t + 1) * blk, :] for t in range(nq)] + [vn_ref[...]]

    def scores(t):
        keys = jnp.concatenate(k_blocks[t:t + 3] + [kc_ref[...]], axis=0)
        q_rows = []
        for j in range(n_grp):
            qg = q_ref[t * blk:(t + 1) * blk, j * LANES:(j + 1) * LANES]
            zero = jnp.zeros_like(qg)
            q_rows += [jnp.where(lane_q < HEAD_DIM, qg, zero), jnp.where(lane_q >= HEAD_DIM, qg, zero)]
        return lax.dot_general(jnp.concatenate(q_rows, axis=0), keys, (((1,), (1,)), ((), ())),
                               preferred_element_type=_F32)

    def softmax(t, s_all):
        mask_a = (tri_prev & (n > 0)) if t == 0 else tri_prev
        mask_c = (tri_next & (n < nstep - 1)) if t == nq - 1 else tri_next
        p_rows, denoms = [], []
        for j in range(n_grp):
            s = s_all[j * rows2:(j + 1) * rows2]
            s = jnp.concatenate([jnp.where(mask_a, s[:, 0:blk], NEG_BIG), s[:, blk:2 * blk],
                                 jnp.where(mask_c, s[:, 2 * blk:3 * blk], NEG_BIG), s[:, 3 * blk:]],
                                axis=1)
            sk = jnp.where(row1 < blk, sink_ref[2 * j], sink_ref[2 * j + 1])
            m = jnp.maximum(jnp.max(s, axis=-1, keepdims=True), sk)
            p = jnp.exp2(s - m)
            denoms.append(jnp.sum(p, axis=-1, keepdims=True) + jnp.exp2(sk - m))
            p_rows.append(p.astype(_BF16))
        return jnp.concatenate(p_rows, axis=0), denoms

    def weighted_values(t, p_all, denoms):
        vals = jnp.concatenate(v_blocks[t:t + 3] + [vc_ref[...]], axis=0)
        o_all = jnp.dot(p_all, vals, preferred_element_type=_F32)
        for j in range(n_grp):
            o2 = o_all[j * rows2:(j + 1) * rows2] / denoms[j]
            og = jnp.where(lane_q < HEAD_DIM, o2[0:blk], o2[blk:])
            o_ref[t * blk:(t + 1) * blk, j * LANES:(j + 1) * LANES] = og.astype(_BF16)

    s_next = scores(0)
    probs = None
    for t in range(nq + 1):
        s_cur, s_next = s_next, (scores(t + 1) if t + 1 < nq else None)
        if probs is not None:
            weighted_values(t - 1, *probs)
        probs = softmax(t, s_cur) if t < nq else None


def _attn_call(sink_p, q, k, v, kc, vc):
    nblk = SEQ // ATTN_BLOCK
    nq = ATTN_STEP_BLOCKS
    nstep = nblk // nq
    own = lambda b, n: (b * nstep + n, 0)
    prev = lambda b, n: (b * nblk + jnp.maximum(nq * n - 1, 0), 0)
    nxt = lambda b, n: (b * nblk + jnp.minimum(nq * n + nq, nblk - 1), 0)
    ctx = lambda b, n: (b, 0)
    kv1 = (ATTN_BLOCK, KV_WIDTH)
    kvm = (nq * ATTN_BLOCK, KV_WIDTH)
    return pl.pallas_call(
        _attn_kernel,
        out_shape=jax.ShapeDtypeStruct((TOKENS, ATTN_WIDTH), _BF16),
        grid=(BATCH, nstep),
        in_specs=[pl.BlockSpec(memory_space=pltpu.SMEM),
                  pl.BlockSpec((nq * ATTN_BLOCK, ATTN_WIDTH), own),
                  pl.BlockSpec(kv1, prev), pl.BlockSpec(kvm, own), pl.BlockSpec(kv1, nxt),
                  pl.BlockSpec(kv1, prev), pl.BlockSpec(kvm, own), pl.BlockSpec(kv1, nxt),
                  pl.BlockSpec((CTX_LEN, KV_WIDTH), ctx),
                  pl.BlockSpec((CTX_LEN, KV_WIDTH), ctx)],
        out_specs=pl.BlockSpec((nq * ATTN_BLOCK, ATTN_WIDTH), own),
        compiler_params=pltpu.CompilerParams(dimension_semantics=("arbitrary", "arbitrary"),
                                             vmem_limit_bytes=VMEM_LIMIT),
        name="attn",
    )(sink_p, q, k, k, k, v, v, v, kc, vc)


def _post_kernel(x_ref, ao_ref, su_ref, sv_ref, mod_ref, lng_ref, lnb_ref, ws_ref, bs_ref,
                 ga_ref, gs_ref, woa_ref, wos_ref, bo_ref, gpost_ref, gpre_ref, wr_ref, br_ref,
                 tri_ref,
                 xmid_ref, h2_ref, idx_ref, gate_ref, rank_ref, cnt_ref,
                 mixed_ref, carry_ref):
    tm = x_ref.shape[0]
    m = mod_ref[0]

    @pl.when(pl.program_id(0) == 0)
    def _():
        carry_ref[...] = jnp.zeros_like(carry_ref)

    n_sub = POST_SPLIT
    ts = tm // n_sub
    tok_per_row = LANES // TOP_K
    lane = lax.broadcasted_iota(jnp.int32, (SGU_CHUNK, LANES), 1)
    st = [dict() for _ in range(n_sub)]
    carry = [carry_ref[...]]

    def rows(h):
        return slice(h * ts, (h + 1) * ts)

    def s1(h):
        gv = _gelu_tanh(sv_ref[rows(h), :])
        mu = jnp.mean(gv, axis=-1, keepdims=True)
        gc = gv - mu
        var = jnp.mean(gc * gc, axis=-1, keepdims=True)
        st[h]["vb"] = (gc * lax.rsqrt(var + EPS) * lng_ref[...] + lnb_ref[...]).astype(_BF16)

    def s2(h):
        vb = st[h].pop("vb")
        for c in range(ts // SGU_CHUNK):
            r0 = c * SGU_CHUNK
            for p in range(SGU_WIDTH // LANES):
                l0 = p * LANES
                r = jnp.dot(ws_ref[p], vb[r0:r0 + SGU_CHUNK, l0:l0 + LANES], preferred_element_type=_F32)
                mixed = jnp.where(lane < SGU_HEAD_DIM, r[0:SGU_CHUNK], r[SGU_CHUNK:])
                mixed_ref[h * ts + r0:h * ts + r0 + SGU_CHUNK, l0:l0 + LANES] = mixed + bs_ref[:, l0:l0 + LANES]
        sgu_o = _gelu_tanh(su_ref[rows(h), :]) * mixed_ref[rows(h), :]
        st[h]["oa"] = _rms(ao_ref[rows(h), :].astype(_F32), ga_ref[...]).astype(_BF16)
        st[h]["os"] = _rms(sgu_o, gs_ref[...]).astype(_BF16)

    def s3(h):
        st[h]["mix"] = (jnp.dot(st[h].pop("oa"), woa_ref[...], preferred_element_type=_F32)
                        + jnp.dot(st[h].pop("os"), wos_ref[...], preferred_element_type=_F32) + bo_ref[...])

    def s4(h):
        x_mid = x_ref[rows(h), :] + m[2:3] * _rms(st[h].pop("mix"), gpost_ref[...])
        xmid_ref[rows(h), :] = x_mid
        h2 = _rms(x_mid, gpre_ref[...]) * (1.0 + m[4:5]) + m[3:4]
        for cc in range(ROW_SUB):
            h2_ref[pl.ds(h * ts * ROW_SUB + cc, ts, stride=ROW_SUB), :] = h2[:, cc * LANES:(cc + 1) * LANES]
        h_hi = h2.astype(_BF16)
        st[h]["h_hi"] = h_hi
        st[h]["h_lo"] = (h2 - h_hi.astype(_F32)).astype(_BF16)

    def s5(h):
        r = (jnp.dot(st[h].pop("h_hi"), wr_ref[...], preferred_element_type=_F32)
             + jnp.dot(st[h].pop("h_lo"), wr_ref[...], preferred_element_type=_F32))
        st[h]["lg"] = r + pltpu.roll(r, LANES - N_EXPERTS, 1) + br_ref[...]

    def s6(h):
        lg = st[h].pop("lg")
        lane_r = lax.broadcasted_iota(jnp.int32, lg.shape, 1)
        lane_f = lane_r.astype(_F32)
        tops, hots = [], []
        for _k in range(TOP_K):
            mx = jnp.max(lg, axis=-1, keepdims=True)
            pick = jnp.min(jnp.where(lg == mx, lane_f, float(LANES)), axis=-1, keepdims=True)
            hot = lane_f == pick
            tops.append((mx, pick))
            hots.append(hot)
            lg = jnp.where(hot, 2.0 * NEG_BIG, lg)
        es = [jnp.exp(t[0] - tops[0][0]) for t in tops]
        esum = es[0] + es[1] + es[2] + es[3]
        multi = jnp.zeros(lg.shape, _F32)
        for hot in hots:
            multi = multi + jnp.where(hot, 1.0, 0.0)
        cum = jnp.dot(tri_ref[...], multi.astype(_BF16), preferred_element_type=_F32) + carry[0]
        row_r = lax.broadcasted_iota(jnp.int32, lg.shape, 0)
        lane_base = (row_r & (tok_per_row - 1)) * TOP_K
        gate_o = jnp.zeros(lg.shape, _F32)
        idx_e = jnp.zeros(lg.shape, _F32)
        rank_hi_e = jnp.zeros(lg.shape, _F32)
        rank_lo_e = jnp.zeros(lg.shape, _F32)
        for kk in range(TOP_K):
            rk = jnp.sum(jnp.where(hots[kk], cum, 0.0), axis=-1, keepdims=True)
            rk_hi = jnp.floor(rk * (1.0 / 256.0))
            here = lane_r == lane_base + kk
            gate_o = jnp.where(lane_r == kk, es[kk] / esum, gate_o)
            idx_e = jnp.where(here, tops[kk][1], idx_e)
            rank_hi_e = jnp.where(here, rk_hi, rank_hi_e)
            rank_lo_e = jnp.where(here, rk - 256.0 * rk_hi, rank_lo_e)
        fr = lax.broadcasted_iota(jnp.int32, (ts // tok_per_row, ts), 0)
        fc = lax.broadcasted_iota(jnp.int32, (ts // tok_per_row, ts), 1)
        fold = jnp.where(lax.shift_right_logical(fc, tok_per_row.bit_length() - 1) == fr, 1.0, 0.0).astype(_BF16)
        fs = slice(h * ts // tok_per_row, (h + 1) * ts // tok_per_row)
        idx_ref[fs, :] = jnp.dot(fold, idx_e.astype(_BF16), preferred_element_type=_F32).astype(jnp.int32)
        rank_ref[fs, :] = (256.0 * jnp.dot(fold, rank_hi_e.astype(_BF16), preferred_element_type=_F32)
                           + jnp.dot(fold, rank_lo_e.astype(_BF16), preferred_element_type=_F32)).astype(jnp.int32)
        gate_ref[rows(h), :] = gate_o
        carry[0] = carry[0] + jnp.sum(multi, axis=0, keepdims=True)

    order = sorted(((k + 2.5 * h, h, k) for h in range(n_sub) for k in range(6)))
    stages = (s1, s2, s3, s4, s5, s6)
    for _, h, k in order:
        stages[k](h)
    carry_ref[...] = carry[0]
    cnt_ref[...] = carry[0].astype(jnp.int32)


def _post_call(x2, ao, su, sv, mod3, lng, lnb, ws2, bsf, ga, gs, woa, wos, bo, gpost, gpre,
               wr, br, tri):
    tm = POST_TILE
    tiles_per_batch = SEQ // tm
    row = lambda i: (i, 0)
    const = lambda i: (0, 0)
    const3 = lambda i: (0, 0, 0)
    return pl.pallas_call(
        _post_kernel,
        out_shape=(jax.ShapeDtypeStruct((TOKENS, D_MODEL), _F32),
                   jax.ShapeDtypeStruct((TOKENS * ROW_SUB, LANES), _F32),
                   jax.ShapeDtypeStruct((TOKENS * TOP_K // LANES, LANES), jnp.int32),
                   jax.ShapeDtypeStruct((TOKENS, LANES), _F32),
                   jax.ShapeDtypeStruct((TOKENS * TOP_K // LANES, LANES), jnp.int32),
                   jax.ShapeDtypeStruct((1, LANES), jnp.int32)),
        grid=(TOKENS // tm,),
        in_specs=[pl.BlockSpec((tm, D_MODEL), row),
                  pl.BlockSpec((tm, ATTN_WIDTH), row),
                  pl.BlockSpec((tm, SGU_WIDTH), row),
                  pl.BlockSpec((tm, SGU_WIDTH), row),
                  pl.BlockSpec((1, N_MOD, D_MODEL), lambda i: (i // tiles_per_batch, 0, 0)),
                  pl.BlockSpec((1, SGU_WIDTH), const),
                  pl.BlockSpec((1, SGU_WIDTH), const),
                  pl.BlockSpec((SGU_WIDTH // LANES, 2 * SGU_CHUNK, SGU_CHUNK), const3),
                  pl.BlockSpec((SGU_CHUNK, SGU_WIDTH), const),
                  pl.BlockSpec((1, ATTN_WIDTH), const),
                  pl.BlockSpec((1, SGU_WIDTH), const),
                  pl.BlockSpec((ATTN_WIDTH, D_MODEL), const),
                  pl.BlockSpec((SGU_WIDTH, D_MODEL), const),
                  pl.BlockSpec((1, D_MODEL), const),
                  pl.BlockSpec((1, D_MODEL), const),
                  pl.BlockSpec((1, D_MODEL), const),
                  pl.BlockSpec((D_MODEL, LANES), const),
                  pl.BlockSpec((1, LANES), const),
                  pl.BlockSpec((tm // POST_SPLIT, tm // POST_SPLIT), const)],
        out_specs=(pl.BlockSpec((tm, D_MODEL), row),
                   pl.BlockSpec((tm * ROW_SUB, LANES), row),
                   pl.BlockSpec((tm * TOP_K // LANES, LANES), row),
                   pl.BlockSpec((tm, LANES), row),
                   pl.BlockSpec((tm * TOP_K // LANES, LANES), row),
                   pl.BlockSpec((1, LANES), const)),
        scratch_shapes=[pltpu.VMEM((tm, SGU_WIDTH), _F32),
                        pltpu.VMEM((1, LANES), _F32)],
        compiler_params=pltpu.CompilerParams(dimension_semantics=("arbitrary",),
                                             vmem_limit_bytes=VMEM_LIMIT),
        name="post",
    )(x2, ao, su, sv, mod3, lng, lnb, ws2, bsf, ga, gs, woa, wos, bo, gpost, gpre, wr, br, tri)


def _dest_kernel(ps_ref, idx_ref, rank_ref, o_ref):
    idx = idx_ref[...]
    dest = rank_ref[...]
    for e in range(N_EXPERTS):
        dest = dest + jnp.where(idx == e, ps_ref[e], 0)
    o_ref[...] = dest


def _dest_call(pad_start, idx_flat, rank_flat):
    return pl.pallas_call(
        _dest_kernel,
        out_shape=jax.ShapeDtypeStruct(idx_flat.shape, jnp.int32),
        grid=(1,),
        in_specs=[pl.BlockSpec(memory_space=pltpu.SMEM),
                  pl.BlockSpec(idx_flat.shape, lambda i: (0, 0)),
                  pl.BlockSpec(rank_flat.shape, lambda i: (0, 0))],
        out_specs=pl.BlockSpec(idx_flat.shape, lambda i: (0, 0)),
        compiler_params=pltpu.CompilerParams(dimension_semantics=("arbitrary",),
                                             vmem_limit_bytes=VMEM_LIMIT),
        name="dest",
    )(pad_start, idx_flat, rank_flat)


def _dispatch_kernel(fs_ref, fl_ref, nu_ref, dest_ref, h2_ref, xs_ref, zero_ref, sem, zsem):
    tm = h2_ref.shape[0] // ROW_SUB
    n_token_steps = TOKENS // tm
    i = pl.program_id(0)

    @pl.when(i < n_token_steps)
    def _():
        def issue(r, carry):
            for kk in range(TOP_K):
                d = pl.multiple_of(dest_ref[r * TOP_K + kk] * ROW_SUB, ROW_SUB)
                pltpu.make_async_copy(h2_ref.at[pl.ds(pl.multiple_of(r * ROW_SUB, ROW_SUB), ROW_SUB)],
                                      xs_ref.at[pl.ds(d, ROW_SUB)], sem).start(priority=kk % 2)
            return carry

        lax.fori_loop(0, tm, issue, 0, unroll=8)
        for kk in range(TOP_K):
            pltpu.make_async_copy(h2_ref, xs_ref.at[pl.ds(0, tm * ROW_SUB)], sem).wait()

    @pl.when(i == n_token_steps)
    def _():
        zero_ref[...] = jnp.zeros_like(zero_ref)
        block_rows = EXPERT_BLOCK * ROW_SUB

        def pad_run(e, wait):
            pos = fs_ref[e]
            length = fl_ref[e]
            for bit in reversed(range(EXPERT_BLOCK.bit_length() - 1)):
                size = 1 << bit
                take = length & size

                @pl.when(take != 0)
                def _():
                    cp = pltpu.make_async_copy(
                        zero_ref.at[pl.ds(0, size * ROW_SUB)],
                        xs_ref.at[pl.ds(pl.multiple_of(pos * ROW_SUB, ROW_SUB), size * ROW_SUB)], zsem)
                    if wait:
                        cp.wait()
                    else:
                        cp.start()

                pos = pos + take

        def tail_block(blk, wait):
            cp = pltpu.make_async_copy(
                zero_ref, xs_ref.at[pl.ds(pl.multiple_of(blk * block_rows, block_rows), block_rows)], zsem)
            if wait:
                cp.wait()
            else:
                cp.start()

        for wait in (False, True):
            lax.fori_loop(0, N_EXPERTS, lambda e, c, w=wait: (pad_run(e, w), c)[1], 0)
            lax.fori_loop(nu_ref[0], N_EXPERT_BLOCKS, lambda blk, c, w=wait: (tail_block(blk, w), c)[1], 0)


def _dispatch_call(fill_start, fill_len, n_used, dest_flat, h2):
    tm = SCATTER_TILE
    n_token_steps = TOKENS // tm
    grid_spec = pltpu.PrefetchScalarGridSpec(
        num_scalar_prefetch=3,
        grid=(n_token_steps + 1,),
        in_specs=[pl.BlockSpec((tm * TOP_K,), lambda i, fs, fl, nu: (jnp.minimum(i, n_token_steps - 1),),
                               memory_space=pltpu.SMEM),
                  pl.BlockSpec((tm * ROW_SUB, LANES),
                               lambda i, fs, fl, nu: (jnp.minimum(i, n_token_steps - 1), 0))],
        out_specs=pl.BlockSpec(memory_space=pl.ANY),
        scratch_shapes=[pltpu.VMEM((EXPERT_BLOCK * ROW_SUB, LANES), _F32),
                        pltpu.SemaphoreType.DMA,
                        pltpu.SemaphoreType.DMA])
    return pl.pallas_call(
        _dispatch_kernel,
        out_shape=jax.ShapeDtypeStruct((SORTED_ROWS * ROW_SUB, LANES), _F32),
        grid_spec=grid_spec,
        compiler_params=pltpu.CompilerParams(dimension_semantics=("arbitrary",),
                                             vmem_limit_bytes=VMEM_LIMIT),
        name="dispatch",
    )(fill_start, fill_len, n_used, dest_flat, h2)


def _expert_kernel(be_ref, slot_ref, first_ref, nxt_ref, nu_ref,
                   xs_ref, wgu_hbm, bgu0_ref, bgu1_ref, wd_hbm, bd0_ref, bd1_ref, y_ref,
                   wgu_stage, wd_stage, wgu_bf, wd_bf, sem):
    p = pl.program_id(0)
    half = EXPERT_BLOCK * ROW_SUB
    n_pairs_used = (nu_ref[0] + 1) // 2

    def stage_copies(e):
        return (pltpu.make_async_copy(wgu_hbm.at[e], wgu_stage, sem.at[0]),
                pltpu.make_async_copy(wd_hbm.at[e], wd_stage, sem.at[1]))

    @pl.when(p == 0)
    def _():
        for cp in stage_copies(be_ref[0]):
            cp.start()

    for b in (2 * p, 2 * p + 1):
        @pl.when((first_ref[b] == 1) & (b < nu_ref[0]))
        def _():
            for cp in stage_copies(be_ref[b]):
                cp.wait()
            wgu_bf[slot_ref[b]] = wgu_stage[...].astype(_BF16)
            wd_bf[slot_ref[b]] = wd_stage[...].astype(_BF16)

            @pl.when(nxt_ref[b] < N_EXPERTS)
            def _():
                for cp in stage_copies(nxt_ref[b]):
                    cp.start()

    def ffn(x, slot, bgu_ref, bd_ref):
        gu = jnp.dot(x, wgu_bf[slot], preferred_element_type=_F32) + bgu_ref[0]
        gate = jnp.minimum(gu[:, 0:D_FF_EXPERT], SWIGLU_LIMIT)
        up = jnp.clip(gu[:, D_FF_EXPERT:], -SWIGLU_LIMIT, SWIGLU_LIMIT)
        act = (up + 1.0) * gate * (1.0 / (1.0 + jnp.exp(-SWIGLU_ALPHA * gate)))
        return jnp.dot(act.astype(_BF16), wd_bf[slot], preferred_element_type=_F32) + bd_ref[0]

    def load_rows(which):
        return jnp.concatenate([xs_ref[pl.ds(which * half + cc, EXPERT_BLOCK, stride=ROW_SUB), :]
                                for cc in range(ROW_SUB)], axis=1).astype(_BF16)

    def store_rows(which, y):
        for cc in range(ROW_SUB):
            y_ref[pl.ds(which * half + cc, EXPERT_BLOCK, stride=ROW_SUB), :] = y[:, cc * LANES:(cc + 1) * LANES]

    @pl.when(p < n_pairs_used)
    def _():
        y0 = ffn(load_rows(0), slot_ref[2 * p], bgu0_ref, bd0_ref)
        x1 = load_rows(1)
        store_rows(0, y0)
        store_rows(1, ffn(x1, slot_ref[2 * p + 1], bgu1_ref, bd1_ref))

    @pl.when(p >= n_pairs_used)
    def _():
        y_ref[...] = jnp.zeros_like(y_ref)


def _expert_call(block_expert, block_slot, block_first, block_next, n_used, xs, wgu, bgu3, wd, bd3):
    tb = 2 * EXPERT_BLOCK
    live = lambda p, be, sl, fi, nx, nu: (jnp.minimum(p, (nu[0] - 1) // 2), 0)
    bsel0 = lambda p, be, sl, fi, nx, nu: (be[2 * p], 0, 0)
    bsel1 = lambda p, be, sl, fi, nx, nu: (be[2 * p + 1], 0, 0)
    grid_spec = pltpu.PrefetchScalarGridSpec(
        num_scalar_prefetch=5,
        grid=(N_EXPERT_BLOCKS // 2,),
        in_specs=[pl.BlockSpec((tb * ROW_SUB, LANES), live),
                  pl.BlockSpec(memory_space=pl.ANY),
                  pl.BlockSpec((1, 1, 2 * D_FF_EXPERT), bsel0),
                  pl.BlockSpec((1, 1, 2 * D_FF_EXPERT), bsel1),
                  pl.BlockSpec(memory_space=pl.ANY),
                  pl.BlockSpec((1, 1, D_MODEL), bsel0),
                  pl.BlockSpec((1, 1, D_MODEL), bsel1)],
        out_specs=pl.BlockSpec((tb * ROW_SUB, LANES), lambda p, be, sl, fi, nx, nu: (p, 0)),
        scratch_shapes=[pltpu.VMEM((D_MODEL, 2 * D_FF_EXPERT), _F32),
                        pltpu.VMEM((D_FF_EXPERT, D_MODEL), _F32),
                        pltpu.VMEM((2, D_MODEL, 2 * D_FF_EXPERT), _BF16),
                        pltpu.VMEM((2, D_FF_EXPERT, D_MODEL), _BF16),
                        pltpu.SemaphoreType.DMA((2,))])
    return pl.pallas_call(
        _expert_kernel,
        out_shape=jax.ShapeDtypeStruct((SORTED_ROWS * ROW_SUB, LANES), _F32),
        grid_spec=grid_spec,
        compiler_params=pltpu.CompilerParams(dimension_semantics=("arbitrary",),
                                             vmem_limit_bytes=VMEM_LIMIT),
        name="expert",
    )(block_expert, block_slot, block_first, block_next, n_used, xs, wgu, bgu3, bgu3, wd, bd3, bd3)


def _combine_kernel(dest_ref, dest_next_ref, gate_ref, xmid_ref, mod_ref, g_ref, y_ref, o_ref,
                    buf_ref, sem):
    tm = xmid_ref.shape[0]
    m = mod_ref[0]
    i = pl.program_id(0)
    n = pl.num_programs(0)

    def gather_rows(idx_ref, slot):
        def issue(r, carry):
            for kk in range(TOP_K):
                d = pl.multiple_of(idx_ref[r * TOP_K + kk] * ROW_SUB, ROW_SUB)
                pltpu.make_async_copy(
                    y_ref.at[pl.ds(d, ROW_SUB)],
                    buf_ref.at[slot, kk, pl.ds(r * GATHER_PITCH, ROW_SUB)],
                    sem.at[slot]).start(priority=kk % 2)
            return carry

        lax.fori_loop(0, tm, issue, 0, unroll=8)

    @pl.when(i == 0)
    def _():
        gather_rows(dest_ref, 0)

    @pl.when(i + 1 < n)
    def _():
        gather_rows(dest_next_ref, (i + 1) % 2)

    slot = i % 2
    for kk in range(TOP_K):
        pltpu.make_async_copy(y_ref.at[pl.ds(0, tm * ROW_SUB)], buf_ref.at[slot, kk, pl.ds(0, tm * ROW_SUB)],
                              sem.at[slot]).wait()
    g = gate_ref[...]
    pieces = []
    for cc in range(ROW_SUB):
        piece = buf_ref[slot, 0, pl.ds(cc, tm, stride=GATHER_PITCH), :] * g[:, 0:1]
        for kk in range(1, TOP_K):
            piece = piece + buf_ref[slot, kk, pl.ds(cc, tm, stride=GATHER_PITCH), :] * g[:, kk:kk + 1]
        pieces.append(piece)
    ffn = jnp.concatenate(pieces, axis=1)
    o_ref[...] = xmid_ref[...] + m[5:6] * _rms(ffn, g_ref[...])


def _combine_call(dest_flat, gates, x_mid, mod3, gpost_ffn, y_rows):
    tm = MOVE_TILE
    tiles_per_batch = SEQ // tm
    n_steps = TOKENS // tm
    return pl.pallas_call(
        _combine_kernel,
        out_shape=jax.ShapeDtypeStruct((TOKENS, D_MODEL), _F32),
        grid=(n_steps,),
        in_specs=[pl.BlockSpec((tm * TOP_K,), lambda i: (i,), memory_space=pltpu.SMEM),
                  pl.BlockSpec((tm * TOP_K,), lambda i: (jnp.minimum(i + 1, n_steps - 1),),
                               memory_space=pltpu.SMEM),
                  pl.BlockSpec((tm, LANES), lambda i: (i, 0)),
                  pl.BlockSpec((tm, D_MODEL), lambda i: (i, 0)),
                  pl.BlockSpec((1, N_MOD, D_MODEL), lambda i: (i // tiles_per_batch, 0, 0)),
                  pl.BlockSpec((1, D_MODEL), lambda i: (0, 0)),
                  pl.BlockSpec(memory_space=pl.ANY)],
        out_specs=pl.BlockSpec((tm, D_MODEL), lambda i: (i, 0)),
        scratch_shapes=[pltpu.VMEM((2, TOP_K, tm * GATHER_PITCH, LANES), _F32),
                        pltpu.SemaphoreType.DMA((2,))],
        compiler_params=pltpu.CompilerParams(dimension_semantics=("arbitrary",),
                                             vmem_limit_bytes=VMEM_LIMIT),
        name="combine",
    )(dest_flat, dest_flat, gates, x_mid, mod3, gpost_ffn, y_rows)


def _rope_tables():
    pos = np.arange(SEQ)
    pos_row = (pos // GRID_W).astype(np.float64)
    pos_col = (pos % GRID_W).astype(np.float64)
    n_freq = HEAD_DIM // 4
    inv_freq = ROPE_THETA ** (-np.arange(n_freq, dtype=np.float64) / n_freq)
    d = np.arange(LANES) % HEAD_DIM
    f = inv_freq[d % n_freq]
    ang = np.where((d < HEAD_DIM // 2)[None, :], pos_row[:, None] * f[None, :], pos_col[:, None] * f[None, :])
    sign = np.where((d % (HEAD_DIM // 2)) < n_freq, -1.0, 1.0)
    return (jnp.asarray(np.cos(ang), dtype=_F32), jnp.asarray(np.sin(ang) * sign[None, :], dtype=_F32))


def _perm_heads(a, axis):
    shape = a.shape
    a = a.reshape(shape[:axis] + (N_HEADS, HEAD_DIM) + shape[axis + 1:])
    a = jnp.take(a, jnp.array(HEAD_PERM), axis=axis)
    return a.reshape(shape)


def kernel(x, c, ctx, c_ctx, w_ada, b_ada, g_pre_mix, g_post_mix, g_pre_ffn, g_post_ffn, w_in, b_in, attn_sink, sgu_ln_g, sgu_ln_b, sgu_w, sgu_b, g_attn_out, g_sgu_out, w_out, b_out, w_router, b_router, w_gate_up, b_gate_up, w_down, b_down):
    l = 0
    x2 = x.reshape(TOKENS, D_MODEL)
    ctx2 = ctx.reshape(BATCH * CTX_LEN, D_MODEL)

    cc = jnp.zeros((16, D_MODEL), _F32).at[:BATCH].set(c).at[BATCH].set(c_ctx)
    mod = _ada_call(cc, w_ada[l], b_ada[l].reshape(1, -1))
    mod3 = mod.reshape(16, N_MOD, D_MODEL)

    w_in_l = w_in[l]
    b_in_l = b_in[l]
    w_in_p = jnp.concatenate([_perm_heads(w_in_l[:, :ATTN_WIDTH], 1), w_in_l[:, ATTN_WIDTH:]], axis=1)
    b_in_p = jnp.concatenate([_perm_heads(b_in_l[:ATTN_WIDTH], 0), b_in_l[ATTN_WIDTH:]], axis=0)
    cos_t, sin_t = _rope_tables()
    g_pre = g_pre_mix[l].reshape(1, -1)

    q, k, v, su, sv = _inproj_call(x2, mod3, g_pre, w_in_p.astype(_BF16), b_in_p.reshape(1, -1), cos_t, sin_t)
    kc, vc = _ctxproj_call(ctx2, mod3, g_pre, w_in_l[:, KV_START:KV_END].astype(_BF16),
                           b_in_l[KV_START:KV_END].reshape(1, -1))

    sink_p = jnp.take(attn_sink[l], jnp.array(HEAD_PERM)) * LOG2_E
    ao = _attn_call(sink_p, q, k, v, kc, vc)

    ws2 = sgu_w[l].reshape(SGU_WIDTH // LANES, 2 * SGU_CHUNK, SGU_CHUNK).astype(_BF16)
    bsf = jnp.repeat(sgu_b[l].T, SGU_HEAD_DIM, axis=1)
    w_out_l = w_out[l]
    woa = _perm_heads(w_out_l[:ATTN_WIDTH], 0).astype(_BF16)
    wos = w_out_l[ATTN_WIDTH:].astype(_BF16)
    ga = _perm_heads(g_attn_out[l], 0).reshape(1, -1)
    wr_hi = w_router[l].astype(_BF16)
    wr_lo = (w_router[l] - wr_hi.astype(_F32)).astype(_BF16)
    wr = (jnp.zeros((D_MODEL, LANES), _BF16).at[:, :N_EXPERTS].set(wr_hi)
          .at[:, N_EXPERTS:2 * N_EXPERTS].set(wr_lo))
    br = jnp.full((1, LANES), NEG_BIG, _F32).at[0, :N_EXPERTS].set(b_router[l])
    ii = np.arange(POST_TILE // POST_SPLIT)
    tri = jnp.asarray(ii[None, :] < ii[:, None], dtype=_BF16)

    x_mid, h2, idx_o, gate_o, rank_o, cnt_o = _post_call(
        x2, ao, su, sv, mod3, sgu_ln_g[l].reshape(1, -1), sgu_ln_b[l].reshape(1, -1), ws2, bsf,
        ga, g_sgu_out[l].reshape(1, -1), woa, wos, b_out[l].reshape(1, -1),
        g_post_mix[l].reshape(1, -1), g_pre_ffn[l].reshape(1, -1), wr, br, tri)

    counts = cnt_o[0, :N_EXPERTS]
    padded = (counts + EXPERT_BLOCK - 1) // EXPERT_BLOCK * EXPERT_BLOCK
    pad_end = jnp.cumsum(padded)
    pad_start = pad_end - padded
    dest = _dest_call(pad_start.astype(jnp.int32), idx_o, rank_o).reshape(-1)
    n_used = (pad_end[-1:] // EXPERT_BLOCK).astype(jnp.int32)

    xs = _dispatch_call((pad_start + counts).astype(jnp.int32), (padded - counts).astype(jnp.int32),
                        n_used, dest, h2)

    e_ids = jnp.arange(N_EXPERTS, dtype=jnp.int32)
    used = padded > 0
    cand = jnp.where(used, e_ids, N_EXPERTS)
    next_used = jnp.min(jnp.where(e_ids[None, :] > e_ids[:, None], cand[None, :], N_EXPERTS), axis=1)
    ordinal = jnp.cumsum(used.astype(jnp.int32)) - 1
    block_id = jnp.arange(N_EXPERT_BLOCKS, dtype=jnp.int32)
    block_start = block_id * EXPERT_BLOCK
    sched_start = jnp.minimum(block_start, (n_used[0] - 1) * EXPERT_BLOCK)
    sched_expert = jnp.minimum(jnp.sum(pad_end[None, :] <= sched_start[:, None], axis=1),
                               N_EXPERTS - 1).astype(jnp.int32)
    onehot = e_ids[None, :] == sched_expert[:, None]
    pick = lambda table: jnp.sum(jnp.where(onehot, table[None, :], 0), axis=1)
    block_slot = (pick(ordinal) & 1).astype(jnp.int32)
    block_first = ((block_start == pick(pad_start)) & (block_id < n_used[0])).astype(jnp.int32)
    block_next = pick(next_used).astype(jnp.int32)
    y_rows = _expert_call(sched_expert, block_slot, block_first, block_next, n_used, xs,
                          w_gate_up[l], b_gate_up[l].reshape(N_EXPERTS, 1, -1),
                          w_down[l], b_down[l].reshape(N_EXPERTS, 1, -1))
    out = _combine_call(dest, gate_o, x_mid, mod3, g_post_ffn[l].reshape(1, -1), y_rows)
    return out.reshape(BATCH, SEQ, D_MODEL)
```
